```python
import math
import jax, jax.numpy as jnp
from jax import lax
import numpy as np

D_MODEL = 1024
BATCH = 2
SEQ = 8192
DEPTH = 2

GRID_W = 64
CTX_LEN = 256
NORM_EPS = 1e-6
N_MOD = 6

DN_HEADS = 4
DN_HEAD_DIM = 128
DN_WIDTH = DN_HEADS * DN_HEAD_DIM
DN_CHUNK = 64
SHORT_CONV = 3

FN_GROUPS = 4
FN_GROUP_DIM = 64
FN_WIDTH = FN_GROUPS * FN_GROUP_DIM

HY_WIDTH = 256
HY_ORDER = 2
HY_EMB_DIM = 33
HY_BANDS = (HY_EMB_DIM - 1) // 2
HY_FILTER_HIDDEN = 64
HY_FAST_DECAY_PCT = 0.3
HY_SLOW_DECAY_PCT = 1.5
HY_DECAY_TARGET = 1e-2

N_BRANCHES = 3
D_FF = 4 * D_MODEL

OFF_Q = 0
OFF_K = OFF_Q + DN_WIDTH
OFF_V = OFF_K + DN_WIDTH
OFF_Z = OFF_V + DN_WIDTH
OFF_BETA = OFF_Z + DN_WIDTH
OFF_A = OFF_BETA + 2 * DN_HEADS
OFF_FN = OFF_A + 2 * DN_HEADS
OFF_HY = OFF_FN + FN_WIDTH
OFF_GATE = OFF_HY + (HY_ORDER + 1) * HY_WIDTH
IN_WIDTH = OFF_GATE + N_BRANCHES * D_MODEL

kernel_name = 'hybrid_deltanet_fnet_hyena_prefix_dit'


def rms_norm(x, gain):
    xf = x.astype(jnp.float32)
    return xf * lax.rsqrt(jnp.mean(xf * xf, axis=-1, keepdims=True) + NORM_EPS) * gain.astype(jnp.float32)


def modulate(h, gain, shift, scale):
    y = rms_norm(h, gain) * (1.0 + scale.astype(jnp.float32)) + shift.astype(jnp.float32)
    return y.astype(h.dtype)


def l2_normalize(t):
    return t * lax.rsqrt(jnp.sum(t * t, axis=-1, keepdims=True) + NORM_EPS)


def grid_dwconv(x, w, rows, cols):
    b, n, ch = x.shape
    y = lax.conv_general_dilated(x.reshape(b, rows, cols, ch), w[:, :, None, :].astype(x.dtype), (1, 1), 'SAME',
                                 dimension_numbers=('NHWC', 'HWIO', 'NHWC'), feature_group_count=ch)
    return y.reshape(b, n, ch)


def seq_dwconv(x, w):
    ch = x.shape[-1]
    return lax.conv_general_dilated(x, w[:, None, :].astype(x.dtype), (1,), 'SAME',
                                    dimension_numbers=('NWC', 'WIO', 'NWC'), feature_group_count=ch)


def delta_prepare(p, conv_w, rows, cols, a_log, dt_bias):
    b, n, _ = p.shape
    f32 = jnp.float32
    qkv = jax.nn.silu(grid_dwconv(p[..., OFF_Q:OFF_Z], conv_w, rows, cols).astype(f32))
    qkv = qkv.reshape(b, n, 3, DN_HEADS, DN_HEAD_DIM)
    q = l2_normalize(qkv[:, :, 0]) * DN_HEAD_DIM ** -0.5
    k = l2_normalize(qkv[:, :, 1])
    v = qkv[:, :, 2]
    beta = jax.nn.sigmoid(p[..., OFF_BETA:OFF_A].astype(f32)).reshape(b, n, 2, DN_HEADS)
    a = p[..., OFF_A:OFF_FN].astype(f32).reshape(b, n, 2, DN_HEADS)
    g = -jnp.exp(a_log.astype(f32)) * jax.nn.softplus(a + dt_bias.astype(f32))
    return q, k, v, beta, g


def gated_delta_rule(q, k, v, beta, g, s0):
    b, n_tok, h, dk = q.shape
    dv = v.shape[-1]
    n_chunk = n_tok // DN_CHUNK

    def chunks(t):
        t = t.reshape((b, n_chunk, DN_CHUNK, h) + t.shape[3:])
        return jnp.moveaxis(t, (1, 3), (0, 2))

    qc, kc, vc, bc, gc = chunks(q), chunks(k), chunks(v), chunks(beta), chunks(g)
    G = jnp.cumsum(gc, axis=-1)
    idx = jnp.arange(DN_CHUNK)
    incl = idx[:, None] >= idx[None, :]
    strict = idx[:, None] > idx[None, :]
    decay = jnp.exp(jnp.where(incl, G[..., :, None] - G[..., None, :], -jnp.inf))
    kb = kc * bc[..., None]
    lhs = jnp.eye(DN_CHUNK, dtype=jnp.float32) + jnp.where(
        strict, jnp.einsum('nbhik,nbhjk->nbhij', kb, kc) * decay, 0.0)
    u_in = lax.linalg.triangular_solve(lhs, vc * bc[..., None], left_side=True, lower=True, unit_diagonal=True)
    w_in = lax.linalg.triangular_solve(lhs, kb * jnp.exp(G)[..., None], left_side=True, lower=True,
                                       unit_diagonal=True)
    qk = jnp.einsum('nbhik,nbhjk->nbhij', qc, kc) * decay
    q_dec = qc * jnp.exp(G)[..., None]
    k_dec = kc * jnp.exp(G[..., -1:] - G)[..., None]
    g_last = jnp.exp(G[..., -1])

    def step(state, inp):
        q_d, k_d, u0, w0, qk_c, gl = inp
        u = u0 - jnp.einsum('bhck,bhkv->bhcv', w0, state)
        o = jnp.einsum('bhck,bhkv->bhcv', q_d, state) + jnp.einsum('bhij,bhjv->bhiv', qk_c, u)
        state = state * gl[..., None, None] + jnp.einsum('bhck,bhcv->bhkv', k_d, u)
        return state, o

    s_final, o = lax.scan(step, s0, (q_dec, k_dec, u_in, w_in, qk, g_last))
    o = jnp.moveaxis(o, (0, 2), (1, 3)).reshape(b, n_tok, h, dv)
    return o, s_final


def bidir_delta(q, k, v, beta, g, s0_fwd, s0_bwd):
    o_f, s_f = gated_delta_rule(q, k, v, beta[:, :, 0], g[:, :, 0], s0_fwd)
    flip = lambda t: jnp.flip(t, axis=1)
    o_b, s_b = gated_delta_rule(flip(q), flip(k), flip(v), flip(beta[:, :, 1]), flip(g[:, :, 1]), s0_bwd)
    return o_f + flip(o_b), s_f, s_b


def fourier_mix(p):
    b, n, _ = p.shape
    pg = p.astype(jnp.float32).reshape(b, n, FN_GROUPS, FN_GROUP_DIM)
    return jnp.fft.fft2(pg, axes=(1, 3), norm='ortho').real.reshape(b, n, FN_WIDTH)


def hyena_kernel_rfft(n, w1, b1, freq1, w2, b2, freq2, w3):
    f32 = jnp.float32
    pos = jnp.arange(n, dtype=f32)
    t = pos / max(n - 1, 1)
    bands = jnp.linspace(1e-4, HY_BANDS - 1, HY_BANDS, dtype=f32)
    ang = (2.0 * math.pi / n) * pos[:, None] * bands[None, :]
    feats = jnp.concatenate([t[:, None], jnp.cos(ang), -jnp.sin(ang)], axis=-1)
    hid = jnp.sin(freq1.astype(f32) * (feats @ w1.astype(f32) + b1.astype(f32)))
    hid = jnp.sin(freq2.astype(f32) * (hid @ w2.astype(f32) + b2.astype(f32)))
    filt = (hid @ w3.astype(f32)).reshape(n, 2, HY_ORDER, HY_WIDTH)
    min_decay = math.log(HY_DECAY_TARGET) / HY_SLOW_DECAY_PCT
    max_decay = math.log(HY_DECAY_TARGET) / HY_FAST_DECAY_PCT
    deltas = jnp.abs(jnp.linspace(min_decay, max_decay, HY_ORDER * HY_WIDTH, dtype=f32)).reshape(HY_ORDER, HY_WIDTH)
    filt = filt * jnp.exp(-t[:, None, None, None] * deltas[None, None])
    fwd, bwd = filt[:, 0], filt[:, 1]
    kern = jnp.concatenate([fwd, jnp.zeros((1, HY_ORDER, HY_WIDTH), f32), bwd[:0:-1]], axis=0)
    kern = kern / jnp.sum(jnp.abs(kern), axis=0, keepdims=True)
    return jnp.fft.rfft(kern, axis=0)


def fft_conv(u, k_freq):
    n = u.shape[1]
    spec = jnp.fft.rfft(u, n=2 * n, axis=1) * k_freq[None]
    return jnp.fft.irfft(spec, n=2 * n, axis=1)[:, :n]


def hyena_branch(p, conv_w, k_freq, bias):
    parts = jnp.split(seq_dwconv(p, conv_w).astype(jnp.float32), HY_ORDER + 1, axis=-1)
    z = parts[HY_ORDER]
    for order in range(HY_ORDER):
        z = parts[order] * (fft_conv(z, k_freq[:, order]) + bias[order].astype(jnp.float32) * z)
    return z


def branch_merge(p, o_dn, hy_kfreq, dn_out_norm, hy_conv, hy_bias, w_branch_a, w_branch_b, w_branch_c, w_out):
    b, n, _ = p.shape
    dt = p.dtype
    z = p[..., OFF_Z:OFF_BETA].astype(jnp.float32).reshape(b, n, DN_HEADS, DN_HEAD_DIM)
    y_a = (rms_norm(o_dn, dn_out_norm) * jax.nn.silu(z)).reshape(b, n, DN_WIDTH)
    y_b = fourier_mix(p[..., OFF_FN:OFF_HY])
    y_c = hyena_branch(p[..., OFF_HY:OFF_GATE], hy_conv, hy_kfreq, hy_bias)
    gates = jax.nn.sigmoid(p[..., OFF_GATE:].astype(jnp.float32)).reshape(b, n, N_BRANCHES, D_MODEL)
    merged = (gates[:, :, 0] * (y_a.astype(dt) @ w_branch_a)
              + gates[:, :, 1] * (y_b.astype(dt) @ w_branch_b)
              + gates[:, :, 2] * (y_c.astype(dt) @ w_branch_c))
    return merged.astype(dt) @ w_out


def sq_relu_mlp(h, w1, w2):
    a = jax.nn.relu(h @ w1)
    return (a * a) @ w2


def setup_inputs(seed: int = 0) -> dict:
    key = jax.random.key(seed)
    ks = jax.random.split(key, 32)
    f32 = jnp.float32

    def nrm(i, shape, scale):
        return scale * jax.random.normal(ks[i], shape, f32)

    x = nrm(0, (BATCH, SEQ, D_MODEL), 1.0)
    c = nrm(1, (BATCH, D_MODEL), 1.0)
    ctx = nrm(2, (BATCH, CTX_LEN, D_MODEL), 1.0)
    c_ctx = nrm(3, (D_MODEL,), 1.0)
    w_mod = nrm(4, (DEPTH, D_MODEL, N_MOD * D_MODEL), D_MODEL ** -0.5)
    b_mod = nrm(5, (DEPTH, N_MOD * D_MODEL), 0.01)
    norm1 = 1.0 + nrm(6, (DEPTH, D_MODEL), 0.01)
    norm2 = 1.0 + nrm(7, (DEPTH, D_MODEL), 0.01)
    w_in = nrm(8, (DEPTH, D_MODEL, IN_WIDTH), D_MODEL ** -0.5)
    dn_conv = nrm(9, (DEPTH, SHORT_CONV, SHORT_CONV, 3 * DN_WIDTH), 1.0 / SHORT_CONV)
    dn_a_log = jnp.log(jax.random.uniform(ks[10], (DEPTH, 2, DN_HEADS), f32, 1.0, 16.0))
    dt0 = jnp.exp(jax.random.uniform(ks[11], (DEPTH, 2, DN_HEADS), f32, math.log(1e-3), math.log(1e-1)))
    dn_dt_bias = dt0 + jnp.log(-jnp.expm1(-dt0))
    dn_out_norm = 1.0 + nrm(12, (DEPTH, DN_HEAD_DIM), 0.01)
    hy_conv = nrm(13, (DEPTH, SHORT_CONV, (HY_ORDER + 1) * HY_WIDTH), SHORT_CONV ** -0.5)
    hy_w1 = nrm(14, (DEPTH, HY_EMB_DIM, HY_FILTER_HIDDEN), HY_EMB_DIM ** -0.5)
    hy_b1 = nrm(15, (DEPTH, HY_FILTER_HIDDEN), 0.1)
    hy_freq1 = 1.0 + nrm(16, (DEPTH, HY_FILTER_HIDDEN), 0.01)
    hy_w2 = nrm(17, (DEPTH, HY_FILTER_HIDDEN, HY_FILTER_HIDDEN), HY_FILTER_HIDDEN ** -0.5)
    hy_b2 = nrm(18, (DEPTH, HY_FILTER_HIDDEN), 0.1)
    hy_freq2 = 1.0 + nrm(19, (DEPTH, HY_FILTER_HIDDEN), 0.01)
    hy_w3 = nrm(20, (DEPTH, HY_FILTER_HIDDEN, 2 * HY_ORDER * HY_WIDTH), HY_FILTER_HIDDEN ** -0.5)
    hy_bias = nrm(21, (DEPTH, HY_ORDER, HY_WIDTH), 1.0)
    w_branch_a = nrm(22, (DEPTH, DN_WIDTH, D_MODEL), DN_WIDTH ** -0.5)
    w_branch_b = nrm(23, (DEPTH, FN_WIDTH, D_MODEL), FN_WIDTH ** -0.5)
    w_branch_c = nrm(24, (DEPTH, HY_WIDTH, D_MODEL), HY_WIDTH ** -0.5)
    w_out = nrm(25, (DEPTH, D_MODEL, D_MODEL), D_MODEL ** -0.5)
    w_ff1 = nrm(26, (DEPTH, D_MODEL, D_FF), D_MODEL ** -0.5)
    w_ff2 = nrm(27, (DEPTH, D_FF, D_MODEL), D_FF ** -0.5)
    final_norm = 1.0 + nrm(28, (D_MODEL,), 0.01)
    return {'x': x, 'c': c, 'ctx': ctx, 'c_ctx': c_ctx, 'w_mod': w_mod, 'b_mod': b_mod,
            'norm1': norm1, 'norm2': norm2, 'w_in': w_in, 'dn_conv': dn_conv, 'dn_a_log': dn_a_log,
            'dn_dt_bias': dn_dt_bias, 'dn_out_norm': dn_out_norm, 'hy_conv': hy_conv, 'hy_w1': hy_w1,
            'hy_b1': hy_b1, 'hy_freq1': hy_freq1, 'hy_w2': hy_w2, 'hy_b2': hy_b2, 'hy_freq2': hy_freq2,
            'hy_w3': hy_w3, 'hy_bias': hy_bias, 'w_branch_a': w_branch_a, 'w_branch_b': w_branch_b,
            'w_branch_c': w_branch_c, 'w_out': w_out, 'w_ff1': w_ff1, 'w_ff2': w_ff2,
            'final_norm': final_norm}


def reference(x, c, ctx, c_ctx, w_mod, b_mod, norm1, norm2, w_in, dn_conv, dn_a_log, dn_dt_bias,
              dn_out_norm, hy_conv, hy_w1, hy_b1, hy_freq1, hy_w2, hy_b2, hy_freq2, hy_w3, hy_bias,
              w_branch_a, w_branch_b, w_branch_c, w_out, w_ff1, w_ff2, final_norm):
    b, n_lat, _ = x.shape
    rows = n_lat // GRID_W
    n_ctx = ctx.shape[1]
    silu_c = jax.nn.silu(c)
    silu_cc = jax.nn.silu(c_ctx)
    s_zero = jnp.zeros((b, DN_HEADS, DN_HEAD_DIM, DN_HEAD_DIM), jnp.float32)
    h, hc = x, ctx
    for l in range(DEPTH):
        last = l == DEPTH - 1
        mx = jnp.split((silu_c @ w_mod[l] + b_mod[l])[:, None, :], N_MOD, axis=-1)
        mc = jnp.split((silu_cc @ w_mod[l] + b_mod[l])[None, None, :], N_MOD, axis=-1)

        p_c = modulate(hc, norm1[l], mc[0], mc[1]) @ w_in[l]
        p_x = modulate(h, norm1[l], mx[0], mx[1]) @ w_in[l]

        dn_c = delta_prepare(p_c, dn_conv[l], 1, n_ctx, dn_a_log[l], dn_dt_bias[l])
        o_c, s_fwd, s_bwd = bidir_delta(*dn_c, s_zero, s_zero)
        dn_x = delta_prepare(p_x, dn_conv[l], rows, GRID_W, dn_a_log[l], dn_dt_bias[l])
        o_x, _, _ = bidir_delta(*dn_x, s_fwd, s_bwd)

        kf_x = hyena_kernel_rfft(n_lat, hy_w1[l], hy_b1[l], hy_freq1[l], hy_w2[l], hy_b2[l], hy_freq2[l], hy_w3[l])
        y_x = branch_merge(p_x, o_x, kf_x, dn_out_norm[l], hy_conv[l], hy_bias[l],
                           w_branch_a[l], w_branch_b[l], w_branch_c[l], w_out[l])
        h = h + mx[2] * y_x
        h = h + mx[5] * sq_relu_mlp(modulate(h, norm2[l], mx[3], mx[4]), w_ff1[l], w_ff2[l])

        if not last:
            kf_c = hyena_kernel_rfft(n_ctx, hy_w1[l], hy_b1[l], hy_freq1[l], hy_w2[l], hy_b2[l], hy_freq2[l], hy_w3[l])
            y_c = branch_merge(p_c, o_c, kf_c, dn_out_norm[l], hy_conv[l], hy_bias[l],
                               w_branch_a[l], w_branch_b[l], w_branch_c[l], w_out[l])
            hc = hc + mc[2] * y_c
            hc = hc + mc[5] * sq_relu_mlp(modulate(hc, norm2[l], mc[3], mc[4]), w_ff1[l], w_ff2[l])

    return rms_norm(h, final_norm).astype(x.dtype)
```

```python
import math
import jax, jax.numpy as jnp
from jax import lax
import numpy as np
from jax.experimental import pallas as pl
from jax.experimental.pallas import tpu as pltpu

D_MODEL = 1024
BATCH = 2
SEQ = 8192
DEPTH = 2

GRID_W = 64
CTX_LEN = 256
NORM_EPS = 1e-6
N_MOD = 6

DN_HEADS = 4
DN_HEAD_DIM = 128
DN_WIDTH = DN_HEADS * DN_HEAD_DIM
DN_CHUNK = 64
SHORT_CONV = 3

FN_GROUPS = 4
FN_GROUP_DIM = 64
FN_WIDTH = FN_GROUPS * FN_GROUP_DIM

HY_WIDTH = 256
HY_ORDER = 2
HY_EMB_DIM = 33
HY_BANDS = (HY_EMB_DIM - 1) // 2
HY_FILTER_HIDDEN = 64
HY_FAST_DECAY_PCT = 0.3
HY_SLOW_DECAY_PCT = 1.5
HY_DECAY_TARGET = 1e-2

N_BRANCHES = 3
D_FF = 4 * D_MODEL

OFF_Q = 0
OFF_K = OFF_Q + DN_WIDTH
OFF_V = OFF_K + DN_WIDTH
OFF_Z = OFF_V + DN_WIDTH
OFF_BETA = OFF_Z + DN_WIDTH
OFF_A = OFF_BETA + 2 * DN_HEADS
OFF_FN = OFF_A + 2 * DN_HEADS
OFF_HY = OFF_FN + FN_WIDTH
OFF_GATE = OFF_HY + (HY_ORDER + 1) * HY_WIDTH
IN_WIDTH = OFF_GATE + N_BRANCHES * D_MODEL


def rms_norm(x, gain):
    xf = x.astype(jnp.float32)
    return xf * lax.rsqrt(jnp.mean(xf * xf, axis=-1, keepdims=True) + NORM_EPS) * gain.astype(jnp.float32)


def modulate(h, gain, shift, scale):
    y = rms_norm(h, gain) * (1.0 + scale.astype(jnp.float32)) + shift.astype(jnp.float32)
    return y.astype(h.dtype)


def l2_normalize(t):
    return t * lax.rsqrt(jnp.sum(t * t, axis=-1, keepdims=True) + NORM_EPS)


def grid_dwconv(x, w, rows, cols):
    b, n, ch = x.shape
    y = lax.conv_general_dilated(x.reshape(b, rows, cols, ch), w[:, :, None, :].astype(x.dtype), (1, 1), 'SAME',
                                 dimension_numbers=('NHWC', 'HWIO', 'NHWC'), feature_group_count=ch)
    return y.reshape(b, n, ch)


def seq_dwconv(x, w):
    ch = x.shape[-1]
    return lax.conv_general_dilated(x, w[:, None, :].astype(x.dtype), (1,), 'SAME',
                                    dimension_numbers=('NWC', 'WIO', 'NWC'), feature_group_count=ch)


def delta_prepare(p, conv_w, rows, cols, a_log, dt_bias):
    b, n, _ = p.shape
    f32 = jnp.float32
    qkv = jax.nn.silu(grid_dwconv(p[..., OFF_Q:OFF_Z], conv_w, rows, cols).astype(f32))
    qkv = qkv.reshape(b, n, 3, DN_HEADS, DN_HEAD_DIM)
    q = l2_normalize(qkv[:, :, 0]) * DN_HEAD_DIM ** -0.5
    k = l2_normalize(qkv[:, :, 1])
    v = qkv[:, :, 2]
    beta = jax.nn.sigmoid(p[..., OFF_BETA:OFF_A].astype(f32)).reshape(b, n, 2, DN_HEADS)
    a = p[..., OFF_A:OFF_FN].astype(f32).reshape(b, n, 2, DN_HEADS)
    g = -jnp.exp(a_log.astype(f32)) * jax.nn.softplus(a + dt_bias.astype(f32))
    return q, k, v, beta, g


def gated_delta_rule(q, k, v, beta, g, s0):
    b, n_tok, h, dk = q.shape
    dv = v.shape[-1]
    n_chunk = n_tok // DN_CHUNK

    def chunks(t):
        t = t.reshape((b, n_chunk, DN_CHUNK, h) + t.shape[3:])
        return jnp.moveaxis(t, (1, 3), (0, 2))

    qc, kc, vc, bc, gc = chunks(q), chunks(k), chunks(v), chunks(beta), chunks(g)
    G = jnp.cumsum(gc, axis=-1)
    idx = jnp.arange(DN_CHUNK)
    incl = idx[:, None] >= idx[None, :]
    strict = idx[:, None] > idx[None, :]
    decay = jnp.exp(jnp.where(incl, G[..., :, None] - G[..., None, :], -jnp.inf))
    kb = kc * bc[..., None]
    lhs = jnp.eye(DN_CHUNK, dtype=jnp.float32) + jnp.where(
        strict, jnp.einsum('nbhik,nbhjk->nbhij', kb, kc) * decay, 0.0)
    u_in = lax.linalg.triangular_solve(lhs, vc * bc[..., None], left_side=True, lower=True, unit_diagonal=True)
    w_in = lax.linalg.triangular_solve(lhs, kb * jnp.exp(G)[..., None], left_side=True, lower=True,
                                       unit_diagonal=True)
    qk = jnp.einsum('nbhik,nbhjk->nbhij', qc, kc) * decay
    q_dec = qc * jnp.exp(G)[..., None]
    k_dec = kc * jnp.exp(G[..., -1:] - G)[..., None]
    g_last = jnp.exp(G[..., -1])

    def step(state, inp):
        q_d, k_d, u0, w0, qk_c, gl = inp
        u = u0 - jnp.einsum('bhck,bhkv->bhcv', w0, state)
        o = jnp.einsum('bhck,bhkv->bhcv', q_d, state) + jnp.einsum('bhij,bhjv->bhiv', qk_c, u)
        state = state * gl[..., None, None] + jnp.einsum('bhck,bhcv->bhkv', k_d, u)
        return state, o

    s_final, o = lax.scan(step, s0, (q_dec, k_dec, u_in, w_in, qk, g_last))
    o = jnp.moveaxis(o, (0, 2), (1, 3)).reshape(b, n_tok, h, dv)
    return o, s_final


def bidir_delta(q, k, v, beta, g, s0_fwd, s0_bwd):
    o_f, s_f = gated_delta_rule(q, k, v, beta[:, :, 0], g[:, :, 0], s0_fwd)
    flip = lambda t: jnp.flip(t, axis=1)
    o_b, s_b = gated_delta_rule(flip(q), flip(k), flip(v), flip(beta[:, :, 1]), flip(g[:, :, 1]), s0_bwd)
    return o_f + flip(o_b), s_f, s_b


def fourier_mix(p):
    b, n, _ = p.shape
    pg = p.astype(jnp.float32).reshape(b, n, FN_GROUPS, FN_GROUP_DIM)
    return jnp.fft.fft2(pg, axes=(1, 3), norm='ortho').real.reshape(b, n, FN_WIDTH)


def hyena_kernel_rfft(n, w1, b1, freq1, w2, b2, freq2, w3):
    f32 = jnp.float32
    pos = jnp.arange(n, dtype=f32)
    t = pos / max(n - 1, 1)
    bands = jnp.linspace(1e-4, HY_BANDS - 1, HY_BANDS, dtype=f32)
    ang = (2.0 * math.pi / n) * pos[:, None] * bands[None, :]
    feats = jnp.concatenate([t[:, None], jnp.cos(ang), -jnp.sin(ang)], axis=-1)
    hid = jnp.sin(freq1.astype(f32) * (feats @ w1.astype(f32) + b1.astype(f32)))
    hid = jnp.sin(freq2.astype(f32) * (hid @ w2.astype(f32) + b2.astype(f32)))
    filt = (hid @ w3.astype(f32)).reshape(n, 2, HY_ORDER, HY_WIDTH)
    min_decay = math.log(HY_DECAY_TARGET) / HY_SLOW_DECAY_PCT
    max_decay = math.log(HY_DECAY_TARGET) / HY_FAST_DECAY_PCT
    deltas = jnp.abs(jnp.linspace(min_decay, max_decay, HY_ORDER * HY_WIDTH, dtype=f32)).reshape(HY_ORDER, HY_WIDTH)
    filt = filt * jnp.exp(-t[:, None, None, None] * deltas[None, None])
    fwd, bwd = filt[:, 0], filt[:, 1]
    kern = jnp.concatenate([fwd, jnp.zeros((1, HY_ORDER, HY_WIDTH), f32), bwd[:0:-1]], axis=0)
    kern = kern / jnp.sum(jnp.abs(kern), axis=0, keepdims=True)
    return jnp.fft.rfft(kern, axis=0)


def fft_conv(u, k_freq):
    n = u.shape[1]
    spec = jnp.fft.rfft(u, n=2 * n, axis=1) * k_freq[None]
    return jnp.fft.irfft(spec, n=2 * n, axis=1)[:, :n]


def hyena_branch(p, conv_w, k_freq, bias):
    parts = jnp.split(seq_dwconv(p, conv_w).astype(jnp.float32), HY_ORDER + 1, axis=-1)
    z = parts[HY_ORDER]
    for order in range(HY_ORDER):
        z = parts[order] * (fft_conv(z, k_freq[:, order]) + bias[order].astype(jnp.float32) * z)
    return z


def branch_merge(p, o_dn, hy_kfreq, dn_out_norm, hy_conv, hy_bias, w_branch_a, w_branch_b, w_branch_c, w_out):
    b, n, _ = p.shape
    dt = p.dtype
    z = p[..., OFF_Z:OFF_BETA].astype(jnp.float32).reshape(b, n, DN_HEADS, DN_HEAD_DIM)
    y_a = (rms_norm(o_dn, dn_out_norm) * jax.nn.silu(z)).reshape(b, n, DN_WIDTH)
    y_b = fourier_mix(p[..., OFF_FN:OFF_HY])
    y_c = hyena_branch(p[..., OFF_HY:OFF_GATE], hy_conv, hy_kfreq, hy_bias)
    gates = jax.nn.sigmoid(p[..., OFF_GATE:].astype(jnp.float32)).reshape(b, n, N_BRANCHES, D_MODEL)
    merged = (gates[:, :, 0] * (y_a.astype(dt) @ w_branch_a)
              + gates[:, :, 1] * (y_b.astype(dt) @ w_branch_b)
              + gates[:, :, 2] * (y_c.astype(dt) @ w_branch_c))
    return merged.astype(dt) @ w_out


def sq_relu_mlp(h, w1, w2):
    a = jax.nn.relu(h @ w1)
    return (a * a) @ w2


def _final_norm_kernel(h_ref, g_ref, o_ref):
    xf = h_ref[...]
    o_ref[...] = xf * lax.rsqrt(jnp.mean(xf * xf, axis=-1, keepdims=True) + NORM_EPS) * g_ref[...]


def final_rms_norm(h, gain):
    b, n, d = h.shape
    tm = 512
    h2 = h.reshape(b * n, d)
    out = pl.pallas_call(
        _final_norm_kernel,
        grid=(b * n // tm,),
        in_specs=[pl.BlockSpec((tm, d), lambda i: (i, 0)), pl.BlockSpec((1, d), lambda i: (0, 0))],
        out_specs=pl.BlockSpec((tm, d), lambda i: (i, 0)),
        out_shape=jax.ShapeDtypeStruct((b * n, d), h.dtype),
        name="final_norm",
    )(h2, gain.reshape(1, d))
    return out.reshape(b, n, d)


def kernel(x, c, ctx, c_ctx, w_mod, b_mod, norm1, norm2, w_in, dn_conv, dn_a_log, dn_dt_bias,
           dn_out_norm, hy_conv, hy_w1, hy_b1, hy_freq1, hy_w2, hy_b2, hy_freq2, hy_w3, hy_bias,
           w_branch_a, w_branch_b, w_branch_c, w_out, w_ff1, w_ff2, final_norm):
    b, n_lat, _ = x.shape
    rows = n_lat // GRID_W
    n_ctx = ctx.shape[1]
    silu_c = jax.nn.silu(c)
    silu_cc = jax.nn.silu(c_ctx)
    s_zero = jnp.zeros((b, DN_HEADS, DN_HEAD_DIM, DN_HEAD_DIM), jnp.float32)
    h, hc = x, ctx
    for l in range(DEPTH):
        last = l == DEPTH - 1
        mx = jnp.split((silu_c @ w_mod[l] + b_mod[l])[:, None, :], N_MOD, axis=-1)
        mc = jnp.split((silu_cc @ w_mod[l] + b_mod[l])[None, None, :], N_MOD, axis=-1)

        p_c = modulate(hc, norm1[l], mc[0], mc[1]) @ w_in[l]
        p_x = modulate(h, norm1[l], mx[0], mx[1]) @ w_in[l]

        dn_c = delta_prepare(p_c, dn_conv[l], 1, n_ctx, dn_a_log[l], dn_dt_bias[l])
        o_c, s_fwd, s_bwd = bidir_delta(*dn_c, s_zero, s_zero)
        dn_x = delta_prepare(p_x, dn_conv[l], rows, GRID_W, dn_a_log[l], dn_dt_bias[l])
        o_x, _, _ = bidir_delta(*dn_x, s_fwd, s_bwd)

        kf_x = hyena_kernel_rfft(n_lat, hy_w1[l], hy_b1[l], hy_freq1[l], hy_w2[l], hy_b2[l], hy_freq2[l], hy_w3[l])
        y_x = branch_merge(p_x, o_x, kf_x, dn_out_norm[l], hy_conv[l], hy_bias[l],
                           w_branch_a[l], w_branch_b[l], w_branch_c[l], w_out[l])
        h = h + mx[2] * y_x
        h = h + mx[5] * sq_relu_mlp(modulate(h, norm2[l], mx[3], mx[4]), w_ff1[l], w_ff2[l])

        if not last:
            kf_c = hyena_kernel_rfft(n_ctx, hy_w1[l], hy_b1[l], hy_freq1[l], hy_w2[l], hy_b2[l], hy_freq2[l], hy_w3[l])
            y_c = branch_merge(p_c, o_c, kf_c, dn_out_norm[l], hy_conv[l], hy_bias[l],
                               w_branch_a[l], w_branch_b[l], w_branch_c[l], w_out[l])
            hc = hc + mc[2] * y_c
            hc = hc + mc[5] * sq_relu_mlp(modulate(hc, norm2[l], mc[3], mc[4]), w_ff1[l], w_ff2[l])

    return final_rms_norm(h, final_norm)
```

```python
import functools
import math

import jax
import jax.numpy as jnp
import numpy as np
from jax import lax
from jax.experimental import pallas as pl
from jax.experimental.pallas import tpu as pltpu

D_MODEL = 1024
DEPTH = 2
GRID_W = 64
NORM_EPS = 1e-6
N_MOD = 6

DN_HEADS = 4
DN_HEAD_DIM = 128
DN_WIDTH = DN_HEADS * DN_HEAD_DIM
DN_CHUNK = 64
SHORT_CONV = 3

FN_GROUPS = 4
FN_GROUP_DIM = 64
FN_WIDTH = FN_GROUPS * FN_GROUP_DIM

HY_WIDTH = 256
HY_ORDER = 2
HY_EMB_DIM = 33
HY_BANDS = (HY_EMB_DIM - 1) // 2
HY_FILTER_HIDDEN = 64
HY_FAST_DECAY_PCT = 0.3
HY_SLOW_DECAY_PCT = 1.5
HY_DECAY_TARGET = 1e-2

N_BRANCHES = 3
D_FF = 4 * D_MODEL

OFF_Q = 0
OFF_Z = 3 * DN_WIDTH
OFF_BETA = OFF_Z + DN_WIDTH
OFF_A = OFF_BETA + 2 * DN_HEADS
OFF_FN = OFF_A + 2 * DN_HEADS
OFF_HY = OFF_FN + FN_WIDTH
OFF_GATE = OFF_HY + (HY_ORDER + 1) * HY_WIDTH
IN_WIDTH = OFF_GATE + N_BRANCHES * D_MODEL

P_GATE = 0
P_QKV = P_GATE + N_BRANCHES * D_MODEL
P_HY = P_QKV + 3 * DN_WIDTH
P_FN = P_HY + (HY_ORDER + 1) * HY_WIDTH
P_Z = P_FN + FN_WIDTH
P_WIDTH = P_Z + DN_WIDTH
SMALL_WIDTH = 128

F32 = jnp.float32
BF16 = jnp.bfloat16
VMEM_LIMIT = 56 * 1024 * 1024


def _cparams(*sem):
    return pltpu.CompilerParams(dimension_semantics=sem, vmem_limit_bytes=VMEM_LIMIT)


def _bdot(a, b):
    return jnp.dot(a.astype(BF16), b.astype(BF16), preferred_element_type=F32)


def _sigmoid(x):
    return 1.0 / (1.0 + jnp.exp(-x))


def _modnorm(xf, gs, sh):
    r = lax.rsqrt(jnp.mean(xf * xf, axis=-1, keepdims=True) + NORM_EPS)
    return xf * r * gs + sh


def _mod_kernel(c_ref, w_ref, b_ref, o_ref):
    c = c_ref[...]
    o_ref[...] = _bdot(c * _sigmoid(c), w_ref[...]) + b_ref[...]


def mod_vectors(c_rows, w_mod, b_mod):
    tn = 1536
    n = N_MOD * D_MODEL
    return pl.pallas_call(
        _mod_kernel,
        grid=(DEPTH, n // tn),
        in_specs=[pl.BlockSpec((8, D_MODEL), lambda l, j: (0, 0)),
                  pl.BlockSpec((None, D_MODEL, tn), lambda l, j: (l, 0, j)),
                  pl.BlockSpec((None, 1, tn), lambda l, j: (l, 0, j))],
        out_specs=pl.BlockSpec((None, 8, tn), lambda l, j: (l, 0, j)),
        out_shape=jax.ShapeDtypeStruct((DEPTH, 8, n), F32),
        compiler_params=_cparams("parallel", "parallel"),
        name="mod_vectors",
    )(c_rows, w_mod, b_mod.reshape(DEPTH, 1, n))


def _in_proj_kernel(x_ref, gs_ref, sh_ref, w_ref, ws_ref, p_ref, small_ref, xn_ref):
    @pl.when(pl.program_id(1) == 0)
    def _():
        xn = _modnorm(x_ref[...], gs_ref[0], sh_ref[0]).astype(BF16)
        xn_ref[...] = xn
        small_ref[...] = jnp.dot(xn, ws_ref[...], preferred_element_type=F32)

    p_ref[...] = jnp.dot(xn_ref[...], w_ref[...], preferred_element_type=F32)


def in_proj(x2d, gs, sh, w_main, w_small, rows_per_mod, tm):
    m = x2d.shape[0]
    tn = 1024
    tiles_per_mod = rows_per_mod // tm
    mod_spec = pl.BlockSpec((1, 1, D_MODEL), lambda i, j: (i // tiles_per_mod, 0, 0))
    return pl.pallas_call(
        _in_proj_kernel,
        grid=(m // tm, P_WIDTH // tn),
        in_specs=[pl.BlockSpec((tm, D_MODEL), lambda i, j: (i, 0)), mod_spec, mod_spec,
                  pl.BlockSpec((D_MODEL, tn), lambda i, j: (0, j)),
                  pl.BlockSpec((D_MODEL, SMALL_WIDTH), lambda i, j: (0, 0))],
        out_specs=[pl.BlockSpec((tm, tn), lambda i, j: (i, j)),
                   pl.BlockSpec((tm, SMALL_WIDTH), lambda i, j: (i, 0))],
        out_shape=[jax.ShapeDtypeStruct((m, P_WIDTH), F32), jax.ShapeDtypeStruct((m, SMALL_WIDTH), F32)],
        scratch_shapes=[pltpu.VMEM((tm, D_MODEL), BF16)],
        compiler_params=_cparams("parallel", "arbitrary"),
        name="in_proj",
    )(x2d, gs, sh, w_main, w_small)


def _merge_kernel(of_ref, ob_ref, z_ref, yb_ref, yc_ref, g_ref, h_ref, gate_ref, nrm_ref,
                  wa_ref, wb_ref, wc_ref, wo_ref, out_ref):
    o = of_ref[...] + ob_ref[...]
    z = z_ref[...]
    heads = []
    for hd in range(DN_HEADS):
        sl = slice(hd * DN_HEAD_DIM, (hd + 1) * DN_HEAD_DIM)
        oh, zh = o[:, sl], z[:, sl]
        r = lax.rsqrt(jnp.mean(oh * oh, axis=-1, keepdims=True) + NORM_EPS)
        heads.append(oh * r * nrm_ref[...] * (zh * _sigmoid(zh)))
    ya = jnp.concatenate(heads, axis=-1)
    merged = (_sigmoid(g_ref[:, 0:D_MODEL]) * _bdot(ya, wa_ref[...])
              + _sigmoid(g_ref[:, D_MODEL:2 * D_MODEL]) * _bdot(yb_ref[...], wb_ref[...])
              + _sigmoid(g_ref[:, 2 * D_MODEL:3 * D_MODEL]) * _bdot(yc_ref[...], wc_ref[...]))
    out_ref[...] = h_ref[...] + gate_ref[0] * _bdot(merged, wo_ref[...])


def merge(o_f, o_b, p, y_b, y_c, h2d, gate, dn_out_norm, wa, wb, wc, wo, rows_per_mod, tm):
    m = h2d.shape[0]
    tiles_per_mod = rows_per_mod // tm
    row = lambda w: pl.BlockSpec((tm, w), lambda i: (i, 0))
    full = lambda a: pl.BlockSpec(a.shape, lambda i: (0,) * a.ndim)
    nrm = dn_out_norm.reshape(1, DN_HEAD_DIM)
    return pl.pallas_call(
        _merge_kernel,
        grid=(m // tm,),
        in_specs=[row(DN_WIDTH), row(DN_WIDTH),
                  pl.BlockSpec((tm, DN_WIDTH), lambda i: (i, P_Z // DN_WIDTH)),
                  row(FN_WIDTH), row(HY_WIDTH),
                  pl.BlockSpec((tm, N_BRANCHES * D_MODEL), lambda i: (i, 0)),
                  row(D_MODEL),
                  pl.BlockSpec((1, 1, D_MODEL), lambda i: (i // tiles_per_mod, 0, 0)),
                  full(nrm), full(wa), full(wb), full(wc), full(wo)],
        out_specs=row(D_MODEL),
        out_shape=jax.ShapeDtypeStruct((m, D_MODEL), F32),
        compiler_params=_cparams("parallel"),
        name="merge",
    )(o_f, o_b, p, y_b, y_c, p, h2d, gate, nrm, wa, wb, wc, wo)


def _mlp_kernel(h_ref, gs_ref, sh_ref, gate_ref, w1_ref, w2_ref, out_ref, xn_ref, acc_ref):
    j = pl.program_id(1)

    @pl.when(j == 0)
    def _():
        xn_ref[...] = _modnorm(h_ref[...], gs_ref[0], sh_ref[0]).astype(BF16)
        acc_ref[...] = jnp.zeros_like(acc_ref)

    a = jnp.maximum(jnp.dot(xn_ref[...], w1_ref[...], preferred_element_type=F32), 0.0)
    acc_ref[...] += jnp.dot((a * a).astype(BF16), w2_ref[...], preferred_element_type=F32)

    @pl.when(j == pl.num_programs(1) - 1)
    def _():
        out_ref[...] = h_ref[...] + gate_ref[0] * acc_ref[...]


def mlp(h2d, gs, sh, gate, w1, w2, rows_per_mod, tm):
    m = h2d.shape[0]
    tf = 512
    tiles_per_mod = rows_per_mod // tm
    mod_spec = pl.BlockSpec((1, 1, D_MODEL), lambda i, j: (i // tiles_per_mod, 0, 0))
    return pl.pallas_call(
        _mlp_kernel,
        grid=(m // tm, D_FF // tf),
        in_specs=[pl.BlockSpec((tm, D_MODEL), lambda i, j: (i, 0)), mod_spec, mod_spec, mod_spec,
                  pl.BlockSpec((D_MODEL, tf), lambda i, j: (0, j)),
                  pl.BlockSpec((tf, D_MODEL), lambda i, j: (j, 0))],
        out_specs=pl.BlockSpec((tm, D_MODEL), lambda i, j: (i, 0)),
        out_shape=jax.ShapeDtypeStruct((m, D_MODEL), F32),
        scratch_shapes=[pltpu.VMEM((tm, D_MODEL), BF16), pltpu.VMEM((tm, D_MODEL), F32)],
        compiler_params=_cparams("parallel", "arbitrary"),
        name="mlp",
    )(h2d, gs, sh, gate, w1, w2)


def _final_norm_kernel(h_ref, g_ref, o_ref):
    xf = h_ref[...]
    o_ref[...] = xf * lax.rsqrt(jnp.mean(xf * xf, axis=-1, keepdims=True) + NORM_EPS) * g_ref[...]


def final_rms_norm(h2d, gain):
    m, d = h2d.shape
    tm = 512
    return pl.pallas_call(
        _final_norm_kernel,
        grid=(m // tm,),
        in_specs=[pl.BlockSpec((tm, d), lambda i: (i, 0)), pl.BlockSpec((1, d), lambda i: (0, 0))],
        out_specs=pl.BlockSpec((tm, d), lambda i: (i, 0)),
        out_shape=jax.ShapeDtypeStruct((m, d), F32),
        compiler_params=_cparams("parallel"),
        name="final_norm",
    )(h2d, gain.reshape(1, d))


def _dnconv_kernel(prev_ref, cur_ref, next_ref, w_ref, o_ref, *, cols, n_tiles):
    t, j = pl.program_id(1), pl.program_id(2)
    tt = cur_ref.shape[0]
    prev = jnp.where(t == 0, 0.0, prev_ref[...])
    nxt = jnp.where(t == n_tiles - 1, 0.0, next_ref[...])
    ext = jnp.concatenate([prev, cur_ref[...], nxt], axis=0)
    n_ext = tt + 2 * cols
    col = lax.broadcasted_iota(jnp.int32, (n_ext, 1), 0) % cols
    left = jnp.where(col == 0, 0.0, pltpu.roll(ext, 1, axis=0))
    right = jnp.where(col == cols - 1, 0.0, pltpu.roll(ext, n_ext - 1, axis=0))
    acc = jnp.zeros((tt, DN_HEAD_DIM), F32)
    for dr in range(SHORT_CONV):
        base = dr * cols
        acc = (acc + w_ref[3 * dr:3 * dr + 1, :] * left[base:base + tt]
               + w_ref[3 * dr + 1:3 * dr + 2, :] * ext[base:base + tt]
               + w_ref[3 * dr + 2:3 * dr + 3, :] * right[base:base + tt])
    y = acc * _sigmoid(acc)
    nrm = lax.rsqrt(jnp.sum(y * y, axis=-1, keepdims=True) + NORM_EPS)
    scale = jnp.where(j < DN_HEADS, nrm * DN_HEAD_DIM ** -0.5, jnp.where(j < 2 * DN_HEADS, nrm, 1.0))
    o_ref[...] = y * scale


def dn_conv_prep(p, conv_w, b, n, rows, cols):
    tr = min(rows, 16)
    tt = tr * cols
    n_tiles = rows // tr
    nblk = b * n // cols
    c0 = P_QKV // DN_HEAD_DIM
    return pl.pallas_call(
        functools.partial(_dnconv_kernel, cols=cols, n_tiles=n_tiles),
        grid=(b, n_tiles, 3 * DN_HEADS),
        in_specs=[pl.BlockSpec((cols, DN_HEAD_DIM),
                               lambda bi, t, j: (jnp.maximum(bi * rows + t * tr - 1, 0), c0 + j)),
                  pl.BlockSpec((tt, DN_HEAD_DIM), lambda bi, t, j: (bi * n_tiles + t, c0 + j)),
                  pl.BlockSpec((cols, DN_HEAD_DIM),
                               lambda bi, t, j: (jnp.minimum(bi * rows + (t + 1) * tr, nblk - 1), c0 + j)),
                  pl.BlockSpec((SHORT_CONV * SHORT_CONV, DN_HEAD_DIM), lambda bi, t, j: (0, j))],
        out_specs=pl.BlockSpec((tt, DN_HEAD_DIM), lambda bi, t, j: (bi * n_tiles + t, j)),
        out_shape=jax.ShapeDtypeStruct((b * n, 3 * DN_WIDTH), F32),
        compiler_params=_cparams("parallel", "parallel", "parallel"),
        name="dn_conv_prep",
    )(p, p, p, conv_w.reshape(SHORT_CONV * SHORT_CONV, 3 * DN_WIDTH))


PREP_CHUNKS = 2


def _softplus(x):
    return jnp.maximum(x, 0.0) + jnp.log(1.0 + jnp.exp(-jnp.abs(x)))


def _delta_prep_kernel(qkv_ref, sm_ref, smt_ref, prm_ref, prmt_ref, u0_ref, lhs1_ref, lhs2_ref, gl_ref):
    cc = DN_CHUNK
    nh = DN_HEADS
    sm = sm_ref[...]
    beta_all = _sigmoid(sm)
    g_all = -jnp.exp(prm_ref[0:1, :]) * _softplus(sm + prm_ref[1:2, :])
    gt_all = -jnp.exp(prmt_ref[:, 0:1]) * _softplus(smt_ref[...] + prmt_ref[:, 1:2])
    lane = lax.broadcasted_iota(jnp.int32, gt_all.shape, 1) % cc
    gt_f, gt_b = gt_all, gt_all
    s = 1
    while s < cc:
        gt_f = gt_f + jnp.where(lane >= s, pltpu.roll(gt_f, s, axis=1), 0.0)
        gt_b = gt_b + jnp.where(lane < cc - s, pltpu.roll(gt_b, gt_all.shape[1] - s, axis=1), 0.0)
        s *= 2
    ri = lax.broadcasted_iota(jnp.int32, (cc, cc), 0)
    ci_ = lax.broadcasted_iota(jnp.int32, (cc, cc), 1)
    sub = lax.broadcasted_iota(jnp.int32, (cc, 1), 0)
    blk = lambda s: (ri // s) == (ci_ // s)
    leaf = 8

    heads, chains = [], []
    for ci in range(PREP_CHUNKS):
        rows = slice(ci * cc, (ci + 1) * cc)
        gc_f = g_all[rows]
        gc_b = gc_f
        s = 1
        while s < cc:
            gc_f = gc_f + jnp.where(sub >= s, pltpu.roll(gc_f, s, axis=0), 0.0)
            gc_b = gc_b + jnp.where(sub < cc - s, pltpu.roll(gc_b, cc - s, axis=0), 0.0)
            s *= 2
        for h in range(nh):
            q = qkv_ref[rows, h * 128:(h + 1) * 128]
            k = qkv_ref[rows, (nh + h) * 128:(nh + h + 1) * 128]
            v = qkv_ref[rows, (2 * nh + h) * 128:(2 * nh + h + 1) * 128]
            heads.append((ci, rows, h, q, k, v, gc_f, gc_b))
    qkks = [lax.dot_general(jnp.concatenate([q, k], axis=0).astype(BF16), k.astype(BF16),
                            (((1,), (1,)), ((), ())), preferred_element_type=F32)
            for (_, _, _, q, k, _, _, _) in heads]
    for (ci, rows, h, q, k, v, gc_f, gc_b), qkk in zip(heads, qkks):
        qk, kk = qkk[:cc], qkk[cc:]
        for d in range(2):
            cb_, ca_ = d * nh + h, 2 * nh + d * nh + h
            beta = beta_all[rows, cb_:cb_ + 1]
            g_col = (gc_f if d == 0 else gc_b)[:, ca_:ca_ + 1]
            g_row = (gt_f if d == 0 else gt_b)[ca_:ca_ + 1, rows]
            incl = (ri >= ci_) if d == 0 else (ri <= ci_)
            strict = (ri > ci_) if d == 0 else (ri < ci_)
            decay = jnp.exp(jnp.where(incl, g_col - g_row, -1e30))
            a = jnp.where(strict, beta * kk * decay, 0.0)
            eg = jnp.exp(g_col)
            g_last = g_col[cc - 1:cc] if d == 0 else g_col[0:1]
            chains.append(dict(ci=ci, rows=rows, h=h, d=d, a=a, qkd=qk * decay, g_last=g_last,
                               rhs=jnp.concatenate([v * beta, k * (beta * eg)], axis=1),
                               q_d=q * eg, k_d=k * jnp.exp(g_last - g_col)))

    pws = [jnp.where(blk(leaf), c["a"], 0.0) for c in chains]
    devs = [-pw for pw in pws]
    for _ in range(2):
        pws = [_bdot(pw, pw) for pw in pws]
        cross = [_bdot(dev, pw) for dev, pw in zip(devs, pws)]
        devs = [dev + pw + x for dev, pw, x in zip(devs, pws, cross)]
    s = leaf
    while s < cc:
        offs = [jnp.where(blk(2 * s) & jnp.logical_not(blk(s)), c["a"], 0.0) for c in chains]
        xs = [off + _bdot(dev, off) for dev, off in zip(devs, offs)]
        devs = [dev - x - _bdot(x, dev) for dev, x in zip(devs, xs)]
        s *= 2
    uws = [c["rhs"] + _bdot(dev, c["rhs"]) for dev, c in zip(devs, chains)]
    for c, uw in zip(chains, uws):
        d, h, ci = c["d"], c["h"], c["ci"]
        u0_ref[d, h, c["rows"], :] = uw[:, :128]
        lhs1_ref[d, h, ci] = jnp.concatenate([uw[:, 128:], c["q_d"]], axis=0).astype(BF16)
        lhs2_ref[d, h, ci] = jnp.concatenate([c["qkd"], c["k_d"].T], axis=0).astype(BF16)
        gl_ref[d, h, ci] = jnp.broadcast_to(jnp.exp(c["g_last"]), (8, 128))


def delta_prep(qkv, small, small_t, prm, prm_t, b, n):
    nc = n // DN_CHUNK
    cb = PREP_CHUNKS
    nblk = nc // cb
    tt = cb * DN_CHUNK
    nh = DN_HEADS
    return pl.pallas_call(
        _delta_prep_kernel,
        grid=(b, nblk),
        in_specs=[pl.BlockSpec((tt, 3 * DN_WIDTH), lambda bi, c: (bi * nblk + c, 0)),
                  pl.BlockSpec((tt, SMALL_WIDTH), lambda bi, c: (bi * nblk + c, 0)),
                  pl.BlockSpec((4 * nh, tt), lambda bi, c: (0, bi * nblk + c)),
                  pl.BlockSpec((2, SMALL_WIDTH), lambda bi, c: (0, 0)),
                  pl.BlockSpec((4 * nh, 2), lambda bi, c: (0, 0))],
        out_specs=[pl.BlockSpec((2, None, nh, tt, 128), lambda bi, c: (0, bi, 0, c, 0)),
                   pl.BlockSpec((2, None, nh, cb, 128, 128), lambda bi, c: (0, bi, 0, c, 0, 0)),
                   pl.BlockSpec((2, None, nh, cb, 192, 64), lambda bi, c: (0, bi, 0, c, 0, 0)),
                   pl.BlockSpec((2, None, nh, cb, 8, 128), lambda bi, c: (0, bi, 0, c, 0, 0))],
        out_shape=[jax.ShapeDtypeStruct((2, b, nh, n, 128), F32),
                   jax.ShapeDtypeStruct((2, b, nh, nc, 128, 128), BF16),
                   jax.ShapeDtypeStruct((2, b, nh, nc, 192, 64), BF16),
                   jax.ShapeDtypeStruct((2, b, nh, nc, 8, 128), F32)],
        compiler_params=_cparams("parallel", "parallel"),
        name="delta_prep",
    )(qkv, small, small_t, prm, prm_t)


def _delta_scan_kernel(u0f_ref, u0b_ref, l1f_ref, l1b_ref, l2f_ref, l2b_ref, glf_ref, glb_ref, s0_ref,
                       of_ref, ob_ref, sout_ref, st_ref, *, sc, nb):
    n = pl.program_id(0)
    cc = DN_CHUNK

    @pl.when(n == 0)
    def _():
        st_ref[...] = s0_ref[...]

    dirs = ((u0f_ref, l1f_ref, l2f_ref, glf_ref, of_ref), (u0b_ref, l1b_ref, l2b_ref, glb_ref, ob_ref))

    def body(i, carry):
        chains = []
        for d in range(2):
            ci = i if d == 0 else sc - 1 - i
            r0 = pl.multiple_of(ci * cc, cc)
            chains += [(d, bi, h, ci, r0) for bi in range(nb) for h in range(DN_HEADS)]
        sts = [st_ref[d, bi, h] for (d, bi, h, _, _) in chains]
        r1s = [jnp.dot(dirs[d][1][bi, h, ci], st.astype(BF16), preferred_element_type=F32)
               for (d, bi, h, ci, _), st in zip(chains, sts)]
        us = [dirs[d][0][bi, h, pl.ds(r0, cc), :] - r1[:cc] for (d, bi, h, _, r0), r1 in zip(chains, r1s)]
        r2s = [jnp.dot(dirs[d][2][bi, h, ci], u.astype(BF16), preferred_element_type=F32)
               for (d, bi, h, ci, _), u in zip(chains, us)]
        for (d, bi, h, ci, r0), st, r1, r2 in zip(chains, sts, r1s, r2s):
            dirs[d][4][bi, pl.ds(r0, cc), h * 128:(h + 1) * 128] = r1[cc:] + r2[:cc]
            st_ref[d, bi, h] = st * dirs[d][3][bi, h, ci, 0:1, :] + r2[cc:]
        return carry

    lax.fori_loop(0, sc, body, 0)

    @pl.when(n == pl.num_programs(0) - 1)
    def _():
        sout_ref[...] = st_ref[...]


def delta_scan(u0, lhs1, lhs2, gl, s0, b, n):
    nc = n // DN_CHUNK
    sc = 4
    nblk = nc // sc
    nh = DN_HEADS
    tt = sc * DN_CHUNK
    fwd = lambda i: i
    bwd = lambda i: nblk - 1 - i

    def specs(d, blk):
        return [pl.BlockSpec((None, b, nh, tt, 128), lambda i: (d, 0, 0, blk(i), 0)),
                pl.BlockSpec((None, b, nh, sc, 128, 128), lambda i: (d, 0, 0, blk(i), 0, 0)),
                pl.BlockSpec((None, b, nh, sc, 192, 64), lambda i: (d, 0, 0, blk(i), 0, 0)),
                pl.BlockSpec((None, b, nh, sc, 8, 128), lambda i: (d, 0, 0, blk(i), 0, 0))]

    sf, sb = specs(0, fwd), specs(1, bwd)
    in_specs = [sf[0], sb[0], sf[1], sb[1], sf[2], sb[2], sf[3], sb[3],
                pl.BlockSpec((2, b, nh, 128, 128), lambda i: (0, 0, 0, 0, 0))]
    return pl.pallas_call(
        functools.partial(_delta_scan_kernel, sc=sc, nb=b),
        grid=(nblk,),
        in_specs=in_specs,
        out_specs=[pl.BlockSpec((b, tt, DN_WIDTH), lambda i: (0, fwd(i), 0)),
                   pl.BlockSpec((b, tt, DN_WIDTH), lambda i: (0, bwd(i), 0)),
                   pl.BlockSpec((2, b, nh, 128, 128), lambda i: (0, 0, 0, 0, 0))],
        out_shape=[jax.ShapeDtypeStruct((b, n, DN_WIDTH), F32), jax.ShapeDtypeStruct((b, n, DN_WIDTH), F32),
                   jax.ShapeDtypeStruct((2, b, nh, 128, 128), F32)],
        scratch_shapes=[pltpu.VMEM((2, b, nh, 128, 128), F32)],
        compiler_params=_cparams("arbitrary"),
        name="delta_scan",
    )(u0, u0, lhs1, lhs1, lhs2, lhs2, gl, gl, s0)


def delta_branch(p, small, conv_w, a_log, dt_bias, s0, b, n, rows, cols):
    nh = DN_HEADS
    qkv = dn_conv_prep(p, conv_w, b, n, rows, cols)
    small_t = small[:, :4 * nh].T
    rate = jnp.concatenate([jnp.zeros((2 * nh,), F32), a_log.reshape(-1)])
    bias = jnp.concatenate([jnp.zeros((2 * nh,), F32), dt_bias.reshape(-1)])
    prm_t = jnp.stack([rate, bias], axis=1)
    prm = jnp.pad(prm_t.T, ((0, 0), (0, SMALL_WIDTH - 4 * nh)))
    u0, lhs1, lhs2, gl = delta_prep(qkv, small, small_t, prm, prm_t, b, n)
    o_f, o_b, s_out = delta_scan(u0, lhs1, lhs2, gl, s0, b, n)
    return o_f.reshape(b * n, DN_WIDTH), o_b.reshape(b * n, DN_WIDTH), s_out


def l2_normalize(t):
    return t * lax.rsqrt(jnp.sum(t * t, axis=-1, keepdims=True) + NORM_EPS)


def grid_dwconv(x, w, rows, cols):
    b, n, ch = x.shape
    y = lax.conv_general_dilated(x.reshape(b, rows, cols, ch), w[:, :, None, :].astype(x.dtype), (1, 1), 'SAME',
                                 dimension_numbers=('NHWC', 'HWIO', 'NHWC'), feature_group_count=ch)
    return y.reshape(b, n, ch)


def seq_dwconv(x, w):
    ch = x.shape[-1]
    return lax.conv_general_dilated(x, w[:, None, :].astype(x.dtype), (1,), 'SAME',
                                    dimension_numbers=('NWC', 'WIO', 'NWC'), feature_group_count=ch)


def delta_prepare(pqkv, small, conv_w, rows, cols, a_log, dt_bias):
    b, n, _ = pqkv.shape
    qkv = jax.nn.silu(grid_dwconv(pqkv, conv_w, rows, cols))
    qkv = qkv.reshape(b, n, 3, DN_HEADS, DN_HEAD_DIM)
    q = l2_normalize(qkv[:, :, 0]) * DN_HEAD_DIM ** -0.5
    k = l2_normalize(qkv[:, :, 1])
    v = qkv[:, :, 2]
    beta = jax.nn.sigmoid(small[..., 0:2 * DN_HEADS]).reshape(b, n, 2, DN_HEADS)
    a = small[..., 2 * DN_HEADS:4 * DN_HEADS].reshape(b, n, 2, DN_HEADS)
    g = -jnp.exp(a_log) * jax.nn.softplus(a + dt_bias)
    return q, k, v, beta, g


def gated_delta_rule(q, k, v, beta, g, s0):
    b, n_tok, h, dk = q.shape
    dv = v.shape[-1]
    n_chunk = n_tok // DN_CHUNK

    def chunks(t):
        t = t.reshape((b, n_chunk, DN_CHUNK, h) + t.shape[3:])
        return jnp.moveaxis(t, (1, 3), (0, 2))

    qc, kc, vc, bc, gc = chunks(q), chunks(k), chunks(v), chunks(beta), chunks(g)
    G = jnp.cumsum(gc, axis=-1)
    idx = jnp.arange(DN_CHUNK)
    incl = idx[:, None] >= idx[None, :]
    strict = idx[:, None] > idx[None, :]
    decay = jnp.exp(jnp.where(incl, G[..., :, None] - G[..., None, :], -jnp.inf))
    kb = kc * bc[..., None]
    lhs = jnp.eye(DN_CHUNK, dtype=jnp.float32) + jnp.where(
        strict, jnp.einsum('nbhik,nbhjk->nbhij', kb, kc) * decay, 0.0)
    u_in = lax.linalg.triangular_solve(lhs, vc * bc[..., None], left_side=True, lower=True, unit_diagonal=True)
    w_in = lax.linalg.triangular_solve(lhs, kb * jnp.exp(G)[..., None], left_side=True, lower=True,
                                       unit_diagonal=True)
    qk = jnp.einsum('nbhik,nbhjk->nbhij', qc, kc) * decay
    q_dec = qc * jnp.exp(G)[..., None]
    k_dec = kc * jnp.exp(G[..., -1:] - G)[..., None]
    g_last = jnp.exp(G[..., -1])

    def step(state, inp):
        q_d, k_d, u0, w0, qk_c, gl = inp
        u = u0 - jnp.einsum('bhck,bhkv->bhcv', w0, state)
        o = jnp.einsum('bhck,bhkv->bhcv', q_d, state) + jnp.einsum('bhij,bhjv->bhiv', qk_c, u)
        state = state * gl[..., None, None] + jnp.einsum('bhck,bhcv->bhkv', k_d, u)
        return state, o

    s_final, o = lax.scan(step, s0, (q_dec, k_dec, u_in, w_in, qk, g_last))
    o = jnp.moveaxis(o, (0, 2), (1, 3)).reshape(b, n_tok, h * dv)
    return o, s_final


def bidir_delta(q, k, v, beta, g, s0_fwd, s0_bwd):
    o_f, s_f = gated_delta_rule(q, k, v, beta[:, :, 0], g[:, :, 0], s0_fwd)
    flip = lambda t: jnp.flip(t, axis=1)
    o_b, s_b = gated_delta_rule(flip(q), flip(k), flip(v), flip(beta[:, :, 1]), flip(g[:, :, 1]), s0_bwd)
    return o_f, flip(o_b), s_f, s_b


def fourier_mix(pfn):
    b, n, _ = pfn.shape
    pg = pfn.reshape(b, n, FN_GROUPS, FN_GROUP_DIM)
    return jnp.fft.fft2(pg, axes=(1, 3), norm='ortho').real.reshape(b, n, FN_WIDTH)


def hyena_kernel_rfft(n, w1, b1, freq1, w2, b2, freq2, w3):
    f32 = jnp.float32
    pos = jnp.arange(n, dtype=f32)
    t = pos / max(n - 1, 1)
    bands = jnp.linspace(1e-4, HY_BANDS - 1, HY_BANDS, dtype=f32)
    ang = (2.0 * math.pi / n) * pos[:, None] * bands[None, :]
    feats = jnp.concatenate([t[:, None], jnp.cos(ang), -jnp.sin(ang)], axis=-1)
    hid = jnp.sin(freq1 * (feats @ w1 + b1))
    hid = jnp.sin(freq2 * (hid @ w2 + b2))
    filt = (hid @ w3).reshape(n, 2, HY_ORDER, HY_WIDTH)
    min_decay = math.log(HY_DECAY_TARGET) / HY_SLOW_DECAY_PCT
    max_decay = math.log(HY_DECAY_TARGET) / HY_FAST_DECAY_PCT
    deltas = jnp.abs(jnp.linspace(min_decay, max_decay, HY_ORDER * HY_WIDTH, dtype=f32)).reshape(HY_ORDER, HY_WIDTH)
    filt = filt * jnp.exp(-t[:, None, None, None] * deltas[None, None])
    fwd, bwd = filt[:, 0], filt[:, 1]
    kern = jnp.concatenate([fwd, jnp.zeros((1, HY_ORDER, HY_WIDTH), f32), bwd[:0:-1]], axis=0)
    kern = kern / jnp.sum(jnp.abs(kern), axis=0, keepdims=True)
    return jnp.fft.rfft(kern, axis=0)


def fft_conv(u, k_freq):
    n = u.shape[1]
    spec = jnp.fft.rfft(u, n=2 * n, axis=1) * k_freq[None]
    return jnp.fft.irfft(spec, n=2 * n, axis=1)[:, :n]


def hyena_branch(phy, conv_w, k_freq, bias):
    parts = jnp.split(seq_dwconv(phy, conv_w), HY_ORDER + 1, axis=-1)
    z = parts[HY_ORDER]
    for order in range(HY_ORDER):
        z = parts[order] * (fft_conv(z, k_freq[:, order]) + bias[order] * z)
    return z


def _prep_w_in(w):
    main = jnp.concatenate([w[:, OFF_GATE:IN_WIDTH], w[:, OFF_Q:OFF_Z], w[:, OFF_HY:OFF_GATE],
                            w[:, OFF_FN:OFF_HY], w[:, OFF_Z:OFF_BETA]], axis=1).astype(BF16)
    small = jnp.pad(w[:, OFF_BETA:OFF_FN], ((0, 0), (0, SMALL_WIDTH - 4 * DN_HEADS))).astype(BF16)
    return main, small


def kernel(x, c, ctx, c_ctx, w_mod, b_mod, norm1, norm2, w_in, dn_conv, dn_a_log, dn_dt_bias,
           dn_out_norm, hy_conv, hy_w1, hy_b1, hy_freq1, hy_w2, hy_b2, hy_freq2, hy_w3, hy_bias,
           w_branch_a, w_branch_b, w_branch_c, w_out, w_ff1, w_ff2, final_norm):
    b, n_lat, d = x.shape
    rows = n_lat // GRID_W
    n_ctx = ctx.shape[1]
    tm_x, tm_c = 1024, n_ctx

    c_rows = jnp.concatenate([c, c_ctx[None], jnp.zeros((8 - b - 1, d), F32)], axis=0)
    mods = mod_vectors(c_rows, w_mod, b_mod)
    s_zero = jnp.zeros((2, b, DN_HEADS, DN_HEAD_DIM, DN_HEAD_DIM), F32)
    h, hc = x.reshape(b * n_lat, d), ctx.reshape(b * n_ctx, d)

    for l in range(DEPTH):
        last = l == DEPTH - 1
        mv = mods[l].reshape(8, N_MOD, 1, d)
        mx = [mv[:b, i] for i in range(N_MOD)]
        mc = [mv[b:b + 1, i] for i in range(N_MOD)]
        w_main, w_small = _prep_w_in(w_in[l])
        wa, wb, wc, wo = (w.astype(BF16) for w in (w_branch_a[l], w_branch_b[l], w_branch_c[l], w_out[l]))
        w1, w2 = w_ff1[l].astype(BF16), w_ff2[l].astype(BF16)
        n1, n2 = norm1[l][None, None, :], norm2[l][None, None, :]

        p_c, small_c = in_proj(hc, n1 * (1.0 + mc[1]), mc[0], w_main, w_small, b * n_ctx, tm_c)
        p_x, small_x = in_proj(h, n1 * (1.0 + mx[1]), mx[0], w_main, w_small, n_lat, tm_x)

        def mix(p, n):
            p3 = p.reshape(b, n, P_WIDTH)
            kf = hyena_kernel_rfft(n, hy_w1[l], hy_b1[l], hy_freq1[l], hy_w2[l], hy_b2[l], hy_freq2[l], hy_w3[l])
            y_b = fourier_mix(p3[..., P_FN:P_Z]).reshape(b * n, FN_WIDTH)
            y_c = hyena_branch(p3[..., P_HY:P_FN], hy_conv[l], kf, hy_bias[l]).reshape(b * n, HY_WIDTH)
            return y_b, y_c

        ocf, ocb, s_ctx = delta_branch(p_c, small_c, dn_conv[l], dn_a_log[l], dn_dt_bias[l], s_zero,
                                       b, n_ctx, 1, n_ctx)
        oxf, oxb, _ = delta_branch(p_x, small_x, dn_conv[l], dn_a_log[l], dn_dt_bias[l], s_ctx,
                                   b, n_lat, rows, GRID_W)

        y_b, y_c = mix(p_x, n_lat)
        h = merge(oxf, oxb, p_x, y_b, y_c, h, mx[2], dn_out_norm[l], wa, wb, wc, wo, n_lat, 512)
        h = mlp(h, n2 * (1.0 + mx[4]), mx[3], mx[5], w1, w2, n_lat, tm_x)

        if not last:
            y_b, y_c = mix(p_c, n_ctx)
            hc = merge(ocf, ocb, p_c, y_b, y_c, hc, mc[2], dn_out_norm[l], wa, wb, wc, wo, b * n_ctx, tm_c)
            hc = mlp(hc, n2 * (1.0 + mc[4]), mc[3], mc[5], w1, w2, b * n_ctx, tm_c)

    return final_rms_norm(h, final_norm).reshape(b, n_lat, d)
```

```python
import functools
import math

import jax
import jax.numpy as jnp
import numpy as np
from jax import lax
from jax.experimental import pallas as pl
from jax.experimental.pallas import tpu as pltpu

D_MODEL = 1024
DEPTH = 2
GRID_W = 64
NORM_EPS = 1e-6
N_MOD = 6

DN_HEADS = 4
DN_HEAD_DIM = 128
DN_WIDTH = DN_HEADS * DN_HEAD_DIM
DN_CHUNK = 64
SHORT_CONV = 3

FN_GROUPS = 4
FN_GROUP_DIM = 64
FN_WIDTH = FN_GROUPS * FN_GROUP_DIM

HY_WIDTH = 256
HY_ORDER = 2
HY_EMB_DIM = 33
HY_BANDS = (HY_EMB_DIM - 1) // 2
HY_FILTER_HIDDEN = 64
HY_FAST_DECAY_PCT = 0.3
HY_SLOW_DECAY_PCT = 1.5
HY_DECAY_TARGET = 1e-2

N_BRANCHES = 3
D_FF = 4 * D_MODEL

OFF_Q = 0
OFF_Z = 3 * DN_WIDTH
OFF_BETA = OFF_Z + DN_WIDTH
OFF_A = OFF_BETA + 2 * DN_HEADS
OFF_FN = OFF_A + 2 * DN_HEADS
OFF_HY = OFF_FN + FN_WIDTH
OFF_GATE = OFF_HY + (HY_ORDER + 1) * HY_WIDTH
IN_WIDTH = OFF_GATE + N_BRANCHES * D_MODEL

P_GATE = 0
P_QKV = P_GATE + N_BRANCHES * D_MODEL
P_HY = P_QKV + 3 * DN_WIDTH
P_FN = P_HY + (HY_ORDER + 1) * HY_WIDTH
P_Z = P_FN + FN_WIDTH
P_WIDTH = P_Z + DN_WIDTH
SMALL_WIDTH = 128

F32 = jnp.float32
BF16 = jnp.bfloat16
VMEM_LIMIT = 56 * 1024 * 1024


def _cparams(*sem):
    return pltpu.CompilerParams(dimension_semantics=sem, vmem_limit_bytes=VMEM_LIMIT)


def _bdot(a, b):
    return jnp.dot(a.astype(BF16), b.astype(BF16), preferred_element_type=F32)


def _sigmoid(x):
    return 1.0 / (1.0 + jnp.exp(-x))


def _modnorm(xf, gs, sh):
    r = lax.rsqrt(jnp.mean(xf * xf, axis=-1, keepdims=True) + NORM_EPS)
    return xf * r * gs + sh


def _mod_kernel(c_ref, w_ref, b_ref, o_ref):
    c = c_ref[...]
    o_ref[...] = _bdot(c * _sigmoid(c), w_ref[...]) + b_ref[...]


def mod_vectors(c_rows, w_mod, b_mod):
    tn = 1536
    n = N_MOD * D_MODEL
    return pl.pallas_call(
        _mod_kernel,
        grid=(DEPTH, n // tn),
        in_specs=[pl.BlockSpec((8, D_MODEL), lambda l, j: (0, 0)),
                  pl.BlockSpec((None, D_MODEL, tn), lambda l, j: (l, 0, j)),
                  pl.BlockSpec((None, 1, tn), lambda l, j: (l, 0, j))],
        out_specs=pl.BlockSpec((None, 8, tn), lambda l, j: (l, 0, j)),
        out_shape=jax.ShapeDtypeStruct((DEPTH, 8, n), F32),
        compiler_params=_cparams("parallel", "parallel"),
        name="mod_vectors",
    )(c_rows, w_mod, b_mod.reshape(DEPTH, 1, n))


def _in_proj_kernel(x_ref, gs_ref, sh_ref, w_ref, ws_ref, p_ref, small_ref, xn_ref):
    @pl.when(pl.program_id(1) == 0)
    def _():
        xn = _modnorm(x_ref[...], gs_ref[0], sh_ref[0]).astype(BF16)
        xn_ref[...] = xn
        small_ref[...] = jnp.dot(xn, ws_ref[...], preferred_element_type=F32)

    p_ref[...] = jnp.dot(xn_ref[...], w_ref[...], preferred_element_type=F32)


def in_proj(x2d, gs, sh, w_main, w_small, rows_per_mod, tm):
    m = x2d.shape[0]
    tn = 1024
    tiles_per_mod = rows_per_mod // tm
    mod_spec = pl.BlockSpec((1, 1, D_MODEL), lambda i, j: (i // tiles_per_mod, 0, 0))
    return pl.pallas_call(
        _in_proj_kernel,
        grid=(m // tm, P_WIDTH // tn),
        in_specs=[pl.BlockSpec((tm, D_MODEL), lambda i, j: (i, 0)), mod_spec, mod_spec,
                  pl.BlockSpec((D_MODEL, tn), lambda i, j: (0, j)),
                  pl.BlockSpec((D_MODEL, SMALL_WIDTH), lambda i, j: (0, 0))],
        out_specs=[pl.BlockSpec((tm, tn), lambda i, j: (i, j)),
                   pl.BlockSpec((tm, SMALL_WIDTH), lambda i, j: (i, 0))],
        out_shape=[jax.ShapeDtypeStruct((m, P_WIDTH), F32), jax.ShapeDtypeStruct((m, SMALL_WIDTH), F32)],
        scratch_shapes=[pltpu.VMEM((tm, D_MODEL), BF16)],
        compiler_params=_cparams("parallel", "arbitrary"),
        name="in_proj",
    )(x2d, gs, sh, w_main, w_small)


def _merge_kernel(of_ref, ob_ref, z_ref, yb_ref, yc_ref, g_ref, h_ref, gate_ref, nrm_ref,
                  wa_ref, wb_ref, wc_ref, wo_ref, out_ref):
    o = of_ref[...] + ob_ref[...]
    z = z_ref[...]
    heads = []
    for hd in range(DN_HEADS):
        sl = slice(hd * DN_HEAD_DIM, (hd + 1) * DN_HEAD_DIM)
        oh, zh = o[:, sl], z[:, sl]
        r = lax.rsqrt(jnp.mean(oh * oh, axis=-1, keepdims=True) + NORM_EPS)
        heads.append(oh * r * nrm_ref[...] * (zh * _sigmoid(zh)))
    ya = jnp.concatenate(heads, axis=-1)
    merged = (_sigmoid(g_ref[:, 0:D_MODEL]) * _bdot(ya, wa_ref[...])
              + _sigmoid(g_ref[:, D_MODEL:2 * D_MODEL]) * _bdot(yb_ref[...], wb_ref[...])
              + _sigmoid(g_ref[:, 2 * D_MODEL:3 * D_MODEL]) * _bdot(yc_ref[...], wc_ref[...]))
    out_ref[...] = h_ref[...] + gate_ref[0] * _bdot(merged, wo_ref[...])


def merge(o_f, o_b, p, y_b, y_c, h2d, gate, dn_out_norm, wa, wb, wc, wo, rows_per_mod, tm):
    m = h2d.shape[0]
    tiles_per_mod = rows_per_mod // tm
    row = lambda w: pl.BlockSpec((tm, w), lambda i: (i, 0))
    full = lambda a: pl.BlockSpec(a.shape, lambda i: (0,) * a.ndim)
    nrm = dn_out_norm.reshape(1, DN_HEAD_DIM)
    return pl.pallas_call(
        _merge_kernel,
        grid=(m // tm,),
        in_specs=[row(DN_WIDTH), row(DN_WIDTH),
                  pl.BlockSpec((tm, DN_WIDTH), lambda i: (i, P_Z // DN_WIDTH)),
                  row(FN_WIDTH), row(HY_WIDTH),
                  pl.BlockSpec((tm, N_BRANCHES * D_MODEL), lambda i: (i, 0)),
                  row(D_MODEL),
                  pl.BlockSpec((1, 1, D_MODEL), lambda i: (i // tiles_per_mod, 0, 0)),
                  full(nrm), full(wa), full(wb), full(wc), full(wo)],
        out_specs=row(D_MODEL),
        out_shape=jax.ShapeDtypeStruct((m, D_MODEL), F32),
        compiler_params=_cparams("parallel"),
        name="merge",
    )(o_f, o_b, p, y_b, y_c, p, h2d, gate, nrm, wa, wb, wc, wo)


def _mlp_kernel(h_ref, gs_ref, sh_ref, gate_ref, w1_ref, w2_ref, out_ref, xn_ref, acc_ref):
    j = pl.program_id(1)

    @pl.when(j == 0)
    def _():
        xn_ref[...] = _modnorm(h_ref[...], gs_ref[0], sh_ref[0]).astype(BF16)
        acc_ref[...] = jnp.zeros_like(acc_ref)

    a = jnp.maximum(jnp.dot(xn_ref[...], w1_ref[...], preferred_element_type=F32), 0.0)
    acc_ref[...] += jnp.dot((a * a).astype(BF16), w2_ref[...], preferred_element_type=F32)

    @pl.when(j == pl.num_programs(1) - 1)
    def _():
        out_ref[...] = h_ref[...] + gate_ref[0] * acc_ref[...]


def mlp(h2d, gs, sh, gate, w1, w2, rows_per_mod, tm):
    m = h2d.shape[0]
    tf = 512
    tiles_per_mod = rows_per_mod // tm
    mod_spec = pl.BlockSpec((1, 1, D_MODEL), lambda i, j: (i // tiles_per_mod, 0, 0))
    return pl.pallas_call(
        _mlp_kernel,
        grid=(m // tm, D_FF // tf),
        in_specs=[pl.BlockSpec((tm, D_MODEL), lambda i, j: (i, 0)), mod_spec, mod_spec, mod_spec,
                  pl.BlockSpec((D_MODEL, tf), lambda i, j: (0, j)),
                  pl.BlockSpec((tf, D_MODEL), lambda i, j: (j, 0))],
        out_specs=pl.BlockSpec((tm, D_MODEL), lambda i, j: (i, 0)),
        out_shape=jax.ShapeDtypeStruct((m, D_MODEL), F32),
        scratch_shapes=[pltpu.VMEM((tm, D_MODEL), BF16), pltpu.VMEM((tm, D_MODEL), F32)],
        compiler_params=_cparams("parallel", "arbitrary"),
        name="mlp",
    )(h2d, gs, sh, gate, w1, w2)


def _final_norm_kernel(h_ref, g_ref, o_ref):
    xf = h_ref[...]
    o_ref[...] = xf * lax.rsqrt(jnp.mean(xf * xf, axis=-1, keepdims=True) + NORM_EPS) * g_ref[...]


def final_rms_norm(h2d, gain):
    m, d = h2d.shape
    tm = 512
    return pl.pallas_call(
        _final_norm_kernel,
        grid=(m // tm,),
        in_specs=[pl.BlockSpec((tm, d), lambda i: (i, 0)), pl.BlockSpec((1, d), lambda i: (0, 0))],
        out_specs=pl.BlockSpec((tm, d), lambda i: (i, 0)),
        out_shape=jax.ShapeDtypeStruct((m, d), F32),
        compiler_params=_cparams("parallel"),
        name="final_norm",
    )(h2d, gain.reshape(1, d))


def _dnconv_kernel(prev_ref, cur_ref, next_ref, w_ref, o_ref, *, cols, n_tiles):
    t, j = pl.program_id(1), pl.program_id(2)
    tt = cur_ref.shape[0]
    prev = jnp.where(t == 0, 0.0, prev_ref[...])
    nxt = jnp.where(t == n_tiles - 1, 0.0, next_ref[...])
    ext = jnp.concatenate([prev, cur_ref[...], nxt], axis=0)
    n_ext = tt + 2 * cols
    col = lax.broadcasted_iota(jnp.int32, (n_ext, 1), 0) % cols
    left = jnp.where(col == 0, 0.0, pltpu.roll(ext, 1, axis=0))
    right = jnp.where(col == cols - 1, 0.0, pltpu.roll(ext, n_ext - 1, axis=0))
    acc = jnp.zeros((tt, DN_HEAD_DIM), F32)
    for dr in range(SHORT_CONV):
        base = dr * cols
        acc = (acc + w_ref[3 * dr:3 * dr + 1, :] * left[base:base + tt]
               + w_ref[3 * dr + 1:3 * dr + 2, :] * ext[base:base + tt]
               + w_ref[3 * dr + 2:3 * dr + 3, :] * right[base:base + tt])
    y = acc * _sigmoid(acc)
    nrm = lax.rsqrt(jnp.sum(y * y, axis=-1, keepdims=True) + NORM_EPS)
    scale = jnp.where(j < DN_HEADS, nrm * DN_HEAD_DIM ** -0.5, jnp.where(j < 2 * DN_HEADS, nrm, 1.0))
    o_ref[...] = y * scale


def dn_conv_prep(p, conv_w, b, n, rows, cols):
    tr = min(rows, 16)
    tt = tr * cols
    n_tiles = rows // tr
    nblk = b * n // cols
    c0 = P_QKV // DN_HEAD_DIM
    return pl.pallas_call(
        functools.partial(_dnconv_kernel, cols=cols, n_tiles=n_tiles),
        grid=(b, n_tiles, 3 * DN_HEADS),
        in_specs=[pl.BlockSpec((cols, DN_HEAD_DIM),
                               lambda bi, t, j: (jnp.maximum(bi * rows + t * tr - 1, 0), c0 + j)),
                  pl.BlockSpec((tt, DN_HEAD_DIM), lambda bi, t, j: (bi * n_tiles + t, c0 + j)),
                  pl.BlockSpec((cols, DN_HEAD_DIM),
                               lambda bi, t, j: (jnp.minimum(bi * rows + (t + 1) * tr, nblk - 1), c0 + j)),
                  pl.BlockSpec((SHORT_CONV * SHORT_CONV, DN_HEAD_DIM), lambda bi, t, j: (0, j))],
        out_specs=pl.BlockSpec((tt, DN_HEAD_DIM), lambda bi, t, j: (bi * n_tiles + t, j)),
        out_shape=jax.ShapeDtypeStruct((b * n, 3 * DN_WIDTH), F32),
        compiler_params=_cparams("parallel", "parallel", "parallel"),
        name="dn_conv_prep",
    )(p, p, p, conv_w.reshape(SHORT_CONV * SHORT_CONV, 3 * DN_WIDTH))


PREP_CHUNKS = 2


def _softplus(x):
    return jnp.maximum(x, 0.0) + jnp.log(1.0 + jnp.exp(-jnp.abs(x)))


def _delta_prep_kernel(qkv_ref, sm_ref, smt_ref, prm_ref, prmt_ref, u0_ref, lhs1_ref, lhs2_ref, gl_ref):
    cc = DN_CHUNK
    nh = DN_HEADS
    sm = sm_ref[...]
    beta_all = _sigmoid(sm)
    g_all = -jnp.exp(prm_ref[0:1, :]) * _softplus(sm + prm_ref[1:2, :])
    gt_all = -jnp.exp(prmt_ref[:, 0:1]) * _softplus(smt_ref[...] + prmt_ref[:, 1:2])
    lane = lax.broadcasted_iota(jnp.int32, gt_all.shape, 1) % cc
    gt_f, gt_b = gt_all, gt_all
    s = 1
    while s < cc:
        gt_f = gt_f + jnp.where(lane >= s, pltpu.roll(gt_f, s, axis=1), 0.0)
        gt_b = gt_b + jnp.where(lane < cc - s, pltpu.roll(gt_b, gt_all.shape[1] - s, axis=1), 0.0)
        s *= 2
    ri = lax.broadcasted_iota(jnp.int32, (cc, cc), 0)
    ci_ = lax.broadcasted_iota(jnp.int32, (cc, cc), 1)
    sub = lax.broadcasted_iota(jnp.int32, (cc, 1), 0)
    blk = lambda s: (ri // s) == (ci_ // s)
    leaf = 8

    heads, chains = [], []
    for ci in range(PREP_CHUNKS):
        rows = slice(ci * cc, (ci + 1) * cc)
        gc_f = g_all[rows]
        gc_b = gc_f
        s = 1
        while s < cc:
            gc_f = gc_f + jnp.where(sub >= s, pltpu.roll(gc_f, s, axis=0), 0.0)
            gc_b = gc_b + jnp.where(sub < cc - s, pltpu.roll(gc_b, cc - s, axis=0), 0.0)
            s *= 2
        for h in range(nh):
            q = qkv_ref[rows, h * 128:(h + 1) * 128]
            k = qkv_ref[rows, (nh + h) * 128:(nh + h + 1) * 128]
            v = qkv_ref[rows, (2 * nh + h) * 128:(2 * nh + h + 1) * 128]
            heads.append((ci, rows, h, q, k, v, gc_f, gc_b))
    qkks = [lax.dot_general(jnp.concatenate([q, k], axis=0).astype(BF16), k.astype(BF16),
                            (((1,), (1,)), ((), ())), preferred_element_type=F32)
            for (_, _, _, q, k, _, _, _) in heads]
    for (ci, rows, h, q, k, v, gc_f, gc_b), qkk in zip(heads, qkks):
        qk, kk = qkk[:cc], qkk[cc:]
        for d in range(2):
            cb_, ca_ = d * nh + h, 2 * nh + d * nh + h
            beta = beta_all[rows, cb_:cb_ + 1]
            g_col = (gc_f if d == 0 else gc_b)[:, ca_:ca_ + 1]
            g_row = (gt_f if d == 0 else gt_b)[ca_:ca_ + 1, rows]
            incl = (ri >= ci_) if d == 0 else (ri <= ci_)
            strict = (ri > ci_) if d == 0 else (ri < ci_)
            decay = jnp.exp(jnp.where(incl, g_col - g_row, -1e30))
            a = jnp.where(strict, beta * kk * decay, 0.0)
            eg = jnp.exp(g_col)
            g_last = g_col[cc - 1:cc] if d == 0 else g_col[0:1]
            chains.append(dict(ci=ci, rows=rows, h=h, d=d, a=a, qkd=qk * decay, g_last=g_last,
                               rhs=jnp.concatenate([v * beta, k * (beta * eg)], axis=1),
                               q_d=q * eg, k_d=k * jnp.exp(g_last - g_col)))

    pws = [jnp.where(blk(leaf), c["a"], 0.0) for c in chains]
    devs = [-pw for pw in pws]
    for _ in range(2):
        pws = [_bdot(pw, pw) for pw in pws]
        cross = [_bdot(dev, pw) for dev, pw in zip(devs, pws)]
        devs = [dev + pw + x for dev, pw, x in zip(devs, pws, cross)]
    s = leaf
    while s < cc:
        offs = [jnp.where(blk(2 * s) & jnp.logical_not(blk(s)), c["a"], 0.0) for c in chains]
        xs = [off + _bdot(dev, off) for dev, off in zip(devs, offs)]
        devs = [dev - x - _bdot(x, dev) for dev, x in zip(devs, xs)]
        s *= 2
    uws = [c["rhs"] + _bdot(dev, c["rhs"]) for dev, c in zip(devs, chains)]
    for c, uw in zip(chains, uws):
        d, h, ci = c["d"], c["h"], c["ci"]
        u0_ref[d, h, c["rows"], :] = uw[:, :128]
        lhs1_ref[d, h, ci] = jnp.concatenate([uw[:, 128:], c["q_d"]], axis=0).astype(BF16)
        lhs2_ref[d, h, ci] = jnp.concatenate([c["qkd"], c["k_d"].T], axis=0).astype(BF16)
        gl_ref[d, h, ci] = jnp.broadcast_to(jnp.exp(c["g_last"]), (8, 128))


def delta_prep(qkv, small, small_t, prm, prm_t, b, n):
    nc = n // DN_CHUNK
    cb = PREP_CHUNKS
    nblk = nc // cb
    tt = cb * DN_CHUNK
    nh = DN_HEADS
    return pl.pallas_call(
        _delta_prep_kernel,
        grid=(b, nblk),
        in_specs=[pl.BlockSpec((tt, 3 * DN_WIDTH), lambda bi, c: (bi * nblk + c, 0)),
                  pl.BlockSpec((tt, SMALL_WIDTH), lambda bi, c: (bi * nblk + c, 0)),
                  pl.BlockSpec((4 * nh, tt), lambda bi, c: (0, bi * nblk + c)),
                  pl.BlockSpec((2, SMALL_WIDTH), lambda bi, c: (0, 0)),
                  pl.BlockSpec((4 * nh, 2), lambda bi, c: (0, 0))],
        out_specs=[pl.BlockSpec((2, None, nh, tt, 128), lambda bi, c: (0, bi, 0, c, 0)),
                   pl.BlockSpec((2, None, nh, cb, 128, 128), lambda bi, c: (0, bi, 0, c, 0, 0)),
                   pl.BlockSpec((2, None, nh, cb, 192, 64), lambda bi, c: (0, bi, 0, c, 0, 0)),
                   pl.BlockSpec((2, None, nh, cb, 8, 128), lambda bi, c: (0, bi, 0, c, 0, 0))],
        out_shape=[jax.ShapeDtypeStruct((2, b, nh, n, 128), F32),
                   jax.ShapeDtypeStruct((2, b, nh, nc, 128, 128), BF16),
                   jax.ShapeDtypeStruct((2, b, nh, nc, 192, 64), BF16),
                   jax.ShapeDtypeStruct((2, b, nh, nc, 8, 128), F32)],
        compiler_params=_cparams("parallel", "parallel"),
        name="delta_prep",
    )(qkv, small, small_t, prm, prm_t)


def _delta_scan_kernel(u0f_ref, u0b_ref, l1f_ref, l1b_ref, l2f_ref, l2b_ref, glf_ref, glb_ref, s0_ref,
                       of_ref, ob_ref, sout_ref, st_ref, *, sc, nb):
    n = pl.program_id(0)
    cc = DN_CHUNK

    @pl.when(n == 0)
    def _():
        st_ref[...] = s0_ref[...]

    dirs = ((u0f_ref, l1f_ref, l2f_ref, glf_ref, of_ref), (u0b_ref, l1b_ref, l2b_ref, glb_ref, ob_ref))

    def body(i, carry):
        chains = []
        for d in range(2):
            ci = i if d == 0 else sc - 1 - i
            r0 = pl.multiple_of(ci * cc, cc)
            chains += [(d, bi, h, ci, r0) for bi in range(nb) for h in range(DN_HEADS)]
        sts = [st_ref[d, bi, h] for (d, bi, h, _, _) in chains]
        r1s = [jnp.dot(dirs[d][1][bi, h, ci], st.astype(BF16), preferred_element_type=F32)
               for (d, bi, h, ci, _), st in zip(chains, sts)]
        us = [dirs[d][0][bi, h, pl.ds(r0, cc), :] - r1[:cc] for (d, bi, h, _, r0), r1 in zip(chains, r1s)]
        r2s = [jnp.dot(dirs[d][2][bi, h, ci], u.astype(BF16), preferred_element_type=F32)
               for (d, bi, h, ci, _), u in zip(chains, us)]
        for (d, bi, h, ci, r0), st, r1, r2 in zip(chains, sts, r1s, r2s):
            dirs[d][4][bi, pl.ds(r0, cc), h * 128:(h + 1) * 128] = r1[cc:] + r2[:cc]
            st_ref[d, bi, h] = st * dirs[d][3][bi, h, ci, 0:1, :] + r2[cc:]
        return carry

    lax.fori_loop(0, sc, body, 0)

    @pl.when(n == pl.num_programs(0) - 1)
    def _():
        sout_ref[...] = st_ref[...]


def delta_scan(u0, lhs1, lhs2, gl, s0, b, n):
    nc = n // DN_CHUNK
    sc = 4
    nblk = nc // sc
    nh = DN_HEADS
    tt = sc * DN_CHUNK
    fwd = lambda i: i
    bwd = lambda i: nblk - 1 - i

    def specs(d, blk):
        return [pl.BlockSpec((None, b, nh, tt, 128), lambda i: (d, 0, 0, blk(i), 0)),
                pl.BlockSpec((None, b, nh, sc, 128, 128), lambda i: (d, 0, 0, blk(i), 0, 0)),
                pl.BlockSpec((None, b, nh, sc, 192, 64), lambda i: (d, 0, 0, blk(i), 0, 0)),
                pl.BlockSpec((None, b, nh, sc, 8, 128), lambda i: (d, 0, 0, blk(i), 0, 0))]

    sf, sb = specs(0, fwd), specs(1, bwd)
    in_specs = [sf[0], sb[0], sf[1], sb[1], sf[2], sb[2], sf[3], sb[3],
                pl.BlockSpec((2, b, nh, 128, 128), lambda i: (0, 0, 0, 0, 0))]
    return pl.pallas_call(
        functools.partial(_delta_scan_kernel, sc=sc, nb=b),
        grid=(nblk,),
        in_specs=in_specs,
        out_specs=[pl.BlockSpec((b, tt, DN_WIDTH), lambda i: (0, fwd(i), 0)),
                   pl.BlockSpec((b, tt, DN_WIDTH), lambda i: (0, bwd(i), 0)),
                   pl.BlockSpec((2, b, nh, 128, 128), lambda i: (0, 0, 0, 0, 0))],
        out_shape=[jax.ShapeDtypeStruct((b, n, DN_WIDTH), F32), jax.ShapeDtypeStruct((b, n, DN_WIDTH), F32),
                   jax.ShapeDtypeStruct((2, b, nh, 128, 128), F32)],
        scratch_shapes=[pltpu.VMEM((2, b, nh, 128, 128), F32)],
        compiler_params=_cparams("arbitrary"),
        name="delta_scan",
    )(u0, u0, lhs1, lhs1, lhs2, lhs2, gl, gl, s0)


def delta_branch(p, small, conv_w, a_log, dt_bias, s0, b, n, rows, cols):
    nh = DN_HEADS
    qkv = dn_conv_prep(p, conv_w, b, n, rows, cols)
    small_t = small[:, :4 * nh].T
    rate = jnp.concatenate([jnp.zeros((2 * nh,), F32), a_log.reshape(-1)])
    bias = jnp.concatenate([jnp.zeros((2 * nh,), F32), dt_bias.reshape(-1)])
    prm_t = jnp.stack([rate, bias], axis=1)
    prm = jnp.pad(prm_t.T, ((0, 0), (0, SMALL_WIDTH - 4 * nh)))
    u0, lhs1, lhs2, gl = delta_prep(qkv, small, small_t, prm, prm_t, b, n)
    o_f, o_b, s_out = delta_scan(u0, lhs1, lhs2, gl, s0, b, n)
    return o_f.reshape(b * n, DN_WIDTH), o_b.reshape(b * n, DN_WIDTH), s_out


def _cos_sin(rows, cols, period):
    ang = 2.0 * np.pi * ((np.arange(rows)[:, None] * np.arange(cols)[None, :]) % period) / period
    return np.cos(ang), np.sin(ang)


def _const_bf16(a):
    return jnp.asarray(a, F32).astype(BF16)


def _stage1_matrix(l1, n1_used):
    c, s = _cos_sin(l1, n1_used, l1)
    return _const_bf16(np.concatenate([c, -s], axis=0))


def _stage2_matrices(l2):
    c, s = _cos_sin(l2, l2, l2)
    fwd = np.block([[c, s], [-s, c]])
    inv = np.block([[c, -s], [s, c]])
    return _const_bf16(fwd), _const_bf16(inv)


def _twiddles(l1, l2, kb):
    ang = 2.0 * np.pi * ((np.arange(l2)[:, None] * np.arange(l1)[None, :]) % (l1 * l2)) / (l1 * l2)
    tw = np.stack([np.cos(ang), np.sin(ang)], axis=0).reshape(2, l2, l1 // kb, kb)
    return jnp.asarray(np.transpose(tw, (2, 0, 1, 3)), F32)


def _left_mm_kernel(m_ref, x_ref, o_ref):
    o_ref[...] = jnp.dot(m_ref[...], x_ref[...].astype(BF16), preferred_element_type=F32)


def left_mm(m, x, tn):
    bsz, k, n = x.shape
    r = m.shape[0]
    return pl.pallas_call(
        _left_mm_kernel,
        grid=(bsz, n // tn),
        in_specs=[pl.BlockSpec((r, k), lambda b, j: (0, 0)),
                  pl.BlockSpec((None, k, tn), lambda b, j: (b, 0, j))],
        out_specs=pl.BlockSpec((None, r, tn), lambda b, j: (b, 0, j)),
        out_shape=jax.ShapeDtypeStruct((bsz, r, n), F32),
        compiler_params=_cparams("parallel", "parallel"),
        name="left_mm",
    )(m, x)


def _twiddle_mul(ar, ai, c, s, conj):
    if conj:
        return ar * c - ai * s, ai * c + ar * s
    return ar * c + ai * s, ai * c - ar * s


def _cplx_apply(m_ref, re, im):
    half = re.shape[0]
    out = jnp.dot(m_ref[...], jnp.concatenate([re, im], axis=0).astype(BF16), preferred_element_type=F32)
    return out[:half], out[half:]


def _fnet_mid_kernel(a_ref, tw_ref, m2_ref, cs_ref, o_ref, *, kb, ch):
    ts = [_twiddle_mul(a_ref[0, j], a_ref[1, j], tw_ref[0, :, j:j + 1], tw_ref[1, :, j:j + 1], False)
          for j in range(kb)]
    us = [_cplx_apply(m2_ref, tr, ti) for tr, ti in ts]
    ys = [jnp.dot(jnp.concatenate([ur, ui], axis=1).astype(BF16), cs_ref[...], preferred_element_type=F32)
          for ur, ui in us]
    for j, y in enumerate(ys):
        o_ref[:, j * ch:(j + 1) * ch] = y


def fnet_mid(a, tw, m2, cs, kb):
    bsz, _, l1, l2, ch = a.shape
    return pl.pallas_call(
        functools.partial(_fnet_mid_kernel, kb=kb, ch=ch),
        grid=(bsz, l1 // kb),
        in_specs=[pl.BlockSpec((None, 2, kb, l2, ch), lambda b, k: (b, 0, k, 0, 0)),
                  pl.BlockSpec((None, 2, l2, kb), lambda b, k: (k, 0, 0, 0)),
                  pl.BlockSpec(m2.shape, lambda b, k: (0, 0)),
                  pl.BlockSpec(cs.shape, lambda b, k: (0, 0))],
        out_specs=pl.BlockSpec((None, l2, kb * ch), lambda b, k: (b, 0, k)),
        out_shape=jax.ShapeDtypeStruct((bsz, l2, l1 * ch), F32),
        compiler_params=_cparams("parallel", "parallel"),
        name="fnet_mid",
    )(a, tw, m2, cs)


def _fnet_channel_matrix(n):
    c, s = _cos_sin(FN_GROUP_DIM, FN_GROUP_DIM, FN_GROUP_DIM)
    eye = np.eye(FN_GROUPS)
    scale = 1.0 / math.sqrt(n * FN_GROUP_DIM)
    return _const_bf16(np.concatenate([np.kron(eye, c), np.kron(eye, s)], axis=0) * scale)


def fnet_branch(p, b, n):
    ch = FN_WIDTH
    x = lax.slice_in_dim(p, P_FN, P_FN + ch, axis=1)
    l1, l2, kb = (64, 128, 8) if n == 8192 else (1, n, 1)
    if l1 > 1:
        a = left_mm(_stage1_matrix(l1, l1), x.reshape(b, l1, l2 * ch), 4096).reshape(b, 2, l1, l2, ch)
    else:
        xr = x.reshape(b, 1, 1, l2, ch)
        a = jnp.concatenate([xr, jnp.zeros_like(xr)], axis=1)
    y = fnet_mid(a, _twiddles(l1, l2, kb), _stage2_matrices(l2)[0], _fnet_channel_matrix(n), kb)
    return y.reshape(b * n, ch)


def _seq_conv_kernel(prev_ref, cur_ref, next_ref, w_ref, o_ref, *, n_tiles):
    t = pl.program_id(1)
    tt = cur_ref.shape[0]
    cur = cur_ref[...]
    row = lax.broadcasted_iota(jnp.int32, (tt, 1), 0)
    before = jnp.where(t == 0, 0.0, prev_ref[7:8, :])
    after = jnp.where(t == n_tiles - 1, 0.0, next_ref[0:1, :])
    left = jnp.where(row == 0, before, pltpu.roll(cur, 1, axis=0))
    right = jnp.where(row == tt - 1, after, pltpu.roll(cur, tt - 1, axis=0))
    o_ref[...] = w_ref[0:1, :] * left + w_ref[1:2, :] * cur + w_ref[2:3, :] * right


def seq_conv(p, conv_w, b, n):
    tt = min(n, 1024)
    n_tiles = n // tt
    w = HY_WIDTH
    c0 = P_HY // w
    outs = []
    for part in range(HY_ORDER + 1):
        outs.append(pl.pallas_call(
            functools.partial(_seq_conv_kernel, n_tiles=n_tiles),
            grid=(b, n_tiles),
            in_specs=[pl.BlockSpec((8, w), lambda bi, t: (jnp.maximum((bi * n_tiles + t) * (tt // 8) - 1, 0),
                                                          c0 + part)),
                      pl.BlockSpec((tt, w), lambda bi, t: (bi * n_tiles + t, c0 + part)),
                      pl.BlockSpec((8, w), lambda bi, t: (jnp.minimum((bi * n_tiles + t + 1) * (tt // 8),
                                                                      b * n // 8 - 1), c0 + part)),
                      pl.BlockSpec((SHORT_CONV, w), lambda bi, t: (0, part))],
            out_specs=pl.BlockSpec((tt, w), lambda bi, t: (bi * n_tiles + t, 0)),
            out_shape=jax.ShapeDtypeStruct((b * n, w), F32),
            compiler_params=_cparams("parallel", "parallel"),
            name="hy_seq_conv",
        )(p, p, p, conv_w))
    return outs


def _hdot(a, b):
    return jnp.dot(a, b, preferred_element_type=F32, precision=lax.Precision.HIGHEST)


def _hy_filter_kernel(f_ref, w1_ref, b1_ref, f1_ref, w2_ref, b2_ref, f2_ref, w3_ref, dl_ref, k_ref, s_ref, *, n, tr):
    i = pl.program_id(0)
    feats = f_ref[...]
    hid = jnp.sin(f1_ref[...] * (_hdot(feats, w1_ref[...]) + b1_ref[...]))
    hid = jnp.sin(f2_ref[...] * (_hdot(hid, w2_ref[...]) + b2_ref[...]))
    filt = _hdot(hid, w3_ref[...]) * jnp.exp(-feats[:, 0:1] * dl_ref[...])
    row = i * tr + lax.broadcasted_iota(jnp.int32, (tr, 1), 0)
    filt = jnp.where(row == n, 0.0, filt)
    k_ref[...] = filt

    @pl.when(i == 0)
    def _():
        s_ref[...] = jnp.zeros_like(s_ref)

    s_ref[...] += jnp.sum(jnp.abs(filt), axis=0, keepdims=True)


def hy_filter(n, w1, b1, freq1, w2, b2, freq2, w3):
    pos = jnp.arange(n, dtype=F32)
    t = pos / max(n - 1, 1)
    bands = jnp.linspace(1e-4, HY_BANDS - 1, HY_BANDS, dtype=F32)
    ang = (2.0 * math.pi / n) * pos[:, None] * bands[None, :]
    feats = jnp.concatenate([t[:, None], jnp.cos(ang), -jnp.sin(ang)], axis=-1)
    feats2 = jnp.concatenate([feats, feats[:1], feats[:0:-1]], axis=0)
    kpad = 128
    feats2 = jnp.pad(feats2, ((0, 0), (0, kpad - HY_EMB_DIM)))
    w1p = jnp.pad(w1, ((0, kpad - HY_EMB_DIM), (0, 0)))
    min_decay = math.log(HY_DECAY_TARGET) / HY_SLOW_DECAY_PCT
    max_decay = math.log(HY_DECAY_TARGET) / HY_FAST_DECAY_PCT
    cw = HY_ORDER * HY_WIDTH
    deltas = jnp.abs(jnp.linspace(min_decay, max_decay, cw, dtype=F32)).reshape(1, cw)
    tr = min(n, 1024)
    half = n // tr
    hd = HY_FILTER_HIDDEN
    vec = lambda v: v.reshape(1, hd)
    full = lambda shp: pl.BlockSpec(shp, lambda i: (0, 0))
    return pl.pallas_call(
        functools.partial(_hy_filter_kernel, n=n, tr=tr),
        grid=(2 * half,),
        in_specs=[pl.BlockSpec((tr, kpad), lambda i: (i, 0)), full((kpad, hd)), full((1, hd)), full((1, hd)),
                  full((hd, hd)), full((1, hd)), full((1, hd)),
                  pl.BlockSpec((hd, cw), lambda i: (0, i // half)), full((1, cw))],
        out_specs=[pl.BlockSpec((tr, cw), lambda i: (i, 0)), full((1, cw))],
        out_shape=[jax.ShapeDtypeStruct((2 * n, cw), F32), jax.ShapeDtypeStruct((1, cw), F32)],
        compiler_params=_cparams("arbitrary"),
        name="hy_filter",
    )(feats2, w1p, vec(b1), vec(freq1), w2, vec(b2), vec(freq2), w3, deltas)


def _hy_spec_kernel(a_ref, tw_ref, m2_ref, s_ref, o_ref, *, kb):
    inv = 1.0 / s_ref[...]
    ts = [_twiddle_mul(a_ref[0, j], a_ref[1, j], tw_ref[0, :, j:j + 1], tw_ref[1, :, j:j + 1], False)
          for j in range(kb)]
    xs = [_cplx_apply(m2_ref, tr, ti) for tr, ti in ts]
    for j, (xr, xi) in enumerate(xs):
        o_ref[0, j] = xr * inv
        o_ref[1, j] = xi * inv


def hy_spec(a, tw, m2, abs_sum, kb):
    _, l1, l2, cw = a.shape
    w = HY_WIDTH
    return pl.pallas_call(
        functools.partial(_hy_spec_kernel, kb=kb),
        grid=(l1 // kb, cw // w),
        in_specs=[pl.BlockSpec((2, kb, l2, w), lambda k, c: (0, k, 0, c)),
                  pl.BlockSpec((None, 2, l2, kb), lambda k, c: (k, 0, 0, 0)),
                  pl.BlockSpec(m2.shape, lambda k, c: (0, 0)),
                  pl.BlockSpec((1, w), lambda k, c: (0, c))],
        out_specs=pl.BlockSpec((2, kb, l2, w), lambda k, c: (0, k, 0, c)),
        out_shape=jax.ShapeDtypeStruct((2, l1, l2, cw), F32),
        compiler_params=_cparams("parallel", "parallel"),
        name="hy_spec",
    )(a, tw, m2, abs_sum)


def _hy_mid_kernel(a_ref, tw_ref, m2_ref, m2c_ref, kf_ref, o_ref, *, kb):
    cs = [(tw_ref[0, :, j:j + 1], tw_ref[1, :, j:j + 1]) for j in range(kb)]
    ts = [_twiddle_mul(a_ref[0, j], a_ref[1, j], c, s, False) for j, (c, s) in enumerate(cs)]
    xs = [_cplx_apply(m2_ref, tr, ti) for tr, ti in ts]
    ps = [(xr * kf_ref[0, j] - xi * kf_ref[1, j], xr * kf_ref[1, j] + xi * kf_ref[0, j])
          for j, (xr, xi) in enumerate(xs)]
    bs = [_cplx_apply(m2c_ref, pr, pi) for pr, pi in ps]
    for j, ((br, bi), (c, s)) in enumerate(zip(bs, cs)):
        o_ref[0, j], o_ref[1, j] = _twiddle_mul(br, bi, c, s, True)


def hy_mid(a, tw, m2, m2c, kf, order, kb):
    bsz, _, l1, l2, w = a.shape
    return pl.pallas_call(
        functools.partial(_hy_mid_kernel, kb=kb),
        grid=(bsz, l1 // kb),
        in_specs=[pl.BlockSpec((None, 2, kb, l2, w), lambda b, k: (b, 0, k, 0, 0)),
                  pl.BlockSpec((None, 2, l2, kb), lambda b, k: (k, 0, 0, 0)),
                  pl.BlockSpec(m2.shape, lambda b, k: (0, 0)),
                  pl.BlockSpec(m2c.shape, lambda b, k: (0, 0)),
                  pl.BlockSpec((2, kb, l2, w), lambda b, k: (0, k, 0, order))],
        out_specs=pl.BlockSpec((None, 2, kb, l2, w), lambda b, k: (b, 0, k, 0, 0)),
        out_shape=jax.ShapeDtypeStruct((bsz, 2, l1, l2, w), F32),
        compiler_params=_cparams("parallel", "parallel"),
        name="hy_mid",
    )(a, tw, m2, m2c, kf)


def _hy_out_kernel(m_ref, bp_ref, xo_ref, z_ref, bias_ref, o_ref):
    y = jnp.dot(m_ref[...], bp_ref[...].astype(BF16), preferred_element_type=F32)
    o_ref[...] = xo_ref[...] * (y + bias_ref[...] * z_ref[...])


def hy_out(m, bp, xo, z, bias_row, tn):
    bsz, k, n = bp.shape
    r = m.shape[0]
    blk = pl.BlockSpec((None, r, tn), lambda b, j: (b, 0, j))
    return pl.pallas_call(
        _hy_out_kernel,
        grid=(bsz, n // tn),
        in_specs=[pl.BlockSpec((r, k), lambda b, j: (0, 0)),
                  pl.BlockSpec((None, k, tn), lambda b, j: (b, 0, j)), blk, blk,
                  pl.BlockSpec((1, tn), lambda b, j: (0, j))],
        out_specs=blk,
        out_shape=jax.ShapeDtypeStruct((bsz, r, n), F32),
        compiler_params=_cparams("parallel", "parallel"),
        name="hy_out",
    )(m, bp, xo, z, bias_row)


def hyena_pallas(p, conv_w, w1, b1, freq1, w2, b2, freq2, w3, bias, b, n):
    w = HY_WIDTH
    big = n == 8192
    l1, l2, kb = (128, 128, 8) if big else (1, 2 * n, 1)
    n1 = l1 // 2 if big else 1
    tw = _twiddles(l1, l2, kb)
    m2, m2c = _stage2_matrices(l2)
    x0, x1, v = seq_conv(p, conv_w, b, n)
    kern, abs_sum = hy_filter(n, w1, b1, freq1, w2, b2, freq2, w3)
    cw = HY_ORDER * w
    if big:
        ak = left_mm(_stage1_matrix(l1, l1), kern.reshape(1, l1, l2 * cw), 8192).reshape(2, l1, l2, cw)
        c1, s1 = _cos_sin(n1, l1, l1)
        m_out = _const_bf16(np.concatenate([c1, -s1], axis=1) / (l1 * l2))
    else:
        ak = jnp.stack([kern, jnp.zeros_like(kern)], axis=0).reshape(2, 1, l2, cw)
        m_out = _const_bf16(np.eye(n, 2 * l2) / l2)
    kf = hy_spec(ak, tw, m2, abs_sum, kb)
    z = v
    for order, xo in enumerate((x0, x1)):
        if big:
            a = left_mm(_stage1_matrix(l1, n1), z.reshape(b, n1, l2 * w), 4096).reshape(b, 2, l1, l2, w)
        else:
            zp = jnp.pad(z.reshape(b, 1, 1, n, w), ((0, 0), (0, 0), (0, 0), (0, n), (0, 0)))
            a = jnp.concatenate([zp, jnp.zeros_like(zp)], axis=1)
        bp = hy_mid(a, tw, m2, m2c, kf, order, kb)
        if big:
            bias_row = jnp.tile(bias[order], l2).reshape(1, l2 * w)
            z = hy_out(m_out, bp.reshape(b, 2 * l1, l2 * w), xo.reshape(b, n1, l2 * w), z.reshape(b, n1, l2 * w),
                       bias_row, 4096).reshape(b * n, w)
        else:
            z = hy_out(m_out, bp.reshape(b, 2 * l2, w), xo.reshape(b, n, w), z.reshape(b, n, w),
                       bias[order].reshape(1, w), w).reshape(b * n, w)
    return z


def _prep_w_in(w):
    main = jnp.concatenate([w[:, OFF_GATE:IN_WIDTH], w[:, OFF_Q:OFF_Z], w[:, OFF_HY:OFF_GATE],
                            w[:, OFF_FN:OFF_HY], w[:, OFF_Z:OFF_BETA]], axis=1).astype(BF16)
    small = jnp.pad(w[:, OFF_BETA:OFF_FN], ((0, 0), (0, SMALL_WIDTH - 4 * DN_HEADS))).astype(BF16)
    return main, small


def kernel(x, c, ctx, c_ctx, w_mod, b_mod, norm1, norm2, w_in, dn_conv, dn_a_log, dn_dt_bias,
           dn_out_norm, hy_conv, hy_w1, hy_b1, hy_freq1, hy_w2, hy_b2, hy_freq2, hy_w3, hy_bias,
           w_branch_a, w_branch_b, w_branch_c, w_out, w_ff1, w_ff2, final_norm):
    b, n_lat, d = x.shape
    rows = n_lat // GRID_W
    n_ctx = ctx.shape[1]
    tm_x, tm_c = 1024, n_ctx

    c_rows = jnp.concatenate([c, c_ctx[None], jnp.zeros((8 - b - 1, d), F32)], axis=0)
    mods = mod_vectors(c_rows, w_mod, b_mod)
    s_zero = jnp.zeros((2, b, DN_HEADS, DN_HEAD_DIM, DN_HEAD_DIM), F32)
    h, hc = x.reshape(b * n_lat, d), ctx.reshape(b * n_ctx, d)

    for l in range(DEPTH):
        last = l == DEPTH - 1
        mv = mods[l].reshape(8, N_MOD, 1, d)
        mx = [mv[:b, i] for i in range(N_MOD)]
        mc = [mv[b:b + 1, i] for i in range(N_MOD)]
        w_main, w_small = _prep_w_in(w_in[l])
        wa, wb, wc, wo = (w.astype(BF16) for w in (w_branch_a[l], w_branch_b[l], w_branch_c[l], w_out[l]))
        w1, w2 = w_ff1[l].astype(BF16), w_ff2[l].astype(BF16)
        n1, n2 = norm1[l][None, None, :], norm2[l][None, None, :]

        p_c, small_c = in_proj(hc, n1 * (1.0 + mc[1]), mc[0], w_main, w_small, b * n_ctx, tm_c)
        p_x, small_x = in_proj(h, n1 * (1.0 + mx[1]), mx[0], w_main, w_small, n_lat, tm_x)

        def mix(p, n):
            y_c = hyena_pallas(p, hy_conv[l], hy_w1[l], hy_b1[l], hy_freq1[l], hy_w2[l], hy_b2[l], hy_freq2[l],
                               hy_w3[l], hy_bias[l], b, n)
            return fnet_branch(p, b, n), y_c

        ocf, ocb, s_ctx = delta_branch(p_c, small_c, dn_conv[l], dn_a_log[l], dn_dt_bias[l], s_zero,
                                       b, n_ctx, 1, n_ctx)
        oxf, oxb, _ = delta_branch(p_x, small_x, dn_conv[l], dn_a_log[l], dn_dt_bias[l], s_ctx,
                                   b, n_lat, rows, GRID_W)

        y_b, y_c = mix(p_x, n_lat)
        h = merge(oxf, oxb, p_x, y_b, y_c, h, mx[2], dn_out_norm[l], wa, wb, wc, wo, n_lat, 512)
        h = mlp(h, n2 * (1.0 + mx[4]), mx[3], mx[5], w1, w2, n_lat, tm_x)

        if not last:
            y_b, y_c = mix(p_c, n_ctx)
            hc = merge(ocf, ocb, p_c, y_b, y_c, hc, mc[2], dn_out_norm[l], wa, wb, wc, wo, b * n_ctx, tm_c)
            hc = mlp(hc, n2 * (1.0 + mc[4]), mc[3], mc[5], w1, w2, b * n_ctx, tm_c)

    return final_rms_norm(h, final_norm).reshape(b, n_lat, d)
```

```python
import functools
import math

import jax
import jax.numpy as jnp
import numpy as np
from jax import lax
from jax.experimental import pallas as pl
from jax.experimental.pallas import tpu as pltpu

D_MODEL = 1024
DEPTH = 2
GRID_W = 64
NORM_EPS = 1e-6
N_MOD = 6

DN_HEADS = 4
DN_HEAD_DIM = 128
DN_WIDTH = DN_HEADS * DN_HEAD_DIM
DN_CHUNK = 64
SHORT_CONV = 3

FN_GROUPS = 4
FN_GROUP_DIM = 64
FN_WIDTH = FN_GROUPS * FN_GROUP_DIM

HY_WIDTH = 256
HY_ORDER = 2
HY_EMB_DIM = 33
HY_BANDS = (HY_EMB_DIM - 1) // 2
HY_FILTER_HIDDEN = 64
HY_FAST_DECAY_PCT = 0.3
HY_SLOW_DECAY_PCT = 1.5
HY_DECAY_TARGET = 1e-2

N_BRANCHES = 3
D_FF = 4 * D_MODEL

OFF_Q = 0
OFF_Z = 3 * DN_WIDTH
OFF_BETA = OFF_Z + DN_WIDTH
OFF_A = OFF_BETA + 2 * DN_HEADS
OFF_FN = OFF_A + 2 * DN_HEADS
OFF_HY = OFF_FN + FN_WIDTH
OFF_GATE = OFF_HY + (HY_ORDER + 1) * HY_WIDTH
IN_WIDTH = OFF_GATE + N_BRANCHES * D_MODEL

P_QKV = 0
P_HY = P_QKV + 3 * DN_WIDTH
P_FN = P_HY + (HY_ORDER + 1) * HY_WIDTH
P_Z = P_FN + FN_WIDTH
P_WIDTH = P_Z + DN_WIDTH
SMALL_WIDTH = 128

F32 = jnp.float32
BF16 = jnp.bfloat16
VMEM_LIMIT = 56 * 1024 * 1024


def _cparams(*sem):
    return pltpu.CompilerParams(dimension_semantics=sem, vmem_limit_bytes=VMEM_LIMIT)


def _bdot(a, b):
    return jnp.dot(a.astype(BF16), b.astype(BF16), preferred_element_type=F32)


def _sigmoid(x):
    return 1.0 / (1.0 + jnp.exp(-x))


def _modnorm(xf, gs, sh):
    r = lax.rsqrt(jnp.mean(xf * xf, axis=-1, keepdims=True) + NORM_EPS)
    return xf * r * gs + sh


def _mod_kernel(c_ref, w_ref, b_ref, o_ref):
    c = c_ref[...]
    o_ref[...] = _bdot(c * _sigmoid(c), w_ref[...]) + b_ref[...]


def mod_vectors(c_rows, w_mod, b_mod):
    tn = 1536
    n = N_MOD * D_MODEL
    return pl.pallas_call(
        _mod_kernel,
        grid=(DEPTH, n // tn),
        in_specs=[pl.BlockSpec((8, D_MODEL), lambda l, j: (0, 0)),
                  pl.BlockSpec((None, D_MODEL, tn), lambda l, j: (l, 0, j)),
                  pl.BlockSpec((None, 1, tn), lambda l, j: (l, 0, j))],
        out_specs=pl.BlockSpec((None, 8, tn), lambda l, j: (l, 0, j)),
        out_shape=jax.ShapeDtypeStruct((DEPTH, 8, n), F32),
        compiler_params=_cparams("parallel", "parallel"),
        name="mod_vectors",
    )(c_rows, w_mod, b_mod.reshape(DEPTH, 1, n))


def _in_proj_kernel(x_ref, gs_ref, sh_ref, w_ref, ws_ref, p_ref, small_ref, xn_ref):
    @pl.when(pl.program_id(1) == 0)
    def _():
        xn = _modnorm(x_ref[...], gs_ref[0], sh_ref[0]).astype(BF16)
        xn_ref[...] = xn
        small_ref[...] = jnp.dot(xn, ws_ref[...], preferred_element_type=F32)

    p_ref[...] = jnp.dot(xn_ref[...], w_ref[...], preferred_element_type=F32)


def in_proj(x2d, gs, sh, w_main, w_small, rows_per_mod, tm):
    m = x2d.shape[0]
    tn = 1024
    tiles_per_mod = rows_per_mod // tm
    mod_spec = pl.BlockSpec((1, 1, D_MODEL), lambda i, j: (i // tiles_per_mod, 0, 0))
    return pl.pallas_call(
        _in_proj_kernel,
        grid=(m // tm, P_WIDTH // tn),
        in_specs=[pl.BlockSpec((tm, D_MODEL), lambda i, j: (i, 0)), mod_spec, mod_spec,
                  pl.BlockSpec((D_MODEL, tn), lambda i, j: (0, j)),
                  pl.BlockSpec((D_MODEL, SMALL_WIDTH), lambda i, j: (0, 0))],
        out_specs=[pl.BlockSpec((tm, tn), lambda i, j: (i, j)),
                   pl.BlockSpec((tm, SMALL_WIDTH), lambda i, j: (i, 0))],
        out_shape=[jax.ShapeDtypeStruct((m, P_WIDTH), F32), jax.ShapeDtypeStruct((m, SMALL_WIDTH), F32)],
        scratch_shapes=[pltpu.VMEM((tm, D_MODEL), BF16)],
        compiler_params=_cparams("parallel", "arbitrary"),
        name="in_proj",
    )(x2d, gs, sh, w_main, w_small)


def _merge_kernel(of_ref, ob_ref, z_ref, yb_ref, yc_ref, h_ref, gs_ref, sh_ref, gate_ref, nrm_ref,
                  wg_ref, wa_ref, wb_ref, wc_ref, wo_ref, out_ref):
    xn = _modnorm(h_ref[...], gs_ref[0], sh_ref[0]).astype(BF16)
    o = of_ref[...] + ob_ref[...]
    z = z_ref[...]
    heads = []
    for hd in range(DN_HEADS):
        sl = slice(hd * DN_HEAD_DIM, (hd + 1) * DN_HEAD_DIM)
        oh, zh = o[:, sl], z[:, sl]
        r = lax.rsqrt(jnp.mean(oh * oh, axis=-1, keepdims=True) + NORM_EPS)
        heads.append(oh * r * nrm_ref[...] * (zh * _sigmoid(zh)))
    ya = jnp.concatenate(heads, axis=-1)
    merged = None
    for i, (y, w_ref) in enumerate(((ya, wa_ref), (yb_ref[...], wb_ref), (yc_ref[...], wc_ref))):
        g = jnp.dot(xn, wg_ref[:, i * D_MODEL:(i + 1) * D_MODEL], preferred_element_type=F32)
        term = _sigmoid(g) * _bdot(y, w_ref[...])
        merged = term if merged is None else merged + term
    out_ref[...] = h_ref[...] + gate_ref[0] * _bdot(merged, wo_ref[...])


def merge(o_f, o_b, p, y_b, y_c, h2d, gs, sh, gate, dn_out_norm, wg, wa, wb, wc, wo, rows_per_mod, tm):
    m = h2d.shape[0]
    tiles_per_mod = rows_per_mod // tm
    row = lambda w: pl.BlockSpec((tm, w), lambda i: (i, 0))
    full = lambda a: pl.BlockSpec(a.shape, lambda i: (0,) * a.ndim, pipeline_mode=pl.Buffered(1))
    mod_spec = pl.BlockSpec((1, 1, D_MODEL), lambda i: (i // tiles_per_mod, 0, 0))
    nrm = dn_out_norm.reshape(1, DN_HEAD_DIM)
    return pl.pallas_call(
        _merge_kernel,
        grid=(m // tm,),
        in_specs=[row(DN_WIDTH), row(DN_WIDTH),
                  pl.BlockSpec((tm, DN_WIDTH), lambda i: (i, P_Z // DN_WIDTH)),
                  row(FN_WIDTH), row(HY_WIDTH), row(D_MODEL), mod_spec, mod_spec, mod_spec,
                  full(nrm), full(wg), full(wa), full(wb), full(wc), full(wo)],
        out_specs=row(D_MODEL),
        out_shape=jax.ShapeDtypeStruct((m, D_MODEL), F32),
        compiler_params=_cparams("parallel"),
        name="merge",
    )(o_f, o_b, p, y_b, y_c, h2d, gs, sh, gate, nrm, wg, wa, wb, wc, wo)


def _mlp_kernel(h_ref, gs_ref, sh_ref, gate_ref, w1_ref, w2_ref, out_ref, xn_ref, acc_ref):
    j = pl.program_id(1)

    @pl.when(j == 0)
    def _():
        xn_ref[...] = _modnorm(h_ref[...], gs_ref[0], sh_ref[0]).astype(BF16)
        acc_ref[...] = jnp.zeros_like(acc_ref)

    a = jnp.maximum(jnp.dot(xn_ref[...], w1_ref[...], preferred_element_type=F32), 0.0)
    acc_ref[...] += jnp.dot((a * a).astype(BF16), w2_ref[...], preferred_element_type=F32)

    @pl.when(j == pl.num_programs(1) - 1)
    def _():
        out_ref[...] = h_ref[...] + gate_ref[0] * acc_ref[...]


def mlp(h2d, gs, sh, gate, w1, w2, rows_per_mod, tm):
    m = h2d.shape[0]
    tf = 512
    tiles_per_mod = rows_per_mod // tm
    mod_spec = pl.BlockSpec((1, 1, D_MODEL), lambda i, j: (i // tiles_per_mod, 0, 0))
    return pl.pallas_call(
        _mlp_kernel,
        grid=(m // tm, D_FF // tf),
        in_specs=[pl.BlockSpec((tm, D_MODEL), lambda i, j: (i, 0)), mod_spec, mod_spec, mod_spec,
                  pl.BlockSpec((D_MODEL, tf), lambda i, j: (0, j)),
                  pl.BlockSpec((tf, D_MODEL), lambda i, j: (j, 0))],
        out_specs=pl.BlockSpec((tm, D_MODEL), lambda i, j: (i, 0)),
        out_shape=jax.ShapeDtypeStruct((m, D_MODEL), F32),
        scratch_shapes=[pltpu.VMEM((tm, D_MODEL), BF16), pltpu.VMEM((tm, D_MODEL), F32)],
        compiler_params=_cparams("parallel", "arbitrary"),
        name="mlp",
    )(h2d, gs, sh, gate, w1, w2)


def _final_norm_kernel(h_ref, g_ref, o_ref):
    xf = h_ref[...]
    o_ref[...] = xf * lax.rsqrt(jnp.mean(xf * xf, axis=-1, keepdims=True) + NORM_EPS) * g_ref[...]


def final_rms_norm(h2d, gain):
    m, d = h2d.shape
    tm = 512
    return pl.pallas_call(
        _final_norm_kernel,
        grid=(m // tm,),
        in_specs=[pl.BlockSpec((tm, d), lambda i: (i, 0)), pl.BlockSpec((1, d), lambda i: (0, 0))],
        out_specs=pl.BlockSpec((tm, d), lambda i: (i, 0)),
        out_shape=jax.ShapeDtypeStruct((m, d), F32),
        compiler_params=_cparams("parallel"),
        name="final_norm",
    )(h2d, gain.reshape(1, d))


def _dnconv_kernel(prev_ref, cur_ref, next_ref, w_ref, o_ref, *, cols, n_tiles):
    t, j = pl.program_id(1), pl.program_id(2)
    tt = cur_ref.shape[0]
    prev = jnp.where(t == 0, 0.0, prev_ref[...])
    nxt = jnp.where(t == n_tiles - 1, 0.0, next_ref[...])
    ext = jnp.concatenate([prev, cur_ref[...], nxt], axis=0)
    n_ext = tt + 2 * cols
    col = lax.broadcasted_iota(jnp.int32, (n_ext, 1), 0) % cols
    left = jnp.where(col == 0, 0.0, pltpu.roll(ext, 1, axis=0))
    right = jnp.where(col == cols - 1, 0.0, pltpu.roll(ext, n_ext - 1, axis=0))
    acc = jnp.zeros((tt, DN_WIDTH), F32)
    for dr in range(SHORT_CONV):
        base = dr * cols
        acc = (acc + w_ref[3 * dr:3 * dr + 1, :] * left[base:base + tt]
               + w_ref[3 * dr + 1:3 * dr + 2, :] * ext[base:base + tt]
               + w_ref[3 * dr + 2:3 * dr + 3, :] * right[base:base + tt])
    y = acc * _sigmoid(acc)
    q_scale = jnp.where(j == 0, DN_HEAD_DIM ** -0.5, 1.0)
    for hd in range(DN_HEADS):
        sl = slice(hd * DN_HEAD_DIM, (hd + 1) * DN_HEAD_DIM)
        yh = y[:, sl]
        nrm = lax.rsqrt(jnp.sum(yh * yh, axis=-1, keepdims=True) + NORM_EPS) * q_scale
        o_ref[:, sl] = yh * jnp.where(j < 2, nrm, 1.0)


def dn_conv_prep(p, conv_w, b, n, rows, cols):
    tr = min(rows, 16)
    tt = tr * cols
    n_tiles = rows // tr
    nblk = b * n // cols
    c0 = P_QKV // DN_WIDTH
    return pl.pallas_call(
        functools.partial(_dnconv_kernel, cols=cols, n_tiles=n_tiles),
        grid=(b, n_tiles, 3),
        in_specs=[pl.BlockSpec((cols, DN_WIDTH),
                               lambda bi, t, j: (jnp.maximum(bi * rows + t * tr - 1, 0), c0 + j)),
                  pl.BlockSpec((tt, DN_WIDTH), lambda bi, t, j: (bi * n_tiles + t, c0 + j)),
                  pl.BlockSpec((cols, DN_WIDTH),
                               lambda bi, t, j: (jnp.minimum(bi * rows + (t + 1) * tr, nblk - 1), c0 + j)),
                  pl.BlockSpec((SHORT_CONV * SHORT_CONV, DN_WIDTH), lambda bi, t, j: (0, j))],
        out_specs=pl.BlockSpec((tt, DN_WIDTH), lambda bi, t, j: (bi * n_tiles + t, j)),
        out_shape=jax.ShapeDtypeStruct((b * n, 3 * DN_WIDTH), F32),
        compiler_params=_cparams("parallel", "parallel", "parallel"),
        name="dn_conv_prep",
    )(p, p, p, conv_w.reshape(SHORT_CONV * SHORT_CONV, 3 * DN_WIDTH))


PREP_CHUNKS = 2


def _softplus(x):
    return jnp.maximum(x, 0.0) + jnp.log(1.0 + jnp.exp(-jnp.abs(x)))


def _delta_prep_kernel(qkv_ref, sm_ref, smt_ref, prm_ref, prmt_ref, u0_ref, lhs1_ref, lhs2_ref, gl_ref):
    cc = DN_CHUNK
    nh = DN_HEADS
    sm = sm_ref[...]
    beta_all = _sigmoid(sm)
    g_all = -jnp.exp(prm_ref[0:1, :]) * _softplus(sm + prm_ref[1:2, :])
    gt_all = -jnp.exp(prmt_ref[:, 0:1]) * _softplus(smt_ref[...] + prmt_ref[:, 1:2])
    lane = lax.broadcasted_iota(jnp.int32, gt_all.shape, 1) % cc
    gt_f, gt_b = gt_all, gt_all
    s = 1
    while s < cc:
        gt_f = gt_f + jnp.where(lane >= s, pltpu.roll(gt_f, s, axis=1), 0.0)
        gt_b = gt_b + jnp.where(lane < cc - s, pltpu.roll(gt_b, gt_all.shape[1] - s, axis=1), 0.0)
        s *= 2
    ri = lax.broadcasted_iota(jnp.int32, (cc, cc), 0)
    ci_ = lax.broadcasted_iota(jnp.int32, (cc, cc), 1)
    sub = lax.broadcasted_iota(jnp.int32, (cc, 1), 0)
    blk = lambda s: (ri // s) == (ci_ // s)
    leaf = 8

    heads, chains = [], []
    for ci in range(PREP_CHUNKS):
        rows = slice(ci * cc, (ci + 1) * cc)
        gc_f = g_all[rows]
        gc_b = gc_f
        s = 1
        while s < cc:
            gc_f = gc_f + jnp.where(sub >= s, pltpu.roll(gc_f, s, axis=0), 0.0)
            gc_b = gc_b + jnp.where(sub < cc - s, pltpu.roll(gc_b, cc - s, axis=0), 0.0)
            s *= 2
        for h in range(nh):
            q = qkv_ref[rows, h * 128:(h + 1) * 128]
            k = qkv_ref[rows, (nh + h) * 128:(nh + h + 1) * 128]
            v = qkv_ref[rows, (2 * nh + h) * 128:(2 * nh + h + 1) * 128]
            heads.append((ci, rows, h, q, k, v, gc_f, gc_b))
    qkks = [lax.dot_general(jnp.concatenate([q, k], axis=0).astype(BF16), k.astype(BF16),
                            (((1,), (1,)), ((), ())), preferred_element_type=F32)
            for (_, _, _, q, k, _, _, _) in heads]
    for (ci, rows, h, q, k, v, gc_f, gc_b), qkk in zip(heads, qkks):
        qk, kk = qkk[:cc], qkk[cc:]
        for d in range(2):
            cb_, ca_ = d * nh + h, 2 * nh + d * nh + h
            beta = beta_all[rows, cb_:cb_ + 1]
            g_col = (gc_f if d == 0 else gc_b)[:, ca_:ca_ + 1]
            g_row = (gt_f if d == 0 else gt_b)[ca_:ca_ + 1, rows]
            incl = (ri >= ci_) if d == 0 else (ri <= ci_)
            strict = (ri > ci_) if d == 0 else (ri < ci_)
            decay = jnp.exp(jnp.where(incl, g_col - g_row, -1e30))
            a = jnp.where(strict, beta * kk * decay, 0.0)
            eg = jnp.exp(g_col)
            g_last = g_col[cc - 1:cc] if d == 0 else g_col[0:1]
            chains.append(dict(ci=ci, rows=rows, h=h, d=d, a=a, qkd=qk * decay, g_last=g_last,
                               rhs=jnp.concatenate([v * beta, k * (beta * eg)], axis=1),
                               q_d=q * eg, k_d=k * jnp.exp(g_last - g_col)))

    pws = [jnp.where(blk(leaf), c["a"], 0.0) for c in chains]
    devs = [-pw for pw in pws]
    for _ in range(2):
        pws = [_bdot(pw, pw) for pw in pws]
        cross = [_bdot(dev, pw) for dev, pw in zip(devs, pws)]
        devs = [dev + pw + x for dev, pw, x in zip(devs, pws, cross)]
    s = leaf
    while s < cc:
        offs = [jnp.where(blk(2 * s) & jnp.logical_not(blk(s)), c["a"], 0.0) for c in chains]
        xs = [off + _bdot(dev, off) for dev, off in zip(devs, offs)]
        devs = [dev - x - _bdot(x, dev) for dev, x in zip(devs, xs)]
        s *= 2
    uws = [c["rhs"] + _bdot(dev, c["rhs"]) for dev, c in zip(devs, chains)]
    for c, uw in zip(chains, uws):
        d, h, ci = c["d"], c["h"], c["ci"]
        u0_ref[d, h, c["rows"], :] = uw[:, :128]
        lhs1_ref[d, h, ci] = jnp.concatenate([uw[:, 128:], c["q_d"]], axis=0).astype(BF16)
        lhs2_ref[d, h, ci] = jnp.concatenate([c["qkd"], c["k_d"].T], axis=0).astype(BF16)
        gl_ref[d, h, ci] = jnp.broadcast_to(jnp.exp(c["g_last"]), (8, 128))


def delta_prep(qkv, small, small_t, prm, prm_t, b, n):
    nc = n // DN_CHUNK
    cb = PREP_CHUNKS
    nblk = nc // cb
    tt = cb * DN_CHUNK
    nh = DN_HEADS
    return pl.pallas_call(
        _delta_prep_kernel,
        grid=(b, nblk),
        in_specs=[pl.BlockSpec((tt, 3 * DN_WIDTH), lambda bi, c: (bi * nblk + c, 0)),
                  pl.BlockSpec((tt, SMALL_WIDTH), lambda bi, c: (bi * nblk + c, 0)),
                  pl.BlockSpec((4 * nh, tt), lambda bi, c: (0, bi * nblk + c)),
                  pl.BlockSpec((2, SMALL_WIDTH), lambda bi, c: (0, 0)),
                  pl.BlockSpec((4 * nh, 2), lambda bi, c: (0, 0))],
        out_specs=[pl.BlockSpec((2, None, nh, tt, 128), lambda bi, c: (0, bi, 0, c, 0)),
                   pl.BlockSpec((2, None, nh, cb, 128, 128), lambda bi, c: (0, bi, 0, c, 0, 0)),
                   pl.BlockSpec((2, None, nh, cb, 192, 64), lambda bi, c: (0, bi, 0, c, 0, 0)),
                   pl.BlockSpec((2, None, nh, cb, 8, 128), lambda bi, c: (0, bi, 0, c, 0, 0))],
        out_shape=[jax.ShapeDtypeStruct((2, b, nh, n, 128), F32),
                   jax.ShapeDtypeStruct((2, b, nh, nc, 128, 128), BF16),
                   jax.ShapeDtypeStruct((2, b, nh, nc, 192, 64), BF16),
                   jax.ShapeDtypeStruct((2, b, nh, nc, 8, 128), F32)],
        compiler_params=_cparams("parallel", "parallel"),
        name="delta_prep",
    )(qkv, small, small_t, prm, prm_t)


def _delta_scan_kernel(u0f_ref, u0b_ref, l1f_ref, l1b_ref, l2f_ref, l2b_ref, glf_ref, glb_ref, s0_ref,
                       of_ref, ob_ref, sout_ref, st_ref, *, sc, nb):
    n = pl.program_id(0)
    cc = DN_CHUNK

    @pl.when(n == 0)
    def _():
        st_ref[...] = s0_ref[...]

    dirs = ((u0f_ref, l1f_ref, l2f_ref, glf_ref, of_ref), (u0b_ref, l1b_ref, l2b_ref, glb_ref, ob_ref))

    def body(i, carry):
        chains = []
        for d in range(2):
            ci = i if d == 0 else sc - 1 - i
            r0 = pl.multiple_of(ci * cc, cc)
            chains += [(d, bi, h, ci, r0) for bi in range(nb) for h in range(DN_HEADS)]
        sts = [st_ref[d, bi, h] for (d, bi, h, _, _) in chains]
        r1s = [jnp.dot(dirs[d][1][bi, h, ci], st.astype(BF16), preferred_element_type=F32)
               for (d, bi, h, ci, _), st in zip(chains, sts)]
        us = [dirs[d][0][bi, h, pl.ds(r0, cc), :] - r1[:cc] for (d, bi, h, _, r0), r1 in zip(chains, r1s)]
        r2s = [jnp.dot(dirs[d][2][bi, h, ci], u.astype(BF16), preferred_element_type=F32)
               for (d, bi, h, ci, _), u in zip(chains, us)]
        for (d, bi, h, ci, r0), st, r1, r2 in zip(chains, sts, r1s, r2s):
            dirs[d][4][bi, pl.ds(r0, cc), h * 128:(h + 1) * 128] = r1[cc:] + r2[:cc]
            st_ref[d, bi, h] = st * dirs[d][3][bi, h, ci, 0:1, :] + r2[cc:]
        return carry

    lax.fori_loop(0, sc, body, 0)

    @pl.when(n == pl.num_programs(0) - 1)
    def _():
        sout_ref[...] = st_ref[...]


def delta_scan(u0, lhs1, lhs2, gl, s0, b, n):
    nc = n // DN_CHUNK
    sc = 4
    nblk = nc // sc
    nh = DN_HEADS
    tt = sc * DN_CHUNK
    fwd = lambda i: i
    bwd = lambda i: nblk - 1 - i

    def specs(d, blk):
        return [pl.BlockSpec((None, b, nh, tt, 128), lambda i: (d, 0, 0, blk(i), 0)),
                pl.BlockSpec((None, b, nh, sc, 128, 128), lambda i: (d, 0, 0, blk(i), 0, 0)),
                pl.BlockSpec((None, b, nh, sc, 192, 64), lambda i: (d, 0, 0, blk(i), 0, 0)),
                pl.BlockSpec((None, b, nh, sc, 8, 128), lambda i: (d, 0, 0, blk(i), 0, 0))]

    sf, sb = specs(0, fwd), specs(1, bwd)
    in_specs = [sf[0], sb[0], sf[1], sb[1], sf[2], sb[2], sf[3], sb[3],
                pl.BlockSpec((2, b, nh, 128, 128), lambda i: (0, 0, 0, 0, 0))]
    return pl.pallas_call(
        functools.partial(_delta_scan_kernel, sc=sc, nb=b),
        grid=(nblk,),
        in_specs=in_specs,
        out_specs=[pl.BlockSpec((b, tt, DN_WIDTH), lambda i: (0, fwd(i), 0)),
                   pl.BlockSpec((b, tt, DN_WIDTH), lambda i: (0, bwd(i), 0)),
                   pl.BlockSpec((2, b, nh, 128, 128), lambda i: (0, 0, 0, 0, 0))],
        out_shape=[jax.ShapeDtypeStruct((b, n, DN_WIDTH), F32), jax.ShapeDtypeStruct((b, n, DN_WIDTH), F32),
                   jax.ShapeDtypeStruct((2, b, nh, 128, 128), F32)],
        scratch_shapes=[pltpu.VMEM((2, b, nh, 128, 128), F32)],
        compiler_params=_cparams("arbitrary"),
        name="delta_scan",
    )(u0, u0, lhs1, lhs1, lhs2, lhs2, gl, gl, s0)


def delta_branch(p, small, conv_w, a_log, dt_bias, s0, b, n, rows, cols):
    nh = DN_HEADS
    qkv = dn_conv_prep(p, conv_w, b, n, rows, cols)
    small_t = small[:, :4 * nh].T
    rate = jnp.concatenate([jnp.zeros((2 * nh,), F32), a_log.reshape(-1)])
    bias = jnp.concatenate([jnp.zeros((2 * nh,), F32), dt_bias.reshape(-1)])
    prm_t = jnp.stack([rate, bias], axis=1)
    prm = jnp.pad(prm_t.T, ((0, 0), (0, SMALL_WIDTH - 4 * nh)))
    u0, lhs1, lhs2, gl = delta_prep(qkv, small, small_t, prm, prm_t, b, n)
    o_f, o_b, s_out = delta_scan(u0, lhs1, lhs2, gl, s0, b, n)
    return o_f.reshape(b * n, DN_WIDTH), o_b.reshape(b * n, DN_WIDTH), s_out


def _cos_sin(rows, cols, period):
    ang = 2.0 * np.pi * ((np.arange(rows)[:, None] * np.arange(cols)[None, :]) % period) / period
    return np.cos(ang), np.sin(ang)


def _const_bf16(a):
    return jnp.asarray(a, F32).astype(BF16)


def _stage1_matrix(l1, n1_used):
    c, s = _cos_sin(l1, n1_used, l1)
    return _const_bf16(np.concatenate([c, -s], axis=0))


def _stage2_matrices(l2):
    c, s = _cos_sin(l2, l2, l2)
    fwd = np.block([[c, s], [-s, c]])
    inv = np.block([[c, -s], [s, c]])
    return _const_bf16(fwd), _const_bf16(inv)


def _twiddles(l1, l2, kb):
    ang = 2.0 * np.pi * ((np.arange(l2)[:, None] * np.arange(l1)[None, :]) % (l1 * l2)) / (l1 * l2)
    tw = np.stack([np.cos(ang), np.sin(ang)], axis=0).reshape(2, l2, l1 // kb, kb)
    return jnp.asarray(np.transpose(tw, (2, 0, 1, 3)), F32)


def _left_mm_kernel(m_ref, x_ref, o_ref):
    o_ref[...] = jnp.dot(m_ref[...], x_ref[...].astype(BF16), preferred_element_type=F32)


def left_mm(m, x, tn):
    bsz, k, n = x.shape
    r = m.shape[0]
    return pl.pallas_call(
        _left_mm_kernel,
        grid=(bsz, n // tn),
        in_specs=[pl.BlockSpec((r, k), lambda b, j: (0, 0)),
                  pl.BlockSpec((None, k, tn), lambda b, j: (b, 0, j))],
        out_specs=pl.BlockSpec((None, r, tn), lambda b, j: (b, 0, j)),
        out_shape=jax.ShapeDtypeStruct((bsz, r, n), F32),
        compiler_params=_cparams("parallel", "parallel"),
        name="left_mm",
    )(m, x)


def _twiddle_mul(ar, ai, c, s, conj):
    if conj:
        return ar * c - ai * s, ai * c + ar * s
    return ar * c + ai * s, ai * c - ar * s


def _cplx_apply(m_ref, re, im):
    half = re.shape[0]
    out = jnp.dot(m_ref[...], jnp.concatenate([re, im], axis=0).astype(BF16), preferred_element_type=F32)
    return out[:half], out[half:]


def _fnet_mid_kernel(a_ref, tw_ref, m2_ref, cs_ref, o_ref, *, kb, ch):
    ts = [_twiddle_mul(a_ref[0, j], a_ref[1, j], tw_ref[0, :, j:j + 1], tw_ref[1, :, j:j + 1], False)
          for j in range(kb)]
    us = [_cplx_apply(m2_ref, tr, ti) for tr, ti in ts]
    ys = [jnp.dot(jnp.concatenate([ur, ui], axis=1).astype(BF16), cs_ref[...], preferred_element_type=F32)
          for ur, ui in us]
    for j, y in enumerate(ys):
        o_ref[:, j * ch:(j + 1) * ch] = y


def fnet_mid(a, tw, m2, cs, kb):
    bsz, _, l1, l2, ch = a.shape
    return pl.pallas_call(
        functools.partial(_fnet_mid_kernel, kb=kb, ch=ch),
        grid=(bsz, l1 // kb),
        in_specs=[pl.BlockSpec((None, 2, kb, l2, ch), lambda b, k: (b, 0, k, 0, 0)),
                  pl.BlockSpec((None, 2, l2, kb), lambda b, k: (k, 0, 0, 0)),
                  pl.BlockSpec(m2.shape, lambda b, k: (0, 0)),
                  pl.BlockSpec(cs.shape, lambda b, k: (0, 0))],
        out_specs=pl.BlockSpec((None, l2, kb * ch), lambda b, k: (b, 0, k)),
        out_shape=jax.ShapeDtypeStruct((bsz, l2, l1 * ch), F32),
        compiler_params=_cparams("parallel", "parallel"),
        name="fnet_mid",
    )(a, tw, m2, cs)


def _fnet_channel_matrix(n):
    c, s = _cos_sin(FN_GROUP_DIM, FN_GROUP_DIM, FN_GROUP_DIM)
    eye = np.eye(FN_GROUPS)
    scale = 1.0 / math.sqrt(n * FN_GROUP_DIM)
    return _const_bf16(np.concatenate([np.kron(eye, c), np.kron(eye, s)], axis=0) * scale)


def fnet_branch(p, b, n):
    ch = FN_WIDTH
    x = lax.slice_in_dim(p, P_FN, P_FN + ch, axis=1)
    l1, l2, kb = (64, 128, 8) if n == 8192 else (1, n, 1)
    if l1 > 1:
        a = left_mm(_stage1_matrix(l1, l1), x.reshape(b, l1, l2 * ch), 4096).reshape(b, 2, l1, l2, ch)
    else:
        xr = x.reshape(b, 1, 1, l2, ch)
        a = jnp.concatenate([xr, jnp.zeros_like(xr)], axis=1)
    y = fnet_mid(a, _twiddles(l1, l2, kb), _stage2_matrices(l2)[0], _fnet_channel_matrix(n), kb)
    return y.reshape(b * n, ch)


def _seq_conv_kernel(prev_ref, cur_ref, next_ref, w_ref, o_ref, *, n_tiles):
    t = pl.program_id(1)
    tt = cur_ref.shape[0]
    cur = cur_ref[...]
    row = lax.broadcasted_iota(jnp.int32, (tt, 1), 0)
    before = jnp.where(t == 0, 0.0, prev_ref[7:8, :])
    after = jnp.where(t == n_tiles - 1, 0.0, next_ref[0:1, :])
    left = jnp.where(row == 0, before, pltpu.roll(cur, 1, axis=0))
    right = jnp.where(row == tt - 1, after, pltpu.roll(cur, tt - 1, axis=0))
    o_ref[...] = w_ref[0:1, :] * left + w_ref[1:2, :] * cur + w_ref[2:3, :] * right


def seq_conv(p, conv_w, b, n):
    tt = min(n, 1024)
    n_tiles = n // tt
    w = HY_WIDTH
    c0 = P_HY // w
    outs = []
    for part in range(HY_ORDER + 1):
        outs.append(pl.pallas_call(
            functools.partial(_seq_conv_kernel, n_tiles=n_tiles),
            grid=(b, n_tiles),
            in_specs=[pl.BlockSpec((8, w), lambda bi, t: (jnp.maximum((bi * n_tiles + t) * (tt // 8) - 1, 0),
                                                          c0 + part)),
                      pl.BlockSpec((tt, w), lambda bi, t: (bi * n_tiles + t, c0 + part)),
                      pl.BlockSpec((8, w), lambda bi, t: (jnp.minimum((bi * n_tiles + t + 1) * (tt // 8),
                                                                      b * n // 8 - 1), c0 + part)),
                      pl.BlockSpec((SHORT_CONV, w), lambda bi, t: (0, part))],
            out_specs=pl.BlockSpec((tt, w), lambda bi, t: (bi * n_tiles + t, 0)),
            out_shape=jax.ShapeDtypeStruct((b * n, w), F32),
            compiler_params=_cparams("parallel", "parallel"),
            name="hy_seq_conv",
        )(p, p, p, conv_w))
    return outs


def _hdot(a, b):
    return jnp.dot(a, b, preferred_element_type=F32, precision=lax.Precision.HIGHEST)


def _hy_filter_kernel(f_ref, w1_ref, b1_ref, f1_ref, w2_ref, b2_ref, f2_ref, w3_ref, dl_ref, k_ref, s_ref, *, n, tr):
    i = pl.program_id(0)
    feats = f_ref[...]
    hid = jnp.sin(f1_ref[...] * (_hdot(feats, w1_ref[...]) + b1_ref[...]))
    hid = jnp.sin(f2_ref[...] * (_hdot(hid, w2_ref[...]) + b2_ref[...]))
    filt = _hdot(hid, w3_ref[...]) * jnp.exp(-feats[:, 0:1] * dl_ref[...])
    row = i * tr + lax.broadcasted_iota(jnp.int32, (tr, 1), 0)
    filt = jnp.where(row == n, 0.0, filt)
    k_ref[...] = filt

    @pl.when(i == 0)
    def _():
        s_ref[...] = jnp.zeros_like(s_ref)

    s_ref[...] += jnp.sum(jnp.abs(filt), axis=0, keepdims=True)


def hy_filter(n, w1, b1, freq1, w2, b2, freq2, w3):
    pos = jnp.arange(n, dtype=F32)
    t = pos / max(n - 1, 1)
    bands = jnp.linspace(1e-4, HY_BANDS - 1, HY_BANDS, dtype=F32)
    ang = (2.0 * math.pi / n) * pos[:, None] * bands[None, :]
    feats = jnp.concatenate([t[:, None], jnp.cos(ang), -jnp.sin(ang)], axis=-1)
    feats2 = jnp.concatenate([feats, feats[:1], feats[:0:-1]], axis=0)
    kpad = 128
    feats2 = jnp.pad(feats2, ((0, 0), (0, kpad - HY_EMB_DIM)))
    w1p = jnp.pad(w1, ((0, kpad - HY_EMB_DIM), (0, 0)))
    min_decay = math.log(HY_DECAY_TARGET) / HY_SLOW_DECAY_PCT
    max_decay = math.log(HY_DECAY_TARGET) / HY_FAST_DECAY_PCT
    cw = HY_ORDER * HY_WIDTH
    deltas = jnp.abs(jnp.linspace(min_decay, max_decay, cw, dtype=F32)).reshape(1, cw)
    tr = min(n, 1024)
    half = n // tr
    hd = HY_FILTER_HIDDEN
    vec = lambda v: v.reshape(1, hd)
    full = lambda shp: pl.BlockSpec(shp, lambda i: (0, 0))
    return pl.pallas_call(
        functools.partial(_hy_filter_kernel, n=n, tr=tr),
        grid=(2 * half,),
        in_specs=[pl.BlockSpec((tr, kpad), lambda i: (i, 0)), full((kpad, hd)), full((1, hd)), full((1, hd)),
                  full((hd, hd)), full((1, hd)), full((1, hd)),
                  pl.BlockSpec((hd, cw), lambda i: (0, i // half)), full((1, cw))],
        out_specs=[pl.BlockSpec((tr, cw), lambda i: (i, 0)), full((1, cw))],
        out_shape=[jax.ShapeDtypeStruct((2 * n, cw), F32), jax.ShapeDtypeStruct((1, cw), F32)],
        compiler_params=_cparams("arbitrary"),
        name="hy_filter",
    )(feats2, w1p, vec(b1), vec(freq1), w2, vec(b2), vec(freq2), w3, deltas)


def _hy_spec_kernel(a_ref, tw_ref, m2_ref, s_ref, o_ref, *, kb):
    inv = 1.0 / s_ref[...]
    ts = [_twiddle_mul(a_ref[0, j], a_ref[1, j], tw_ref[0, :, j:j + 1], tw_ref[1, :, j:j + 1], False)
          for j in range(kb)]
    xs = [_cplx_apply(m2_ref, tr, ti) for tr, ti in ts]
    for j, (xr, xi) in enumerate(xs):
        o_ref[0, j] = xr * inv
        o_ref[1, j] = xi * inv


def hy_spec(a, tw, m2, abs_sum, kb):
    _, l1, l2, cw = a.shape
    w = HY_WIDTH
    return pl.pallas_call(
        functools.partial(_hy_spec_kernel, kb=kb),
        grid=(l1 // kb, cw // w),
        in_specs=[pl.BlockSpec((2, kb, l2, w), lambda k, c: (0, k, 0, c)),
                  pl.BlockSpec((None, 2, l2, kb), lambda k, c: (k, 0, 0, 0)),
                  pl.BlockSpec(m2.shape, lambda k, c: (0, 0)),
                  pl.BlockSpec((1, w), lambda k, c: (0, c))],
        out_specs=pl.BlockSpec((2, kb, l2, w), lambda k, c: (0, k, 0, c)),
        out_shape=jax.ShapeDtypeStruct((2, l1, l2, cw), F32),
        compiler_params=_cparams("parallel", "parallel"),
        name="hy_spec",
    )(a, tw, m2, abs_sum)


def _hy_mid_kernel(a_ref, tw_ref, m2_ref, m2c_ref, kf_ref, o_ref, *, kb):
    cs = [(tw_ref[0, :, j:j + 1], tw_ref[1, :, j:j + 1]) for j in range(kb)]
    ts = [_twiddle_mul(a_ref[0, j], a_ref[1, j], c, s, False) for j, (c, s) in enumerate(cs)]
    xs = [_cplx_apply(m2_ref, tr, ti) for tr, ti in ts]
    ps = [(xr * kf_ref[0, j] - xi * kf_ref[1, j], xr * kf_ref[1, j] + xi * kf_ref[0, j])
          for j, (xr, xi) in enumerate(xs)]
    bs = [_cplx_apply(m2c_ref, pr, pi) for pr, pi in ps]
    for j, ((br, bi), (c, s)) in enumerate(zip(bs, cs)):
        o_ref[0, j], o_ref[1, j] = _twiddle_mul(br, bi, c, s, True)


def hy_mid(a, tw, m2, m2c, kf, order, kb):
    bsz, _, l1, l2, w = a.shape
    return pl.pallas_call(
        functools.partial(_hy_mid_kernel, kb=kb),
        grid=(bsz, l1 // kb),
        in_specs=[pl.BlockSpec((None, 2, kb, l2, w), lambda b, k: (b, 0, k, 0, 0)),
                  pl.BlockSpec((None, 2, l2, kb), lambda b, k: (k, 0, 0, 0)),
                  pl.BlockSpec(m2.shape, lambda b, k: (0, 0)),
                  pl.BlockSpec(m2c.shape, lambda b, k: (0, 0)),
                  pl.BlockSpec((2, kb, l2, w), lambda b, k: (0, k, 0, order))],
        out_specs=pl.BlockSpec((None, 2, kb, l2, w), lambda b, k: (b, 0, k, 0, 0)),
        out_shape=jax.ShapeDtypeStruct((bsz, 2, l1, l2, w), F32),
        compiler_params=_cparams("parallel", "parallel"),
        name="hy_mid",
    )(a, tw, m2, m2c, kf)


def _hy_out_kernel(m_ref, bp_ref, xo_ref, z_ref, bias_ref, o_ref):
    y = jnp.dot(m_ref[...], bp_ref[...].astype(BF16), preferred_element_type=F32)
    o_ref[...] = xo_ref[...] * (y + bias_ref[...] * z_ref[...])


def hy_out(m, bp, xo, z, bias_row, tn):
    bsz, k, n = bp.shape
    r = m.shape[0]
    blk = pl.BlockSpec((None, r, tn), lambda b, j: (b, 0, j))
    return pl.pallas_call(
        _hy_out_kernel,
        grid=(bsz, n // tn),
        in_specs=[pl.BlockSpec((r, k), lambda b, j: (0, 0)),
                  pl.BlockSpec((None, k, tn), lambda b, j: (b, 0, j)), blk, blk,
                  pl.BlockSpec((1, tn), lambda b, j: (0, j))],
        out_specs=blk,
        out_shape=jax.ShapeDtypeStruct((bsz, r, n), F32),
        compiler_params=_cparams("parallel", "parallel"),
        name="hy_out",
    )(m, bp, xo, z, bias_row)


SUB = 8
HY_KB = 16


def _kron_stage1(l1, n1_used, kb):
    c, s = _cos_sin(l1, n1_used, l1)
    m = np.stack([c, -s], axis=0).reshape(2, l1 // kb, kb, n1_used)
    m = np.transpose(m, (1, 0, 2, 3)).reshape(l1 // kb, 2 * kb, n1_used)
    return _const_bf16(np.stack([np.kron(blk, np.eye(SUB)) for blk in m]))


def _kron_stage_out(l1, n1_used, kb):
    c, s = _cos_sin(n1_used, l1, l1)
    m = np.stack([c, -s], axis=1).reshape(n1_used, 2, l1 // kb, kb) / (l1 * l1)
    m = np.transpose(m, (2, 0, 1, 3)).reshape(l1 // kb, n1_used, 2 * kb)
    return _const_bf16(np.stack([np.kron(blk, np.eye(SUB)) for blk in m]))


def _strided_stage_in(kin_ref, src_ref, a_ref):
    n_sub, width = src_ref.shape[1] // SUB, src_ref.shape[2]
    rows = src_ref.shape[0] * SUB
    res = [jnp.dot(kin_ref[...], src_ref[:, j * SUB:(j + 1) * SUB, :].reshape(rows, width).astype(BF16),
                   preferred_element_type=F32) for j in range(n_sub)]
    for j, r in enumerate(res):
        a_ref[:, j * SUB:(j + 1) * SUB, :] = r.reshape(a_ref.shape[0], SUB, width)


def _hy_conv_kernel(z_ref, xo_ref, kin_ref, tw_ref, m2_ref, m2c_ref, kf_ref, kout_ref, bias_ref, o_ref,
                    a_ref, b_ref, *, kb):
    k = pl.program_id(1)

    @pl.when(k == 0)
    def _():
        o_ref[...] = jnp.zeros_like(o_ref)

    _strided_stage_in(kin_ref, z_ref, a_ref)
    cs = [(tw_ref[0, :, j:j + 1], tw_ref[1, :, j:j + 1]) for j in range(kb)]
    ts = [_twiddle_mul(a_ref[j], a_ref[kb + j], c, s, False) for j, (c, s) in enumerate(cs)]
    xs = [_cplx_apply(m2_ref, tr, ti) for tr, ti in ts]
    ps = [(xr * kf_ref[0, j] - xi * kf_ref[1, j], xr * kf_ref[1, j] + xi * kf_ref[0, j])
          for j, (xr, xi) in enumerate(xs)]
    bs = [_cplx_apply(m2c_ref, pr, pi) for pr, pi in ps]
    for j, ((br, bi), (c, s)) in enumerate(zip(bs, cs)):
        b_ref[j], b_ref[kb + j] = _twiddle_mul(br, bi, c, s, True)
    n_sub, width = o_ref.shape[1] // SUB, o_ref.shape[2]
    res = [jnp.dot(kout_ref[...], b_ref[:, j * SUB:(j + 1) * SUB, :].reshape(2 * kb * SUB, width).astype(BF16),
                   preferred_element_type=F32) for j in range(n_sub)]
    for j, r in enumerate(res):
        o_ref[:, j * SUB:(j + 1) * SUB, :] += r.reshape(o_ref.shape[0], SUB, width)

    @pl.when(k == pl.num_programs(1) - 1)
    def _():
        o_ref[...] = xo_ref[...] * (o_ref[...] + bias_ref[...] * z_ref[...])


def hy_conv_long(z, xo, kf, bias, order, b, n):
    w, kb = HY_WIDTH, HY_KB
    l1 = l2 = 128
    n1 = n // l2
    tw = _twiddles(l1, l2, kb)
    m2, m2c = _stage2_matrices(l2)
    kin, kout = _kron_stage1(l1, n1, kb), _kron_stage_out(l1, n1, kb)
    tok = pl.BlockSpec((None, n1, l2, w), lambda bi, k: (bi, 0, 0, 0), pipeline_mode=pl.Buffered(1))
    const = lambda a: pl.BlockSpec(a.shape, lambda bi, k: (0,) * a.ndim)
    out = pl.pallas_call(
        functools.partial(_hy_conv_kernel, kb=kb),
        grid=(b, l1 // kb),
        in_specs=[tok, tok,
                  pl.BlockSpec((None,) + kin.shape[1:], lambda bi, k: (k, 0, 0)),
                  pl.BlockSpec((None, 2, l2, kb), lambda bi, k: (k, 0, 0, 0)),
                  const(m2), const(m2c),
                  pl.BlockSpec((2, kb, l2, w), lambda bi, k: (0, k, 0, order)),
                  pl.BlockSpec((None,) + kout.shape[1:], lambda bi, k: (k, 0, 0)),
                  pl.BlockSpec((1, w), lambda bi, k: (0, 0))],
        out_specs=pl.BlockSpec((None, n1, l2, w), lambda bi, k: (bi, 0, 0, 0), pipeline_mode=pl.Buffered(1)),
        out_shape=jax.ShapeDtypeStruct((b, n1, l2, w), F32),
        scratch_shapes=[pltpu.VMEM((2 * kb, l2, w), F32), pltpu.VMEM((2 * kb, l2, w), F32)],
        compiler_params=_cparams("parallel", "arbitrary"),
        name="hy_conv_long",
    )(z.reshape(b, n1, l2, w), xo.reshape(b, n1, l2, w), kin, tw, m2, m2c, kf, kout, bias.reshape(1, w))
    return out.reshape(b * n, w)


def _hy_spec_long_kernel(kern_ref, kin_ref, tw_ref, m2_ref, s_ref, o_ref, a_ref, *, kb):
    _strided_stage_in(kin_ref, kern_ref, a_ref)
    inv = 1.0 / s_ref[...]
    ts = [_twiddle_mul(a_ref[j], a_ref[kb + j], tw_ref[0, :, j:j + 1], tw_ref[1, :, j:j + 1], False)
          for j in range(kb)]
    xs = [_cplx_apply(m2_ref, tr, ti) for tr, ti in ts]
    for j, (xr, xi) in enumerate(xs):
        o_ref[0, j] = xr * inv
        o_ref[1, j] = xi * inv


def hy_spec_long(kern, abs_sum):
    w, kb = HY_WIDTH, HY_KB
    l1 = l2 = 128
    cw = kern.shape[1]
    tw = _twiddles(l1, l2, kb)
    m2, _ = _stage2_matrices(l2)
    kin = _kron_stage1(l1, l1, kb)
    return pl.pallas_call(
        functools.partial(_hy_spec_long_kernel, kb=kb),
        grid=(cw // w, l1 // kb),
        in_specs=[pl.BlockSpec((l1, l2, w), lambda c, k: (0, 0, c), pipeline_mode=pl.Buffered(1)),
                  pl.BlockSpec((None,) + kin.shape[1:], lambda c, k: (k, 0, 0)),
                  pl.BlockSpec((None, 2, l2, kb), lambda c, k: (k, 0, 0, 0)),
                  pl.BlockSpec(m2.shape, lambda c, k: (0, 0)),
                  pl.BlockSpec((1, w), lambda c, k: (0, c))],
        out_specs=pl.BlockSpec((2, kb, l2, w), lambda c, k: (0, k, 0, c)),
        out_shape=jax.ShapeDtypeStruct((2, l1, l2, cw), F32),
        scratch_shapes=[pltpu.VMEM((2 * kb, l2, w), F32)],
        compiler_params=_cparams("parallel", "arbitrary"),
        name="hy_spec_long",
    )(kern.reshape(l1, l2, cw), kin, tw, m2, abs_sum)


def hyena_pallas(p, conv_w, w1, b1, freq1, w2, b2, freq2, w3, bias, b, n):
    w = HY_WIDTH
    x0, x1, v = seq_conv(p, conv_w, b, n)
    kern, abs_sum = hy_filter(n, w1, b1, freq1, w2, b2, freq2, w3)
    z = v
    if n == 8192:
        kf = hy_spec_long(kern, abs_sum)
        for order, xo in enumerate((x0, x1)):
            z = hy_conv_long(z, xo, kf, bias[order], order, b, n)
        return z
    l2 = 2 * n
    tw = _twiddles(1, l2, 1)
    m2, m2c = _stage2_matrices(l2)
    ak = jnp.stack([kern, jnp.zeros_like(kern)], axis=0).reshape(2, 1, l2, HY_ORDER * w)
    m_out = _const_bf16(np.eye(n, 2 * l2) / l2)
    kf = hy_spec(ak, tw, m2, abs_sum, 1)
    for order, xo in enumerate((x0, x1)):
        zp = jnp.pad(z.reshape(b, 1, 1, n, w), ((0, 0), (0, 0), (0, 0), (0, n), (0, 0)))
        a = jnp.concatenate([zp, jnp.zeros_like(zp)], axis=1)
        bp = hy_mid(a, tw, m2, m2c, kf, order, 1)
        z = hy_out(m_out, bp.reshape(b, 2 * l2, w), xo.reshape(b, n, w), z.reshape(b, n, w),
                   bias[order].reshape(1, w), w).reshape(b * n, w)
    return z


def _prep_w_in(w):
    main = jnp.concatenate([w[:, OFF_Q:OFF_Z], w[:, OFF_HY:OFF_GATE], w[:, OFF_FN:OFF_HY], w[:, OFF_Z:OFF_BETA]],
                           axis=1).astype(BF16)
    small = jnp.pad(w[:, OFF_BETA:OFF_FN], ((0, 0), (0, SMALL_WIDTH - 4 * DN_HEADS))).astype(BF16)
    return main, small, w[:, OFF_GATE:IN_WIDTH].astype(BF16)


def kernel(x, c, ctx, c_ctx, w_mod, b_mod, norm1, norm2, w_in, dn_conv, dn_a_log, dn_dt_bias,
           dn_out_norm, hy_conv, hy_w1, hy_b1, hy_freq1, hy_w2, hy_b2, hy_freq2, hy_w3, hy_bias,
           w_branch_a, w_branch_b, w_branch_c, w_out, w_ff1, w_ff2, final_norm):
    b, n_lat, d = x.shape
    rows = n_lat // GRID_W
    n_ctx = ctx.shape[1]
    tm_x, tm_c = 1024, n_ctx

    c_rows = jnp.concatenate([c, c_ctx[None], jnp.zeros((8 - b - 1, d), F32)], axis=0)
    mods = mod_vectors(c_rows, w_mod, b_mod)
    s_zero = jnp.zeros((2, b, DN_HEADS, DN_HEAD_DIM, DN_HEAD_DIM), F32)
    h, hc = x.reshape(b * n_lat, d), ctx.reshape(b * n_ctx, d)

    for l in range(DEPTH):
        last = l == DEPTH - 1
        mv = mods[l].reshape(8, N_MOD, 1, d)
        mx = [mv[:b, i] for i in range(N_MOD)]
        mc = [mv[b:b + 1, i] for i in range(N_MOD)]
        w_main, w_small, w_gate = _prep_w_in(w_in[l])
        wa, wb, wc, wo = (w.astype(BF16) for w in (w_branch_a[l], w_branch_b[l], w_branch_c[l], w_out[l]))
        w1, w2 = w_ff1[l].astype(BF16), w_ff2[l].astype(BF16)
        n1, n2 = norm1[l][None, None, :], norm2[l][None, None, :]

        p_c, small_c = in_proj(hc, n1 * (1.0 + mc[1]), mc[0], w_main, w_small, b * n_ctx, tm_c)
        p_x, small_x = in_proj(h, n1 * (1.0 + mx[1]), mx[0], w_main, w_small, n_lat, tm_x)

        def mix(p, n):
            y_c = hyena_pallas(p, hy_conv[l], hy_w1[l], hy_b1[l], hy_freq1[l], hy_w2[l], hy_b2[l], hy_freq2[l],
                               hy_w3[l], hy_bias[l], b, n)
            return fnet_branch(p, b, n), y_c

        ocf, ocb, s_ctx = delta_branch(p_c, small_c, dn_conv[l], dn_a_log[l], dn_dt_bias[l], s_zero,
                                       b, n_ctx, 1, n_ctx)
        oxf, oxb, _ = delta_branch(p_x, small_x, dn_conv[l], dn_a_log[l], dn_dt_bias[l], s_ctx,
                                   b, n_lat, rows, GRID_W)

        y_b, y_c = mix(p_x, n_lat)
        h = merge(oxf, oxb, p_x, y_b, y_c, h, n1 * (1.0 + mx[1]), mx[0], mx[2], dn_out_norm[l],
                  w_gate, wa, wb, wc, wo, n_lat, 512)
        h = mlp(h, n2 * (1.0 + mx[4]), mx[3], mx[5], w1, w2, n_lat, tm_x)

        if not last:
            y_b, y_c = mix(p_c, n_ctx)
            hc = merge(ocf, ocb, p_c, y_b, y_c, hc, n1 * (1.0 + mc[1]), mc[0], mc[2], dn_out_norm[l],
                       w_gate, wa, wb, wc, wo, b * n_ctx, tm_c)
            hc = mlp(hc, n2 * (1.0 + mc[4]), mc[3], mc[5], w1, w2, b * n_ctx, tm_c)

    return final_rms_norm(h, final_norm).reshape(b, n_lat, d)
```

```python
import functools
import math

import jax
import jax.numpy as jnp
import numpy as np
from jax import lax
from jax.experimental import pallas as pl
from jax.experimental.pallas import tpu as pltpu

D_MODEL = 1024
DEPTH = 2
GRID_W = 64
NORM_EPS = 1e-6
N_MOD = 6

DN_HEADS = 4
DN_HEAD_DIM = 128
DN_WIDTH = DN_HEADS * DN_HEAD_DIM
DN_CHUNK = 64
SHORT_CONV = 3

FN_GROUPS = 4
FN_GROUP_DIM = 64
FN_WIDTH = FN_GROUPS * FN_GROUP_DIM

HY_WIDTH = 256
HY_ORDER = 2
HY_EMB_DIM = 33
HY_BANDS = (HY_EMB_DIM - 1) // 2
HY_FILTER_HIDDEN = 64
HY_FAST_DECAY_PCT = 0.3
HY_SLOW_DECAY_PCT = 1.5
HY_DECAY_TARGET = 1e-2

N_BRANCHES = 3
D_FF = 4 * D_MODEL

OFF_Q = 0
OFF_Z = 3 * DN_WIDTH
OFF_BETA = OFF_Z + DN_WIDTH
OFF_A = OFF_BETA + 2 * DN_HEADS
OFF_FN = OFF_A + 2 * DN_HEADS
OFF_HY = OFF_FN + FN_WIDTH
OFF_GATE = OFF_HY + (HY_ORDER + 1) * HY_WIDTH
IN_WIDTH = OFF_GATE + N_BRANCHES * D_MODEL

P_QKV = 0
P_HY = P_QKV + 3 * DN_WIDTH
P_FN = P_HY + (HY_ORDER + 1) * HY_WIDTH
P_Z = P_FN + FN_WIDTH
P_WIDTH = P_Z + DN_WIDTH
SMALL_WIDTH = 128

F32 = jnp.float32
BF16 = jnp.bfloat16
VMEM_LIMIT = 56 * 1024 * 1024


def _cparams(*sem):
    return pltpu.CompilerParams(dimension_semantics=sem, vmem_limit_bytes=VMEM_LIMIT)


def _bdot(a, b):
    return jnp.dot(a.astype(BF16), b.astype(BF16), preferred_element_type=F32)


def _sigmoid(x):
    return 1.0 / (1.0 + jnp.exp(-x))


def _modnorm(xf, gs, sh):
    r = lax.rsqrt(jnp.mean(xf * xf, axis=-1, keepdims=True) + NORM_EPS)
    return xf * r * gs + sh


def _mod_kernel(c_ref, w_ref, b_ref, o_ref):
    c = c_ref[...]
    o_ref[...] = _bdot(c * _sigmoid(c), w_ref[...]) + b_ref[...]


def mod_vectors(c_rows, w_mod, b_mod):
    tn = 1536
    n = N_MOD * D_MODEL
    return pl.pallas_call(
        _mod_kernel,
        grid=(DEPTH, n // tn),
        in_specs=[pl.BlockSpec((8, D_MODEL), lambda l, j: (0, 0)),
                  pl.BlockSpec((None, D_MODEL, tn), lambda l, j: (l, 0, j)),
                  pl.BlockSpec((None, 1, tn), lambda l, j: (l, 0, j))],
        out_specs=pl.BlockSpec((None, 8, tn), lambda l, j: (l, 0, j)),
        out_shape=jax.ShapeDtypeStruct((DEPTH, 8, n), F32),
        compiler_params=_cparams("parallel", "parallel"),
        name="mod_vectors",
    )(c_rows, w_mod, b_mod.reshape(DEPTH, 1, n))


def _in_proj_kernel(x_ref, gs_ref, sh_ref, w_ref, ws_ref, p_ref, small_ref, xn_ref):
    @pl.when(pl.program_id(1) == 0)
    def _():
        xn = _modnorm(x_ref[...], gs_ref[0], sh_ref[0]).astype(BF16)
        xn_ref[...] = xn
        small_ref[...] = jnp.dot(xn, ws_ref[...], preferred_element_type=F32)

    p_ref[...] = jnp.dot(xn_ref[...], w_ref[...], preferred_element_type=F32)


def in_proj(x2d, gs, sh, w_main, w_small, rows_per_mod, tm):
    m = x2d.shape[0]
    tn = 1024
    tiles_per_mod = rows_per_mod // tm
    mod_spec = pl.BlockSpec((1, 1, D_MODEL), lambda i, j: (i // tiles_per_mod, 0, 0))
    return pl.pallas_call(
        _in_proj_kernel,
        grid=(m // tm, P_WIDTH // tn),
        in_specs=[pl.BlockSpec((tm, D_MODEL), lambda i, j: (i, 0)), mod_spec, mod_spec,
                  pl.BlockSpec((D_MODEL, tn), lambda i, j: (0, j)),
                  pl.BlockSpec((D_MODEL, SMALL_WIDTH), lambda i, j: (0, 0))],
        out_specs=[pl.BlockSpec((tm, tn), lambda i, j: (i, j)),
                   pl.BlockSpec((tm, SMALL_WIDTH), lambda i, j: (i, 0))],
        out_shape=[jax.ShapeDtypeStruct((m, P_WIDTH), F32), jax.ShapeDtypeStruct((m, SMALL_WIDTH), F32)],
        scratch_shapes=[pltpu.VMEM((tm, D_MODEL), BF16)],
        compiler_params=_cparams("parallel", "arbitrary"),
        name="in_proj",
    )(x2d, gs, sh, w_main, w_small)


def _merge_kernel(of_ref, ob_ref, z_ref, yb_ref, yc_ref, h_ref, gs_ref, sh_ref, gate_ref, nrm_ref,
                  wg_ref, wa_ref, wb_ref, wc_ref, wo_ref, out_ref):
    xn = _modnorm(h_ref[...], gs_ref[0], sh_ref[0]).astype(BF16)
    o = of_ref[...] + ob_ref[...]
    z = z_ref[...]
    heads = []
    for hd in range(DN_HEADS):
        sl = slice(hd * DN_HEAD_DIM, (hd + 1) * DN_HEAD_DIM)
        oh, zh = o[:, sl], z[:, sl]
        r = lax.rsqrt(jnp.mean(oh * oh, axis=-1, keepdims=True) + NORM_EPS)
        heads.append(oh * r * nrm_ref[...] * (zh * _sigmoid(zh)))
    ya = jnp.concatenate(heads, axis=-1)
    merged = None
    for i, (y, w_ref) in enumerate(((ya, wa_ref), (yb_ref[...], wb_ref), (yc_ref[...], wc_ref))):
        g = jnp.dot(xn, wg_ref[:, i * D_MODEL:(i + 1) * D_MODEL], preferred_element_type=F32)
        term = _sigmoid(g) * _bdot(y, w_ref[...])
        merged = term if merged is None else merged + term
    out_ref[...] = h_ref[...] + gate_ref[0] * _bdot(merged, wo_ref[...])


def merge(o_f, o_b, p, y_b, y_c, h2d, gs, sh, gate, dn_out_norm, wg, wa, wb, wc, wo, rows_per_mod, tm):
    m = h2d.shape[0]
    tiles_per_mod = rows_per_mod // tm
    row = lambda w: pl.BlockSpec((tm, w), lambda i: (i, 0))
    full = lambda a: pl.BlockSpec(a.shape, lambda i: (0,) * a.ndim, pipeline_mode=pl.Buffered(1))
    mod_spec = pl.BlockSpec((1, 1, D_MODEL), lambda i: (i // tiles_per_mod, 0, 0))
    nrm = dn_out_norm.reshape(1, DN_HEAD_DIM)
    return pl.pallas_call(
        _merge_kernel,
        grid=(m // tm,),
        in_specs=[row(DN_WIDTH), row(DN_WIDTH),
                  pl.BlockSpec((tm, DN_WIDTH), lambda i: (i, P_Z // DN_WIDTH)),
                  row(FN_WIDTH), row(HY_WIDTH), row(D_MODEL), mod_spec, mod_spec, mod_spec,
                  full(nrm), full(wg), full(wa), full(wb), full(wc), full(wo)],
        out_specs=row(D_MODEL),
        out_shape=jax.ShapeDtypeStruct((m, D_MODEL), F32),
        compiler_params=_cparams("parallel"),
        name="merge",
    )(o_f, o_b, p, y_b, y_c, h2d, gs, sh, gate, nrm, wg, wa, wb, wc, wo)


def _mlp_kernel(h_ref, gs_ref, sh_ref, gate_ref, w1_ref, w2_ref, fin_ref, out_ref, xn_ref, acc_ref, *, final):
    j = pl.program_id(1)

    @pl.when(j == 0)
    def _():
        xn_ref[...] = _modnorm(h_ref[...], gs_ref[0], sh_ref[0]).astype(BF16)
        acc_ref[...] = jnp.zeros_like(acc_ref)

    a = jnp.maximum(jnp.dot(xn_ref[...], w1_ref[...], preferred_element_type=F32), 0.0)
    acc_ref[...] += jnp.dot((a * a).astype(BF16), w2_ref[...], preferred_element_type=F32)

    @pl.when(j == pl.num_programs(1) - 1)
    def _():
        y = h_ref[...] + gate_ref[0] * acc_ref[...]
        if final:
            y = y * lax.rsqrt(jnp.mean(y * y, axis=-1, keepdims=True) + NORM_EPS) * fin_ref[...]
        out_ref[...] = y


def mlp(h2d, gs, sh, gate, w1, w2, rows_per_mod, tm, final_gain=None):
    m = h2d.shape[0]
    tf = 512
    tiles_per_mod = rows_per_mod // tm
    mod_spec = pl.BlockSpec((1, 1, D_MODEL), lambda i, j: (i // tiles_per_mod, 0, 0))
    final = final_gain is not None
    fin = (final_gain if final else jnp.ones((D_MODEL,), F32)).reshape(1, D_MODEL)
    return pl.pallas_call(
        functools.partial(_mlp_kernel, final=final),
        grid=(m // tm, D_FF // tf),
        in_specs=[pl.BlockSpec((tm, D_MODEL), lambda i, j: (i, 0)), mod_spec, mod_spec, mod_spec,
                  pl.BlockSpec((D_MODEL, tf), lambda i, j: (0, j)),
                  pl.BlockSpec((tf, D_MODEL), lambda i, j: (j, 0)),
                  pl.BlockSpec((1, D_MODEL), lambda i, j: (0, 0))],
        out_specs=pl.BlockSpec((tm, D_MODEL), lambda i, j: (i, 0)),
        out_shape=jax.ShapeDtypeStruct((m, D_MODEL), F32),
        scratch_shapes=[pltpu.VMEM((tm, D_MODEL), BF16), pltpu.VMEM((tm, D_MODEL), F32)],
        compiler_params=_cparams("parallel", "arbitrary"),
        name="mlp",
    )(h2d, gs, sh, gate, w1, w2, fin)


def _dnconv_kernel(prev_ref, cur_ref, next_ref, w_ref, o_ref, *, cols, n_tiles):
    t, j = pl.program_id(1), pl.program_id(2)
    tt = cur_ref.shape[0]
    prev = jnp.where(t == 0, 0.0, prev_ref[...])
    nxt = jnp.where(t == n_tiles - 1, 0.0, next_ref[...])
    ext = jnp.concatenate([prev, cur_ref[...], nxt], axis=0)
    n_ext = tt + 2 * cols
    col = lax.broadcasted_iota(jnp.int32, (n_ext, 1), 0) % cols
    left = jnp.where(col == 0, 0.0, pltpu.roll(ext, 1, axis=0))
    right = jnp.where(col == cols - 1, 0.0, pltpu.roll(ext, n_ext - 1, axis=0))
    acc = jnp.zeros((tt, DN_WIDTH), F32)
    for dr in range(SHORT_CONV):
        base = dr * cols
        acc = (acc + w_ref[3 * dr:3 * dr + 1, :] * left[base:base + tt]
               + w_ref[3 * dr + 1:3 * dr + 2, :] * ext[base:base + tt]
               + w_ref[3 * dr + 2:3 * dr + 3, :] * right[base:base + tt])
    y = acc * _sigmoid(acc)
    q_scale = jnp.where(j == 0, DN_HEAD_DIM ** -0.5, 1.0)
    for hd in range(DN_HEADS):
        sl = slice(hd * DN_HEAD_DIM, (hd + 1) * DN_HEAD_DIM)
        yh = y[:, sl]
        nrm = lax.rsqrt(jnp.sum(yh * yh, axis=-1, keepdims=True) + NORM_EPS) * q_scale
        o_ref[:, sl] = yh * jnp.where(j < 2, nrm, 1.0)


def dn_conv_prep(p, conv_w, b, n, rows, cols):
    tr = min(rows, 16)
    tt = tr * cols
    n_tiles = rows // tr
    nblk = b * n // cols
    c0 = P_QKV // DN_WIDTH
    return pl.pallas_call(
        functools.partial(_dnconv_kernel, cols=cols, n_tiles=n_tiles),
        grid=(b, n_tiles, 3),
        in_specs=[pl.BlockSpec((cols, DN_WIDTH),
                               lambda bi, t, j: (jnp.maximum(bi * rows + t * tr - 1, 0), c0 + j)),
                  pl.BlockSpec((tt, DN_WIDTH), lambda bi, t, j: (bi * n_tiles + t, c0 + j)),
                  pl.BlockSpec((cols, DN_WIDTH),
                               lambda bi, t, j: (jnp.minimum(bi * rows + (t + 1) * tr, nblk - 1), c0 + j)),
                  pl.BlockSpec((SHORT_CONV * SHORT_CONV, DN_WIDTH), lambda bi, t, j: (0, j))],
        out_specs=pl.BlockSpec((tt, DN_WIDTH), lambda bi, t, j: (bi * n_tiles + t, j)),
        out_shape=jax.ShapeDtypeStruct((b * n, 3 * DN_WIDTH), F32),
        compiler_params=_cparams("parallel", "parallel", "parallel"),
        name="dn_conv_prep",
    )(p, p, p, conv_w.reshape(SHORT_CONV * SHORT_CONV, 3 * DN_WIDTH))


PREP_CHUNKS = 2


def _softplus(x):
    return jnp.maximum(x, 0.0) + jnp.log(1.0 + jnp.exp(-jnp.abs(x)))


def _delta_prep_kernel(qkv_ref, sm_ref, smt_ref, prm_ref, prmt_ref, u0_ref, lhs1_ref, lhs2_ref, gl_ref):
    cc = DN_CHUNK
    nh = DN_HEADS
    sm = sm_ref[...]
    beta_all = _sigmoid(sm)
    g_all = -jnp.exp(prm_ref[0:1, :]) * _softplus(sm + prm_ref[1:2, :])
    gt_all = -jnp.exp(prmt_ref[:, 0:1]) * _softplus(smt_ref[...] + prmt_ref[:, 1:2])
    lt = 2 * cc
    lane = lax.broadcasted_iota(jnp.int32, (gt_all.shape[0], lt), 1) % cc
    gt_tiles = []
    for t in range(gt_all.shape[1] // lt):
        gt_f = gt_b = gt_all[:, t * lt:(t + 1) * lt]
        s = 1
        while s < cc:
            gt_f = gt_f + jnp.where(lane >= s, pltpu.roll(gt_f, s, axis=1), 0.0)
            gt_b = gt_b + jnp.where(lane < cc - s, pltpu.roll(gt_b, lt - s, axis=1), 0.0)
            s *= 2
        gt_tiles.append((gt_f, gt_b))
    ri = lax.broadcasted_iota(jnp.int32, (cc, cc), 0)
    ci_ = lax.broadcasted_iota(jnp.int32, (cc, cc), 1)
    sub = lax.broadcasted_iota(jnp.int32, (cc, 1), 0)
    blk = lambda s: (ri // s) == (ci_ // s)
    leaf = 8

    heads, chains = [], []
    for ci in range(PREP_CHUNKS):
        rows = slice(ci * cc, (ci + 1) * cc)
        gc_f = g_all[rows]
        gc_b = gc_f
        s = 1
        while s < cc:
            gc_f = gc_f + jnp.where(sub >= s, pltpu.roll(gc_f, s, axis=0), 0.0)
            gc_b = gc_b + jnp.where(sub < cc - s, pltpu.roll(gc_b, cc - s, axis=0), 0.0)
            s *= 2
        for h in range(nh):
            q = qkv_ref[rows, h * 128:(h + 1) * 128]
            k = qkv_ref[rows, (nh + h) * 128:(nh + h + 1) * 128]
            v = qkv_ref[rows, (2 * nh + h) * 128:(2 * nh + h + 1) * 128]
            heads.append((ci, rows, h, q, k, v, gc_f, gc_b))
    qkks = [lax.dot_general(jnp.concatenate([q, k], axis=0).astype(BF16), k.astype(BF16),
                            (((1,), (1,)), ((), ())), preferred_element_type=F32)
            for (_, _, _, q, k, _, _, _) in heads]
    for (ci, rows, h, q, k, v, gc_f, gc_b), qkk in zip(heads, qkks):
        qk, kk = qkk[:cc], qkk[cc:]
        for d in range(2):
            cb_, ca_ = d * nh + h, 2 * nh + d * nh + h
            beta = beta_all[rows, cb_:cb_ + 1]
            g_col = (gc_f if d == 0 else gc_b)[:, ca_:ca_ + 1]
            lo = (ci % 2) * cc
            g_row = gt_tiles[ci // 2][d][ca_:ca_ + 1, lo:lo + cc]
            incl = (ri >= ci_) if d == 0 else (ri <= ci_)
            strict = (ri > ci_) if d == 0 else (ri < ci_)
            decay = jnp.exp(jnp.where(incl, g_col - g_row, -1e30))
            a = jnp.where(strict, beta * kk * decay, 0.0)
            eg = jnp.exp(g_col)
            g_last = g_col[cc - 1:cc] if d == 0 else g_col[0:1]
            chains.append(dict(ci=ci, rows=rows, h=h, d=d, a=a, qkd=qk * decay, g_last=g_last,
                               rhs=jnp.concatenate([v * beta, k * (beta * eg)], axis=1),
                               q_d=q * eg, k_d=k * jnp.exp(g_last - g_col)))

    pws = [jnp.where(blk(leaf), c["a"], 0.0) for c in chains]
    devs = [-pw for pw in pws]
    for _ in range(2):
        pws = [_bdot(pw, pw) for pw in pws]
        cross = [_bdot(dev, pw) for dev, pw in zip(devs, pws)]
        devs = [dev + pw + x for dev, pw, x in zip(devs, pws, cross)]
    s = leaf
    while s < cc:
        offs = [jnp.where(blk(2 * s) & jnp.logical_not(blk(s)), c["a"], 0.0) for c in chains]
        xs = [off + _bdot(dev, off) for dev, off in zip(devs, offs)]
        devs = [dev - x - _bdot(x, dev) for dev, x in zip(devs, xs)]
        s *= 2
    uws = [c["rhs"] + _bdot(dev, c["rhs"]) for dev, c in zip(devs, chains)]
    for c, uw in zip(chains, uws):
        d, h, ci = c["d"], c["h"], c["ci"]
        u0_ref[d, h, c["rows"], :] = uw[:, :128]
        lhs1_ref[d, h, ci] = jnp.concatenate([uw[:, 128:], c["q_d"]], axis=0).astype(BF16)
        lhs2_ref[d, h, ci] = jnp.concatenate([c["qkd"], c["k_d"].T], axis=0).astype(BF16)
        gl_ref[d, h, ci] = jnp.broadcast_to(jnp.exp(c["g_last"]), (8, 128))


def delta_prep(qkv, small, small_t, prm, prm_t, b, n):
    nc = n // DN_CHUNK
    cb = PREP_CHUNKS
    nblk = nc // cb
    tt = cb * DN_CHUNK
    nh = DN_HEADS
    return pl.pallas_call(
        _delta_prep_kernel,
        grid=(b, nblk),
        in_specs=[pl.BlockSpec((tt, 3 * DN_WIDTH), lambda bi, c: (bi * nblk + c, 0)),
                  pl.BlockSpec((tt, SMALL_WIDTH), lambda bi, c: (bi * nblk + c, 0)),
                  pl.BlockSpec((4 * nh, tt), lambda bi, c: (0, bi * nblk + c)),
                  pl.BlockSpec((2, SMALL_WIDTH), lambda bi, c: (0, 0)),
                  pl.BlockSpec((4 * nh, 2), lambda bi, c: (0, 0))],
        out_specs=[pl.BlockSpec((2, None, nh, tt, 128), lambda bi, c: (0, bi, 0, c, 0)),
                   pl.BlockSpec((2, None, nh, cb, 128, 128), lambda bi, c: (0, bi, 0, c, 0, 0)),
                   pl.BlockSpec((2, None, nh, cb, 192, 64), lambda bi, c: (0, bi, 0, c, 0, 0)),
                   pl.BlockSpec((2, None, nh, cb, 8, 128), lambda bi, c: (0, bi, 0, c, 0, 0))],
        out_shape=[jax.ShapeDtypeStruct((2, b, nh, n, 128), F32),
                   jax.ShapeDtypeStruct((2, b, nh, nc, 128, 128), BF16),
                   jax.ShapeDtypeStruct((2, b, nh, nc, 192, 64), BF16),
                   jax.ShapeDtypeStruct((2, b, nh, nc, 8, 128), F32)],
        compiler_params=_cparams("parallel", "parallel"),
        name="delta_prep",
    )(qkv, small, small_t, prm, prm_t)


def _delta_scan_kernel(u0f_ref, u0b_ref, l1f_ref, l1b_ref, l2f_ref, l2b_ref, glf_ref, glb_ref, s0_ref,
                       of_ref, ob_ref, sout_ref, st_ref, *, sc, nb):
    n = pl.program_id(0)
    cc = DN_CHUNK

    @pl.when(n == 0)
    def _():
        st_ref[...] = s0_ref[...]

    dirs = ((u0f_ref, l1f_ref, l2f_ref, glf_ref, of_ref), (u0b_ref, l1b_ref, l2b_ref, glb_ref, ob_ref))

    def body(i, carry):
        chains = []
        for d in range(2):
            ci = i if d == 0 else sc - 1 - i
            r0 = pl.multiple_of(ci * cc, cc)
            chains += [(d, bi, h, ci, r0) for bi in range(nb) for h in range(DN_HEADS)]
        sts = [st_ref[d, bi, h] for (d, bi, h, _, _) in chains]
        r1s = [jnp.dot(dirs[d][1][bi, h, ci], st.astype(BF16), preferred_element_type=F32)
               for (d, bi, h, ci, _), st in zip(chains, sts)]
        us = [dirs[d][0][bi, h, pl.ds(r0, cc), :] - r1[:cc] for (d, bi, h, _, r0), r1 in zip(chains, r1s)]
        r2s = [jnp.dot(dirs[d][2][bi, h, ci], u.astype(BF16), preferred_element_type=F32)
               for (d, bi, h, ci, _), u in zip(chains, us)]
        for (d, bi, h, ci, r0), st, r1, r2 in zip(chains, sts, r1s, r2s):
            dirs[d][4][bi, pl.ds(r0, cc), h * 128:(h + 1) * 128] = r1[cc:] + r2[:cc]
            st_ref[d, bi, h] = st * dirs[d][3][bi, h, ci, 0:1, :] + r2[cc:]
        return carry

    lax.fori_loop(0, sc, body, 0)

    @pl.when(n == pl.num_programs(0) - 1)
    def _():
        sout_ref[...] = st_ref[...]


def delta_scan(u0, lhs1, lhs2, gl, s0, b, n):
    nc = n // DN_CHUNK
    sc = 4
    nblk = nc // sc
    nh = DN_HEADS
    tt = sc * DN_CHUNK
    fwd = lambda i: i
    bwd = lambda i: nblk - 1 - i

    def specs(d, blk):
        return [pl.BlockSpec((None, b, nh, tt, 128), lambda i: (d, 0, 0, blk(i), 0)),
                pl.BlockSpec((None, b, nh, sc, 128, 128), lambda i: (d, 0, 0, blk(i), 0, 0)),
                pl.BlockSpec((None, b, nh, sc, 192, 64), lambda i: (d, 0, 0, blk(i), 0, 0)),
                pl.BlockSpec((None, b, nh, sc, 8, 128), lambda i: (d, 0, 0, blk(i), 0, 0))]

    sf, sb = specs(0, fwd), specs(1, bwd)
    in_specs = [sf[0], sb[0], sf[1], sb[1], sf[2], sb[2], sf[3], sb[3],
                pl.BlockSpec((2, b, nh, 128, 128), lambda i: (0, 0, 0, 0, 0))]
    return pl.pallas_call(
        functools.partial(_delta_scan_kernel, sc=sc, nb=b),
        grid=(nblk,),
        in_specs=in_specs,
        out_specs=[pl.BlockSpec((b, tt, DN_WIDTH), lambda i: (0, fwd(i), 0)),
                   pl.BlockSpec((b, tt, DN_WIDTH), lambda i: (0, bwd(i), 0)),
                   pl.BlockSpec((2, b, nh, 128, 128), lambda i: (0, 0, 0, 0, 0))],
        out_shape=[jax.ShapeDtypeStruct((b, n, DN_WIDTH), F32), jax.ShapeDtypeStruct((b, n, DN_WIDTH), F32),
                   jax.ShapeDtypeStruct((2, b, nh, 128, 128), F32)],
        scratch_shapes=[pltpu.VMEM((2, b, nh, 128, 128), F32)],
        compiler_params=_cparams("arbitrary"),
        name="delta_scan",
    )(u0, u0, lhs1, lhs1, lhs2, lhs2, gl, gl, s0)


def delta_branch(p, small, conv_w, a_log, dt_bias, s0, b, n, rows, cols):
    nh = DN_HEADS
    qkv = dn_conv_prep(p, conv_w, b, n, rows, cols)
    small_t = small[:, :4 * nh].T
    rate = jnp.concatenate([jnp.zeros((2 * nh,), F32), a_log.reshape(-1)])
    bias = jnp.concatenate([jnp.zeros((2 * nh,), F32), dt_bias.reshape(-1)])
    prm_t = jnp.stack([rate, bias], axis=1)
    prm = jnp.pad(prm_t.T, ((0, 0), (0, SMALL_WIDTH - 4 * nh)))
    u0, lhs1, lhs2, gl = delta_prep(qkv, small, small_t, prm, prm_t, b, n)
    o_f, o_b, s_out = delta_scan(u0, lhs1, lhs2, gl, s0, b, n)
    return o_f.reshape(b * n, DN_WIDTH), o_b.reshape(b * n, DN_WIDTH), s_out


def _cos_sin(rows, cols, period):
    ang = 2.0 * np.pi * ((np.arange(rows)[:, None] * np.arange(cols)[None, :]) % period) / period
    return np.cos(ang), np.sin(ang)


def _const_bf16(a):
    return jnp.asarray(a, F32).astype(BF16)


def _stage2_matrices(l2):
    c, s = _cos_sin(l2, l2, l2)
    fwd = np.block([[c, s], [-s, c]])
    inv = np.block([[c, -s], [s, c]])
    return _const_bf16(fwd), _const_bf16(inv)


def _twiddles(l1, l2, kb):
    ang = 2.0 * np.pi * ((np.arange(l2)[:, None] * np.arange(l1)[None, :]) % (l1 * l2)) / (l1 * l2)
    tw = np.stack([np.cos(ang), np.sin(ang)], axis=0).reshape(2, l2, l1 // kb, kb)
    return jnp.asarray(np.transpose(tw, (2, 0, 1, 3)), F32)


def _twiddle_mul(ar, ai, c, s, conj):
    if conj:
        return ar * c - ai * s, ai * c + ar * s
    return ar * c + ai * s, ai * c - ar * s


def _cplx_apply(m_ref, re, im):
    half = re.shape[0]
    out = jnp.dot(m_ref[...], jnp.concatenate([re, im], axis=0).astype(BF16), preferred_element_type=F32)
    return out[:half], out[half:]


def _fnet_mid_kernel(a_ref, tw_ref, m2_ref, cs_ref, o_ref, *, kb, ch):
    ts = [_twiddle_mul(a_ref[0, j], a_ref[1, j], tw_ref[0, :, j:j + 1], tw_ref[1, :, j:j + 1], False)
          for j in range(kb)]
    us = [_cplx_apply(m2_ref, tr, ti) for tr, ti in ts]
    ys = [jnp.dot(jnp.concatenate([ur, ui], axis=1).astype(BF16), cs_ref[...], preferred_element_type=F32)
          for ur, ui in us]
    for j, y in enumerate(ys):
        o_ref[:, j * ch:(j + 1) * ch] = y


def fnet_mid(a, tw, m2, cs, kb):
    bsz, _, l1, l2, ch = a.shape
    return pl.pallas_call(
        functools.partial(_fnet_mid_kernel, kb=kb, ch=ch),
        grid=(bsz, l1 // kb),
        in_specs=[pl.BlockSpec((None, 2, kb, l2, ch), lambda b, k: (b, 0, k, 0, 0)),
                  pl.BlockSpec((None, 2, l2, kb), lambda b, k: (k, 0, 0, 0)),
                  pl.BlockSpec(m2.shape, lambda b, k: (0, 0)),
                  pl.BlockSpec(cs.shape, lambda b, k: (0, 0))],
        out_specs=pl.BlockSpec((None, l2, kb * ch), lambda b, k: (b, 0, k)),
        out_shape=jax.ShapeDtypeStruct((bsz, l2, l1 * ch), F32),
        compiler_params=_cparams("parallel", "parallel"),
        name="fnet_mid",
    )(a, tw, m2, cs)


def _fnet_channel_matrix(n):
    c, s = _cos_sin(FN_GROUP_DIM, FN_GROUP_DIM, FN_GROUP_DIM)
    eye = np.eye(FN_GROUPS)
    scale = 1.0 / math.sqrt(n * FN_GROUP_DIM)
    return _const_bf16(np.concatenate([np.kron(eye, c), np.kron(eye, s)], axis=0) * scale)


FN_SUB = 8


def _fnet_long_kernel(x_ref, kin_ref, tw_ref, m2k_ref, cs_ref, o_ref, a_ref):
    n_sub = x_ref.shape[1] // FN_SUB
    rows, width = x_ref.shape[0] * FN_SUB, x_ref.shape[2]
    half = FN_SUB * FN_SUB
    res = [jnp.dot(kin_ref[...], x_ref[:, j * FN_SUB:(j + 1) * FN_SUB, :].reshape(rows, width).astype(BF16),
                   preferred_element_type=F32) for j in range(n_sub)]
    for j, r in enumerate(res):
        a_ref[0, j * half:(j + 1) * half, :] = r[:half]
        a_ref[1, j * half:(j + 1) * half, :] = r[half:]
    tr, ti = _twiddle_mul(a_ref[0], a_ref[1], tw_ref[0], tw_ref[1], False)
    ur, ui = _cplx_apply(m2k_ref, tr, ti)
    y = jnp.dot(jnp.concatenate([ur, ui], axis=1).astype(BF16), cs_ref[...], preferred_element_type=F32)
    o_ref[...] = y.reshape(o_ref.shape)


def fnet_long(p, b, n):
    ch, sub = FN_WIDTH, FN_SUB
    l1, l2 = n // 128, 128
    nblk = l1 // sub
    c1, s1 = _cos_sin(l1, l1, l1)
    m1 = np.stack([c1, -s1], axis=0).reshape(2, nblk, sub, l1)
    kin = np.einsum('pbkn,jq->bpjknq', m1, np.eye(sub)).reshape(nblk, 2 * sub * sub, l1 * sub)
    ang = 2.0 * np.pi * ((np.arange(l2)[:, None] * np.arange(l1)[None, :]) % n) / n
    tw = np.stack([np.cos(ang), np.sin(ang)], axis=0).reshape(2, l2, nblk, sub)
    tw = np.transpose(tw, (2, 0, 1, 3)).reshape(nblk, 2, l2 * sub, 1)
    c2, s2 = _cos_sin(l2, l2, l2)
    m2k = _const_bf16(np.kron(np.block([[c2, s2], [-s2, c2]]), np.eye(sub)))
    cs = _fnet_channel_matrix(n)
    const = lambda a: pl.BlockSpec(a.shape, lambda bi, k: (0,) * a.ndim, pipeline_mode=pl.Buffered(1))
    y = pl.pallas_call(
        _fnet_long_kernel,
        grid=(b, nblk),
        in_specs=[pl.BlockSpec((None, l1, l2, ch), lambda bi, k: (bi, 0, 0, P_FN // ch),
                               pipeline_mode=pl.Buffered(1)),
                  pl.BlockSpec((None,) + kin.shape[1:], lambda bi, k: (k, 0, 0)),
                  pl.BlockSpec((None, 2, l2 * sub, 1), lambda bi, k: (k, 0, 0, 0)),
                  const(m2k), const(cs)],
        out_specs=pl.BlockSpec((None, l2, sub, ch), lambda bi, k: (bi, 0, k, 0)),
        out_shape=jax.ShapeDtypeStruct((b, l2, l1, ch), F32),
        scratch_shapes=[pltpu.VMEM((2, l2 * sub, ch), F32)],
        compiler_params=_cparams("parallel", "parallel"),
        name="fnet_long",
    )(p.reshape(b, l1, l2, P_WIDTH), _const_bf16(kin), jnp.asarray(tw, F32), m2k, cs)
    return y.reshape(b * n, ch)


def fnet_branch(p, b, n):
    if n == 8192:
        return fnet_long(p, b, n)
    ch = FN_WIDTH
    xr = lax.slice_in_dim(p, P_FN, P_FN + ch, axis=1).reshape(b, 1, 1, n, ch)
    a = jnp.concatenate([xr, jnp.zeros_like(xr)], axis=1)
    y = fnet_mid(a, _twiddles(1, n, 1), _stage2_matrices(n)[0], _fnet_channel_matrix(n), 1)
    return y.reshape(b * n, ch)


def _seq_conv_kernel(prev_ref, cur_ref, next_ref, w_ref, *o_refs, n_tiles):
    t = pl.program_id(1)
    tt = cur_ref.shape[0]
    cur = cur_ref[...]
    row = lax.broadcasted_iota(jnp.int32, (tt, 1), 0)
    before = jnp.where(t == 0, 0.0, prev_ref[7:8, :])
    after = jnp.where(t == n_tiles - 1, 0.0, next_ref[0:1, :])
    left = jnp.where(row == 0, before, pltpu.roll(cur, 1, axis=0))
    right = jnp.where(row == tt - 1, after, pltpu.roll(cur, tt - 1, axis=0))
    y = w_ref[0:1, :] * left + w_ref[1:2, :] * cur + w_ref[2:3, :] * right
    for part, o_ref in enumerate(o_refs):
        o_ref[...] = y[:, part * HY_WIDTH:(part + 1) * HY_WIDTH]


def seq_conv(p, conv_w, b, n):
    tt = min(n, 1024)
    n_tiles = n // tt
    parts = HY_ORDER + 1
    w = parts * HY_WIDTH
    c0 = P_HY // w
    out_spec = pl.BlockSpec((tt, HY_WIDTH), lambda bi, t: (bi * n_tiles + t, 0))
    return pl.pallas_call(
        functools.partial(_seq_conv_kernel, n_tiles=n_tiles),
        grid=(b, n_tiles),
        in_specs=[pl.BlockSpec((8, w), lambda bi, t: (jnp.maximum((bi * n_tiles + t) * (tt // 8) - 1, 0), c0)),
                  pl.BlockSpec((tt, w), lambda bi, t: (bi * n_tiles + t, c0)),
                  pl.BlockSpec((8, w), lambda bi, t: (jnp.minimum((bi * n_tiles + t + 1) * (tt // 8),
                                                                  b * n // 8 - 1), c0)),
                  pl.BlockSpec((SHORT_CONV, w), lambda bi, t: (0, 0))],
        out_specs=[out_spec] * parts,
        out_shape=[jax.ShapeDtypeStruct((b * n, HY_WIDTH), F32)] * parts,
        compiler_params=_cparams("parallel", "parallel"),
        name="hy_seq_conv",
    )(p, p, p, conv_w)


def _hdot(a, b):
    return jnp.dot(a, b, preferred_element_type=F32, precision=lax.Precision.HIGHEST)


def _hy_filter_kernel(f_ref, w1_ref, b1_ref, f1_ref, w2_ref, b2_ref, f2_ref, w3_ref, dl_ref, k_ref, s_ref, *, n, tr):
    i = pl.program_id(0)
    feats = f_ref[...]
    hid = jnp.sin(f1_ref[...] * (_hdot(feats, w1_ref[...]) + b1_ref[...]))
    hid = jnp.sin(f2_ref[...] * (_hdot(hid, w2_ref[...]) + b2_ref[...]))
    filt = _hdot(hid, w3_ref[...]) * jnp.exp(-feats[:, 0:1] * dl_ref[...])
    row = i * tr + lax.broadcasted_iota(jnp.int32, (tr, 1), 0)
    filt = jnp.where(row == n, 0.0, filt)
    k_ref[...] = filt

    @pl.when(i == 0)
    def _():
        s_ref[...] = jnp.zeros_like(s_ref)

    s_ref[...] += jnp.sum(jnp.abs(filt), axis=0, keepdims=True)


def hy_filter(n, w1, b1, freq1, w2, b2, freq2, w3):
    pos = jnp.arange(n, dtype=F32)
    t = pos / max(n - 1, 1)
    bands = jnp.linspace(1e-4, HY_BANDS - 1, HY_BANDS, dtype=F32)
    ang = (2.0 * math.pi / n) * pos[:, None] * bands[None, :]
    feats = jnp.concatenate([t[:, None], jnp.cos(ang), -jnp.sin(ang)], axis=-1)
    feats2 = jnp.concatenate([feats, feats[:1], feats[:0:-1]], axis=0)
    kpad = 128
    feats2 = jnp.pad(feats2, ((0, 0), (0, kpad - HY_EMB_DIM)))
    w1p = jnp.pad(w1, ((0, kpad - HY_EMB_DIM), (0, 0)))
    min_decay = math.log(HY_DECAY_TARGET) / HY_SLOW_DECAY_PCT
    max_decay = math.log(HY_DECAY_TARGET) / HY_FAST_DECAY_PCT
    cw = HY_ORDER * HY_WIDTH
    deltas = jnp.abs(jnp.linspace(min_decay, max_decay, cw, dtype=F32)).reshape(1, cw)
    tr = min(n, 1024)
    half = n // tr
    hd = HY_FILTER_HIDDEN
    vec = lambda v: v.reshape(1, hd)
    full = lambda shp: pl.BlockSpec(shp, lambda i: (0, 0))
    return pl.pallas_call(
        functools.partial(_hy_filter_kernel, n=n, tr=tr),
        grid=(2 * half,),
        in_specs=[pl.BlockSpec((tr, kpad), lambda i: (i, 0)), full((kpad, hd)), full((1, hd)), full((1, hd)),
                  full((hd, hd)), full((1, hd)), full((1, hd)),
                  pl.BlockSpec((hd, cw), lambda i: (0, i // half)), full((1, cw))],
        out_specs=[pl.BlockSpec((tr, cw), lambda i: (i, 0)), full((1, cw))],
        out_shape=[jax.ShapeDtypeStruct((2 * n, cw), F32), jax.ShapeDtypeStruct((1, cw), F32)],
        compiler_params=_cparams("arbitrary"),
        name="hy_filter",
    )(feats2, w1p, vec(b1), vec(freq1), w2, vec(b2), vec(freq2), w3, deltas)


def _hy_spec_kernel(a_ref, tw_ref, m2_ref, s_ref, o_ref, *, kb):
    inv = 1.0 / s_ref[...]
    ts = [_twiddle_mul(a_ref[0, j], a_ref[1, j], tw_ref[0, :, j:j + 1], tw_ref[1, :, j:j + 1], False)
          for j in range(kb)]
    xs = [_cplx_apply(m2_ref, tr, ti) for tr, ti in ts]
    for j, (xr, xi) in enumerate(xs):
        o_ref[0, j] = xr * inv
        o_ref[1, j] = xi * inv


def hy_spec(a, tw, m2, abs_sum, kb):
    _, l1, l2, cw = a.shape
    w = HY_WIDTH
    return pl.pallas_call(
        functools.partial(_hy_spec_kernel, kb=kb),
        grid=(l1 // kb, cw // w),
        in_specs=[pl.BlockSpec((2, kb, l2, w), lambda k, c: (0, k, 0, c)),
                  pl.BlockSpec((None, 2, l2, kb), lambda k, c: (k, 0, 0, 0)),
                  pl.BlockSpec(m2.shape, lambda k, c: (0, 0)),
                  pl.BlockSpec((1, w), lambda k, c: (0, c))],
        out_specs=pl.BlockSpec((2, kb, l2, w), lambda k, c: (0, k, 0, c)),
        out_shape=jax.ShapeDtypeStruct((2, l1, l2, cw), F32),
        compiler_params=_cparams("parallel", "parallel"),
        name="hy_spec",
    )(a, tw, m2, abs_sum)


def _hy_mid_kernel(a_ref, tw_ref, m2_ref, m2c_ref, kf_ref, o_ref, *, kb):
    cs = [(tw_ref[0, :, j:j + 1], tw_ref[1, :, j:j + 1]) for j in range(kb)]
    ts = [_twiddle_mul(a_ref[0, j], a_ref[1, j], c, s, False) for j, (c, s) in enumerate(cs)]
    xs = [_cplx_apply(m2_ref, tr, ti) for tr, ti in ts]
    ps = [(xr * kf_ref[0, j] - xi * kf_ref[1, j], xr * kf_ref[1, j] + xi * kf_ref[0, j])
          for j, (xr, xi) in enumerate(xs)]
    bs = [_cplx_apply(m2c_ref, pr, pi) for pr, pi in ps]
    for j, ((br, bi), (c, s)) in enumerate(zip(bs, cs)):
        o_ref[0, j], o_ref[1, j] = _twiddle_mul(br, bi, c, s, True)


def hy_mid(a, tw, m2, m2c, kf, order, kb):
    bsz, _, l1, l2, w = a.shape
    return pl.pallas_call(
        functools.partial(_hy_mid_kernel, kb=kb),
        grid=(bsz, l1 // kb),
        in_specs=[pl.BlockSpec((None, 2, kb, l2, w), lambda b, k: (b, 0, k, 0, 0)),
                  pl.BlockSpec((None, 2, l2, kb), lambda b, k: (k, 0, 0, 0)),
                  pl.BlockSpec(m2.shape, lambda b, k: (0, 0)),
                  pl.BlockSpec(m2c.shape, lambda b, k: (0, 0)),
                  pl.BlockSpec((2, kb, l2, w), lambda b, k: (0, k, 0, order))],
        out_specs=pl.BlockSpec((None, 2, kb, l2, w), lambda b, k: (b, 0, k, 0, 0)),
        out_shape=jax.ShapeDtypeStruct((bsz, 2, l1, l2, w), F32),
        compiler_params=_cparams("parallel", "parallel"),
        name="hy_mid",
    )(a, tw, m2, m2c, kf)


def _hy_out_kernel(m_ref, bp_ref, xo_ref, z_ref, bias_ref, o_ref):
    y = jnp.dot(m_ref[...], bp_ref[...].astype(BF16), preferred_element_type=F32)
    o_ref[...] = xo_ref[...] * (y + bias_ref[...] * z_ref[...])


def hy_out(m, bp, xo, z, bias_row, tn):
    bsz, k, n = bp.shape
    r = m.shape[0]
    blk = pl.BlockSpec((None, r, tn), lambda b, j: (b, 0, j))
    return pl.pallas_call(
        _hy_out_kernel,
        grid=(bsz, n // tn),
        in_specs=[pl.BlockSpec((r, k), lambda b, j: (0, 0)),
                  pl.BlockSpec((None, k, tn), lambda b, j: (b, 0, j)), blk, blk,
                  pl.BlockSpec((1, tn), lambda b, j: (0, j))],
        out_specs=blk,
        out_shape=jax.ShapeDtypeStruct((bsz, r, n), F32),
        compiler_params=_cparams("parallel", "parallel"),
        name="hy_out",
    )(m, bp, xo, z, bias_row)


SUB = 8
HY_KB = 16


def _kron_stage1(l1, n1_used, kb):
    c, s = _cos_sin(l1, n1_used, l1)
    m = np.stack([c, -s], axis=0).reshape(2, l1 // kb, kb, n1_used)
    m = np.transpose(m, (1, 0, 2, 3)).reshape(l1 // kb, 2 * kb, n1_used)
    return _const_bf16(np.stack([np.kron(blk, np.eye(SUB)) for blk in m]))


def _kron_stage_out(l1, n1_used, kb):
    c, s = _cos_sin(n1_used, l1, l1)
    m = np.stack([c, -s], axis=1).reshape(n1_used, 2, l1 // kb, kb) / (l1 * l1)
    m = np.transpose(m, (2, 0, 1, 3)).reshape(l1 // kb, n1_used, 2 * kb)
    return _const_bf16(np.stack([np.kron(blk, np.eye(SUB)) for blk in m]))


def _strided_stage_in(kin_ref, src_ref, a_ref):
    n_sub, width = src_ref.shape[1] // SUB, src_ref.shape[2]
    rows = src_ref.shape[0] * SUB
    res = [jnp.dot(kin_ref[...], src_ref[:, j * SUB:(j + 1) * SUB, :].reshape(rows, width).astype(BF16),
                   preferred_element_type=F32) for j in range(n_sub)]
    for j, r in enumerate(res):
        a_ref[:, j * SUB:(j + 1) * SUB, :] = r.reshape(a_ref.shape[0], SUB, width)


def _hy_conv_kernel(z_ref, xo_ref, kin_ref, tw_ref, m2_ref, m2c_ref, kf_ref, kout_ref, bias_ref, o_ref,
                    a_ref, b_ref, *, kb):
    k = pl.program_id(1)

    @pl.when(k == 0)
    def _():
        o_ref[...] = jnp.zeros_like(o_ref)

    _strided_stage_in(kin_ref, z_ref, a_ref)
    cs = [(tw_ref[0, :, j:j + 1], tw_ref[1, :, j:j + 1]) for j in range(kb)]
    ts = [_twiddle_mul(a_ref[j], a_ref[kb + j], c, s, False) for j, (c, s) in enumerate(cs)]
    xs = [_cplx_apply(m2_ref, tr, ti) for tr, ti in ts]
    ps = [(xr * kf_ref[0, j] - xi * kf_ref[1, j], xr * kf_ref[1, j] + xi * kf_ref[0, j])
          for j, (xr, xi) in enumerate(xs)]
    bs = [_cplx_apply(m2c_ref, pr, pi) for pr, pi in ps]
    for j, ((br, bi), (c, s)) in enumerate(zip(bs, cs)):
        b_ref[j], b_ref[kb + j] = _twiddle_mul(br, bi, c, s, True)
    n_sub, width = o_ref.shape[1] // SUB, o_ref.shape[2]
    res = [jnp.dot(kout_ref[...], b_ref[:, j * SUB:(j + 1) * SUB, :].reshape(2 * kb * SUB, width).astype(BF16),
                   preferred_element_type=F32) for j in range(n_sub)]
    for j, r in enumerate(res):
        o_ref[:, j * SUB:(j + 1) * SUB, :] += r.reshape(o_ref.shape[0], SUB, width)

    @pl.when(k == pl.num_programs(1) - 1)
    def _():
        o_ref[...] = xo_ref[...] * (o_ref[...] + bias_ref[...] * z_ref[...])


def hy_conv_long(z, xo, kf, bias, order, b, n):
    w, kb = HY_WIDTH, HY_KB
    l1 = l2 = 128
    n1 = n // l2
    tw = _twiddles(l1, l2, kb)
    m2, m2c = _stage2_matrices(l2)
    kin, kout = _kron_stage1(l1, n1, kb), _kron_stage_out(l1, n1, kb)
    tok = pl.BlockSpec((None, n1, l2, w), lambda bi, k: (bi, 0, 0, 0), pipeline_mode=pl.Buffered(1))
    const = lambda a: pl.BlockSpec(a.shape, lambda bi, k: (0,) * a.ndim)
    out = pl.pallas_call(
        functools.partial(_hy_conv_kernel, kb=kb),
        grid=(b, l1 // kb),
        in_specs=[tok, tok,
                  pl.BlockSpec((None,) + kin.shape[1:], lambda bi, k: (k, 0, 0)),
                  pl.BlockSpec((None, 2, l2, kb), lambda bi, k: (k, 0, 0, 0)),
                  const(m2), const(m2c),
                  pl.BlockSpec((2, kb, l2, w), lambda bi, k: (0, k, 0, order)),
                  pl.BlockSpec((None,) + kout.shape[1:], lambda bi, k: (k, 0, 0)),
                  pl.BlockSpec((1, w), lambda bi, k: (0, 0))],
        out_specs=pl.BlockSpec((None, n1, l2, w), lambda bi, k: (bi, 0, 0, 0), pipeline_mode=pl.Buffered(1)),
        out_shape=jax.ShapeDtypeStruct((b, n1, l2, w), F32),
        scratch_shapes=[pltpu.VMEM((2 * kb, l2, w), F32), pltpu.VMEM((2 * kb, l2, w), F32)],
        compiler_params=_cparams("parallel", "arbitrary"),
        name="hy_conv_long",
    )(z.reshape(b, n1, l2, w), xo.reshape(b, n1, l2, w), kin, tw, m2, m2c, kf, kout, bias.reshape(1, w))
    return out.reshape(b * n, w)


def _hy_spec_long_kernel(kern_ref, kin_ref, tw_ref, m2_ref, s_ref, o_ref, a_ref, *, kb):
    _strided_stage_in(kin_ref, kern_ref, a_ref)
    inv = 1.0 / s_ref[...]
    ts = [_twiddle_mul(a_ref[j], a_ref[kb + j], tw_ref[0, :, j:j + 1], tw_ref[1, :, j:j + 1], False)
          for j in range(kb)]
    xs = [_cplx_apply(m2_ref, tr, ti) for tr, ti in ts]
    for j, (xr, xi) in enumerate(xs):
        o_ref[0, j] = xr * inv
        o_ref[1, j] = xi * inv


def hy_spec_long(kern, abs_sum):
    w, kb = HY_WIDTH, HY_KB
    l1 = l2 = 128
    cw = kern.shape[1]
    tw = _twiddles(l1, l2, kb)
    m2, _ = _stage2_matrices(l2)
    kin = _kron_stage1(l1, l1, kb)
    return pl.pallas_call(
        functools.partial(_hy_spec_long_kernel, kb=kb),
        grid=(cw // w, l1 // kb),
        in_specs=[pl.BlockSpec((l1, l2, w), lambda c, k: (0, 0, c), pipeline_mode=pl.Buffered(1)),
                  pl.BlockSpec((None,) + kin.shape[1:], lambda c, k: (k, 0, 0)),
                  pl.BlockSpec((None, 2, l2, kb), lambda c, k: (k, 0, 0, 0)),
                  pl.BlockSpec(m2.shape, lambda c, k: (0, 0)),
                  pl.BlockSpec((1, w), lambda c, k: (0, c))],
        out_specs=pl.BlockSpec((2, kb, l2, w), lambda c, k: (0, k, 0, c)),
        out_shape=jax.ShapeDtypeStruct((2, l1, l2, cw), F32),
        scratch_shapes=[pltpu.VMEM((2 * kb, l2, w), F32)],
        compiler_params=_cparams("parallel", "arbitrary"),
        name="hy_spec_long",
    )(kern.reshape(l1, l2, cw), kin, tw, m2, abs_sum)


def hyena_pallas(p, conv_w, w1, b1, freq1, w2, b2, freq2, w3, bias, b, n):
    w = HY_WIDTH
    x0, x1, v = seq_conv(p, conv_w, b, n)
    kern, abs_sum = hy_filter(n, w1, b1, freq1, w2, b2, freq2, w3)
    z = v
    if n == 8192:
        kf = hy_spec_long(kern, abs_sum)
        for order, xo in enumerate((x0, x1)):
            z = hy_conv_long(z, xo, kf, bias[order], order, b, n)
        return z
    l2 = 2 * n
    tw = _twiddles(1, l2, 1)
    m2, m2c = _stage2_matrices(l2)
    ak = jnp.stack([kern, jnp.zeros_like(kern)], axis=0).reshape(2, 1, l2, HY_ORDER * w)
    m_out = _const_bf16(np.eye(n, 2 * l2) / l2)
    kf = hy_spec(ak, tw, m2, abs_sum, 1)
    for order, xo in enumerate((x0, x1)):
        zp = jnp.pad(z.reshape(b, 1, 1, n, w), ((0, 0), (0, 0), (0, 0), (0, n), (0, 0)))
        a = jnp.concatenate([zp, jnp.zeros_like(zp)], axis=1)
        bp = hy_mid(a, tw, m2, m2c, kf, order, 1)
        z = hy_out(m_out, bp.reshape(b, 2 * l2, w), xo.reshape(b, n, w), z.reshape(b, n, w),
                   bias[order].reshape(1, w), w).reshape(b * n, w)
    return z


def _prep_w_in(w):
    main = jnp.concatenate([w[:, OFF_Q:OFF_Z], w[:, OFF_HY:OFF_GATE], w[:, OFF_FN:OFF_HY], w[:, OFF_Z:OFF_BETA]],
                           axis=1).astype(BF16)
    small = jnp.pad(w[:, OFF_BETA:OFF_FN], ((0, 0), (0, SMALL_WIDTH - 4 * DN_HEADS))).astype(BF16)
    return main, small, w[:, OFF_GATE:IN_WIDTH].astype(BF16)


def kernel(x, c, ctx, c_ctx, w_mod, b_mod, norm1, norm2, w_in, dn_conv, dn_a_log, dn_dt_bias,
           dn_out_norm, hy_conv, hy_w1, hy_b1, hy_freq1, hy_w2, hy_b2, hy_freq2, hy_w3, hy_bias,
           w_branch_a, w_branch_b, w_branch_c, w_out, w_ff1, w_ff2, final_norm):
    b, n_lat, d = x.shape
    rows = n_lat // GRID_W
    n_ctx = ctx.shape[1]
    tm_x, tm_c = 1024, n_ctx

    c_rows = jnp.concatenate([c, c_ctx[None], jnp.zeros((8 - b - 1, d), F32)], axis=0)
    mods = mod_vectors(c_rows, w_mod, b_mod)
    s_zero = jnp.zeros((2, b, DN_HEADS, DN_HEAD_DIM, DN_HEAD_DIM), F32)
    h, hc = x.reshape(b * n_lat, d), ctx.reshape(b * n_ctx, d)

    for l in range(DEPTH):
        last = l == DEPTH - 1
        mv = mods[l].reshape(8, N_MOD, 1, d)
        mx = [mv[:b, i] for i in range(N_MOD)]
        mc = [mv[b:b + 1, i] for i in range(N_MOD)]
        w_main, w_small, w_gate = _prep_w_in(w_in[l])
        wa, wb, wc, wo = (w.astype(BF16) for w in (w_branch_a[l], w_branch_b[l], w_branch_c[l], w_out[l]))
        w1, w2 = w_ff1[l].astype(BF16), w_ff2[l].astype(BF16)
        n1, n2 = norm1[l][None, None, :], norm2[l][None, None, :]

        p_c, small_c = in_proj(hc, n1 * (1.0 + mc[1]), mc[0], w_main, w_small, b * n_ctx, tm_c)
        p_x, small_x = in_proj(h, n1 * (1.0 + mx[1]), mx[0], w_main, w_small, n_lat, tm_x)

        def mix(p, n):
            y_c = hyena_pallas(p, hy_conv[l], hy_w1[l], hy_b1[l], hy_freq1[l], hy_w2[l], hy_b2[l], hy_freq2[l],
                               hy_w3[l], hy_bias[l], b, n)
            return fnet_branch(p, b, n), y_c

        ocf, ocb, s_ctx = delta_branch(p_c, small_c, dn_conv[l], dn_a_log[l], dn_dt_bias[l], s_zero,
                                       b, n_ctx, 1, n_ctx)
        oxf, oxb, _ = delta_branch(p_x, small_x, dn_conv[l], dn_a_log[l], dn_dt_bias[l], s_ctx,
                                   b, n_lat, rows, GRID_W)

        y_b, y_c = mix(p_x, n_lat)
        h = merge(oxf, oxb, p_x, y_b, y_c, h, n1 * (1.0 + mx[1]), mx[0], mx[2], dn_out_norm[l],
                  w_gate, wa, wb, wc, wo, n_lat, 512)
        h = mlp(h, n2 * (1.0 + mx[4]), mx[3], mx[5], w1, w2, n_lat, tm_x, final_norm if last else None)

        if not last:
            y_b, y_c = mix(p_c, n_ctx)
            hc = merge(ocf, ocb, p_c, y_b, y_c, hc, n1 * (1.0 + mc[1]), mc[0], mc[2], dn_out_norm[l],
                       w_gate, wa, wb, wc, wo, b * n_ctx, tm_c)
            hc = mlp(hc, n2 * (1.0 + mc[4]), mc[3], mc[5], w1, w2, b * n_ctx, tm_c)

    return h.reshape(b, n_lat, d)
```

```python
import functools
import math

import jax
import jax.numpy as jnp
import numpy as np
from jax import lax
from jax.experimental import pallas as pl
from jax.experimental.pallas import tpu as pltpu

D_MODEL = 1024
DEPTH = 2
GRID_W = 64
NORM_EPS = 1e-6
N_MOD = 6

DN_HEADS = 4
DN_HEAD_DIM = 128
DN_WIDTH = DN_HEADS * DN_HEAD_DIM
DN_CHUNK = 64
SHORT_CONV = 3

FN_GROUPS = 4
FN_GROUP_DIM = 64
FN_WIDTH = FN_GROUPS * FN_GROUP_DIM

HY_WIDTH = 256
HY_ORDER = 2
HY_EMB_DIM = 33
HY_BANDS = (HY_EMB_DIM - 1) // 2
HY_FILTER_HIDDEN = 64
HY_FAST_DECAY_PCT = 0.3
HY_SLOW_DECAY_PCT = 1.5
HY_DECAY_TARGET = 1e-2

N_BRANCHES = 3
D_FF = 4 * D_MODEL

OFF_Q = 0
OFF_Z = 3 * DN_WIDTH
OFF_BETA = OFF_Z + DN_WIDTH
OFF_A = OFF_BETA + 2 * DN_HEADS
OFF_FN = OFF_A + 2 * DN_HEADS
OFF_HY = OFF_FN + FN_WIDTH
OFF_GATE = OFF_HY + (HY_ORDER + 1) * HY_WIDTH
IN_WIDTH = OFF_GATE + N_BRANCHES * D_MODEL

P_QKV = 0
P_HY = P_QKV + 3 * DN_WIDTH
P_FN = P_HY + (HY_ORDER + 1) * HY_WIDTH
P_Z = P_FN + FN_WIDTH
P_WIDTH = P_Z + DN_WIDTH
SMALL_WIDTH = 128

F32 = jnp.float32
BF16 = jnp.bfloat16
VMEM_LIMIT = 56 * 1024 * 1024


def _cparams(*sem):
    return pltpu.CompilerParams(dimension_semantics=sem, vmem_limit_bytes=VMEM_LIMIT)


def _bdot(a, b):
    return jnp.dot(a.astype(BF16), b.astype(BF16), preferred_element_type=F32)


def _sigmoid(x):
    return 1.0 / (1.0 + jnp.exp(-x))


def _modnorm(xf, gs, sh):
    r = lax.rsqrt(jnp.mean(xf * xf, axis=-1, keepdims=True) + NORM_EPS)
    return xf * r * gs + sh


def _mod_kernel(c_ref, w_ref, b_ref, o_ref):
    c = c_ref[...]
    o_ref[...] = _bdot(c * _sigmoid(c), w_ref[...]) + b_ref[...]


def mod_vectors(c_rows, w_mod, b_mod):
    tn = 1536
    n = N_MOD * D_MODEL
    return pl.pallas_call(
        _mod_kernel,
        grid=(DEPTH, n // tn),
        in_specs=[pl.BlockSpec((8, D_MODEL), lambda l, j: (0, 0)),
                  pl.BlockSpec((None, D_MODEL, tn), lambda l, j: (l, 0, j)),
                  pl.BlockSpec((None, 1, tn), lambda l, j: (l, 0, j))],
        out_specs=pl.BlockSpec((None, 8, tn), lambda l, j: (l, 0, j)),
        out_shape=jax.ShapeDtypeStruct((DEPTH, 8, n), F32),
        compiler_params=_cparams("parallel", "parallel"),
        name="mod_vectors",
    )(c_rows, w_mod, b_mod.reshape(DEPTH, 1, n))


def _in_proj_kernel(x_ref, gs_ref, sh_ref, w_ref, ws_ref, p_ref, small_ref, xn_ref):
    @pl.when(pl.program_id(1) == 0)
    def _():
        xn = _modnorm(x_ref[...], gs_ref[0], sh_ref[0]).astype(BF16)
        xn_ref[...] = xn
        small_ref[...] = jnp.dot(xn, ws_ref[...], preferred_element_type=F32)

    p_ref[...] = jnp.dot(xn_ref[...], w_ref[...], preferred_element_type=F32)


def in_proj(x2d, gs, sh, w_main, w_small, rows_per_mod, tm):
    m = x2d.shape[0]
    tn = 1024
    tiles_per_mod = rows_per_mod // tm
    mod_spec = pl.BlockSpec((1, 1, D_MODEL), lambda i, j: (i // tiles_per_mod, 0, 0))
    return pl.pallas_call(
        _in_proj_kernel,
        grid=(m // tm, P_WIDTH // tn),
        in_specs=[pl.BlockSpec((tm, D_MODEL), lambda i, j: (i, 0)), mod_spec, mod_spec,
                  pl.BlockSpec((D_MODEL, tn), lambda i, j: (0, j)),
                  pl.BlockSpec((D_MODEL, SMALL_WIDTH), lambda i, j: (0, 0))],
        out_specs=[pl.BlockSpec((tm, tn), lambda i, j: (i, j)),
                   pl.BlockSpec((tm, SMALL_WIDTH), lambda i, j: (i, 0))],
        out_shape=[jax.ShapeDtypeStruct((m, P_WIDTH), F32), jax.ShapeDtypeStruct((m, SMALL_WIDTH), F32)],
        scratch_shapes=[pltpu.VMEM((tm, D_MODEL), BF16)],
        compiler_params=_cparams("parallel", "arbitrary"),
        name="in_proj",
    )(x2d, gs, sh, w_main, w_small)


def _merge_kernel(of_ref, ob_ref, z_ref, yb_ref, yc_ref, h_ref, gs_ref, sh_ref, gate_ref, nrm_ref,
                  wg_ref, wa_ref, wb_ref, wc_ref, wo_ref, out_ref):
    xn = _modnorm(h_ref[...], gs_ref[0], sh_ref[0]).astype(BF16)
    o = of_ref[...] + ob_ref[...]
    z = z_ref[...]
    heads = []
    for hd in range(DN_HEADS):
        sl = slice(hd * DN_HEAD_DIM, (hd + 1) * DN_HEAD_DIM)
        oh, zh = o[:, sl], z[:, sl]
        r = lax.rsqrt(jnp.mean(oh * oh, axis=-1, keepdims=True) + NORM_EPS)
        heads.append(oh * r * nrm_ref[...] * (zh * _sigmoid(zh)))
    ya = jnp.concatenate(heads, axis=-1)
    merged = None
    for i, (y, w_ref) in enumerate(((ya, wa_ref), (yb_ref[...], wb_ref), (yc_ref[...], wc_ref))):
        g = jnp.dot(xn, wg_ref[:, i * D_MODEL:(i + 1) * D_MODEL], preferred_element_type=F32)
        term = _sigmoid(g) * _bdot(y, w_ref[...])
        merged = term if merged is None else merged + term
    out_ref[...] = h_ref[...] + gate_ref[0] * _bdot(merged, wo_ref[...])


def merge(o_f, o_b, p, y_b, y_c, h2d, gs, sh, gate, dn_out_norm, wg, wa, wb, wc, wo, rows_per_mod, tm):
    m = h2d.shape[0]
    tiles_per_mod = rows_per_mod // tm
    row = lambda w: pl.BlockSpec((tm, w), lambda i: (i, 0))
    full = lambda a: pl.BlockSpec(a.shape, lambda i: (0,) * a.ndim, pipeline_mode=pl.Buffered(1))
    mod_spec = pl.BlockSpec((1, 1, D_MODEL), lambda i: (i // tiles_per_mod, 0, 0))
    nrm = dn_out_norm.reshape(1, DN_HEAD_DIM)
    return pl.pallas_call(
        _merge_kernel,
        grid=(m // tm,),
        in_specs=[row(DN_WIDTH), row(DN_WIDTH),
                  pl.BlockSpec((tm, DN_WIDTH), lambda i: (i, P_Z // DN_WIDTH)),
                  row(FN_WIDTH), row(HY_WIDTH), row(D_MODEL), mod_spec, mod_spec, mod_spec,
                  full(nrm), full(wg), full(wa), full(wb), full(wc), full(wo)],
        out_specs=row(D_MODEL),
        out_shape=jax.ShapeDtypeStruct((m, D_MODEL), F32),
        compiler_params=_cparams("parallel"),
        name="merge",
    )(o_f, o_b, p, y_b, y_c, h2d, gs, sh, gate, nrm, wg, wa, wb, wc, wo)


def _mlp_kernel(h_ref, gs_ref, sh_ref, gate_ref, w1_ref, w2_ref, fin_ref, out_ref, xn_ref, acc_ref, *, final):
    j = pl.program_id(1)

    @pl.when(j == 0)
    def _():
        xn_ref[...] = _modnorm(h_ref[...], gs_ref[0], sh_ref[0]).astype(BF16)
        acc_ref[...] = jnp.zeros_like(acc_ref)

    a = jnp.maximum(jnp.dot(xn_ref[...], w1_ref[...], preferred_element_type=F32), 0.0)
    acc_ref[...] += jnp.dot((a * a).astype(BF16), w2_ref[...], preferred_element_type=F32)

    @pl.when(j == pl.num_programs(1) - 1)
    def _():
        y = h_ref[...] + gate_ref[0] * acc_ref[...]
        if final:
            y = y * lax.rsqrt(jnp.mean(y * y, axis=-1, keepdims=True) + NORM_EPS) * fin_ref[...]
        out_ref[...] = y


def mlp(h2d, gs, sh, gate, w1, w2, rows_per_mod, tm, final_gain=None):
    m = h2d.shape[0]
    tf = 512
    tiles_per_mod = rows_per_mod // tm
    mod_spec = pl.BlockSpec((1, 1, D_MODEL), lambda i, j: (i // tiles_per_mod, 0, 0))
    final = final_gain is not None
    fin = (final_gain if final else jnp.ones((D_MODEL,), F32)).reshape(1, D_MODEL)
    return pl.pallas_call(
        functools.partial(_mlp_kernel, final=final),
        grid=(m // tm, D_FF // tf),
        in_specs=[pl.BlockSpec((tm, D_MODEL), lambda i, j: (i, 0)), mod_spec, mod_spec, mod_spec,
                  pl.BlockSpec((D_MODEL, tf), lambda i, j: (0, j)),
                  pl.BlockSpec((tf, D_MODEL), lambda i, j: (j, 0)),
                  pl.BlockSpec((1, D_MODEL), lambda i, j: (0, 0))],
        out_specs=pl.BlockSpec((tm, D_MODEL), lambda i, j: (i, 0)),
        out_shape=jax.ShapeDtypeStruct((m, D_MODEL), F32),
        scratch_shapes=[pltpu.VMEM((tm, D_MODEL), BF16), pltpu.VMEM((tm, D_MODEL), F32)],
        compiler_params=_cparams("parallel", "arbitrary"),
        name="mlp",
    )(h2d, gs, sh, gate, w1, w2, fin)


def _dnconv_kernel(prev_ref, cur_ref, next_ref, w_ref, o_ref, *, cols, n_tiles):
    t, j = pl.program_id(1), pl.program_id(2)
    tt = cur_ref.shape[0]
    prev = jnp.where(t == 0, 0.0, prev_ref[...])
    nxt = jnp.where(t == n_tiles - 1, 0.0, next_ref[...])
    ext = jnp.concatenate([prev, cur_ref[...], nxt], axis=0)
    n_ext = tt + 2 * cols
    col = lax.broadcasted_iota(jnp.int32, (n_ext, 1), 0) % cols
    left = jnp.where(col == 0, 0.0, pltpu.roll(ext, 1, axis=0))
    right = jnp.where(col == cols - 1, 0.0, pltpu.roll(ext, n_ext - 1, axis=0))
    acc = jnp.zeros((tt, DN_WIDTH), F32)
    for dr in range(SHORT_CONV):
        base = dr * cols
        acc = (acc + w_ref[3 * dr:3 * dr + 1, :] * left[base:base + tt]
               + w_ref[3 * dr + 1:3 * dr + 2, :] * ext[base:base + tt]
               + w_ref[3 * dr + 2:3 * dr + 3, :] * right[base:base + tt])
    y = acc * _sigmoid(acc)
    q_scale = jnp.where(j == 0, DN_HEAD_DIM ** -0.5, 1.0)
    for hd in range(DN_HEADS):
        sl = slice(hd * DN_HEAD_DIM, (hd + 1) * DN_HEAD_DIM)
        yh = y[:, sl]
        nrm = lax.rsqrt(jnp.sum(yh * yh, axis=-1, keepdims=True) + NORM_EPS) * q_scale
        o_ref[:, sl] = yh * jnp.where(j < 2, nrm, 1.0)


def dn_conv_prep(p, conv_w, b, n, rows, cols):
    tr = min(rows, 16)
    tt = tr * cols
    n_tiles = rows // tr
    nblk = b * n // cols
    c0 = P_QKV // DN_WIDTH
    return pl.pallas_call(
        functools.partial(_dnconv_kernel, cols=cols, n_tiles=n_tiles),
        grid=(b, n_tiles, 3),
        in_specs=[pl.BlockSpec((cols, DN_WIDTH),
                               lambda bi, t, j: (jnp.maximum(bi * rows + t * tr - 1, 0), c0 + j)),
                  pl.BlockSpec((tt, DN_WIDTH), lambda bi, t, j: (bi * n_tiles + t, c0 + j)),
                  pl.BlockSpec((cols, DN_WIDTH),
                               lambda bi, t, j: (jnp.minimum(bi * rows + (t + 1) * tr, nblk - 1), c0 + j)),
                  pl.BlockSpec((SHORT_CONV * SHORT_CONV, DN_WIDTH), lambda bi, t, j: (0, j))],
        out_specs=pl.BlockSpec((tt, DN_WIDTH), lambda bi, t, j: (bi * n_tiles + t, j)),
        out_shape=jax.ShapeDtypeStruct((b * n, 3 * DN_WIDTH), F32),
        compiler_params=_cparams("parallel", "parallel", "parallel"),
        name="dn_conv_prep",
    )(p, p, p, conv_w.reshape(SHORT_CONV * SHORT_CONV, 3 * DN_WIDTH))


PREP_CHUNKS = 8
PREP_GROUP = 2


def _softplus(x):
    return jnp.maximum(x, 0.0) + jnp.log(1.0 + jnp.exp(-jnp.abs(x)))


def _delta_prep_kernel(qkv_ref, sm_ref, smt_ref, prm_ref, prmt_ref, u0_ref, lhs1_ref, lhs2_ref, gl_ref, *, n_chunks):
    cc = DN_CHUNK
    nh = DN_HEADS
    sm = sm_ref[...]
    beta_all = _sigmoid(sm)
    g_all = -jnp.exp(prm_ref[0:1, :]) * _softplus(sm + prm_ref[1:2, :])
    gt_all = -jnp.exp(prmt_ref[:, 0:1]) * _softplus(smt_ref[...] + prmt_ref[:, 1:2])
    lt = 2 * cc
    lane = lax.broadcasted_iota(jnp.int32, (gt_all.shape[0], lt), 1) % cc
    gt_tiles = []
    for t in range(gt_all.shape[1] // lt):
        gt_f = gt_b = gt_all[:, t * lt:(t + 1) * lt]
        s = 1
        while s < cc:
            gt_f = gt_f + jnp.where(lane >= s, pltpu.roll(gt_f, s, axis=1), 0.0)
            gt_b = gt_b + jnp.where(lane < cc - s, pltpu.roll(gt_b, lt - s, axis=1), 0.0)
            s *= 2
        gt_tiles.append((gt_f, gt_b))
    ri = lax.broadcasted_iota(jnp.int32, (cc, cc), 0)
    ci_ = lax.broadcasted_iota(jnp.int32, (cc, cc), 1)
    sub = lax.broadcasted_iota(jnp.int32, (cc, 1), 0)
    blk = lambda s: (ri // s) == (ci_ // s)
    leaf = 8

    heads = []
    for ci in range(n_chunks):
        rows = slice(ci * cc, (ci + 1) * cc)
        gc_f = g_all[rows]
        gc_b = gc_f
        s = 1
        while s < cc:
            gc_f = gc_f + jnp.where(sub >= s, pltpu.roll(gc_f, s, axis=0), 0.0)
            gc_b = gc_b + jnp.where(sub < cc - s, pltpu.roll(gc_b, cc - s, axis=0), 0.0)
            s *= 2
        for h in range(nh):
            q = qkv_ref[rows, h * 128:(h + 1) * 128]
            k = qkv_ref[rows, (nh + h) * 128:(nh + h + 1) * 128]
            v = qkv_ref[rows, (2 * nh + h) * 128:(2 * nh + h + 1) * 128]
            heads.append((ci, rows, h, q, k, v, gc_f, gc_b))
    qkks = [lax.dot_general(jnp.concatenate([q, k], axis=0).astype(BF16), k.astype(BF16),
                            (((1,), (1,)), ((), ())), preferred_element_type=F32)
            for (_, _, _, q, k, _, _, _) in heads]
    per_group = PREP_GROUP * nh
    for g0 in range(0, len(heads), per_group):
        chains = []
        for (ci, rows, h, q, k, v, gc_f, gc_b), qkk in zip(heads[g0:g0 + per_group], qkks[g0:g0 + per_group]):
            qk, kk = qkk[:cc], qkk[cc:]
            for d in range(2):
                cb_, ca_ = d * nh + h, 2 * nh + d * nh + h
                beta = jnp.broadcast_to(beta_all[rows, cb_:cb_ + 1], (cc, DN_HEAD_DIM))
                g_col = jnp.broadcast_to((gc_f if d == 0 else gc_b)[:, ca_:ca_ + 1], (cc, DN_HEAD_DIM))
                lo = (ci % 2) * cc
                g_row = gt_tiles[ci // 2][d][ca_:ca_ + 1, lo:lo + cc]
                incl = (ri >= ci_) if d == 0 else (ri <= ci_)
                strict = (ri > ci_) if d == 0 else (ri < ci_)
                decay = jnp.exp(jnp.where(incl, g_col[:, :cc] - g_row, -1e30))
                a = jnp.where(strict, beta[:, :cc] * kk * decay, 0.0)
                eg = jnp.exp(g_col)
                g_last = g_col[cc - 1:cc] if d == 0 else g_col[0:1]
                chains.append(dict(ci=ci, rows=rows, h=h, d=d, a=a, qkd=qk * decay, g_last=g_last,
                                   rhs=jnp.concatenate([v * beta, k * (beta * eg)], axis=1),
                                   q_d=q * eg, k_d=k * jnp.exp(g_last - g_col)))

        pws = [jnp.where(blk(leaf), c["a"], 0.0) for c in chains]
        devs = [-pw for pw in pws]
        for _ in range(2):
            pws = [_bdot(pw, pw) for pw in pws]
            cross = [_bdot(dev, pw) for dev, pw in zip(devs, pws)]
            devs = [dev + pw + x for dev, pw, x in zip(devs, pws, cross)]
        s = leaf
        while s < cc:
            offs = [jnp.where(blk(2 * s) & jnp.logical_not(blk(s)), c["a"], 0.0) for c in chains]
            xs = [off + _bdot(dev, off) for dev, off in zip(devs, offs)]
            devs = [dev - x - _bdot(x, dev) for dev, x in zip(devs, xs)]
            s *= 2
        uws = [c["rhs"] + _bdot(dev, c["rhs"]) for dev, c in zip(devs, chains)]
        for c, uw in zip(chains, uws):
            d, h, ci = c["d"], c["h"], c["ci"]
            u0_ref[d, h, c["rows"], :] = uw[:, :128]
            lhs1_ref[d, h, ci] = jnp.concatenate([uw[:, 128:], c["q_d"]], axis=0).astype(BF16)
            lhs2_ref[d, h, ci] = jnp.concatenate([c["qkd"], c["k_d"].T], axis=0).astype(BF16)
            gl_ref[d, h, ci] = jnp.broadcast_to(jnp.exp(c["g_last"]), (8, DN_HEAD_DIM))


def delta_prep(qkv, small, small_t, prm, prm_t, b, n):
    nc = n // DN_CHUNK
    cb = min(PREP_CHUNKS, nc)
    nblk = nc // cb
    tt = cb * DN_CHUNK
    nh = DN_HEADS
    return pl.pallas_call(
        functools.partial(_delta_prep_kernel, n_chunks=cb),
        grid=(b, nblk),
        in_specs=[pl.BlockSpec((tt, 3 * DN_WIDTH), lambda bi, c: (bi * nblk + c, 0)),
                  pl.BlockSpec((tt, SMALL_WIDTH), lambda bi, c: (bi * nblk + c, 0)),
                  pl.BlockSpec((4 * nh, tt), lambda bi, c: (0, bi * nblk + c)),
                  pl.BlockSpec((2, SMALL_WIDTH), lambda bi, c: (0, 0)),
                  pl.BlockSpec((4 * nh, 2), lambda bi, c: (0, 0))],
        out_specs=[pl.BlockSpec((2, None, nh, tt, 128), lambda bi, c: (0, bi, 0, c, 0)),
                   pl.BlockSpec((2, None, nh, cb, 128, 128), lambda bi, c: (0, bi, 0, c, 0, 0)),
                   pl.BlockSpec((2, None, nh, cb, 192, 64), lambda bi, c: (0, bi, 0, c, 0, 0)),
                   pl.BlockSpec((2, None, nh, cb, 8, 128), lambda bi, c: (0, bi, 0, c, 0, 0))],
        out_shape=[jax.ShapeDtypeStruct((2, b, nh, n, 128), F32),
                   jax.ShapeDtypeStruct((2, b, nh, nc, 128, 128), BF16),
                   jax.ShapeDtypeStruct((2, b, nh, nc, 192, 64), BF16),
                   jax.ShapeDtypeStruct((2, b, nh, nc, 8, 128), F32)],
        compiler_params=_cparams("parallel", "parallel"),
        name="delta_prep",
    )(qkv, small, small_t, prm, prm_t)


def _delta_scan_kernel(u0f_ref, u0b_ref, l1f_ref, l1b_ref, l2f_ref, l2b_ref, glf_ref, glb_ref, s0_ref,
                       of_ref, ob_ref, sout_ref, st_ref, *, sc, nb):
    n = pl.program_id(0)
    cc = DN_CHUNK

    @pl.when(n == 0)
    def _():
        st_ref[...] = s0_ref[...]

    dirs = ((u0f_ref, l1f_ref, l2f_ref, glf_ref, of_ref), (u0b_ref, l1b_ref, l2b_ref, glb_ref, ob_ref))

    def body(i, carry):
        chains = []
        for d in range(2):
            ci = i if d == 0 else sc - 1 - i
            r0 = pl.multiple_of(ci * cc, cc)
            chains += [(d, bi, h, ci, r0) for bi in range(nb) for h in range(DN_HEADS)]
        sts = [st_ref[d, bi, h] for (d, bi, h, _, _) in chains]
        r1s = [jnp.dot(dirs[d][1][bi, h, ci], st.astype(BF16), preferred_element_type=F32)
               for (d, bi, h, ci, _), st in zip(chains, sts)]
        us = [dirs[d][0][bi, h, pl.ds(r0, cc), :] - r1[:cc] for (d, bi, h, _, r0), r1 in zip(chains, r1s)]
        r2s = [jnp.dot(dirs[d][2][bi, h, ci], u.astype(BF16), preferred_element_type=F32)
               for (d, bi, h, ci, _), u in zip(chains, us)]
        for (d, bi, h, ci, r0), st, r1, r2 in zip(chains, sts, r1s, r2s):
            dirs[d][4][bi, pl.ds(r0, cc), h * 128:(h + 1) * 128] = r1[cc:] + r2[:cc]
            st_ref[d, bi, h] = st * dirs[d][3][bi, h, ci, 0:1, :] + r2[cc:]
        return carry

    lax.fori_loop(0, sc, body, 0)

    @pl.when(n == pl.num_programs(0) - 1)
    def _():
        sout_ref[...] = st_ref[...]


def delta_scan(u0, lhs1, lhs2, gl, s0, b, n):
    nc = n // DN_CHUNK
    sc = min(8, nc)
    nblk = nc // sc
    nh = DN_HEADS
    tt = sc * DN_CHUNK
    fwd = lambda i: i
    bwd = lambda i: nblk - 1 - i

    def specs(d, blk):
        return [pl.BlockSpec((None, b, nh, tt, 128), lambda i: (d, 0, 0, blk(i), 0)),
                pl.BlockSpec((None, b, nh, sc, 128, 128), lambda i: (d, 0, 0, blk(i), 0, 0)),
                pl.BlockSpec((None, b, nh, sc, 192, 64), lambda i: (d, 0, 0, blk(i), 0, 0)),
                pl.BlockSpec((None, b, nh, sc, 8, 128), lambda i: (d, 0, 0, blk(i), 0, 0))]

    sf, sb = specs(0, fwd), specs(1, bwd)
    in_specs = [sf[0], sb[0], sf[1], sb[1], sf[2], sb[2], sf[3], sb[3],
                pl.BlockSpec((2, b, nh, 128, 128), lambda i: (0, 0, 0, 0, 0))]
    return pl.pallas_call(
        functools.partial(_delta_scan_kernel, sc=sc, nb=b),
        grid=(nblk,),
        in_specs=in_specs,
        out_specs=[pl.BlockSpec((b, tt, DN_WIDTH), lambda i: (0, fwd(i), 0)),
                   pl.BlockSpec((b, tt, DN_WIDTH), lambda i: (0, bwd(i), 0)),
                   pl.BlockSpec((2, b, nh, 128, 128), lambda i: (0, 0, 0, 0, 0))],
        out_shape=[jax.ShapeDtypeStruct((b, n, DN_WIDTH), F32), jax.ShapeDtypeStruct((b, n, DN_WIDTH), F32),
                   jax.ShapeDtypeStruct((2, b, nh, 128, 128), F32)],
        scratch_shapes=[pltpu.VMEM((2, b, nh, 128, 128), F32)],
        compiler_params=_cparams("arbitrary"),
        name="delta_scan",
    )(u0, u0, lhs1, lhs1, lhs2, lhs2, gl, gl, s0)


def delta_branch(p, small, conv_w, a_log, dt_bias, s0, b, n, rows, cols):
    nh = DN_HEADS
    qkv = dn_conv_prep(p, conv_w, b, n, rows, cols)
    small_t = small[:, :4 * nh].T
    rate = jnp.concatenate([jnp.zeros((2 * nh,), F32), a_log.reshape(-1)])
    bias = jnp.concatenate([jnp.zeros((2 * nh,), F32), dt_bias.reshape(-1)])
    prm_t = jnp.stack([rate, bias], axis=1)
    prm = jnp.pad(prm_t.T, ((0, 0), (0, SMALL_WIDTH - 4 * nh)))
    u0, lhs1, lhs2, gl = delta_prep(qkv, small, small_t, prm, prm_t, b, n)
    o_f, o_b, s_out = delta_scan(u0, lhs1, lhs2, gl, s0, b, n)
    return o_f.reshape(b * n, DN_WIDTH), o_b.reshape(b * n, DN_WIDTH), s_out


def _cos_sin(rows, cols, period):
    ang = 2.0 * np.pi * ((np.arange(rows)[:, None] * np.arange(cols)[None, :]) % period) / period
    return np.cos(ang), np.sin(ang)


def _const_bf16(a):
    return jnp.asarray(a, F32).astype(BF16)


def _stage2_matrices(l2):
    c, s = _cos_sin(l2, l2, l2)
    fwd = np.block([[c, s], [-s, c]])
    inv = np.block([[c, -s], [s, c]])
    return _const_bf16(fwd), _const_bf16(inv)


def _twiddles(l1, l2, kb):
    ang = 2.0 * np.pi * ((np.arange(l2)[:, None] * np.arange(l1)[None, :]) % (l1 * l2)) / (l1 * l2)
    tw = np.stack([np.cos(ang), np.sin(ang)], axis=0).reshape(2, l2, l1 // kb, kb)
    return jnp.asarray(np.transpose(tw, (2, 0, 1, 3)), F32)


def _twiddle_mul(ar, ai, c, s, conj):
    if conj:
        return ar * c - ai * s, ai * c + ar * s
    return ar * c + ai * s, ai * c - ar * s


def _cplx_apply(m_ref, re, im):
    half = re.shape[0]
    out = jnp.dot(m_ref[...], jnp.concatenate([re, im], axis=0).astype(BF16), preferred_element_type=F32)
    return out[:half], out[half:]


def _fnet_mid_kernel(a_ref, tw_ref, m2_ref, cs_ref, o_ref, *, kb, ch):
    ts = [_twiddle_mul(a_ref[0, j], a_ref[1, j], tw_ref[0, :, j:j + 1], tw_ref[1, :, j:j + 1], False)
          for j in range(kb)]
    us = [_cplx_apply(m2_ref, tr, ti) for tr, ti in ts]
    ys = [jnp.dot(jnp.concatenate([ur, ui], axis=1).astype(BF16), cs_ref[...], preferred_element_type=F32)
          for ur, ui in us]
    for j, y in enumerate(ys):
        o_ref[:, j * ch:(j + 1) * ch] = y


def fnet_mid(a, tw, m2, cs, kb):
    bsz, _, l1, l2, ch = a.shape
    return pl.pallas_call(
        functools.partial(_fnet_mid_kernel, kb=kb, ch=ch),
        grid=(bsz, l1 // kb),
        in_specs=[pl.BlockSpec((None, 2, kb, l2, ch), lambda b, k: (b, 0, k, 0, 0)),
                  pl.BlockSpec((None, 2, l2, kb), lambda b, k: (k, 0, 0, 0)),
                  pl.BlockSpec(m2.shape, lambda b, k: (0, 0)),
                  pl.BlockSpec(cs.shape, lambda b, k: (0, 0))],
        out_specs=pl.BlockSpec((None, l2, kb * ch), lambda b, k: (b, 0, k)),
        out_shape=jax.ShapeDtypeStruct((bsz, l2, l1 * ch), F32),
        compiler_params=_cparams("parallel", "parallel"),
        name="fnet_mid",
    )(a, tw, m2, cs)


def _fnet_channel_matrix(n):
    c, s = _cos_sin(FN_GROUP_DIM, FN_GROUP_DIM, FN_GROUP_DIM)
    eye = np.eye(FN_GROUPS)
    scale = 1.0 / math.sqrt(n * FN_GROUP_DIM)
    return _const_bf16(np.concatenate([np.kron(eye, c), np.kron(eye, s)], axis=0) * scale)


FN_SUB = 8


def _fnet_long_kernel(x_ref, kin_ref, tw_ref, m2k_ref, cs_ref, o_ref, a_ref):
    n_sub = x_ref.shape[1] // FN_SUB
    rows, width = x_ref.shape[0] * FN_SUB, x_ref.shape[2]
    half = FN_SUB * FN_SUB
    for j in range(n_sub):
        r = jnp.dot(kin_ref[...], x_ref[:, j * FN_SUB:(j + 1) * FN_SUB, :].reshape(rows, width).astype(BF16),
                    preferred_element_type=F32)
        a_ref[0, j * half:(j + 1) * half, :] = r[:half]
        a_ref[1, j * half:(j + 1) * half, :] = r[half:]
    tr, ti = _twiddle_mul(a_ref[0], a_ref[1], tw_ref[0], tw_ref[1], False)
    ur, ui = _cplx_apply(m2k_ref, tr, ti)
    y = jnp.dot(jnp.concatenate([ur, ui], axis=1).astype(BF16), cs_ref[...], preferred_element_type=F32)
    o_ref[...] = y.reshape(o_ref.shape)


def fnet_long(p, b, n):
    ch, sub = FN_WIDTH, FN_SUB
    l1, l2 = n // 128, 128
    nblk = l1 // sub
    c1, s1 = _cos_sin(l1, l1, l1)
    m1 = np.stack([c1, -s1], axis=0).reshape(2, nblk, sub, l1)
    kin = np.einsum('pbkn,jq->bpjknq', m1, np.eye(sub)).reshape(nblk, 2 * sub * sub, l1 * sub)
    ang = 2.0 * np.pi * ((np.arange(l2)[:, None] * np.arange(l1)[None, :]) % n) / n
    tw = np.stack([np.cos(ang), np.sin(ang)], axis=0).reshape(2, l2, nblk, sub)
    tw = np.transpose(tw, (2, 0, 1, 3)).reshape(nblk, 2, l2 * sub, 1)
    c2, s2 = _cos_sin(l2, l2, l2)
    m2k = _const_bf16(np.kron(np.block([[c2, s2], [-s2, c2]]), np.eye(sub)))
    cs = _fnet_channel_matrix(n)
    const = lambda a: pl.BlockSpec(a.shape, lambda bi, k: (0,) * a.ndim, pipeline_mode=pl.Buffered(1))
    y = pl.pallas_call(
        _fnet_long_kernel,
        grid=(b, nblk),
        in_specs=[pl.BlockSpec((None, l1, l2, ch), lambda bi, k: (bi, 0, 0, P_FN // ch),
                               pipeline_mode=pl.Buffered(1)),
                  pl.BlockSpec((None,) + kin.shape[1:], lambda bi, k: (k, 0, 0)),
                  pl.BlockSpec((None, 2, l2 * sub, 1), lambda bi, k: (k, 0, 0, 0)),
                  const(m2k), const(cs)],
        out_specs=pl.BlockSpec((None, l2, sub, ch), lambda bi, k: (bi, 0, k, 0)),
        out_shape=jax.ShapeDtypeStruct((b, l2, l1, ch), F32),
        scratch_shapes=[pltpu.VMEM((2, l2 * sub, ch), F32)],
        compiler_params=_cparams("parallel", "parallel"),
        name="fnet_long",
    )(p.reshape(b, l1, l2, P_WIDTH), _const_bf16(kin), jnp.asarray(tw, F32), m2k, cs)
    return y.reshape(b * n, ch)


def fnet_branch(p, b, n):
    if n == 8192:
        return fnet_long(p, b, n)
    ch = FN_WIDTH
    xr = lax.slice_in_dim(p, P_FN, P_FN + ch, axis=1).reshape(b, 1, 1, n, ch)
    a = jnp.concatenate([xr, jnp.zeros_like(xr)], axis=1)
    y = fnet_mid(a, _twiddles(1, n, 1), _stage2_matrices(n)[0], _fnet_channel_matrix(n), 1)
    return y.reshape(b * n, ch)


def _seq_conv_kernel(prev_ref, cur_ref, next_ref, w_ref, *o_refs, n_tiles):
    t = pl.program_id(1)
    tt = cur_ref.shape[0]
    cur = cur_ref[...]
    row = lax.broadcasted_iota(jnp.int32, (tt, 1), 0)
    before = jnp.where(t == 0, 0.0, prev_ref[7:8, :])
    after = jnp.where(t == n_tiles - 1, 0.0, next_ref[0:1, :])
    left = jnp.where(row == 0, before, pltpu.roll(cur, 1, axis=0))
    right = jnp.where(row == tt - 1, after, pltpu.roll(cur, tt - 1, axis=0))
    y = w_ref[0:1, :] * left + w_ref[1:2, :] * cur + w_ref[2:3, :] * right
    for part, o_ref in enumerate(o_refs):
        o_ref[...] = y[:, part * HY_WIDTH:(part + 1) * HY_WIDTH]


def seq_conv(p, conv_w, b, n):
    tt = min(n, 1024)
    n_tiles = n // tt
    parts = HY_ORDER + 1
    w = parts * HY_WIDTH
    c0 = P_HY // w
    out_spec = pl.BlockSpec((tt, HY_WIDTH), lambda bi, t: (bi * n_tiles + t, 0))
    return pl.pallas_call(
        functools.partial(_seq_conv_kernel, n_tiles=n_tiles),
        grid=(b, n_tiles),
        in_specs=[pl.BlockSpec((8, w), lambda bi, t: (jnp.maximum((bi * n_tiles + t) * (tt // 8) - 1, 0), c0)),
                  pl.BlockSpec((tt, w), lambda bi, t: (bi * n_tiles + t, c0)),
                  pl.BlockSpec((8, w), lambda bi, t: (jnp.minimum((bi * n_tiles + t + 1) * (tt // 8),
                                                                  b * n // 8 - 1), c0)),
                  pl.BlockSpec((SHORT_CONV, w), lambda bi, t: (0, 0))],
        out_specs=[out_spec] * parts,
        out_shape=[jax.ShapeDtypeStruct((b * n, HY_WIDTH), F32)] * parts,
        compiler_params=_cparams("parallel", "parallel"),
        name="hy_seq_conv",
    )(p, p, p, conv_w)


def _hdot(a, b):
    return jnp.dot(a, b, preferred_element_type=F32, precision=lax.Precision.HIGHEST)


def _hy_filter_kernel(f_ref, w1_ref, b1_ref, f1_ref, w2_ref, b2_ref, f2_ref, w3_ref, dl_ref, k_ref, s_ref, *, n, tr):
    i = pl.program_id(0)
    feats = f_ref[...]
    hid = jnp.sin(f1_ref[...] * (_hdot(feats, w1_ref[...]) + b1_ref[...]))
    hid = jnp.sin(f2_ref[...] * (_hdot(hid, w2_ref[...]) + b2_ref[...]))
    filt = _hdot(hid, w3_ref[...]) * jnp.exp(-feats[:, 0:1] * dl_ref[...])
    row = i * tr + lax.broadcasted_iota(jnp.int32, (tr, 1), 0)
    filt = jnp.where(row == n, 0.0, filt)
    k_ref[...] = filt

    @pl.when(i == 0)
    def _():
        s_ref[...] = jnp.zeros_like(s_ref)

    s_ref[...] += jnp.sum(jnp.abs(filt), axis=0, keepdims=True)


def hy_filter(n, w1, b1, freq1, w2, b2, freq2, w3):
    pos = jnp.arange(n, dtype=F32)
    t = pos / max(n - 1, 1)
    bands = jnp.linspace(1e-4, HY_BANDS - 1, HY_BANDS, dtype=F32)
    ang = (2.0 * math.pi / n) * pos[:, None] * bands[None, :]
    feats = jnp.concatenate([t[:, None], jnp.cos(ang), -jnp.sin(ang)], axis=-1)
    feats2 = jnp.concatenate([feats, feats[:1], feats[:0:-1]], axis=0)
    kpad = 128
    feats2 = jnp.pad(feats2, ((0, 0), (0, kpad - HY_EMB_DIM)))
    w1p = jnp.pad(w1, ((0, kpad - HY_EMB_DIM), (0, 0)))
    min_decay = math.log(HY_DECAY_TARGET) / HY_SLOW_DECAY_PCT
    max_decay = math.log(HY_DECAY_TARGET) / HY_FAST_DECAY_PCT
    cw = HY_ORDER * HY_WIDTH
    deltas = jnp.abs(jnp.linspace(min_decay, max_decay, cw, dtype=F32)).reshape(1, cw)
    tr = min(n, 1024)
    half = n // tr
    hd = HY_FILTER_HIDDEN
    vec = lambda v: v.reshape(1, hd)
    full = lambda shp: pl.BlockSpec(shp, lambda i: (0, 0))
    return pl.pallas_call(
        functools.partial(_hy_filter_kernel, n=n, tr=tr),
        grid=(2 * half,),
        in_specs=[pl.BlockSpec((tr, kpad), lambda i: (i, 0)), full((kpad, hd)), full((1, hd)), full((1, hd)),
                  full((hd, hd)), full((1, hd)), full((1, hd)),
                  pl.BlockSpec((hd, cw), lambda i: (0, i // half)), full((1, cw))],
        out_specs=[pl.BlockSpec((tr, cw), lambda i: (i, 0)), full((1, cw))],
        out_shape=[jax.ShapeDtypeStruct((2 * n, cw), F32), jax.ShapeDtypeStruct((1, cw), F32)],
        compiler_params=_cparams("arbitrary"),
        name="hy_filter",
    )(feats2, w1p, vec(b1), vec(freq1), w2, vec(b2), vec(freq2), w3, deltas)


def _hy_spec_kernel(a_ref, tw_ref, m2_ref, s_ref, o_ref, *, kb):
    inv = 1.0 / s_ref[...]
    ts = [_twiddle_mul(a_ref[0, j], a_ref[1, j], tw_ref[0, :, j:j + 1], tw_ref[1, :, j:j + 1], False)
          for j in range(kb)]
    xs = [_cplx_apply(m2_ref, tr, ti) for tr, ti in ts]
    for j, (xr, xi) in enumerate(xs):
        o_ref[0, j] = xr * inv
        o_ref[1, j] = xi * inv


def hy_spec(a, tw, m2, abs_sum, kb):
    _, l1, l2, cw = a.shape
    w = HY_WIDTH
    return pl.pallas_call(
        functools.partial(_hy_spec_kernel, kb=kb),
        grid=(l1 // kb, cw // w),
        in_specs=[pl.BlockSpec((2, kb, l2, w), lambda k, c: (0, k, 0, c)),
                  pl.BlockSpec((None, 2, l2, kb), lambda k, c: (k, 0, 0, 0)),
                  pl.BlockSpec(m2.shape, lambda k, c: (0, 0)),
                  pl.BlockSpec((1, w), lambda k, c: (0, c))],
        out_specs=pl.BlockSpec((2, kb, l2, w), lambda k, c: (0, k, 0, c)),
        out_shape=jax.ShapeDtypeStruct((2, l1, l2, cw), F32),
        compiler_params=_cparams("parallel", "parallel"),
        name="hy_spec",
    )(a, tw, m2, abs_sum)


def _hy_mid_kernel(a_ref, tw_ref, m2_ref, m2c_ref, kf_ref, o_ref, *, kb):
    cs = [(tw_ref[0, :, j:j + 1], tw_ref[1, :, j:j + 1]) for j in range(kb)]
    ts = [_twiddle_mul(a_ref[0, j], a_ref[1, j], c, s, False) for j, (c, s) in enumerate(cs)]
    xs = [_cplx_apply(m2_ref, tr, ti) for tr, ti in ts]
    ps = [(xr * kf_ref[0, j] - xi * kf_ref[1, j], xr * kf_ref[1, j] + xi * kf_ref[0, j])
          for j, (xr, xi) in enumerate(xs)]
    bs = [_cplx_apply(m2c_ref, pr, pi) for pr, pi in ps]
    for j, ((br, bi), (c, s)) in enumerate(zip(bs, cs)):
        o_ref[0, j], o_ref[1, j] = _twiddle_mul(br, bi, c, s, True)


def hy_mid(a, tw, m2, m2c, kf, order, kb):
    bsz, _, l1, l2, w = a.shape
    return pl.pallas_call(
        functools.partial(_hy_mid_kernel, kb=kb),
        grid=(bsz, l1 // kb),
        in_specs=[pl.BlockSpec((None, 2, kb, l2, w), lambda b, k: (b, 0, k, 0, 0)),
                  pl.BlockSpec((None, 2, l2, kb), lambda b, k: (k, 0, 0, 0)),
                  pl.BlockSpec(m2.shape, lambda b, k: (0, 0)),
                  pl.BlockSpec(m2c.shape, lambda b, k: (0, 0)),
                  pl.BlockSpec((2, kb, l2, w), lambda b, k: (0, k, 0, order))],
        out_specs=pl.BlockSpec((None, 2, kb, l2, w), lambda b, k: (b, 0, k, 0, 0)),
        out_shape=jax.ShapeDtypeStruct((bsz, 2, l1, l2, w), F32),
        compiler_params=_cparams("parallel", "parallel"),
        name="hy_mid",
    )(a, tw, m2, m2c, kf)


def _hy_out_kernel(m_ref, bp_ref, xo_ref, z_ref, bias_ref, o_ref):
    y = jnp.dot(m_ref[...], bp_ref[...].astype(BF16), preferred_element_type=F32)
    o_ref[...] = xo_ref[...] * (y + bias_ref[...] * z_ref[...])


def hy_out(m, bp, xo, z, bias_row, tn):
    bsz, k, n = bp.shape
    r = m.shape[0]
    blk = pl.BlockSpec((None, r, tn), lambda b, j: (b, 0, j))
    return pl.pallas_call(
        _hy_out_kernel,
        grid=(bsz, n // tn),
        in_specs=[pl.BlockSpec((r, k), lambda b, j: (0, 0)),
                  pl.BlockSpec((None, k, tn), lambda b, j: (b, 0, j)), blk, blk,
                  pl.BlockSpec((1, tn), lambda b, j: (0, j))],
        out_specs=blk,
        out_shape=jax.ShapeDtypeStruct((bsz, r, n), F32),
        compiler_params=_cparams("parallel", "parallel"),
        name="hy_out",
    )(m, bp, xo, z, bias_row)


SUB = 8
HY_KB = 16


def _kron_stage1(l1, n1_used, kb):
    c, s = _cos_sin(l1, n1_used, l1)
    m = np.stack([c, -s], axis=0).reshape(2, l1 // kb, kb, n1_used)
    m = np.transpose(m, (1, 0, 2, 3)).reshape(l1 // kb, 2 * kb, n1_used)
    return _const_bf16(np.stack([np.kron(blk, np.eye(SUB)) for blk in m]))


def _kron_stage_out(l1, n1_used, kb):
    c, s = _cos_sin(n1_used, l1, l1)
    m = np.stack([c, -s], axis=1).reshape(n1_used, 2, l1 // kb, kb) / (l1 * l1)
    m = np.transpose(m, (2, 0, 1, 3)).reshape(l1 // kb, n1_used, 2 * kb)
    return _const_bf16(np.stack([np.kron(blk, np.eye(SUB)) for blk in m]))


def _strided_stage_in(kin_ref, src_ref, a_ref):
    n_sub, width = src_ref.shape[1] // SUB, src_ref.shape[2]
    rows = src_ref.shape[0] * SUB
    for j in range(n_sub):
        r = jnp.dot(kin_ref[...], src_ref[:, j * SUB:(j + 1) * SUB, :].reshape(rows, width).astype(BF16),
                    preferred_element_type=F32)
        a_ref[:, j * SUB:(j + 1) * SUB, :] = r.reshape(a_ref.shape[0], SUB, width)


def _hy_conv_kernel(z_ref, xo_ref, kin_ref, tw_ref, m2_ref, m2c_ref, kf_ref, kout_ref, bias_ref, o_ref,
                    a_ref, b_ref, *, kb):
    k = pl.program_id(1)

    @pl.when(k == 0)
    def _():
        o_ref[...] = jnp.zeros_like(o_ref)

    _strided_stage_in(kin_ref, z_ref, a_ref)
    grp = 4
    for g0 in range(0, kb, grp):
        js = range(g0, g0 + grp)
        cs = [(tw_ref[0, :, j:j + 1], tw_ref[1, :, j:j + 1]) for j in js]
        ts = [_twiddle_mul(a_ref[j], a_ref[kb + j], c, s, False) for j, (c, s) in zip(js, cs)]
        xs = [_cplx_apply(m2_ref, tr, ti) for tr, ti in ts]
        ps = [(xr * kf_ref[0, j] - xi * kf_ref[1, j], xr * kf_ref[1, j] + xi * kf_ref[0, j])
              for j, (xr, xi) in zip(js, xs)]
        bs = [_cplx_apply(m2c_ref, pr, pi) for pr, pi in ps]
        for j, (br, bi), (c, s) in zip(js, bs, cs):
            b_ref[j], b_ref[kb + j] = _twiddle_mul(br, bi, c, s, True)
    n_sub, width = o_ref.shape[1] // SUB, o_ref.shape[2]
    for j in range(n_sub):
        r = jnp.dot(kout_ref[...], b_ref[:, j * SUB:(j + 1) * SUB, :].reshape(2 * kb * SUB, width).astype(BF16),
                    preferred_element_type=F32)
        o_ref[:, j * SUB:(j + 1) * SUB, :] += r.reshape(o_ref.shape[0], SUB, width)

    @pl.when(k == pl.num_programs(1) - 1)
    def _():
        o_ref[...] = xo_ref[...] * (o_ref[...] + bias_ref[...] * z_ref[...])


def hy_conv_long(z, xo, kf, bias, order, b, n):
    w, kb = HY_WIDTH, HY_KB
    l1 = l2 = 128
    n1 = n // l2
    tw = _twiddles(l1, l2, kb)
    m2, m2c = _stage2_matrices(l2)
    kin, kout = _kron_stage1(l1, n1, kb), _kron_stage_out(l1, n1, kb)
    tok = pl.BlockSpec((None, n1, l2, w), lambda bi, k: (bi, 0, 0, 0), pipeline_mode=pl.Buffered(1))
    const = lambda a: pl.BlockSpec(a.shape, lambda bi, k: (0,) * a.ndim)
    out = pl.pallas_call(
        functools.partial(_hy_conv_kernel, kb=kb),
        grid=(b, l1 // kb),
        in_specs=[tok, tok,
                  pl.BlockSpec((None,) + kin.shape[1:], lambda bi, k: (k, 0, 0)),
                  pl.BlockSpec((None, 2, l2, kb), lambda bi, k: (k, 0, 0, 0)),
                  const(m2), const(m2c),
                  pl.BlockSpec((2, kb, l2, w), lambda bi, k: (0, k, 0, order)),
                  pl.BlockSpec((None,) + kout.shape[1:], lambda bi, k: (k, 0, 0)),
                  pl.BlockSpec((1, w), lambda bi, k: (0, 0))],
        out_specs=pl.BlockSpec((None, n1, l2, w), lambda bi, k: (bi, 0, 0, 0), pipeline_mode=pl.Buffered(1)),
        out_shape=jax.ShapeDtypeStruct((b, n1, l2, w), F32),
        scratch_shapes=[pltpu.VMEM((2 * kb, l2, w), F32), pltpu.VMEM((2 * kb, l2, w), F32)],
        compiler_params=_cparams("parallel", "arbitrary"),
        name="hy_conv_long",
    )(z.reshape(b, n1, l2, w), xo.reshape(b, n1, l2, w), kin, tw, m2, m2c, kf, kout, bias.reshape(1, w))
    return out.reshape(b * n, w)


def _hy_spec_long_kernel(kern_ref, kin_ref, tw_ref, m2_ref, s_ref, o_ref, a_ref, *, kb):
    _strided_stage_in(kin_ref, kern_ref, a_ref)
    inv = 1.0 / s_ref[...]
    ts = [_twiddle_mul(a_ref[j], a_ref[kb + j], tw_ref[0, :, j:j + 1], tw_ref[1, :, j:j + 1], False)
          for j in range(kb)]
    xs = [_cplx_apply(m2_ref, tr, ti) for tr, ti in ts]
    for j, (xr, xi) in enumerate(xs):
        o_ref[0, j] = xr * inv
        o_ref[1, j] = xi * inv


def hy_spec_long(kern, abs_sum):
    w, kb = HY_WIDTH, HY_KB
    l1 = l2 = 128
    cw = kern.shape[1]
    tw = _twiddles(l1, l2, kb)
    m2, _ = _stage2_matrices(l2)
    kin = _kron_stage1(l1, l1, kb)
    return pl.pallas_call(
        functools.partial(_hy_spec_long_kernel, kb=kb),
        grid=(cw // w, l1 // kb),
        in_specs=[pl.BlockSpec((l1, l2, w), lambda c, k: (0, 0, c), pipeline_mode=pl.Buffered(1)),
                  pl.BlockSpec((None,) + kin.shape[1:], lambda c, k: (k, 0, 0)),
                  pl.BlockSpec((None, 2, l2, kb), lambda c, k: (k, 0, 0, 0)),
                  pl.BlockSpec(m2.shape, lambda c, k: (0, 0)),
                  pl.BlockSpec((1, w), lambda c, k: (0, c))],
        out_specs=pl.BlockSpec((2, kb, l2, w), lambda c, k: (0, k, 0, c)),
        out_shape=jax.ShapeDtypeStruct((2, l1, l2, cw), F32),
        scratch_shapes=[pltpu.VMEM((2 * kb, l2, w), F32)],
        compiler_params=_cparams("parallel", "arbitrary"),
        name="hy_spec_long",
    )(kern.reshape(l1, l2, cw), kin, tw, m2, abs_sum)


def hyena_pallas(p, conv_w, w1, b1, freq1, w2, b2, freq2, w3, bias, b, n):
    w = HY_WIDTH
    x0, x1, v = seq_conv(p, conv_w, b, n)
    kern, abs_sum = hy_filter(n, w1, b1, freq1, w2, b2, freq2, w3)
    z = v
    if n == 8192:
        kf = hy_spec_long(kern, abs_sum)
        for order, xo in enumerate((x0, x1)):
            z = hy_conv_long(z, xo, kf, bias[order], order, b, n)
        return z
    l2 = 2 * n
    tw = _twiddles(1, l2, 1)
    m2, m2c = _stage2_matrices(l2)
    ak = jnp.stack([kern, jnp.zeros_like(kern)], axis=0).reshape(2, 1, l2, HY_ORDER * w)
    m_out = _const_bf16(np.eye(n, 2 * l2) / l2)
    kf = hy_spec(ak, tw, m2, abs_sum, 1)
    for order, xo in enumerate((x0, x1)):
        zp = jnp.pad(z.reshape(b, 1, 1, n, w), ((0, 0), (0, 0), (0, 0), (0, n), (0, 0)))
        a = jnp.concatenate([zp, jnp.zeros_like(zp)], axis=1)
        bp = hy_mid(a, tw, m2, m2c, kf, order, 1)
        z = hy_out(m_out, bp.reshape(b, 2 * l2, w), xo.reshape(b, n, w), z.reshape(b, n, w),
                   bias[order].reshape(1, w), w).reshape(b * n, w)
    return z


def _prep_w_in(w):
    main = jnp.concatenate([w[:, OFF_Q:OFF_Z], w[:, OFF_HY:OFF_GATE], w[:, OFF_FN:OFF_HY], w[:, OFF_Z:OFF_BETA]],
                           axis=1).astype(BF16)
    small = jnp.pad(w[:, OFF_BETA:OFF_FN], ((0, 0), (0, SMALL_WIDTH - 4 * DN_HEADS))).astype(BF16)
    return main, small, w[:, OFF_GATE:IN_WIDTH].astype(BF16)


def kernel(x, c, ctx, c_ctx, w_mod, b_mod, norm1, norm2, w_in, dn_conv, dn_a_log, dn_dt_bias,
           dn_out_norm, hy_conv, hy_w1, hy_b1, hy_freq1, hy_w2, hy_b2, hy_freq2, hy_w3, hy_bias,
           w_branch_a, w_branch_b, w_branch_c, w_out, w_ff1, w_ff2, final_norm):
    b, n_lat, d = x.shape
    rows = n_lat // GRID_W
    n_ctx = ctx.shape[1]
    tm_x, tm_c = 1024, n_ctx

    c_rows = jnp.concatenate([c, c_ctx[None], jnp.zeros((8 - b - 1, d), F32)], axis=0)
    mods = mod_vectors(c_rows, w_mod, b_mod)
    s_zero = jnp.zeros((2, b, DN_HEADS, DN_HEAD_DIM, DN_HEAD_DIM), F32)
    h, hc = x.reshape(b * n_lat, d), ctx.reshape(b * n_ctx, d)

    for l in range(DEPTH):
        last = l == DEPTH - 1
        mv = mods[l].reshape(8, N_MOD, 1, d)
        mx = [mv[:b, i] for i in range(N_MOD)]
        mc = [mv[b:b + 1, i] for i in range(N_MOD)]
        w_main, w_small, w_gate = _prep_w_in(w_in[l])
        wa, wb, wc, wo = (w.astype(BF16) for w in (w_branch_a[l], w_branch_b[l], w_branch_c[l], w_out[l]))
        w1, w2 = w_ff1[l].astype(BF16), w_ff2[l].astype(BF16)
        n1, n2 = norm1[l][None, None, :], norm2[l][None, None, :]

        p_c, small_c = in_proj(hc, n1 * (1.0 + mc[1]), mc[0], w_main, w_small, b * n_ctx, tm_c)
        p_x, small_x = in_proj(h, n1 * (1.0 + mx[1]), mx[0], w_main, w_small, n_lat, tm_x)

        def mix(p, n):
            y_c = hyena_pallas(p, hy_conv[l], hy_w1[l], hy_b1[l], hy_freq1[l], hy_w2[l], hy_b2[l], hy_freq2[l],
                               hy_w3[l], hy_bias[l], b, n)
            return fnet_branch(p, b, n), y_c

        ocf, ocb, s_ctx = delta_branch(p_c, small_c, dn_conv[l], dn_a_log[l], dn_dt_bias[l], s_zero,
                                       b, n_ctx, 1, n_ctx)
        oxf, oxb, _ = delta_branch(p_x, small_x, dn_conv[l], dn_a_log[l], dn_dt_bias[l], s_ctx,
                                   b, n_lat, rows, GRID_W)

        y_b, y_c = mix(p_x, n_lat)
        h = merge(oxf, oxb, p_x, y_b, y_c, h, n1 * (1.0 + mx[1]), mx[0], mx[2], dn_out_norm[l],
                  w_gate, wa, wb, wc, wo, n_lat, 512)
        h = mlp(h, n2 * (1.0 + mx[4]), mx[3], mx[5], w1, w2, n_lat, tm_x, final_norm if last else None)

        if not last:
            y_b, y_c = mix(p_c, n_ctx)
            hc = merge(ocf, ocb, p_c, y_b, y_c, hc, n1 * (1.0 + mc[1]), mc[0], mc[2], dn_out_norm[l],
                       w_gate, wa, wb, wc, wo, b * n_ctx, tm_c)
            hc = mlp(hc, n2 * (1.0 + mc[4]), mc[3], mc[5], w1, w2, b * n_ctx, tm_c)

    return h.reshape(b, n_lat, d)
```

```python
import functools
import math

import jax
import jax.numpy as jnp
import numpy as np
from jax import lax
from jax.experimental import pallas as pl
from jax.experimental.pallas import tpu as pltpu

D_MODEL = 1024
DEPTH = 2
GRID_W = 64
NORM_EPS = 1e-6
N_MOD = 6

DN_HEADS = 4
DN_HEAD_DIM = 128
DN_WIDTH = DN_HEADS * DN_HEAD_DIM
DN_CHUNK = 64
SHORT_CONV = 3

FN_GROUPS = 4
FN_GROUP_DIM = 64
FN_WIDTH = FN_GROUPS * FN_GROUP_DIM

HY_WIDTH = 256
HY_ORDER = 2
HY_EMB_DIM = 33
HY_BANDS = (HY_EMB_DIM - 1) // 2
HY_FILTER_HIDDEN = 64
HY_FAST_DECAY_PCT = 0.3
HY_SLOW_DECAY_PCT = 1.5
HY_DECAY_TARGET = 1e-2

N_BRANCHES = 3
D_FF = 4 * D_MODEL

OFF_Q = 0
OFF_Z = 3 * DN_WIDTH
OFF_BETA = OFF_Z + DN_WIDTH
OFF_A = OFF_BETA + 2 * DN_HEADS
OFF_FN = OFF_A + 2 * DN_HEADS
OFF_HY = OFF_FN + FN_WIDTH
OFF_GATE = OFF_HY + (HY_ORDER + 1) * HY_WIDTH
IN_WIDTH = OFF_GATE + N_BRANCHES * D_MODEL

P_QKV = 0
P_HY = P_QKV + 3 * DN_WIDTH
P_FN = P_HY + (HY_ORDER + 1) * HY_WIDTH
P_Z = P_FN + FN_WIDTH
P_WIDTH = P_Z + DN_WIDTH
SMALL_WIDTH = 128

F32 = jnp.float32
BF16 = jnp.bfloat16
VMEM_LIMIT = 56 * 1024 * 1024


def _cparams(*sem):
    return pltpu.CompilerParams(dimension_semantics=sem, vmem_limit_bytes=VMEM_LIMIT)


def _bdot(a, b):
    return jnp.dot(a.astype(BF16), b.astype(BF16), preferred_element_type=F32)


def _sigmoid(x):
    return 1.0 / (1.0 + jnp.exp(-x))


def _modnorm(xf, gs, sh):
    r = lax.rsqrt(jnp.mean(xf * xf, axis=-1, keepdims=True) + NORM_EPS)
    return xf * r * gs + sh


def _mod_kernel(c_ref, w_ref, b_ref, o_ref):
    c = c_ref[...]
    o_ref[...] = _bdot(c * _sigmoid(c), w_ref[...]) + b_ref[...]


def mod_vectors(c_rows, w_mod, b_mod):
    tn = 1536
    n = N_MOD * D_MODEL
    return pl.pallas_call(
        _mod_kernel,
        grid=(DEPTH, n // tn),
        in_specs=[pl.BlockSpec((8, D_MODEL), lambda l, j: (0, 0)),
                  pl.BlockSpec((None, D_MODEL, tn), lambda l, j: (l, 0, j)),
                  pl.BlockSpec((None, 1, tn), lambda l, j: (l, 0, j))],
        out_specs=pl.BlockSpec((None, 8, tn), lambda l, j: (l, 0, j)),
        out_shape=jax.ShapeDtypeStruct((DEPTH, 8, n), F32),
        compiler_params=_cparams("parallel", "parallel"),
        name="mod_vectors",
    )(c_rows, w_mod, b_mod.reshape(DEPTH, 1, n))


def _in_proj_kernel(x_ref, gs_ref, sh_ref, w_ref, ws_ref, p_ref, small_ref, xn_ref):
    @pl.when(pl.program_id(1) == 0)
    def _():
        xn = _modnorm(x_ref[...], gs_ref[0], sh_ref[0]).astype(BF16)
        xn_ref[...] = xn
        small_ref[...] = jnp.dot(xn, ws_ref[...], preferred_element_type=F32)

    p_ref[...] = jnp.dot(xn_ref[...], w_ref[...], preferred_element_type=F32)


def in_proj(x2d, gs, sh, w_main, w_small, rows_per_mod, tm):
    m = x2d.shape[0]
    tn = 1024
    tiles_per_mod = rows_per_mod // tm
    mod_spec = pl.BlockSpec((1, 1, D_MODEL), lambda i, j: (i // tiles_per_mod, 0, 0))
    return pl.pallas_call(
        _in_proj_kernel,
        grid=(m // tm, P_WIDTH // tn),
        in_specs=[pl.BlockSpec((tm, D_MODEL), lambda i, j: (i, 0)), mod_spec, mod_spec,
                  pl.BlockSpec((D_MODEL, tn), lambda i, j: (0, j)),
                  pl.BlockSpec((D_MODEL, SMALL_WIDTH), lambda i, j: (0, 0))],
        out_specs=[pl.BlockSpec((tm, tn), lambda i, j: (i, j)),
                   pl.BlockSpec((tm, SMALL_WIDTH), lambda i, j: (i, 0))],
        out_shape=[jax.ShapeDtypeStruct((m, P_WIDTH), F32), jax.ShapeDtypeStruct((m, SMALL_WIDTH), F32)],
        scratch_shapes=[pltpu.VMEM((tm, D_MODEL), BF16)],
        compiler_params=_cparams("parallel", "arbitrary"),
        name="in_proj",
    )(x2d, gs, sh, w_main, w_small)


def _merge_kernel(of_ref, ob_ref, z_ref, yb_ref, yc_ref, h_ref, gs_ref, sh_ref, gate_ref, nrm_ref,
                  wg_ref, wa_ref, wb_ref, wc_ref, wo_ref, out_ref):
    xn = _modnorm(h_ref[...], gs_ref[0], sh_ref[0]).astype(BF16)
    o = of_ref[...] + ob_ref[...]
    z = z_ref[...]
    heads = []
    for hd in range(DN_HEADS):
        sl = slice(hd * DN_HEAD_DIM, (hd + 1) * DN_HEAD_DIM)
        oh, zh = o[:, sl], z[:, sl]
        r = lax.rsqrt(jnp.mean(oh * oh, axis=-1, keepdims=True) + NORM_EPS)
        heads.append(oh * r * nrm_ref[...] * (zh * _sigmoid(zh)))
    ya = jnp.concatenate(heads, axis=-1)
    merged = None
    for i, (y, w_ref) in enumerate(((ya, wa_ref), (yb_ref[...], wb_ref), (yc_ref[...], wc_ref))):
        g = jnp.dot(xn, wg_ref[:, i * D_MODEL:(i + 1) * D_MODEL], preferred_element_type=F32)
        term = _sigmoid(g) * _bdot(y, w_ref[...])
        merged = term if merged is None else merged + term
    out_ref[...] = h_ref[...] + gate_ref[0] * _bdot(merged, wo_ref[...])


def merge(o_f, o_b, p, y_b, y_c, h2d, gs, sh, gate, dn_out_norm, wg, wa, wb, wc, wo, rows_per_mod, tm):
    m = h2d.shape[0]
    tiles_per_mod = rows_per_mod // tm
    row = lambda w: pl.BlockSpec((tm, w), lambda i: (i, 0))
    full = lambda a: pl.BlockSpec(a.shape, lambda i: (0,) * a.ndim, pipeline_mode=pl.Buffered(1))
    mod_spec = pl.BlockSpec((1, 1, D_MODEL), lambda i: (i // tiles_per_mod, 0, 0))
    nrm = dn_out_norm.reshape(1, DN_HEAD_DIM)
    return pl.pallas_call(
        _merge_kernel,
        grid=(m // tm,),
        in_specs=[row(DN_WIDTH), row(DN_WIDTH),
                  pl.BlockSpec((tm, DN_WIDTH), lambda i: (i, P_Z // DN_WIDTH)),
                  row(FN_WIDTH), row(HY_WIDTH), row(D_MODEL), mod_spec, mod_spec, mod_spec,
                  full(nrm), full(wg), full(wa), full(wb), full(wc), full(wo)],
        out_specs=row(D_MODEL),
        out_shape=jax.ShapeDtypeStruct((m, D_MODEL), F32),
        compiler_params=_cparams("parallel"),
        name="merge",
    )(o_f, o_b, p, y_b, y_c, h2d, gs, sh, gate, nrm, wg, wa, wb, wc, wo)


def _mlp_kernel(h_ref, gs_ref, sh_ref, gate_ref, w1_ref, w2_ref, fin_ref, out_ref, xn_ref, acc_ref, *, final):
    j = pl.program_id(1)

    @pl.when(j == 0)
    def _():
        xn_ref[...] = _modnorm(h_ref[...], gs_ref[0], sh_ref[0]).astype(BF16)
        acc_ref[...] = jnp.zeros_like(acc_ref)

    a = jnp.maximum(jnp.dot(xn_ref[...], w1_ref[...], preferred_element_type=F32), 0.0)
    acc_ref[...] += jnp.dot((a * a).astype(BF16), w2_ref[...], preferred_element_type=F32)

    @pl.when(j == pl.num_programs(1) - 1)
    def _():
        y = h_ref[...] + gate_ref[0] * acc_ref[...]
        if final:
            y = y * lax.rsqrt(jnp.mean(y * y, axis=-1, keepdims=True) + NORM_EPS) * fin_ref[...]
        out_ref[...] = y


def mlp(h2d, gs, sh, gate, w1, w2, rows_per_mod, tm, final_gain=None):
    m = h2d.shape[0]
    tf = 1024
    tiles_per_mod = rows_per_mod // tm
    mod_spec = pl.BlockSpec((1, 1, D_MODEL), lambda i, j: (i // tiles_per_mod, 0, 0))
    final = final_gain is not None
    fin = (final_gain if final else jnp.ones((D_MODEL,), F32)).reshape(1, D_MODEL)
    return pl.pallas_call(
        functools.partial(_mlp_kernel, final=final),
        grid=(m // tm, D_FF // tf),
        in_specs=[pl.BlockSpec((tm, D_MODEL), lambda i, j: (i, 0)), mod_spec, mod_spec, mod_spec,
                  pl.BlockSpec((D_MODEL, tf), lambda i, j: (0, j)),
                  pl.BlockSpec((tf, D_MODEL), lambda i, j: (j, 0)),
                  pl.BlockSpec((1, D_MODEL), lambda i, j: (0, 0))],
        out_specs=pl.BlockSpec((tm, D_MODEL), lambda i, j: (i, 0)),
        out_shape=jax.ShapeDtypeStruct((m, D_MODEL), F32),
        scratch_shapes=[pltpu.VMEM((tm, D_MODEL), BF16), pltpu.VMEM((tm, D_MODEL), F32)],
        compiler_params=_cparams("parallel", "arbitrary"),
        name="mlp",
    )(h2d, gs, sh, gate, w1, w2, fin)


def _dnconv_kernel(prev_ref, cur_ref, next_ref, w_ref, o_ref, *, cols, n_tiles):
    t, j = pl.program_id(1), pl.program_id(2)
    tt = cur_ref.shape[0]
    prev = jnp.where(t == 0, 0.0, prev_ref[...])
    nxt = jnp.where(t == n_tiles - 1, 0.0, next_ref[...])
    ext = jnp.concatenate([prev, cur_ref[...], nxt], axis=0)
    n_ext = tt + 2 * cols
    col = lax.broadcasted_iota(jnp.int32, (n_ext, 1), 0) % cols
    left = jnp.where(col == 0, 0.0, pltpu.roll(ext, 1, axis=0))
    right = jnp.where(col == cols - 1, 0.0, pltpu.roll(ext, n_ext - 1, axis=0))
    acc = jnp.zeros((tt, DN_WIDTH), F32)
    for dr in range(SHORT_CONV):
        base = dr * cols
        acc = (acc + w_ref[3 * dr:3 * dr + 1, :] * left[base:base + tt]
               + w_ref[3 * dr + 1:3 * dr + 2, :] * ext[base:base + tt]
               + w_ref[3 * dr + 2:3 * dr + 3, :] * right[base:base + tt])
    y = acc * _sigmoid(acc)
    q_scale = jnp.where(j == 0, DN_HEAD_DIM ** -0.5, 1.0)
    for hd in range(DN_HEADS):
        sl = slice(hd * DN_HEAD_DIM, (hd + 1) * DN_HEAD_DIM)
        yh = y[:, sl]
        nrm = lax.rsqrt(jnp.sum(yh * yh, axis=-1, keepdims=True) + NORM_EPS) * q_scale
        o_ref[:, sl] = yh * jnp.where(j < 2, nrm, 1.0)


def dn_conv_prep(p, conv_w, b, n, rows, cols):
    tr = min(rows, 32)
    tt = tr * cols
    n_tiles = rows // tr
    nblk = b * n // cols
    c0 = P_QKV // DN_WIDTH
    return pl.pallas_call(
        functools.partial(_dnconv_kernel, cols=cols, n_tiles=n_tiles),
        grid=(b, n_tiles, 3),
        in_specs=[pl.BlockSpec((cols, DN_WIDTH),
                               lambda bi, t, j: (jnp.maximum(bi * rows + t * tr - 1, 0), c0 + j)),
                  pl.BlockSpec((tt, DN_WIDTH), lambda bi, t, j: (bi * n_tiles + t, c0 + j)),
                  pl.BlockSpec((cols, DN_WIDTH),
                               lambda bi, t, j: (jnp.minimum(bi * rows + (t + 1) * tr, nblk - 1), c0 + j)),
                  pl.BlockSpec((SHORT_CONV * SHORT_CONV, DN_WIDTH), lambda bi, t, j: (0, j))],
        out_specs=pl.BlockSpec((tt, DN_WIDTH), lambda bi, t, j: (bi * n_tiles + t, j)),
        out_shape=jax.ShapeDtypeStruct((b * n, 3 * DN_WIDTH), F32),
        compiler_params=_cparams("parallel", "parallel", "parallel"),
        name="dn_conv_prep",
    )(p, p, p, conv_w.reshape(SHORT_CONV * SHORT_CONV, 3 * DN_WIDTH))


PREP_CHUNKS = 8
PREP_GROUP = 2


def _softplus(x):
    return jnp.maximum(x, 0.0) + jnp.log(1.0 + jnp.exp(-jnp.abs(x)))


def _delta_prep_kernel(qkv_ref, sm_ref, smt_ref, prm_ref, prmt_ref, u0_ref, lhs1_ref, lhs2_ref, gl_ref, *, n_chunks):
    cc = DN_CHUNK
    nh = DN_HEADS
    sm = sm_ref[...]
    beta_all = _sigmoid(sm)
    g_all = -jnp.exp(prm_ref[0:1, :]) * _softplus(sm + prm_ref[1:2, :])
    gt_all = -jnp.exp(prmt_ref[:, 0:1]) * _softplus(smt_ref[...] + prmt_ref[:, 1:2])
    lt = 2 * cc
    lane = lax.broadcasted_iota(jnp.int32, (gt_all.shape[0], lt), 1) % cc
    gt_tiles = []
    for t in range(gt_all.shape[1] // lt):
        gt_f = gt_b = gt_all[:, t * lt:(t + 1) * lt]
        s = 1
        while s < cc:
            gt_f = gt_f + jnp.where(lane >= s, pltpu.roll(gt_f, s, axis=1), 0.0)
            gt_b = gt_b + jnp.where(lane < cc - s, pltpu.roll(gt_b, lt - s, axis=1), 0.0)
            s *= 2
        gt_tiles.append((gt_f, gt_b))
    ri = lax.broadcasted_iota(jnp.int32, (cc, cc), 0)
    ci_ = lax.broadcasted_iota(jnp.int32, (cc, cc), 1)
    sub = lax.broadcasted_iota(jnp.int32, (cc, 1), 0)
    blk = lambda s: (ri // s) == (ci_ // s)
    leaf = 8

    heads = []
    for ci in range(n_chunks):
        rows = slice(ci * cc, (ci + 1) * cc)
        gc_f = g_all[rows]
        gc_b = gc_f
        s = 1
        while s < cc:
            gc_f = gc_f + jnp.where(sub >= s, pltpu.roll(gc_f, s, axis=0), 0.0)
            gc_b = gc_b + jnp.where(sub < cc - s, pltpu.roll(gc_b, cc - s, axis=0), 0.0)
            s *= 2
        for h in range(nh):
            q = qkv_ref[rows, h * 128:(h + 1) * 128]
            k = qkv_ref[rows, (nh + h) * 128:(nh + h + 1) * 128]
            v = qkv_ref[rows, (2 * nh + h) * 128:(2 * nh + h + 1) * 128]
            heads.append((ci, rows, h, q, k, v, gc_f, gc_b))
    qkks = [lax.dot_general(jnp.concatenate([q, k], axis=0).astype(BF16), k.astype(BF16),
                            (((1,), (1,)), ((), ())), preferred_element_type=F32)
            for (_, _, _, q, k, _, _, _) in heads]
    per_group = PREP_GROUP * nh
    for g0 in range(0, len(heads), per_group):
        chains = []
        for (ci, rows, h, q, k, v, gc_f, gc_b), qkk in zip(heads[g0:g0 + per_group], qkks[g0:g0 + per_group]):
            qk, kk = qkk[:cc], qkk[cc:]
            for d in range(2):
                cb_, ca_ = d * nh + h, 2 * nh + d * nh + h
                beta = jnp.broadcast_to(beta_all[rows, cb_:cb_ + 1], (cc, DN_HEAD_DIM))
                g_col = jnp.broadcast_to((gc_f if d == 0 else gc_b)[:, ca_:ca_ + 1], (cc, DN_HEAD_DIM))
                lo = (ci % 2) * cc
                g_row = gt_tiles[ci // 2][d][ca_:ca_ + 1, lo:lo + cc]
                incl = (ri >= ci_) if d == 0 else (ri <= ci_)
                strict = (ri > ci_) if d == 0 else (ri < ci_)
                decay = jnp.exp(jnp.where(incl, g_col[:, :cc] - g_row, -1e30))
                a = jnp.where(strict, beta[:, :cc] * kk * decay, 0.0)
                eg = jnp.exp(g_col)
                g_last = g_col[cc - 1:cc] if d == 0 else g_col[0:1]
                chains.append(dict(ci=ci, rows=rows, h=h, d=d, a=a, qkd=qk * decay, g_last=g_last,
                                   rhs=jnp.concatenate([v * beta, k * (beta * eg)], axis=1),
                                   q_d=q * eg, k_d=k * jnp.exp(g_last - g_col)))

        pws = [jnp.where(blk(leaf), c["a"], 0.0) for c in chains]
        devs = [-pw for pw in pws]
        for _ in range(2):
            pws = [_bdot(pw, pw) for pw in pws]
            cross = [_bdot(dev, pw) for dev, pw in zip(devs, pws)]
            devs = [dev + pw + x for dev, pw, x in zip(devs, pws, cross)]
        s = leaf
        while s < cc:
            offs = [jnp.where(blk(2 * s) & jnp.logical_not(blk(s)), c["a"], 0.0) for c in chains]
            xs = [off + _bdot(dev, off) for dev, off in zip(devs, offs)]
            devs = [dev - x - _bdot(x, dev) for dev, x in zip(devs, xs)]
            s *= 2
        uws = [c["rhs"] + _bdot(dev, c["rhs"]) for dev, c in zip(devs, chains)]
        for c, uw in zip(chains, uws):
            d, h, ci = c["d"], c["h"], c["ci"]
            u0_ref[d, h, c["rows"], :] = uw[:, :128]
            lhs1_ref[d, h, ci] = jnp.concatenate([uw[:, 128:], c["q_d"]], axis=0).astype(BF16)
            lhs2_ref[d, h, ci] = jnp.concatenate([c["qkd"], c["k_d"].T], axis=0).astype(BF16)
            gl_ref[d, h, ci] = jnp.broadcast_to(jnp.exp(c["g_last"]), (8, DN_HEAD_DIM))


def delta_prep(qkv, small, small_t, prm, prm_t, b, n):
    nc = n // DN_CHUNK
    cb = min(PREP_CHUNKS, nc)
    nblk = nc // cb
    tt = cb * DN_CHUNK
    nh = DN_HEADS
    return pl.pallas_call(
        functools.partial(_delta_prep_kernel, n_chunks=cb),
        grid=(b, nblk),
        in_specs=[pl.BlockSpec((tt, 3 * DN_WIDTH), lambda bi, c: (bi * nblk + c, 0)),
                  pl.BlockSpec((tt, SMALL_WIDTH), lambda bi, c: (bi * nblk + c, 0)),
                  pl.BlockSpec((4 * nh, tt), lambda bi, c: (0, bi * nblk + c)),
                  pl.BlockSpec((2, SMALL_WIDTH), lambda bi, c: (0, 0)),
                  pl.BlockSpec((4 * nh, 2), lambda bi, c: (0, 0))],
        out_specs=[pl.BlockSpec((2, None, nh, tt, 128), lambda bi, c: (0, bi, 0, c, 0)),
                   pl.BlockSpec((2, None, nh, cb, 128, 128), lambda bi, c: (0, bi, 0, c, 0, 0)),
                   pl.BlockSpec((2, None, nh, cb, 192, 64), lambda bi, c: (0, bi, 0, c, 0, 0)),
                   pl.BlockSpec((2, None, nh, cb, 8, 128), lambda bi, c: (0, bi, 0, c, 0, 0))],
        out_shape=[jax.ShapeDtypeStruct((2, b, nh, n, 128), F32),
                   jax.ShapeDtypeStruct((2, b, nh, nc, 128, 128), BF16),
                   jax.ShapeDtypeStruct((2, b, nh, nc, 192, 64), BF16),
                   jax.ShapeDtypeStruct((2, b, nh, nc, 8, 128), F32)],
        compiler_params=_cparams("parallel", "parallel"),
        name="delta_prep",
    )(qkv, small, small_t, prm, prm_t)


def _delta_scan_kernel(u0f_ref, u0b_ref, l1f_ref, l1b_ref, l2f_ref, l2b_ref, glf_ref, glb_ref, s0_ref,
                       of_ref, ob_ref, sout_ref, st_ref, *, sc, nb):
    n = pl.program_id(0)
    cc = DN_CHUNK

    @pl.when(n == 0)
    def _():
        st_ref[...] = s0_ref[...]

    dirs = ((u0f_ref, l1f_ref, l2f_ref, glf_ref, of_ref), (u0b_ref, l1b_ref, l2b_ref, glb_ref, ob_ref))

    def body(i, carry):
        chains = []
        for d in range(2):
            ci = i if d == 0 else sc - 1 - i
            r0 = pl.multiple_of(ci * cc, cc)
            chains += [(d, bi, h, ci, r0) for bi in range(nb) for h in range(DN_HEADS)]
        sts = [st_ref[d, bi, h] for (d, bi, h, _, _) in chains]
        r1s = [jnp.dot(dirs[d][1][bi, h, ci], st.astype(BF16), preferred_element_type=F32)
               for (d, bi, h, ci, _), st in zip(chains, sts)]
        us = [dirs[d][0][bi, h, pl.ds(r0, cc), :] - r1[:cc] for (d, bi, h, _, r0), r1 in zip(chains, r1s)]
        r2s = [jnp.dot(dirs[d][2][bi, h, ci], u.astype(BF16), preferred_element_type=F32)
               for (d, bi, h, ci, _), u in zip(chains, us)]
        for (d, bi, h, ci, r0), st, r1, r2 in zip(chains, sts, r1s, r2s):
            dirs[d][4][bi, pl.ds(r0, cc), h * 128:(h + 1) * 128] = r1[cc:] + r2[:cc]
            st_ref[d, bi, h] = st * dirs[d][3][bi, h, ci, 0:1, :] + r2[cc:]
        return carry

    lax.fori_loop(0, sc, body, 0)

    @pl.when(n == pl.num_programs(0) - 1)
    def _():
        sout_ref[...] = st_ref[...]


def delta_scan(u0, lhs1, lhs2, gl, s0, b, n):
    nc = n // DN_CHUNK
    sc = min(8, nc)
    nblk = nc // sc
    nh = DN_HEADS
    tt = sc * DN_CHUNK
    fwd = lambda i: i
    bwd = lambda i: nblk - 1 - i

    def specs(d, blk):
        return [pl.BlockSpec((None, b, nh, tt, 128), lambda i: (d, 0, 0, blk(i), 0)),
                pl.BlockSpec((None, b, nh, sc, 128, 128), lambda i: (d, 0, 0, blk(i), 0, 0)),
                pl.BlockSpec((None, b, nh, sc, 192, 64), lambda i: (d, 0, 0, blk(i), 0, 0)),
                pl.BlockSpec((None, b, nh, sc, 8, 128), lambda i: (d, 0, 0, blk(i), 0, 0))]

    sf, sb = specs(0, fwd), specs(1, bwd)
    in_specs = [sf[0], sb[0], sf[1], sb[1], sf[2], sb[2], sf[3], sb[3],
                pl.BlockSpec((2, b, nh, 128, 128), lambda i: (0, 0, 0, 0, 0))]
    return pl.pallas_call(
        functools.partial(_delta_scan_kernel, sc=sc, nb=b),
        grid=(nblk,),
        in_specs=in_specs,
        out_specs=[pl.BlockSpec((b, tt, DN_WIDTH), lambda i: (0, fwd(i), 0)),
                   pl.BlockSpec((b, tt, DN_WIDTH), lambda i: (0, bwd(i), 0)),
                   pl.BlockSpec((2, b, nh, 128, 128), lambda i: (0, 0, 0, 0, 0))],
        out_shape=[jax.ShapeDtypeStruct((b, n, DN_WIDTH), F32), jax.ShapeDtypeStruct((b, n, DN_WIDTH), F32),
                   jax.ShapeDtypeStruct((2, b, nh, 128, 128), F32)],
        scratch_shapes=[pltpu.VMEM((2, b, nh, 128, 128), F32)],
        compiler_params=_cparams("arbitrary"),
        name="delta_scan",
    )(u0, u0, lhs1, lhs1, lhs2, lhs2, gl, gl, s0)


def delta_branch(p, small, conv_w, a_log, dt_bias, s0, b, n, rows, cols):
    nh = DN_HEADS
    qkv = dn_conv_prep(p, conv_w, b, n, rows, cols)
    small_t = small[:, :4 * nh].T
    rate = jnp.concatenate([jnp.zeros((2 * nh,), F32), a_log.reshape(-1)])
    bias = jnp.concatenate([jnp.zeros((2 * nh,), F32), dt_bias.reshape(-1)])
    prm_t = jnp.stack([rate, bias], axis=1)
    prm = jnp.pad(prm_t.T, ((0, 0), (0, SMALL_WIDTH - 4 * nh)))
    u0, lhs1, lhs2, gl = delta_prep(qkv, small, small_t, prm, prm_t, b, n)
    o_f, o_b, s_out = delta_scan(u0, lhs1, lhs2, gl, s0, b, n)
    return o_f.reshape(b * n, DN_WIDTH), o_b.reshape(b * n, DN_WIDTH), s_out


def _cos_sin(rows, cols, period):
    ang = 2.0 * np.pi * ((np.arange(rows)[:, None] * np.arange(cols)[None, :]) % period) / period
    return np.cos(ang), np.sin(ang)


def _const_bf16(a):
    return jnp.asarray(a, F32).astype(BF16)


def _stage2_matrices(l2):
    c, s = _cos_sin(l2, l2, l2)
    fwd = np.block([[c, s], [-s, c]])
    inv = np.block([[c, -s], [s, c]])
    return _const_bf16(fwd), _const_bf16(inv)


def _twiddles(l1, l2, kb):
    ang = 2.0 * np.pi * ((np.arange(l2)[:, None] * np.arange(l1)[None, :]) % (l1 * l2)) / (l1 * l2)
    tw = np.stack([np.cos(ang), np.sin(ang)], axis=0).reshape(2, l2, l1 // kb, kb)
    return jnp.asarray(np.transpose(tw, (2, 0, 1, 3)), F32)


def _twiddle_mul(ar, ai, c, s, conj):
    if conj:
        return ar * c - ai * s, ai * c + ar * s
    return ar * c + ai * s, ai * c - ar * s


def _cplx_apply(m_ref, re, im):
    half = re.shape[0]
    out = jnp.dot(m_ref[...], jnp.concatenate([re, im], axis=0).astype(BF16), preferred_element_type=F32)
    return out[:half], out[half:]


def _fnet_mid_kernel(a_ref, tw_ref, m2_ref, cs_ref, o_ref, *, kb, ch):
    ts = [_twiddle_mul(a_ref[0, j], a_ref[1, j], tw_ref[0, :, j:j + 1], tw_ref[1, :, j:j + 1], False)
          for j in range(kb)]
    us = [_cplx_apply(m2_ref, tr, ti) for tr, ti in ts]
    ys = [jnp.dot(jnp.concatenate([ur, ui], axis=1).astype(BF16), cs_ref[...], preferred_element_type=F32)
          for ur, ui in us]
    for j, y in enumerate(ys):
        o_ref[:, j * ch:(j + 1) * ch] = y


def fnet_mid(a, tw, m2, cs, kb):
    bsz, _, l1, l2, ch = a.shape
    return pl.pallas_call(
        functools.partial(_fnet_mid_kernel, kb=kb, ch=ch),
        grid=(bsz, l1 // kb),
        in_specs=[pl.BlockSpec((None, 2, kb, l2, ch), lambda b, k: (b, 0, k, 0, 0)),
                  pl.BlockSpec((None, 2, l2, kb), lambda b, k: (k, 0, 0, 0)),
                  pl.BlockSpec(m2.shape, lambda b, k: (0, 0)),
                  pl.BlockSpec(cs.shape, lambda b, k: (0, 0))],
        out_specs=pl.BlockSpec((None, l2, kb * ch), lambda b, k: (b, 0, k)),
        out_shape=jax.ShapeDtypeStruct((bsz, l2, l1 * ch), F32),
        compiler_params=_cparams("parallel", "parallel"),
        name="fnet_mid",
    )(a, tw, m2, cs)


def _fnet_channel_matrix(n):
    c, s = _cos_sin(FN_GROUP_DIM, FN_GROUP_DIM, FN_GROUP_DIM)
    eye = np.eye(FN_GROUPS)
    scale = 1.0 / math.sqrt(n * FN_GROUP_DIM)
    return _const_bf16(np.concatenate([np.kron(eye, c), np.kron(eye, s)], axis=0) * scale)


FN_SUB = 8


def _fnet_long_kernel(x_ref, kin_ref, tw_ref, m2k_ref, cs_ref, o_ref, a_ref):
    n_sub = x_ref.shape[1] // FN_SUB
    rows, width = x_ref.shape[0] * FN_SUB, x_ref.shape[2]
    half = FN_SUB * FN_SUB
    for j in range(n_sub):
        r = jnp.dot(kin_ref[...], x_ref[:, j * FN_SUB:(j + 1) * FN_SUB, :].reshape(rows, width).astype(BF16),
                    preferred_element_type=F32)
        a_ref[0, j * half:(j + 1) * half, :] = r[:half]
        a_ref[1, j * half:(j + 1) * half, :] = r[half:]
    tr, ti = _twiddle_mul(a_ref[0], a_ref[1], tw_ref[0], tw_ref[1], False)
    ur, ui = _cplx_apply(m2k_ref, tr, ti)
    y = jnp.dot(jnp.concatenate([ur, ui], axis=1).astype(BF16), cs_ref[...], preferred_element_type=F32)
    o_ref[...] = y.reshape(o_ref.shape)


def fnet_long(p, b, n):
    ch, sub = FN_WIDTH, FN_SUB
    l1, l2 = n // 128, 128
    nblk = l1 // sub
    c1, s1 = _cos_sin(l1, l1, l1)
    m1 = np.stack([c1, -s1], axis=0).reshape(2, nblk, sub, l1)
    kin = np.einsum('pbkn,jq->bpjknq', m1, np.eye(sub)).reshape(nblk, 2 * sub * sub, l1 * sub)
    ang = 2.0 * np.pi * ((np.arange(l2)[:, None] * np.arange(l1)[None, :]) % n) / n
    tw = np.stack([np.cos(ang), np.sin(ang)], axis=0).reshape(2, l2, nblk, sub)
    tw = np.transpose(tw, (2, 0, 1, 3)).reshape(nblk, 2, l2 * sub, 1)
    c2, s2 = _cos_sin(l2, l2, l2)
    m2k = _const_bf16(np.kron(np.block([[c2, s2], [-s2, c2]]), np.eye(sub)))
    cs = _fnet_channel_matrix(n)
    const = lambda a: pl.BlockSpec(a.shape, lambda bi, k: (0,) * a.ndim, pipeline_mode=pl.Buffered(1))
    y = pl.pallas_call(
        _fnet_long_kernel,
        grid=(b, nblk),
        in_specs=[pl.BlockSpec((None, l1, l2, ch), lambda bi, k: (bi, 0, 0, P_FN // ch),
                               pipeline_mode=pl.Buffered(1)),
                  pl.BlockSpec((None,) + kin.shape[1:], lambda bi, k: (k, 0, 0)),
                  pl.BlockSpec((None, 2, l2 * sub, 1), lambda bi, k: (k, 0, 0, 0)),
                  const(m2k), const(cs)],
        out_specs=pl.BlockSpec((None, l2, sub, ch), lambda bi, k: (bi, 0, k, 0)),
        out_shape=jax.ShapeDtypeStruct((b, l2, l1, ch), F32),
        scratch_shapes=[pltpu.VMEM((2, l2 * sub, ch), F32)],
        compiler_params=_cparams("parallel", "parallel"),
        name="fnet_long",
    )(p.reshape(b, l1, l2, P_WIDTH), _const_bf16(kin), jnp.asarray(tw, F32), m2k, cs)
    return y.reshape(b * n, ch)


def fnet_branch(p, b, n):
    if n == 8192:
        return fnet_long(p, b, n)
    ch = FN_WIDTH
    xr = lax.slice_in_dim(p, P_FN, P_FN + ch, axis=1).reshape(b, 1, 1, n, ch)
    a = jnp.concatenate([xr, jnp.zeros_like(xr)], axis=1)
    y = fnet_mid(a, _twiddles(1, n, 1), _stage2_matrices(n)[0], _fnet_channel_matrix(n), 1)
    return y.reshape(b * n, ch)


def _seq_conv_kernel(prev_ref, cur_ref, next_ref, w_ref, *o_refs, n_tiles):
    t = pl.program_id(1)
    tt = cur_ref.shape[0]
    cur = cur_ref[...]
    row = lax.broadcasted_iota(jnp.int32, (tt, 1), 0)
    before = jnp.where(t == 0, 0.0, prev_ref[7:8, :])
    after = jnp.where(t == n_tiles - 1, 0.0, next_ref[0:1, :])
    left = jnp.where(row == 0, before, pltpu.roll(cur, 1, axis=0))
    right = jnp.where(row == tt - 1, after, pltpu.roll(cur, tt - 1, axis=0))
    y = w_ref[0:1, :] * left + w_ref[1:2, :] * cur + w_ref[2:3, :] * right
    for part, o_ref in enumerate(o_refs):
        o_ref[...] = y[:, part * HY_WIDTH:(part + 1) * HY_WIDTH]


def seq_conv(p, conv_w, b, n):
    tt = min(n, 1024)
    n_tiles = n // tt
    parts = HY_ORDER + 1
    w = parts * HY_WIDTH
    c0 = P_HY // w
    out_spec = pl.BlockSpec((tt, HY_WIDTH), lambda bi, t: (bi * n_tiles + t, 0))
    return pl.pallas_call(
        functools.partial(_seq_conv_kernel, n_tiles=n_tiles),
        grid=(b, n_tiles),
        in_specs=[pl.BlockSpec((8, w), lambda bi, t: (jnp.maximum((bi * n_tiles + t) * (tt // 8) - 1, 0), c0)),
                  pl.BlockSpec((tt, w), lambda bi, t: (bi * n_tiles + t, c0)),
                  pl.BlockSpec((8, w), lambda bi, t: (jnp.minimum((bi * n_tiles + t + 1) * (tt // 8),
                                                                  b * n // 8 - 1), c0)),
                  pl.BlockSpec((SHORT_CONV, w), lambda bi, t: (0, 0))],
        out_specs=[out_spec] * parts,
        out_shape=[jax.ShapeDtypeStruct((b * n, HY_WIDTH), F32)] * parts,
        compiler_params=_cparams("parallel", "parallel"),
        name="hy_seq_conv",
    )(p, p, p, conv_w)


def _hdot(a, b):
    return jnp.dot(a, b, preferred_element_type=F32, precision=lax.Precision.HIGHEST)


def _hy_filter_kernel(f_ref, w1_ref, b1_ref, f1_ref, w2_ref, b2_ref, f2_ref, w3_ref, dl_ref, k_ref, s_ref, *, n, tr):
    i = pl.program_id(0)
    feats = f_ref[...]
    hid = jnp.sin(f1_ref[...] * (_hdot(feats, w1_ref[...]) + b1_ref[...]))
    hid = jnp.sin(f2_ref[...] * (_hdot(hid, w2_ref[...]) + b2_ref[...]))
    filt = _hdot(hid, w3_ref[...]) * jnp.exp(-feats[:, 0:1] * dl_ref[...])
    row = i * tr + lax.broadcasted_iota(jnp.int32, (tr, 1), 0)
    filt = jnp.where(row == n, 0.0, filt)
    k_ref[...] = filt

    @pl.when(i == 0)
    def _():
        s_ref[...] = jnp.zeros_like(s_ref)

    s_ref[...] += jnp.sum(jnp.abs(filt), axis=0, keepdims=True)


def hy_filter(n, w1, b1, freq1, w2, b2, freq2, w3):
    pos = jnp.arange(n, dtype=F32)
    t = pos / max(n - 1, 1)
    bands = jnp.linspace(1e-4, HY_BANDS - 1, HY_BANDS, dtype=F32)
    ang = (2.0 * math.pi / n) * pos[:, None] * bands[None, :]
    feats = jnp.concatenate([t[:, None], jnp.cos(ang), -jnp.sin(ang)], axis=-1)
    feats2 = jnp.concatenate([feats, feats[:1], feats[:0:-1]], axis=0)
    kpad = 128
    feats2 = jnp.pad(feats2, ((0, 0), (0, kpad - HY_EMB_DIM)))
    w1p = jnp.pad(w1, ((0, kpad - HY_EMB_DIM), (0, 0)))
    min_decay = math.log(HY_DECAY_TARGET) / HY_SLOW_DECAY_PCT
    max_decay = math.log(HY_DECAY_TARGET) / HY_FAST_DECAY_PCT
    cw = HY_ORDER * HY_WIDTH
    deltas = jnp.abs(jnp.linspace(min_decay, max_decay, cw, dtype=F32)).reshape(1, cw)
    tr = min(n, 1024)
    half = n // tr
    hd = HY_FILTER_HIDDEN
    vec = lambda v: v.reshape(1, hd)
    full = lambda shp: pl.BlockSpec(shp, lambda i: (0, 0))
    return pl.pallas_call(
        functools.partial(_hy_filter_kernel, n=n, tr=tr),
        grid=(2 * half,),
        in_specs=[pl.BlockSpec((tr, kpad), lambda i: (i, 0)), full((kpad, hd)), full((1, hd)), full((1, hd)),
                  full((hd, hd)), full((1, hd)), full((1, hd)),
                  pl.BlockSpec((hd, cw), lambda i: (0, i // half)), full((1, cw))],
        out_specs=[pl.BlockSpec((tr, cw), lambda i: (i, 0)), full((1, cw))],
        out_shape=[jax.ShapeDtypeStruct((2 * n, cw), F32), jax.ShapeDtypeStruct((1, cw), F32)],
        compiler_params=_cparams("arbitrary"),
        name="hy_filter",
    )(feats2, w1p, vec(b1), vec(freq1), w2, vec(b2), vec(freq2), w3, deltas)


def _hy_spec_kernel(a_ref, tw_ref, m2_ref, s_ref, o_ref, *, kb):
    inv = 1.0 / s_ref[...]
    ts = [_twiddle_mul(a_ref[0, j], a_ref[1, j], tw_ref[0, :, j:j + 1], tw_ref[1, :, j:j + 1], False)
          for j in range(kb)]
    xs = [_cplx_apply(m2_ref, tr, ti) for tr, ti in ts]
    for j, (xr, xi) in enumerate(xs):
        o_ref[0, j] = xr * inv
        o_ref[1, j] = xi * inv


def hy_spec(a, tw, m2, abs_sum, kb):
    _, l1, l2, cw = a.shape
    w = HY_WIDTH
    return pl.pallas_call(
        functools.partial(_hy_spec_kernel, kb=kb),
        grid=(l1 // kb, cw // w),
        in_specs=[pl.BlockSpec((2, kb, l2, w), lambda k, c: (0, k, 0, c)),
                  pl.BlockSpec((None, 2, l2, kb), lambda k, c: (k, 0, 0, 0)),
                  pl.BlockSpec(m2.shape, lambda k, c: (0, 0)),
                  pl.BlockSpec((1, w), lambda k, c: (0, c))],
        out_specs=pl.BlockSpec((2, kb, l2, w), lambda k, c: (0, k, 0, c)),
        out_shape=jax.ShapeDtypeStruct((2, l1, l2, cw), F32),
        compiler_params=_cparams("parallel", "parallel"),
        name="hy_spec",
    )(a, tw, m2, abs_sum)


def _hy_mid_kernel(a_ref, tw_ref, m2_ref, m2c_ref, kf_ref, o_ref, *, kb):
    cs = [(tw_ref[0, :, j:j + 1], tw_ref[1, :, j:j + 1]) for j in range(kb)]
    ts = [_twiddle_mul(a_ref[0, j], a_ref[1, j], c, s, False) for j, (c, s) in enumerate(cs)]
    xs = [_cplx_apply(m2_ref, tr, ti) for tr, ti in ts]
    ps = [(xr * kf_ref[0, j] - xi * kf_ref[1, j], xr * kf_ref[1, j] + xi * kf_ref[0, j])
          for j, (xr, xi) in enumerate(xs)]
    bs = [_cplx_apply(m2c_ref, pr, pi) for pr, pi in ps]
    for j, ((br, bi), (c, s)) in enumerate(zip(bs, cs)):
        o_ref[0, j], o_ref[1, j] = _twiddle_mul(br, bi, c, s, True)


def hy_mid(a, tw, m2, m2c, kf, order, kb):
    bsz, _, l1, l2, w = a.shape
    return pl.pallas_call(
        functools.partial(_hy_mid_kernel, kb=kb),
        grid=(bsz, l1 // kb),
        in_specs=[pl.BlockSpec((None, 2, kb, l2, w), lambda b, k: (b, 0, k, 0, 0)),
                  pl.BlockSpec((None, 2, l2, kb), lambda b, k: (k, 0, 0, 0)),
                  pl.BlockSpec(m2.shape, lambda b, k: (0, 0)),
                  pl.BlockSpec(m2c.shape, lambda b, k: (0, 0)),
                  pl.BlockSpec((2, kb, l2, w), lambda b, k: (0, k, 0, order))],
        out_specs=pl.BlockSpec((None, 2, kb, l2, w), lambda b, k: (b, 0, k, 0, 0)),
        out_shape=jax.ShapeDtypeStruct((bsz, 2, l1, l2, w), F32),
        compiler_params=_cparams("parallel", "parallel"),
        name="hy_mid",
    )(a, tw, m2, m2c, kf)


def _hy_out_kernel(m_ref, bp_ref, xo_ref, z_ref, bias_ref, o_ref):
    y = jnp.dot(m_ref[...], bp_ref[...].astype(BF16), preferred_element_type=F32)
    o_ref[...] = xo_ref[...] * (y + bias_ref[...] * z_ref[...])


def hy_out(m, bp, xo, z, bias_row, tn):
    bsz, k, n = bp.shape
    r = m.shape[0]
    blk = pl.BlockSpec((None, r, tn), lambda b, j: (b, 0, j))
    return pl.pallas_call(
        _hy_out_kernel,
        grid=(bsz, n // tn),
        in_specs=[pl.BlockSpec((r, k), lambda b, j: (0, 0)),
                  pl.BlockSpec((None, k, tn), lambda b, j: (b, 0, j)), blk, blk,
                  pl.BlockSpec((1, tn), lambda b, j: (0, j))],
        out_specs=blk,
        out_shape=jax.ShapeDtypeStruct((bsz, r, n), F32),
        compiler_params=_cparams("parallel", "parallel"),
        name="hy_out",
    )(m, bp, xo, z, bias_row)


SUB = 8
HY_KB = 16


def _half_spectrum_blocks(l1, kb):
    return -(-(l1 // 2 + 1) // kb)


def _kron_stage1(l1, n1_used, kb):
    c, s = _cos_sin(l1, n1_used, l1)
    m = np.stack([c, -s], axis=0).reshape(2, l1 // kb, kb, n1_used)
    m = np.transpose(m, (1, 0, 2, 3)).reshape(l1 // kb, 2 * kb, n1_used)[:_half_spectrum_blocks(l1, kb)]
    return _const_bf16(np.stack([np.kron(blk, np.eye(SUB)) for blk in m]))


def _kron_stage_out(l1, n1_used, kb):
    c, s = _cos_sin(n1_used, l1, l1)
    k1 = np.arange(l1)
    mult = np.where((k1 == 0) | (k1 == l1 // 2), 1.0, np.where(k1 < l1 // 2, 2.0, 0.0))
    m = np.stack([c * mult, -s * mult], axis=1).reshape(n1_used, 2, l1 // kb, kb) / (l1 * l1)
    m = np.transpose(m, (2, 0, 1, 3)).reshape(l1 // kb, n1_used, 2 * kb)[:_half_spectrum_blocks(l1, kb)]
    return _const_bf16(np.stack([np.kron(blk, np.eye(SUB)) for blk in m]))


def _strided_stage_in(kin_ref, src_ref, a_ref):
    n_sub, width = src_ref.shape[1] // SUB, src_ref.shape[2]
    rows = src_ref.shape[0] * SUB
    for j in range(n_sub):
        r = jnp.dot(kin_ref[...], src_ref[:, j * SUB:(j + 1) * SUB, :].reshape(rows, width).astype(BF16),
                    preferred_element_type=F32)
        a_ref[:, j * SUB:(j + 1) * SUB, :] = r.reshape(a_ref.shape[0], SUB, width)


def _hy_conv_kernel(z_ref, xo_ref, kin_ref, tw_ref, m2_ref, m2c_ref, kf_ref, kout_ref, bias_ref, o_ref,
                    a_ref, b_ref, *, kb):
    k = pl.program_id(1)

    @pl.when(k == 0)
    def _():
        o_ref[...] = jnp.zeros_like(o_ref)

    _strided_stage_in(kin_ref, z_ref, a_ref)
    grp = kb
    for g0 in range(0, kb, grp):
        js = range(g0, g0 + grp)
        cs = [(tw_ref[0, :, j:j + 1], tw_ref[1, :, j:j + 1]) for j in js]
        ts = [_twiddle_mul(a_ref[j], a_ref[kb + j], c, s, False) for j, (c, s) in zip(js, cs)]
        xs = [_cplx_apply(m2_ref, tr, ti) for tr, ti in ts]
        ps = [(xr * kf_ref[0, j] - xi * kf_ref[1, j], xr * kf_ref[1, j] + xi * kf_ref[0, j])
              for j, (xr, xi) in zip(js, xs)]
        bs = [_cplx_apply(m2c_ref, pr, pi) for pr, pi in ps]
        for j, (br, bi), (c, s) in zip(js, bs, cs):
            b_ref[j], b_ref[kb + j] = _twiddle_mul(br, bi, c, s, True)
    n_sub, width = o_ref.shape[1] // SUB, o_ref.shape[2]
    for j in range(n_sub):
        r = jnp.dot(kout_ref[...], b_ref[:, j * SUB:(j + 1) * SUB, :].reshape(2 * kb * SUB, width).astype(BF16),
                    preferred_element_type=F32)
        o_ref[:, j * SUB:(j + 1) * SUB, :] += r.reshape(o_ref.shape[0], SUB, width)

    @pl.when(k == pl.num_programs(1) - 1)
    def _():
        o_ref[...] = xo_ref[...] * (o_ref[...] + bias_ref[...] * z_ref[...])


def hy_conv_long(z, xo, kf, bias, order, b, n):
    w, kb = HY_WIDTH, HY_KB
    l1 = l2 = 128
    n1 = n // l2
    nblk = _half_spectrum_blocks(l1, kb)
    tw = _twiddles(l1, l2, kb)[:nblk]
    m2, m2c = _stage2_matrices(l2)
    kin, kout = _kron_stage1(l1, n1, kb), _kron_stage_out(l1, n1, kb)
    tok = pl.BlockSpec((None, n1, l2, w), lambda bi, k: (bi, 0, 0, 0), pipeline_mode=pl.Buffered(1))
    const = lambda a: pl.BlockSpec(a.shape, lambda bi, k: (0,) * a.ndim)
    out = pl.pallas_call(
        functools.partial(_hy_conv_kernel, kb=kb),
        grid=(b, nblk),
        in_specs=[tok, tok,
                  pl.BlockSpec((None,) + kin.shape[1:], lambda bi, k: (k, 0, 0)),
                  pl.BlockSpec((None, 2, l2, kb), lambda bi, k: (k, 0, 0, 0)),
                  const(m2), const(m2c),
                  pl.BlockSpec((2, kb, l2, w), lambda bi, k: (0, k, 0, order)),
                  pl.BlockSpec((None,) + kout.shape[1:], lambda bi, k: (k, 0, 0)),
                  pl.BlockSpec((1, w), lambda bi, k: (0, 0))],
        out_specs=pl.BlockSpec((None, n1, l2, w), lambda bi, k: (bi, 0, 0, 0), pipeline_mode=pl.Buffered(1)),
        out_shape=jax.ShapeDtypeStruct((b, n1, l2, w), F32),
        scratch_shapes=[pltpu.VMEM((2 * kb, l2, w), F32), pltpu.VMEM((2 * kb, l2, w), F32)],
        compiler_params=_cparams("parallel", "arbitrary"),
        name="hy_conv_long",
    )(z.reshape(b, n1, l2, w), xo.reshape(b, n1, l2, w), kin, tw, m2, m2c, kf, kout, bias.reshape(1, w))
    return out.reshape(b * n, w)


def _hy_spec_long_kernel(kern_ref, kin_ref, tw_ref, m2_ref, s_ref, o_ref, a_ref, *, kb):
    _strided_stage_in(kin_ref, kern_ref, a_ref)
    inv = 1.0 / s_ref[...]
    ts = [_twiddle_mul(a_ref[j], a_ref[kb + j], tw_ref[0, :, j:j + 1], tw_ref[1, :, j:j + 1], False)
          for j in range(kb)]
    xs = [_cplx_apply(m2_ref, tr, ti) for tr, ti in ts]
    for j, (xr, xi) in enumerate(xs):
        o_ref[0, j] = xr * inv
        o_ref[1, j] = xi * inv


def hy_spec_long(kern, abs_sum):
    w, kb = HY_WIDTH, HY_KB
    l1 = l2 = 128
    cw = kern.shape[1]
    nblk = _half_spectrum_blocks(l1, kb)
    tw = _twiddles(l1, l2, kb)[:nblk]
    m2, _ = _stage2_matrices(l2)
    kin = _kron_stage1(l1, l1, kb)
    return pl.pallas_call(
        functools.partial(_hy_spec_long_kernel, kb=kb),
        grid=(cw // w, nblk),
        in_specs=[pl.BlockSpec((l1, l2, w), lambda c, k: (0, 0, c), pipeline_mode=pl.Buffered(1)),
                  pl.BlockSpec((None,) + kin.shape[1:], lambda c, k: (k, 0, 0)),
                  pl.BlockSpec((None, 2, l2, kb), lambda c, k: (k, 0, 0, 0)),
                  pl.BlockSpec(m2.shape, lambda c, k: (0, 0)),
                  pl.BlockSpec((1, w), lambda c, k: (0, c))],
        out_specs=pl.BlockSpec((2, kb, l2, w), lambda c, k: (0, k, 0, c)),
        out_shape=jax.ShapeDtypeStruct((2, nblk * kb, l2, cw), F32),
        scratch_shapes=[pltpu.VMEM((2 * kb, l2, w), F32)],
        compiler_params=_cparams("parallel", "arbitrary"),
        name="hy_spec_long",
    )(kern.reshape(l1, l2, cw), kin, tw, m2, abs_sum)


def hyena_pallas(p, conv_w, w1, b1, freq1, w2, b2, freq2, w3, bias, b, n):
    w = HY_WIDTH
    x0, x1, v = seq_conv(p, conv_w, b, n)
    kern, abs_sum = hy_filter(n, w1, b1, freq1, w2, b2, freq2, w3)
    z = v
    if n == 8192:
        kf = hy_spec_long(kern, abs_sum)
        for order, xo in enumerate((x0, x1)):
            z = hy_conv_long(z, xo, kf, bias[order], order, b, n)
        return z
    l2 = 2 * n
    tw = _twiddles(1, l2, 1)
    m2, m2c = _stage2_matrices(l2)
    ak = jnp.stack([kern, jnp.zeros_like(kern)], axis=0).reshape(2, 1, l2, HY_ORDER * w)
    m_out = _const_bf16(np.eye(n, 2 * l2) / l2)
    kf = hy_spec(ak, tw, m2, abs_sum, 1)
    for order, xo in enumerate((x0, x1)):
        zp = jnp.pad(z.reshape(b, 1, 1, n, w), ((0, 0), (0, 0), (0, 0), (0, n), (0, 0)))
        a = jnp.concatenate([zp, jnp.zeros_like(zp)], axis=1)
        bp = hy_mid(a, tw, m2, m2c, kf, order, 1)
        z = hy_out(m_out, bp.reshape(b, 2 * l2, w), xo.reshape(b, n, w), z.reshape(b, n, w),
                   bias[order].reshape(1, w), w).reshape(b * n, w)
    return z


def _prep_w_in(w):
    main = jnp.concatenate([w[:, OFF_Q:OFF_Z], w[:, OFF_HY:OFF_GATE], w[:, OFF_FN:OFF_HY], w[:, OFF_Z:OFF_BETA]],
                           axis=1).astype(BF16)
    small = jnp.pad(w[:, OFF_BETA:OFF_FN], ((0, 0), (0, SMALL_WIDTH - 4 * DN_HEADS))).astype(BF16)
    return main, small, w[:, OFF_GATE:IN_WIDTH].astype(BF16)


def kernel(x, c, ctx, c_ctx, w_mod, b_mod, norm1, norm2, w_in, dn_conv, dn_a_log, dn_dt_bias,
           dn_out_norm, hy_conv, hy_w1, hy_b1, hy_freq1, hy_w2, hy_b2, hy_freq2, hy_w3, hy_bias,
           w_branch_a, w_branch_b, w_branch_c, w_out, w_ff1, w_ff2, final_norm):
    b, n_lat, d = x.shape
    rows = n_lat // GRID_W
    n_ctx = ctx.shape[1]
    tm_x, tm_c = 1024, n_ctx

    c_rows = jnp.concatenate([c, c_ctx[None], jnp.zeros((8 - b - 1, d), F32)], axis=0)
    mods = mod_vectors(c_rows, w_mod, b_mod)
    s_zero = jnp.zeros((2, b, DN_HEADS, DN_HEAD_DIM, DN_HEAD_DIM), F32)
    h, hc = x.reshape(b * n_lat, d), ctx.reshape(b * n_ctx, d)

    for l in range(DEPTH):
        last = l == DEPTH - 1
        mv = mods[l].reshape(8, N_MOD, 1, d)
        mx = [mv[:b, i] for i in range(N_MOD)]
        mc = [mv[b:b + 1, i] for i in range(N_MOD)]
        w_main, w_small, w_gate = _prep_w_in(w_in[l])
        wa, wb, wc, wo = (w.astype(BF16) for w in (w_branch_a[l], w_branch_b[l], w_branch_c[l], w_out[l]))
        w1, w2 = w_ff1[l].astype(BF16), w_ff2[l].astype(BF16)
        n1, n2 = norm1[l][None, None, :], norm2[l][None, None, :]

        p_c, small_c = in_proj(hc, n1 * (1.0 + mc[1]), mc[0], w_main, w_small, b * n_ctx, tm_c)
        p_x, small_x = in_proj(h, n1 * (1.0 + mx[1]), mx[0], w_main, w_small, n_lat, tm_x)

        def mix(p, n):
            y_c = hyena_pallas(p, hy_conv[l], hy_w1[l], hy_b1[l], hy_freq1[l], hy_w2[l], hy_b2[l], hy_freq2[l],
                               hy_w3[l], hy_bias[l], b, n)
            return fnet_branch(p, b, n), y_c

        ocf, ocb, s_ctx = delta_branch(p_c, small_c, dn_conv[l], dn_a_log[l], dn_dt_bias[l], s_zero,
                                       b, n_ctx, 1, n_ctx)
        oxf, oxb, _ = delta_branch(p_x, small_x, dn_conv[l], dn_a_log[l], dn_dt_bias[l], s_ctx,
                                   b, n_lat, rows, GRID_W)

        y_b, y_c = mix(p_x, n_lat)
        h = merge(oxf, oxb, p_x, y_b, y_c, h, n1 * (1.0 + mx[1]), mx[0], mx[2], dn_out_norm[l],
                  w_gate, wa, wb, wc, wo, n_lat, 512)
        h = mlp(h, n2 * (1.0 + mx[4]), mx[3], mx[5], w1, w2, n_lat, tm_x, final_norm if last else None)

        if not last:
            y_b, y_c = mix(p_c, n_ctx)
            hc = merge(ocf, ocb, p_c, y_b, y_c, hc, n1 * (1.0 + mc[1]), mc[0], mc[2], dn_out_norm[l],
                       w_gate, wa, wb, wc, wo, b * n_ctx, tm_c)
            hc = mlp(hc, n2 * (1.0 + mc[4]), mc[3], mc[5], w1, w2, b * n_ctx, tm_c)

    return h.reshape(b, n_lat, d)
```

```python
import functools
import math

import jax
import jax.numpy as jnp
import numpy as np
from jax import lax
from jax.experimental import pallas as pl
from jax.experimental.pallas import tpu as pltpu

D_MODEL = 1024
DEPTH = 2
GRID_W = 64
NORM_EPS = 1e-6
N_MOD = 6

DN_HEADS = 4
DN_HEAD_DIM = 128
DN_WIDTH = DN_HEADS * DN_HEAD_DIM
DN_CHUNK = 64
SHORT_CONV = 3

FN_GROUPS = 4
FN_GROUP_DIM = 64
FN_WIDTH = FN_GROUPS * FN_GROUP_DIM

HY_WIDTH = 256
HY_ORDER = 2
HY_EMB_DIM = 33
HY_BANDS = (HY_EMB_DIM - 1) // 2
HY_FILTER_HIDDEN = 64
HY_FAST_DECAY_PCT = 0.3
HY_SLOW_DECAY_PCT = 1.5
HY_DECAY_TARGET = 1e-2

N_BRANCHES = 3
D_FF = 4 * D_MODEL

OFF_Q = 0
OFF_Z = 3 * DN_WIDTH
OFF_BETA = OFF_Z + DN_WIDTH
OFF_A = OFF_BETA + 2 * DN_HEADS
OFF_FN = OFF_A + 2 * DN_HEADS
OFF_HY = OFF_FN + FN_WIDTH
OFF_GATE = OFF_HY + (HY_ORDER + 1) * HY_WIDTH
IN_WIDTH = OFF_GATE + N_BRANCHES * D_MODEL

P_QKV = 0
P_HY = P_QKV + 3 * DN_WIDTH
P_FN = P_HY + (HY_ORDER + 1) * HY_WIDTH
P_Z = P_FN + FN_WIDTH
P_WIDTH = P_Z + DN_WIDTH
SMALL_WIDTH = 128

F32 = jnp.float32
BF16 = jnp.bfloat16
VMEM_LIMIT = 56 * 1024 * 1024


def _cparams(*sem):
    return pltpu.CompilerParams(dimension_semantics=sem, vmem_limit_bytes=VMEM_LIMIT)


def _bdot(a, b):
    return jnp.dot(a.astype(BF16), b.astype(BF16), preferred_element_type=F32)


def _sigmoid(x):
    return 0.5 * jnp.tanh(0.5 * x) + 0.5


def _modnorm(xf, gs, sh):
    r = lax.rsqrt(jnp.mean(xf * xf, axis=-1, keepdims=True) + NORM_EPS)
    return xf * r * gs + sh


def _mod_kernel(c_ref, w_ref, b_ref, o_ref):
    c = c_ref[...]
    o_ref[...] = _bdot(c * _sigmoid(c), w_ref[...]) + b_ref[...]


def mod_vectors(c_rows, w_mod, b_mod):
    tn = 1536
    n = N_MOD * D_MODEL
    return pl.pallas_call(
        _mod_kernel,
        grid=(DEPTH, n // tn),
        in_specs=[pl.BlockSpec((8, D_MODEL), lambda l, j: (0, 0)),
                  pl.BlockSpec((None, D_MODEL, tn), lambda l, j: (l, 0, j)),
                  pl.BlockSpec((None, 1, tn), lambda l, j: (l, 0, j))],
        out_specs=pl.BlockSpec((None, 8, tn), lambda l, j: (l, 0, j)),
        out_shape=jax.ShapeDtypeStruct((DEPTH, 8, n), F32),
        compiler_params=_cparams("parallel", "parallel"),
        name="mod_vectors",
    )(c_rows, w_mod, b_mod.reshape(DEPTH, 1, n))


def _in_proj_kernel(x_ref, gs_ref, sh_ref, w_ref, ws_ref, p_ref, small_ref, xn_ref):
    @pl.when(pl.program_id(1) == 0)
    def _():
        xn = _modnorm(x_ref[...], gs_ref[0], sh_ref[0]).astype(BF16)
        xn_ref[...] = xn
        small_ref[...] = jnp.dot(xn, ws_ref[...], preferred_element_type=F32)

    p_ref[...] = jnp.dot(xn_ref[...], w_ref[...], preferred_element_type=F32)


def in_proj(x2d, gs, sh, w_main, w_small, rows_per_mod, tm):
    m = x2d.shape[0]
    tn = 1024
    tiles_per_mod = rows_per_mod // tm
    mod_spec = pl.BlockSpec((1, 1, D_MODEL), lambda i, j: (i // tiles_per_mod, 0, 0))
    return pl.pallas_call(
        _in_proj_kernel,
        grid=(m // tm, P_WIDTH // tn),
        in_specs=[pl.BlockSpec((tm, D_MODEL), lambda i, j: (i, 0)), mod_spec, mod_spec,
                  pl.BlockSpec((D_MODEL, tn), lambda i, j: (0, j)),
                  pl.BlockSpec((D_MODEL, SMALL_WIDTH), lambda i, j: (0, 0))],
        out_specs=[pl.BlockSpec((tm, tn), lambda i, j: (i, j)),
                   pl.BlockSpec((tm, SMALL_WIDTH), lambda i, j: (i, 0))],
        out_shape=[jax.ShapeDtypeStruct((m, P_WIDTH), F32), jax.ShapeDtypeStruct((m, SMALL_WIDTH), F32)],
        scratch_shapes=[pltpu.VMEM((tm, D_MODEL), BF16)],
        compiler_params=_cparams("parallel", "arbitrary"),
        name="in_proj",
    )(x2d, gs, sh, w_main, w_small)


def _merge_kernel(of_ref, ob_ref, z_ref, yb_ref, yc_ref, h_ref, gs_ref, sh_ref, gate_ref, nrm_ref,
                  wg_ref, wa_ref, wb_ref, wc_ref, wo_ref, out_ref):
    xn = _modnorm(h_ref[...], gs_ref[0], sh_ref[0]).astype(BF16)
    o = of_ref[...] + ob_ref[...]
    z = z_ref[...]
    heads = []
    for hd in range(DN_HEADS):
        sl = slice(hd * DN_HEAD_DIM, (hd + 1) * DN_HEAD_DIM)
        oh, zh = o[:, sl], z[:, sl]
        r = lax.rsqrt(jnp.mean(oh * oh, axis=-1, keepdims=True) + NORM_EPS)
        heads.append(oh * r * nrm_ref[...] * (zh * _sigmoid(zh)))
    ya = jnp.concatenate(heads, axis=-1)
    merged = None
    for i, (y, w_ref) in enumerate(((ya, wa_ref), (yb_ref[...], wb_ref), (yc_ref[...], wc_ref))):
        g = jnp.dot(xn, wg_ref[:, i * D_MODEL:(i + 1) * D_MODEL], preferred_element_type=F32)
        term = _sigmoid(g) * _bdot(y, w_ref[...])
        merged = term if merged is None else merged + term
    out_ref[...] = h_ref[...] + gate_ref[0] * _bdot(merged, wo_ref[...])


def merge(o_f, o_b, p, y_b, y_c, h2d, gs, sh, gate, dn_out_norm, wg, wa, wb, wc, wo, rows_per_mod, tm):
    m = h2d.shape[0]
    tiles_per_mod = rows_per_mod // tm
    row = lambda w: pl.BlockSpec((tm, w), lambda i: (i, 0))
    full = lambda a: pl.BlockSpec(a.shape, lambda i: (0,) * a.ndim, pipeline_mode=pl.Buffered(1))
    mod_spec = pl.BlockSpec((1, 1, D_MODEL), lambda i: (i // tiles_per_mod, 0, 0))
    nrm = dn_out_norm.reshape(1, DN_HEAD_DIM)
    return pl.pallas_call(
        _merge_kernel,
        grid=(m // tm,),
        in_specs=[row(DN_WIDTH), row(DN_WIDTH),
                  pl.BlockSpec((tm, DN_WIDTH), lambda i: (i, P_Z // DN_WIDTH)),
                  row(FN_WIDTH), row(HY_WIDTH), row(D_MODEL), mod_spec, mod_spec, mod_spec,
                  full(nrm), full(wg), full(wa), full(wb), full(wc), full(wo)],
        out_specs=row(D_MODEL),
        out_shape=jax.ShapeDtypeStruct((m, D_MODEL), F32),
        compiler_params=_cparams("parallel"),
        name="merge",
    )(o_f, o_b, p, y_b, y_c, h2d, gs, sh, gate, nrm, wg, wa, wb, wc, wo)


def _mlp_kernel(h_ref, gs_ref, sh_ref, gate_ref, w1_ref, w2_ref, fin_ref, out_ref, xn_ref, acc_ref, *, final):
    j = pl.program_id(1)

    @pl.when(j == 0)
    def _():
        xn_ref[...] = _modnorm(h_ref[...], gs_ref[0], sh_ref[0]).astype(BF16)
        acc_ref[...] = jnp.zeros_like(acc_ref)

    a = jnp.maximum(jnp.dot(xn_ref[...], w1_ref[...], preferred_element_type=F32), 0.0)
    acc_ref[...] += jnp.dot((a * a).astype(BF16), w2_ref[...], preferred_element_type=F32)

    @pl.when(j == pl.num_programs(1) - 1)
    def _():
        y = h_ref[...] + gate_ref[0] * acc_ref[...]
        if final:
            y = y * lax.rsqrt(jnp.mean(y * y, axis=-1, keepdims=True) + NORM_EPS) * fin_ref[...]
        out_ref[...] = y


def mlp(h2d, gs, sh, gate, w1, w2, rows_per_mod, tm, final_gain=None):
    m = h2d.shape[0]
    tf = 1024
    tiles_per_mod = rows_per_mod // tm
    mod_spec = pl.BlockSpec((1, 1, D_MODEL), lambda i, j: (i // tiles_per_mod, 0, 0))
    final = final_gain is not None
    fin = (final_gain if final else jnp.ones((D_MODEL,), F32)).reshape(1, D_MODEL)
    return pl.pallas_call(
        functools.partial(_mlp_kernel, final=final),
        grid=(m // tm, D_FF // tf),
        in_specs=[pl.BlockSpec((tm, D_MODEL), lambda i, j: (i, 0)), mod_spec, mod_spec, mod_spec,
                  pl.BlockSpec((D_MODEL, tf), lambda i, j: (0, j)),
                  pl.BlockSpec((tf, D_MODEL), lambda i, j: (j, 0)),
                  pl.BlockSpec((1, D_MODEL), lambda i, j: (0, 0))],
        out_specs=pl.BlockSpec((tm, D_MODEL), lambda i, j: (i, 0)),
        out_shape=jax.ShapeDtypeStruct((m, D_MODEL), F32),
        scratch_shapes=[pltpu.VMEM((tm, D_MODEL), BF16), pltpu.VMEM((tm, D_MODEL), F32)],
        compiler_params=_cparams("parallel", "arbitrary"),
        name="mlp",
    )(h2d, gs, sh, gate, w1, w2, fin)


def _dnconv_kernel(prev_ref, cur_ref, next_ref, w_ref, o_ref, *, cols, n_tiles):
    t, j = pl.program_id(1), pl.program_id(2)
    tt = cur_ref.shape[0]
    prev = jnp.where(t == 0, 0.0, prev_ref[...])
    nxt = jnp.where(t == n_tiles - 1, 0.0, next_ref[...])
    ext = jnp.concatenate([prev, cur_ref[...], nxt], axis=0)
    n_ext = tt + 2 * cols
    col = lax.broadcasted_iota(jnp.int32, (n_ext, 1), 0) % cols
    left = jnp.where(col == 0, 0.0, pltpu.roll(ext, 1, axis=0))
    right = jnp.where(col == cols - 1, 0.0, pltpu.roll(ext, n_ext - 1, axis=0))
    acc = jnp.zeros((tt, DN_WIDTH), F32)
    for dr in range(SHORT_CONV):
        base = dr * cols
        acc = (acc + w_ref[3 * dr:3 * dr + 1, :] * left[base:base + tt]
               + w_ref[3 * dr + 1:3 * dr + 2, :] * ext[base:base + tt]
               + w_ref[3 * dr + 2:3 * dr + 3, :] * right[base:base + tt])
    y = acc * _sigmoid(acc)
    q_scale = jnp.where(j == 0, DN_HEAD_DIM ** -0.5, 1.0)
    for hd in range(DN_HEADS):
        sl = slice(hd * DN_HEAD_DIM, (hd + 1) * DN_HEAD_DIM)
        yh = y[:, sl]
        nrm = lax.rsqrt(jnp.sum(yh * yh, axis=-1, keepdims=True) + NORM_EPS) * q_scale
        o_ref[:, sl] = yh * jnp.where(j < 2, nrm, 1.0)


def dn_conv_prep(p, conv_w, b, n, rows, cols):
    tr = min(rows, 32)
    tt = tr * cols
    n_tiles = rows // tr
    nblk = b * n // cols
    c0 = P_QKV // DN_WIDTH
    return pl.pallas_call(
        functools.partial(_dnconv_kernel, cols=cols, n_tiles=n_tiles),
        grid=(b, n_tiles, 3),
        in_specs=[pl.BlockSpec((cols, DN_WIDTH),
                               lambda bi, t, j: (jnp.maximum(bi * rows + t * tr - 1, 0), c0 + j)),
                  pl.BlockSpec((tt, DN_WIDTH), lambda bi, t, j: (bi * n_tiles + t, c0 + j)),
                  pl.BlockSpec((cols, DN_WIDTH),
                               lambda bi, t, j: (jnp.minimum(bi * rows + (t + 1) * tr, nblk - 1), c0 + j)),
                  pl.BlockSpec((SHORT_CONV * SHORT_CONV, DN_WIDTH), lambda bi, t, j: (0, j))],
        out_specs=pl.BlockSpec((tt, DN_WIDTH), lambda bi, t, j: (bi * n_tiles + t, j)),
        out_shape=jax.ShapeDtypeStruct((b * n, 3 * DN_WIDTH), F32),
        compiler_params=_cparams("parallel", "parallel", "parallel"),
        name="dn_conv_prep",
    )(p, p, p, conv_w.reshape(SHORT_CONV * SHORT_CONV, 3 * DN_WIDTH))


PREP_CHUNKS = 8
PREP_GROUP = 4


def _softplus(x):
    return jnp.maximum(x, 0.0) + jnp.log(1.0 + jnp.exp(-jnp.abs(x)))


def _delta_prep_kernel(qkv_ref, sm_ref, smt_ref, prm_ref, prmt_ref, u0_ref, lhs1_ref, lhs2_ref, gl_ref, *, n_chunks):
    cc = DN_CHUNK
    nh = DN_HEADS
    sm = sm_ref[...]
    beta_all = _sigmoid(sm)
    g_all = -jnp.exp(prm_ref[0:1, :]) * _softplus(sm + prm_ref[1:2, :])
    gt_all = -jnp.exp(prmt_ref[:, 0:1]) * _softplus(smt_ref[...] + prmt_ref[:, 1:2])
    lt = 2 * cc
    lane = lax.broadcasted_iota(jnp.int32, (gt_all.shape[0], lt), 1) % cc
    gt_tiles = []
    for t in range(gt_all.shape[1] // lt):
        gt_f = gt_b = gt_all[:, t * lt:(t + 1) * lt]
        s = 1
        while s < cc:
            gt_f = gt_f + jnp.where(lane >= s, pltpu.roll(gt_f, s, axis=1), 0.0)
            gt_b = gt_b + jnp.where(lane < cc - s, pltpu.roll(gt_b, lt - s, axis=1), 0.0)
            s *= 2
        gt_tiles.append((gt_f, gt_b))
    ri = lax.broadcasted_iota(jnp.int32, (cc, cc), 0)
    ci_ = lax.broadcasted_iota(jnp.int32, (cc, cc), 1)
    sub = lax.broadcasted_iota(jnp.int32, (cc, 1), 0)
    blk = lambda s: (ri // s) == (ci_ // s)
    leaf = 8

    heads = []
    for ci in range(n_chunks):
        rows = slice(ci * cc, (ci + 1) * cc)
        gc_f = g_all[rows]
        gc_b = gc_f
        s = 1
        while s < cc:
            gc_f = gc_f + jnp.where(sub >= s, pltpu.roll(gc_f, s, axis=0), 0.0)
            gc_b = gc_b + jnp.where(sub < cc - s, pltpu.roll(gc_b, cc - s, axis=0), 0.0)
            s *= 2
        for h in range(nh):
            q = qkv_ref[rows, h * 128:(h + 1) * 128]
            k = qkv_ref[rows, (nh + h) * 128:(nh + h + 1) * 128]
            v = qkv_ref[rows, (2 * nh + h) * 128:(2 * nh + h + 1) * 128]
            heads.append((ci, rows, h, q, k, v, gc_f, gc_b))
    qkks = [lax.dot_general(jnp.concatenate([q, k], axis=0).astype(BF16), k.astype(BF16),
                            (((1,), (1,)), ((), ())), preferred_element_type=F32)
            for (_, _, _, q, k, _, _, _) in heads]
    per_group = PREP_GROUP * nh
    for g0 in range(0, len(heads), per_group):
        chains = []
        for (ci, rows, h, q, k, v, gc_f, gc_b), qkk in zip(heads[g0:g0 + per_group], qkks[g0:g0 + per_group]):
            qk, kk = qkk[:cc], qkk[cc:]
            for d in range(2):
                cb_, ca_ = d * nh + h, 2 * nh + d * nh + h
                beta = jnp.broadcast_to(beta_all[rows, cb_:cb_ + 1], (cc, DN_HEAD_DIM))
                g_col = jnp.broadcast_to((gc_f if d == 0 else gc_b)[:, ca_:ca_ + 1], (cc, DN_HEAD_DIM))
                lo = (ci % 2) * cc
                g_row = gt_tiles[ci // 2][d][ca_:ca_ + 1, lo:lo + cc]
                incl = (ri >= ci_) if d == 0 else (ri <= ci_)
                strict = (ri > ci_) if d == 0 else (ri < ci_)
                decay = jnp.exp(jnp.where(incl, g_col[:, :cc] - g_row, -1e30))
                a = jnp.where(strict, beta[:, :cc] * kk * decay, 0.0)
                eg = jnp.exp(g_col)
                g_last = g_col[cc - 1:cc] if d == 0 else g_col[0:1]
                chains.append(dict(ci=ci, rows=rows, h=h, d=d, a=a, qkd=qk * decay, g_last=g_last,
                                   rhs=jnp.concatenate([v * beta, k * (beta * eg)], axis=1),
                                   q_d=q * eg, k_d=k * jnp.exp(g_last - g_col)))

        pws = [jnp.where(blk(leaf), c["a"], 0.0) for c in chains]
        devs = [-pw for pw in pws]
        for _ in range(2):
            pws = [_bdot(pw, pw) for pw in pws]
            cross = [_bdot(dev, pw) for dev, pw in zip(devs, pws)]
            devs = [dev + pw + x for dev, pw, x in zip(devs, pws, cross)]
        s = leaf
        while s < cc:
            offs = [jnp.where(blk(2 * s) & jnp.logical_not(blk(s)), c["a"], 0.0) for c in chains]
            xs = [off + _bdot(dev, off) for dev, off in zip(devs, offs)]
            devs = [dev - x - _bdot(x, dev) for dev, x in zip(devs, xs)]
            s *= 2
        uws = [c["rhs"] + _bdot(dev, c["rhs"]) for dev, c in zip(devs, chains)]
        for c, uw in zip(chains, uws):
            d, h, ci = c["d"], c["h"], c["ci"]
            u0_ref[d, h, c["rows"], :] = uw[:, :128]
            lhs1_ref[d, h, ci] = jnp.concatenate([uw[:, 128:], c["q_d"]], axis=0).astype(BF16)
            lhs2_ref[d, h, ci] = jnp.concatenate([c["qkd"], c["k_d"].T], axis=0).astype(BF16)
            gl_ref[d, h, ci] = jnp.broadcast_to(jnp.exp(c["g_last"]), (8, DN_HEAD_DIM))


def delta_prep(qkv, small, small_t, prm, prm_t, b, n):
    nc = n // DN_CHUNK
    cb = min(PREP_CHUNKS, nc)
    nblk = nc // cb
    tt = cb * DN_CHUNK
    nh = DN_HEADS
    return pl.pallas_call(
        functools.partial(_delta_prep_kernel, n_chunks=cb),
        grid=(b, nblk),
        in_specs=[pl.BlockSpec((tt, 3 * DN_WIDTH), lambda bi, c: (bi * nblk + c, 0)),
                  pl.BlockSpec((tt, SMALL_WIDTH), lambda bi, c: (bi * nblk + c, 0)),
                  pl.BlockSpec((4 * nh, tt), lambda bi, c: (0, bi * nblk + c)),
                  pl.BlockSpec((2, SMALL_WIDTH), lambda bi, c: (0, 0)),
                  pl.BlockSpec((4 * nh, 2), lambda bi, c: (0, 0))],
        out_specs=[pl.BlockSpec((2, None, nh, tt, 128), lambda bi, c: (0, bi, 0, c, 0)),
                   pl.BlockSpec((2, None, nh, cb, 128, 128), lambda bi, c: (0, bi, 0, c, 0, 0)),
                   pl.BlockSpec((2, None, nh, cb, 192, 64), lambda bi, c: (0, bi, 0, c, 0, 0)),
                   pl.BlockSpec((2, None, nh, cb, 8, 128), lambda bi, c: (0, bi, 0, c, 0, 0))],
        out_shape=[jax.ShapeDtypeStruct((2, b, nh, n, 128), F32),
                   jax.ShapeDtypeStruct((2, b, nh, nc, 128, 128), BF16),
                   jax.ShapeDtypeStruct((2, b, nh, nc, 192, 64), BF16),
                   jax.ShapeDtypeStruct((2, b, nh, nc, 8, 128), F32)],
        compiler_params=_cparams("parallel", "parallel"),
        name="delta_prep",
    )(qkv, small, small_t, prm, prm_t)


def _delta_scan_kernel(u0f_ref, u0b_ref, l1f_ref, l1b_ref, l2f_ref, l2b_ref, glf_ref, glb_ref, s0_ref,
                       of_ref, ob_ref, sout_ref, st_ref, *, sc, nb):
    n = pl.program_id(0)
    cc = DN_CHUNK

    @pl.when(n == 0)
    def _():
        st_ref[...] = s0_ref[...]

    dirs = ((u0f_ref, l1f_ref, l2f_ref, glf_ref, of_ref), (u0b_ref, l1b_ref, l2b_ref, glb_ref, ob_ref))

    def body(i, carry):
        chains = []
        for d in range(2):
            ci = i if d == 0 else sc - 1 - i
            r0 = pl.multiple_of(ci * cc, cc)
            chains += [(d, bi, h, ci, r0) for bi in range(nb) for h in range(DN_HEADS)]
        sts = [st_ref[d, bi, h] for (d, bi, h, _, _) in chains]
        r1s = [jnp.dot(dirs[d][1][bi, h, ci], st.astype(BF16), preferred_element_type=F32)
               for (d, bi, h, ci, _), st in zip(chains, sts)]
        us = [dirs[d][0][bi, h, pl.ds(r0, cc), :] - r1[:cc] for (d, bi, h, _, r0), r1 in zip(chains, r1s)]
        r2s = [jnp.dot(dirs[d][2][bi, h, ci], u.astype(BF16), preferred_element_type=F32)
               for (d, bi, h, ci, _), u in zip(chains, us)]
        for (d, bi, h, ci, r0), st, r1, r2 in zip(chains, sts, r1s, r2s):
            dirs[d][4][bi, pl.ds(r0, cc), h * 128:(h + 1) * 128] = r1[cc:] + r2[:cc]
            st_ref[d, bi, h] = st * dirs[d][3][bi, h, ci, 0:1, :] + r2[cc:]
        return carry

    lax.fori_loop(0, sc, body, 0)

    @pl.when(n == pl.num_programs(0) - 1)
    def _():
        sout_ref[...] = st_ref[...]


def delta_scan(u0, lhs1, lhs2, gl, s0, b, n):
    nc = n // DN_CHUNK
    sc = min(8, nc)
    nblk = nc // sc
    nh = DN_HEADS
    tt = sc * DN_CHUNK
    fwd = lambda i: i
    bwd = lambda i: nblk - 1 - i

    def specs(d, blk):
        return [pl.BlockSpec((None, b, nh, tt, 128), lambda i: (d, 0, 0, blk(i), 0)),
                pl.BlockSpec((None, b, nh, sc, 128, 128), lambda i: (d, 0, 0, blk(i), 0, 0)),
                pl.BlockSpec((None, b, nh, sc, 192, 64), lambda i: (d, 0, 0, blk(i), 0, 0)),
                pl.BlockSpec((None, b, nh, sc, 8, 128), lambda i: (d, 0, 0, blk(i), 0, 0))]

    sf, sb = specs(0, fwd), specs(1, bwd)
    in_specs = [sf[0], sb[0], sf[1], sb[1], sf[2], sb[2], sf[3], sb[3],
                pl.BlockSpec((2, b, nh, 128, 128), lambda i: (0, 0, 0, 0, 0))]
    return pl.pallas_call(
        functools.partial(_delta_scan_kernel, sc=sc, nb=b),
        grid=(nblk,),
        in_specs=in_specs,
        out_specs=[pl.BlockSpec((b, tt, DN_WIDTH), lambda i: (0, fwd(i), 0)),
                   pl.BlockSpec((b, tt, DN_WIDTH), lambda i: (0, bwd(i), 0)),
                   pl.BlockSpec((2, b, nh, 128, 128), lambda i: (0, 0, 0, 0, 0))],
        out_shape=[jax.ShapeDtypeStruct((b, n, DN_WIDTH), F32), jax.ShapeDtypeStruct((b, n, DN_WIDTH), F32),
                   jax.ShapeDtypeStruct((2, b, nh, 128, 128), F32)],
        scratch_shapes=[pltpu.VMEM((2, b, nh, 128, 128), F32)],
        compiler_params=_cparams("arbitrary"),
        name="delta_scan",
    )(u0, u0, lhs1, lhs1, lhs2, lhs2, gl, gl, s0)


def delta_branch(p, small, conv_w, a_log, dt_bias, s0, b, n, rows, cols):
    nh = DN_HEADS
    qkv = dn_conv_prep(p, conv_w, b, n, rows, cols)
    small_t = small[:, :4 * nh].T
    rate = jnp.concatenate([jnp.zeros((2 * nh,), F32), a_log.reshape(-1)])
    bias = jnp.concatenate([jnp.zeros((2 * nh,), F32), dt_bias.reshape(-1)])
    prm_t = jnp.stack([rate, bias], axis=1)
    prm = jnp.pad(prm_t.T, ((0, 0), (0, SMALL_WIDTH - 4 * nh)))
    u0, lhs1, lhs2, gl = delta_prep(qkv, small, small_t, prm, prm_t, b, n)
    o_f, o_b, s_out = delta_scan(u0, lhs1, lhs2, gl, s0, b, n)
    return o_f.reshape(b * n, DN_WIDTH), o_b.reshape(b * n, DN_WIDTH), s_out


def _cos_sin(rows, cols, period):
    ang = 2.0 * np.pi * ((np.arange(rows)[:, None] * np.arange(cols)[None, :]) % period) / period
    return np.cos(ang), np.sin(ang)


def _const_bf16(a):
    return jnp.asarray(a, F32).astype(BF16)


def _stage2_matrices(l2):
    c, s = _cos_sin(l2, l2, l2)
    fwd = np.block([[c, s], [-s, c]])
    inv = np.block([[c, -s], [s, c]])
    return _const_bf16(fwd), _const_bf16(inv)


def _twiddles(l1, l2, kb):
    ang = 2.0 * np.pi * ((np.arange(l2)[:, None] * np.arange(l1)[None, :]) % (l1 * l2)) / (l1 * l2)
    tw = np.stack([np.cos(ang), np.sin(ang)], axis=0).reshape(2, l2, l1 // kb, kb)
    return jnp.asarray(np.transpose(tw, (2, 0, 1, 3)), F32)


def _twiddle_mul(ar, ai, c, s, conj):
    if conj:
        return ar * c - ai * s, ai * c + ar * s
    return ar * c + ai * s, ai * c - ar * s


def _cplx_apply(m_ref, re, im):
    half = re.shape[0]
    out = jnp.dot(m_ref[...], jnp.concatenate([re, im], axis=0).astype(BF16), preferred_element_type=F32)
    return out[:half], out[half:]


def _fnet_mid_kernel(a_ref, tw_ref, m2_ref, cs_ref, o_ref, *, kb, ch):
    ts = [_twiddle_mul(a_ref[0, j], a_ref[1, j], tw_ref[0, :, j:j + 1], tw_ref[1, :, j:j + 1], False)
          for j in range(kb)]
    us = [_cplx_apply(m2_ref, tr, ti) for tr, ti in ts]
    ys = [jnp.dot(jnp.concatenate([ur, ui], axis=1).astype(BF16), cs_ref[...], preferred_element_type=F32)
          for ur, ui in us]
    for j, y in enumerate(ys):
        o_ref[:, j * ch:(j + 1) * ch] = y


def fnet_mid(a, tw, m2, cs, kb):
    bsz, _, l1, l2, ch = a.shape
    return pl.pallas_call(
        functools.partial(_fnet_mid_kernel, kb=kb, ch=ch),
        grid=(bsz, l1 // kb),
        in_specs=[pl.BlockSpec((None, 2, kb, l2, ch), lambda b, k: (b, 0, k, 0, 0)),
                  pl.BlockSpec((None, 2, l2, kb), lambda b, k: (k, 0, 0, 0)),
                  pl.BlockSpec(m2.shape, lambda b, k: (0, 0)),
                  pl.BlockSpec(cs.shape, lambda b, k: (0, 0))],
        out_specs=pl.BlockSpec((None, l2, kb * ch), lambda b, k: (b, 0, k)),
        out_shape=jax.ShapeDtypeStruct((bsz, l2, l1 * ch), F32),
        compiler_params=_cparams("parallel", "parallel"),
        name="fnet_mid",
    )(a, tw, m2, cs)


def _fnet_channel_matrix(n):
    c, s = _cos_sin(FN_GROUP_DIM, FN_GROUP_DIM, FN_GROUP_DIM)
    eye = np.eye(FN_GROUPS)
    scale = 1.0 / math.sqrt(n * FN_GROUP_DIM)
    return _const_bf16(np.concatenate([np.kron(eye, c), np.kron(eye, s)], axis=0) * scale)


FN_SUB = 8


def _fnet_long_kernel(x_ref, kin_ref, tw_ref, m2k_ref, cs_ref, o_ref, a_ref):
    n_sub = x_ref.shape[1] // FN_SUB
    rows, width = x_ref.shape[0] * FN_SUB, x_ref.shape[2]
    half = FN_SUB * FN_SUB
    for j in range(n_sub):
        r = jnp.dot(kin_ref[...], x_ref[:, j * FN_SUB:(j + 1) * FN_SUB, :].reshape(rows, width).astype(BF16),
                    preferred_element_type=F32)
        a_ref[0, j * half:(j + 1) * half, :] = r[:half]
        a_ref[1, j * half:(j + 1) * half, :] = r[half:]
    tr, ti = _twiddle_mul(a_ref[0], a_ref[1], tw_ref[0], tw_ref[1], False)
    ur, ui = _cplx_apply(m2k_ref, tr, ti)
    y = jnp.dot(jnp.concatenate([ur, ui], axis=1).astype(BF16), cs_ref[...], preferred_element_type=F32)
    o_ref[...] = y.reshape(o_ref.shape)


def fnet_long(p, b, n):
    ch, sub = FN_WIDTH, FN_SUB
    l1, l2 = n // 128, 128
    nblk = l1 // sub
    c1, s1 = _cos_sin(l1, l1, l1)
    m1 = np.stack([c1, -s1], axis=0).reshape(2, nblk, sub, l1)
    kin = np.einsum('pbkn,jq->bpjknq', m1, np.eye(sub)).reshape(nblk, 2 * sub * sub, l1 * sub)
    ang = 2.0 * np.pi * ((np.arange(l2)[:, None] * np.arange(l1)[None, :]) % n) / n
    tw = np.stack([np.cos(ang), np.sin(ang)], axis=0).reshape(2, l2, nblk, sub)
    tw = np.transpose(tw, (2, 0, 1, 3)).reshape(nblk, 2, l2 * sub, 1)
    c2, s2 = _cos_sin(l2, l2, l2)
    m2k = _const_bf16(np.kron(np.block([[c2, s2], [-s2, c2]]), np.eye(sub)))
    cs = _fnet_channel_matrix(n)
    const = lambda a: pl.BlockSpec(a.shape, lambda bi, k: (0,) * a.ndim, pipeline_mode=pl.Buffered(1))
    y = pl.pallas_call(
        _fnet_long_kernel,
        grid=(b, nblk),
        in_specs=[pl.BlockSpec((None, l1, l2, ch), lambda bi, k: (bi, 0, 0, P_FN // ch),
                               pipeline_mode=pl.Buffered(1)),
                  pl.BlockSpec((None,) + kin.shape[1:], lambda bi, k: (k, 0, 0)),
                  pl.BlockSpec((None, 2, l2 * sub, 1), lambda bi, k: (k, 0, 0, 0)),
                  const(m2k), const(cs)],
        out_specs=pl.BlockSpec((None, l2, sub, ch), lambda bi, k: (bi, 0, k, 0)),
        out_shape=jax.ShapeDtypeStruct((b, l2, l1, ch), F32),
        scratch_shapes=[pltpu.VMEM((2, l2 * sub, ch), F32)],
        compiler_params=_cparams("parallel", "parallel"),
        name="fnet_long",
    )(p.reshape(b, l1, l2, P_WIDTH), _const_bf16(kin), jnp.asarray(tw, F32), m2k, cs)
    return y.reshape(b * n, ch)


def fnet_branch(p, b, n):
    if n == 8192:
        return fnet_long(p, b, n)
    ch = FN_WIDTH
    xr = lax.slice_in_dim(p, P_FN, P_FN + ch, axis=1).reshape(b, 1, 1, n, ch)
    a = jnp.concatenate([xr, jnp.zeros_like(xr)], axis=1)
    y = fnet_mid(a, _twiddles(1, n, 1), _stage2_matrices(n)[0], _fnet_channel_matrix(n), 1)
    return y.reshape(b * n, ch)


def _seq_conv_kernel(prev_ref, cur_ref, next_ref, w_ref, *o_refs, n_tiles):
    t = pl.program_id(1)
    tt = cur_ref.shape[0]
    cur = cur_ref[...]
    row = lax.broadcasted_iota(jnp.int32, (tt, 1), 0)
    before = jnp.where(t == 0, 0.0, prev_ref[7:8, :])
    after = jnp.where(t == n_tiles - 1, 0.0, next_ref[0:1, :])
    left = jnp.where(row == 0, before, pltpu.roll(cur, 1, axis=0))
    right = jnp.where(row == tt - 1, after, pltpu.roll(cur, tt - 1, axis=0))
    y = w_ref[0:1, :] * left + w_ref[1:2, :] * cur + w_ref[2:3, :] * right
    for part, o_ref in enumerate(o_refs):
        o_ref[...] = y[:, part * HY_WIDTH:(part + 1) * HY_WIDTH]


def seq_conv(p, conv_w, b, n):
    tt = min(n, 1024)
    n_tiles = n // tt
    parts = HY_ORDER + 1
    w = parts * HY_WIDTH
    c0 = P_HY // w
    out_spec = pl.BlockSpec((tt, HY_WIDTH), lambda bi, t: (bi * n_tiles + t, 0))
    return pl.pallas_call(
        functools.partial(_seq_conv_kernel, n_tiles=n_tiles),
        grid=(b, n_tiles),
        in_specs=[pl.BlockSpec((8, w), lambda bi, t: (jnp.maximum((bi * n_tiles + t) * (tt // 8) - 1, 0), c0)),
                  pl.BlockSpec((tt, w), lambda bi, t: (bi * n_tiles + t, c0)),
                  pl.BlockSpec((8, w), lambda bi, t: (jnp.minimum((bi * n_tiles + t + 1) * (tt // 8),
                                                                  b * n // 8 - 1), c0)),
                  pl.BlockSpec((SHORT_CONV, w), lambda bi, t: (0, 0))],
        out_specs=[out_spec] * parts,
        out_shape=[jax.ShapeDtypeStruct((b * n, HY_WIDTH), F32)] * parts,
        compiler_params=_cparams("parallel", "parallel"),
        name="hy_seq_conv",
    )(p, p, p, conv_w)


def _hdot(a, b):
    return jnp.dot(a, b, preferred_element_type=F32, precision=lax.Precision.HIGHEST)


def _hy_filter_kernel(ft_ref, t_ref, w1t_ref, b1_ref, f1_ref, w2t_ref, b2_ref, f2_ref, w3_ref, dl_ref, k_ref, s_ref,
                      *, n, tr):
    i = pl.program_id(0)
    hid = jnp.sin(f1_ref[...] * (_hdot(w1t_ref[...], ft_ref[...]) + b1_ref[...]))
    hid = jnp.sin(f2_ref[...] * (_hdot(w2t_ref[...], hid) + b2_ref[...]))
    filt = _hdot(hid.T, w3_ref[...]) * jnp.exp(-t_ref[...] * dl_ref[...])
    row = i * tr + lax.broadcasted_iota(jnp.int32, (tr, 1), 0)
    filt = jnp.where(row == n, 0.0, filt)
    k_ref[...] = filt

    @pl.when(i == 0)
    def _():
        s_ref[...] = jnp.zeros_like(s_ref)

    s_ref[...] += jnp.sum(jnp.abs(filt), axis=0, keepdims=True)


def hy_filter(n, w1, b1, freq1, w2, b2, freq2, w3):
    pos = jnp.arange(n, dtype=F32)
    t = pos / max(n - 1, 1)
    bands = jnp.linspace(1e-4, HY_BANDS - 1, HY_BANDS, dtype=F32)
    ang = (2.0 * math.pi / n) * pos[:, None] * bands[None, :]
    feats = jnp.concatenate([t[:, None], jnp.cos(ang), -jnp.sin(ang)], axis=-1)
    feats2 = jnp.concatenate([feats, feats[:1], feats[:0:-1]], axis=0)
    kpad = 128
    feats_t = jnp.pad(feats2, ((0, 0), (0, kpad - HY_EMB_DIM))).T
    w1t = jnp.pad(w1, ((0, kpad - HY_EMB_DIM), (0, 0))).T
    min_decay = math.log(HY_DECAY_TARGET) / HY_SLOW_DECAY_PCT
    max_decay = math.log(HY_DECAY_TARGET) / HY_FAST_DECAY_PCT
    cw = HY_ORDER * HY_WIDTH
    deltas = jnp.abs(jnp.linspace(min_decay, max_decay, cw, dtype=F32)).reshape(1, cw)
    tr = min(n, 1024)
    half = n // tr
    hd = HY_FILTER_HIDDEN
    vec = lambda v: v.reshape(hd, 1)
    full = lambda shp: pl.BlockSpec(shp, lambda i: (0, 0))
    return pl.pallas_call(
        functools.partial(_hy_filter_kernel, n=n, tr=tr),
        grid=(2 * half,),
        in_specs=[pl.BlockSpec((kpad, tr), lambda i: (0, i)), pl.BlockSpec((tr, 1), lambda i: (i, 0)),
                  full((hd, kpad)), full((hd, 1)), full((hd, 1)), full((hd, hd)), full((hd, 1)), full((hd, 1)),
                  pl.BlockSpec((hd, cw), lambda i: (0, i // half)), full((1, cw))],
        out_specs=[pl.BlockSpec((tr, cw), lambda i: (i, 0)), full((1, cw))],
        out_shape=[jax.ShapeDtypeStruct((2 * n, cw), F32), jax.ShapeDtypeStruct((1, cw), F32)],
        compiler_params=_cparams("arbitrary"),
        name="hy_filter",
    )(feats_t, feats2[:, 0:1], w1t, vec(b1), vec(freq1), w2.T, vec(b2), vec(freq2), w3, deltas)


def _hy_spec_kernel(a_ref, tw_ref, m2_ref, s_ref, o_ref, *, kb):
    inv = 1.0 / s_ref[...]
    ts = [_twiddle_mul(a_ref[0, j], a_ref[1, j], tw_ref[0, :, j:j + 1], tw_ref[1, :, j:j + 1], False)
          for j in range(kb)]
    xs = [_cplx_apply(m2_ref, tr, ti) for tr, ti in ts]
    for j, (xr, xi) in enumerate(xs):
        o_ref[0, j] = xr * inv
        o_ref[1, j] = xi * inv


def hy_spec(a, tw, m2, abs_sum, kb):
    _, l1, l2, cw = a.shape
    w = HY_WIDTH
    return pl.pallas_call(
        functools.partial(_hy_spec_kernel, kb=kb),
        grid=(l1 // kb, cw // w),
        in_specs=[pl.BlockSpec((2, kb, l2, w), lambda k, c: (0, k, 0, c)),
                  pl.BlockSpec((None, 2, l2, kb), lambda k, c: (k, 0, 0, 0)),
                  pl.BlockSpec(m2.shape, lambda k, c: (0, 0)),
                  pl.BlockSpec((1, w), lambda k, c: (0, c))],
        out_specs=pl.BlockSpec((2, kb, l2, w), lambda k, c: (0, k, 0, c)),
        out_shape=jax.ShapeDtypeStruct((2, l1, l2, cw), F32),
        compiler_params=_cparams("parallel", "parallel"),
        name="hy_spec",
    )(a, tw, m2, abs_sum)


def _hy_mid_kernel(a_ref, tw_ref, m2_ref, m2c_ref, kf_ref, o_ref, *, kb):
    cs = [(tw_ref[0, :, j:j + 1], tw_ref[1, :, j:j + 1]) for j in range(kb)]
    ts = [_twiddle_mul(a_ref[0, j], a_ref[1, j], c, s, False) for j, (c, s) in enumerate(cs)]
    xs = [_cplx_apply(m2_ref, tr, ti) for tr, ti in ts]
    ps = [(xr * kf_ref[0, j] - xi * kf_ref[1, j], xr * kf_ref[1, j] + xi * kf_ref[0, j])
          for j, (xr, xi) in enumerate(xs)]
    bs = [_cplx_apply(m2c_ref, pr, pi) for pr, pi in ps]
    for j, ((br, bi), (c, s)) in enumerate(zip(bs, cs)):
        o_ref[0, j], o_ref[1, j] = _twiddle_mul(br, bi, c, s, True)


def hy_mid(a, tw, m2, m2c, kf, order, kb):
    bsz, _, l1, l2, w = a.shape
    return pl.pallas_call(
        functools.partial(_hy_mid_kernel, kb=kb),
        grid=(bsz, l1 // kb),
        in_specs=[pl.BlockSpec((None, 2, kb, l2, w), lambda b, k: (b, 0, k, 0, 0)),
                  pl.BlockSpec((None, 2, l2, kb), lambda b, k: (k, 0, 0, 0)),
                  pl.BlockSpec(m2.shape, lambda b, k: (0, 0)),
                  pl.BlockSpec(m2c.shape, lambda b, k: (0, 0)),
                  pl.BlockSpec((2, kb, l2, w), lambda b, k: (0, k, 0, order))],
        out_specs=pl.BlockSpec((None, 2, kb, l2, w), lambda b, k: (b, 0, k, 0, 0)),
        out_shape=jax.ShapeDtypeStruct((bsz, 2, l1, l2, w), F32),
        compiler_params=_cparams("parallel", "parallel"),
        name="hy_mid",
    )(a, tw, m2, m2c, kf)


def _hy_out_kernel(m_ref, bp_ref, xo_ref, z_ref, bias_ref, o_ref):
    y = jnp.dot(m_ref[...], bp_ref[...].astype(BF16), preferred_element_type=F32)
    o_ref[...] = xo_ref[...] * (y + bias_ref[...] * z_ref[...])


def hy_out(m, bp, xo, z, bias_row, tn):
    bsz, k, n = bp.shape
    r = m.shape[0]
    blk = pl.BlockSpec((None, r, tn), lambda b, j: (b, 0, j))
    return pl.pallas_call(
        _hy_out_kernel,
        grid=(bsz, n // tn),
        in_specs=[pl.BlockSpec((r, k), lambda b, j: (0, 0)),
                  pl.BlockSpec((None, k, tn), lambda b, j: (b, 0, j)), blk, blk,
                  pl.BlockSpec((1, tn), lambda b, j: (0, j))],
        out_specs=blk,
        out_shape=jax.ShapeDtypeStruct((bsz, r, n), F32),
        compiler_params=_cparams("parallel", "parallel"),
        name="hy_out",
    )(m, bp, xo, z, bias_row)


SUB = 8
HY_KB = 16


def _half_spectrum_blocks(l1, kb):
    return -(-(l1 // 2 + 1) // kb)


def _kron_stage1(l1, n1_used, kb):
    c, s = _cos_sin(l1, n1_used, l1)
    m = np.stack([c, -s], axis=0).reshape(2, l1 // kb, kb, n1_used)
    m = np.transpose(m, (1, 0, 2, 3)).reshape(l1 // kb, 2 * kb, n1_used)[:_half_spectrum_blocks(l1, kb)]
    return _const_bf16(np.stack([np.kron(blk, np.eye(SUB)) for blk in m]))


def _kron_stage_out(l1, n1_used, kb):
    c, s = _cos_sin(n1_used, l1, l1)
    k1 = np.arange(l1)
    mult = np.where((k1 == 0) | (k1 == l1 // 2), 1.0, np.where(k1 < l1 // 2, 2.0, 0.0))
    m = np.stack([c * mult, -s * mult], axis=1).reshape(n1_used, 2, l1 // kb, kb) / (l1 * l1)
    m = np.transpose(m, (2, 0, 1, 3)).reshape(l1 // kb, n1_used, 2 * kb)[:_half_spectrum_blocks(l1, kb)]
    return _const_bf16(np.stack([np.kron(blk, np.eye(SUB)) for blk in m]))


def _strided_stage_in(kin_ref, src_ref, a_ref):
    n_sub, width = src_ref.shape[1] // SUB, src_ref.shape[2]
    rows = src_ref.shape[0] * SUB
    for j in range(n_sub):
        r = jnp.dot(kin_ref[...], src_ref[:, j * SUB:(j + 1) * SUB, :].reshape(rows, width).astype(BF16),
                    preferred_element_type=F32)
        a_ref[:, j * SUB:(j + 1) * SUB, :] = r.reshape(a_ref.shape[0], SUB, width)


def _hy_conv_kernel(z_ref, xo_ref, kin_ref, tw_ref, m2_ref, m2c_ref, kf_ref, kout_ref, bias_ref, o_ref,
                    a_ref, b_ref, *, kb):
    k = pl.program_id(1)

    @pl.when(k == 0)
    def _():
        o_ref[...] = jnp.zeros_like(o_ref)

    _strided_stage_in(kin_ref, z_ref, a_ref)
    grp = kb
    for g0 in range(0, kb, grp):
        js = range(g0, g0 + grp)
        cs = [(tw_ref[0, :, j:j + 1], tw_ref[1, :, j:j + 1]) for j in js]
        ts = [_twiddle_mul(a_ref[j], a_ref[kb + j], c, s, False) for j, (c, s) in zip(js, cs)]
        xs = [_cplx_apply(m2_ref, tr, ti) for tr, ti in ts]
        ps = [(xr * kf_ref[0, j] - xi * kf_ref[1, j], xr * kf_ref[1, j] + xi * kf_ref[0, j])
              for j, (xr, xi) in zip(js, xs)]
        bs = [_cplx_apply(m2c_ref, pr, pi) for pr, pi in ps]
        for j, (br, bi), (c, s) in zip(js, bs, cs):
            b_ref[j], b_ref[kb + j] = _twiddle_mul(br, bi, c, s, True)
    n_sub, width = o_ref.shape[1] // SUB, o_ref.shape[2]
    for j in range(n_sub):
        r = jnp.dot(kout_ref[...], b_ref[:, j * SUB:(j + 1) * SUB, :].reshape(2 * kb * SUB, width).astype(BF16),
                    preferred_element_type=F32)
        o_ref[:, j * SUB:(j + 1) * SUB, :] += r.reshape(o_ref.shape[0], SUB, width)

    @pl.when(k == pl.num_programs(1) - 1)
    def _():
        o_ref[...] = xo_ref[...] * (o_ref[...] + bias_ref[...] * z_ref[...])


def hy_conv_long(z, xo, kf, bias, order, b, n):
    w, kb = HY_WIDTH, HY_KB
    l1 = l2 = 128
    n1 = n // l2
    nblk = _half_spectrum_blocks(l1, kb)
    tw = _twiddles(l1, l2, kb)[:nblk]
    m2, m2c = _stage2_matrices(l2)
    kin, kout = _kron_stage1(l1, n1, kb), _kron_stage_out(l1, n1, kb)
    tok = pl.BlockSpec((None, n1, l2, w), lambda bi, k: (bi, 0, 0, 0), pipeline_mode=pl.Buffered(1))
    const = lambda a: pl.BlockSpec(a.shape, lambda bi, k: (0,) * a.ndim)
    out = pl.pallas_call(
        functools.partial(_hy_conv_kernel, kb=kb),
        grid=(b, nblk),
        in_specs=[tok, tok,
                  pl.BlockSpec((None,) + kin.shape[1:], lambda bi, k: (k, 0, 0)),
                  pl.BlockSpec((None, 2, l2, kb), lambda bi, k: (k, 0, 0, 0)),
                  const(m2), const(m2c),
                  pl.BlockSpec((2, kb, l2, w), lambda bi, k: (0, k, 0, order)),
                  pl.BlockSpec((None,) + kout.shape[1:], lambda bi, k: (k, 0, 0)),
                  pl.BlockSpec((1, w), lambda bi, k: (0, 0))],
        out_specs=pl.BlockSpec((None, n1, l2, w), lambda bi, k: (bi, 0, 0, 0), pipeline_mode=pl.Buffered(1)),
        out_shape=jax.ShapeDtypeStruct((b, n1, l2, w), F32),
        scratch_shapes=[pltpu.VMEM((2 * kb, l2, w), F32), pltpu.VMEM((2 * kb, l2, w), F32)],
        compiler_params=_cparams("parallel", "arbitrary"),
        name="hy_conv_long",
    )(z.reshape(b, n1, l2, w), xo.reshape(b, n1, l2, w), kin, tw, m2, m2c, kf, kout, bias.reshape(1, w))
    return out.reshape(b * n, w)


def _hy_spec_long_kernel(kern_ref, kin_ref, tw_ref, m2_ref, s_ref, o_ref, a_ref, *, kb):
    _strided_stage_in(kin_ref, kern_ref, a_ref)
    inv = 1.0 / s_ref[...]
    ts = [_twiddle_mul(a_ref[j], a_ref[kb + j], tw_ref[0, :, j:j + 1], tw_ref[1, :, j:j + 1], False)
          for j in range(kb)]
    xs = [_cplx_apply(m2_ref, tr, ti) for tr, ti in ts]
    for j, (xr, xi) in enumerate(xs):
        o_ref[0, j] = xr * inv
        o_ref[1, j] = xi * inv


def hy_spec_long(kern, abs_sum):
    w, kb = HY_WIDTH, HY_KB
    l1 = l2 = 128
    cw = kern.shape[1]
    nblk = _half_spectrum_blocks(l1, kb)
    tw = _twiddles(l1, l2, kb)[:nblk]
    m2, _ = _stage2_matrices(l2)
    kin = _kron_stage1(l1, l1, kb)
    return pl.pallas_call(
        functools.partial(_hy_spec_long_kernel, kb=kb),
        grid=(cw // w, nblk),
        in_specs=[pl.BlockSpec((l1, l2, w), lambda c, k: (0, 0, c), pipeline_mode=pl.Buffered(1)),
                  pl.BlockSpec((None,) + kin.shape[1:], lambda c, k: (k, 0, 0)),
                  pl.BlockSpec((None, 2, l2, kb), lambda c, k: (k, 0, 0, 0)),
                  pl.BlockSpec(m2.shape, lambda c, k: (0, 0)),
                  pl.BlockSpec((1, w), lambda c, k: (0, c))],
        out_specs=pl.BlockSpec((2, kb, l2, w), lambda c, k: (0, k, 0, c)),
        out_shape=jax.ShapeDtypeStruct((2, nblk * kb, l2, cw), F32),
        scratch_shapes=[pltpu.VMEM((2 * kb, l2, w), F32)],
        compiler_params=_cparams("parallel", "arbitrary"),
        name="hy_spec_long",
    )(kern.reshape(l1, l2, cw), kin, tw, m2, abs_sum)


def hyena_pallas(p, conv_w, w1, b1, freq1, w2, b2, freq2, w3, bias, b, n):
    w = HY_WIDTH
    x0, x1, v = seq_conv(p, conv_w, b, n)
    kern, abs_sum = hy_filter(n, w1, b1, freq1, w2, b2, freq2, w3)
    z = v
    if n == 8192:
        kf = hy_spec_long(kern, abs_sum)
        for order, xo in enumerate((x0, x1)):
            z = hy_conv_long(z, xo, kf, bias[order], order, b, n)
        return z
    l2 = 2 * n
    tw = _twiddles(1, l2, 1)
    m2, m2c = _stage2_matrices(l2)
    ak = jnp.stack([kern, jnp.zeros_like(kern)], axis=0).reshape(2, 1, l2, HY_ORDER * w)
    m_out = _const_bf16(np.eye(n, 2 * l2) / l2)
    kf = hy_spec(ak, tw, m2, abs_sum, 1)
    for order, xo in enumerate((x0, x1)):
        zp = jnp.pad(z.reshape(b, 1, 1, n, w), ((0, 0), (0, 0), (0, 0), (0, n), (0, 0)))
        a = jnp.concatenate([zp, jnp.zeros_like(zp)], axis=1)
        bp = hy_mid(a, tw, m2, m2c, kf, order, 1)
        z = hy_out(m_out, bp.reshape(b, 2 * l2, w), xo.reshape(b, n, w), z.reshape(b, n, w),
                   bias[order].reshape(1, w), w).reshape(b * n, w)
    return z


def _prep_w_in(w):
    main = jnp.concatenate([w[:, OFF_Q:OFF_Z], w[:, OFF_HY:OFF_GATE], w[:, OFF_FN:OFF_HY], w[:, OFF_Z:OFF_BETA]],
                           axis=1).astype(BF16)
    small = jnp.pad(w[:, OFF_BETA:OFF_FN], ((0, 0), (0, SMALL_WIDTH - 4 * DN_HEADS))).astype(BF16)
    return main, small, w[:, OFF_GATE:IN_WIDTH].astype(BF16)


def kernel(x, c, ctx, c_ctx, w_mod, b_mod, norm1, norm2, w_in, dn_conv, dn_a_log, dn_dt_bias,
           dn_out_norm, hy_conv, hy_w1, hy_b1, hy_freq1, hy_w2, hy_b2, hy_freq2, hy_w3, hy_bias,
           w_branch_a, w_branch_b, w_branch_c, w_out, w_ff1, w_ff2, final_norm):
    b, n_lat, d = x.shape
    rows = n_lat // GRID_W
    n_ctx = ctx.shape[1]
    tm_x, tm_c = 1024, n_ctx

    c_rows = jnp.concatenate([c, c_ctx[None], jnp.zeros((8 - b - 1, d), F32)], axis=0)
    mods = mod_vectors(c_rows, w_mod, b_mod)
    s_zero = jnp.zeros((2, b, DN_HEADS, DN_HEAD_DIM, DN_HEAD_DIM), F32)
    h, hc = x.reshape(b * n_lat, d), ctx.reshape(b * n_ctx, d)

    for l in range(DEPTH):
        last = l == DEPTH - 1
        mv = mods[l].reshape(8, N_MOD, 1, d)
        mx = [mv[:b, i] for i in range(N_MOD)]
        mc = [mv[b:b + 1, i] for i in range(N_MOD)]
        w_main, w_small, w_gate = _prep_w_in(w_in[l])
        wa, wb, wc, wo = (w.astype(BF16) for w in (w_branch_a[l], w_branch_b[l], w_branch_c[l], w_out[l]))
        w1, w2 = w_ff1[l].astype(BF16), w_ff2[l].astype(BF16)
        n1, n2 = norm1[l][None, None, :], norm2[l][None, None, :]

        p_c, small_c = in_proj(hc, n1 * (1.0 + mc[1]), mc[0], w_main, w_small, b * n_ctx, tm_c)
        p_x, small_x = in_proj(h, n1 * (1.0 + mx[1]), mx[0], w_main, w_small, n_lat, tm_x)

        def mix(p, n):
            y_c = hyena_pallas(p, hy_conv[l], hy_w1[l], hy_b1[l], hy_freq1[l], hy_w2[l], hy_b2[l], hy_freq2[l],
                               hy_w3[l], hy_bias[l], b, n)
            return fnet_branch(p, b, n), y_c

        ocf, ocb, s_ctx = delta_branch(p_c, small_c, dn_conv[l], dn_a_log[l], dn_dt_bias[l], s_zero,
                                       b, n_ctx, 1, n_ctx)
        oxf, oxb, _ = delta_branch(p_x, small_x, dn_conv[l], dn_a_log[l], dn_dt_bias[l], s_ctx,
                                   b, n_lat, rows, GRID_W)

        y_b, y_c = mix(p_x, n_lat)
        h = merge(oxf, oxb, p_x, y_b, y_c, h, n1 * (1.0 + mx[1]), mx[0], mx[2], dn_out_norm[l],
                  w_gate, wa, wb, wc, wo, n_lat, 512)
        h = mlp(h, n2 * (1.0 + mx[4]), mx[3], mx[5], w1, w2, n_lat, tm_x, final_norm if last else None)

        if not last:
            y_b, y_c = mix(p_c, n_ctx)
            hc = merge(ocf, ocb, p_c, y_b, y_c, hc, n1 * (1.0 + mc[1]), mc[0], mc[2], dn_out_norm[l],
                       w_gate, wa, wb, wc, wo, b * n_ctx, tm_c)
            hc = mlp(hc, n2 * (1.0 + mc[4]), mc[3], mc[5], w1, w2, b * n_ctx, tm_c)

    return h.reshape(b, n_lat, d)
```

```python
import functools
import math

import jax
import jax.numpy as jnp
import numpy as np
from jax import lax
from jax.experimental import pallas as pl
from jax.experimental.pallas import tpu as pltpu

D_MODEL = 1024
DEPTH = 2
GRID_W = 64
NORM_EPS = 1e-6
N_MOD = 6

DN_HEADS = 4
DN_HEAD_DIM = 128
DN_WIDTH = DN_HEADS * DN_HEAD_DIM
DN_CHUNK = 64
SHORT_CONV = 3

FN_GROUPS = 4
FN_GROUP_DIM = 64
FN_WIDTH = FN_GROUPS * FN_GROUP_DIM

HY_WIDTH = 256
HY_ORDER = 2
HY_EMB_DIM = 33
HY_BANDS = (HY_EMB_DIM - 1) // 2
HY_FILTER_HIDDEN = 64
HY_FAST_DECAY_PCT = 0.3
HY_SLOW_DECAY_PCT = 1.5
HY_DECAY_TARGET = 1e-2

N_BRANCHES = 3
D_FF = 4 * D_MODEL

OFF_Q = 0
OFF_Z = 3 * DN_WIDTH
OFF_BETA = OFF_Z + DN_WIDTH
OFF_A = OFF_BETA + 2 * DN_HEADS
OFF_FN = OFF_A + 2 * DN_HEADS
OFF_HY = OFF_FN + FN_WIDTH
OFF_GATE = OFF_HY + (HY_ORDER + 1) * HY_WIDTH
IN_WIDTH = OFF_GATE + N_BRANCHES * D_MODEL

P_QKV = 0
P_HY = P_QKV + 3 * DN_WIDTH
P_FN = P_HY + (HY_ORDER + 1) * HY_WIDTH
P_Z = P_FN + FN_WIDTH
P_WIDTH = P_Z + DN_WIDTH
SMALL_WIDTH = 128

F32 = jnp.float32
BF16 = jnp.bfloat16
VMEM_LIMIT = 56 * 1024 * 1024


def _cparams(*sem):
    return pltpu.CompilerParams(dimension_semantics=sem, vmem_limit_bytes=VMEM_LIMIT)


def _bdot(a, b):
    return jnp.dot(a.astype(BF16), b.astype(BF16), preferred_element_type=F32)


def _sigmoid(x):
    return 0.5 * jnp.tanh(0.5 * x) + 0.5


def _modnorm(xf, gs, sh):
    r = lax.rsqrt(jnp.mean(xf * xf, axis=-1, keepdims=True) + NORM_EPS)
    return xf * r * gs + sh


def _mod_kernel(c_ref, w_ref, b_ref, o_ref):
    c = c_ref[...]
    o_ref[...] = _bdot(c * _sigmoid(c), w_ref[...]) + b_ref[...]


def mod_vectors(c_rows, w_mod, b_mod):
    tn = 1536
    n = N_MOD * D_MODEL
    return pl.pallas_call(
        _mod_kernel,
        grid=(DEPTH, n // tn),
        in_specs=[pl.BlockSpec((8, D_MODEL), lambda l, j: (0, 0)),
                  pl.BlockSpec((None, D_MODEL, tn), lambda l, j: (l, 0, j)),
                  pl.BlockSpec((None, 1, tn), lambda l, j: (l, 0, j))],
        out_specs=pl.BlockSpec((None, 8, tn), lambda l, j: (l, 0, j)),
        out_shape=jax.ShapeDtypeStruct((DEPTH, 8, n), F32),
        compiler_params=_cparams("parallel", "parallel"),
        name="mod_vectors",
    )(c_rows, w_mod, b_mod.reshape(DEPTH, 1, n))


def _in_proj_kernel(x_ref, gs_ref, sh_ref, w_ref, ws_ref, p_ref, small_ref, *, tn):
    xn = _modnorm(x_ref[...], gs_ref[0], sh_ref[0]).astype(BF16)
    small_ref[...] = jnp.dot(xn, ws_ref[...], preferred_element_type=F32)
    for j in range(p_ref.shape[1] // tn):
        p_ref[:, j * tn:(j + 1) * tn] = jnp.dot(xn, w_ref[:, j * tn:(j + 1) * tn], preferred_element_type=F32)


def in_proj(x2d, gs, sh, w_main, w_small, rows_per_mod, tm):
    m = x2d.shape[0]
    tiles_per_mod = rows_per_mod // tm
    mod_spec = pl.BlockSpec((1, 1, D_MODEL), lambda i: (i // tiles_per_mod, 0, 0))
    resident = lambda a: pl.BlockSpec(a.shape, lambda i: (0, 0), pipeline_mode=pl.Buffered(1))
    return pl.pallas_call(
        functools.partial(_in_proj_kernel, tn=1024),
        grid=(m // tm,),
        in_specs=[pl.BlockSpec((tm, D_MODEL), lambda i: (i, 0)), mod_spec, mod_spec,
                  resident(w_main), resident(w_small)],
        out_specs=[pl.BlockSpec((tm, P_WIDTH), lambda i: (i, 0)),
                   pl.BlockSpec((tm, SMALL_WIDTH), lambda i: (i, 0))],
        out_shape=[jax.ShapeDtypeStruct((m, P_WIDTH), F32), jax.ShapeDtypeStruct((m, SMALL_WIDTH), F32)],
        compiler_params=_cparams("parallel"),
        name="in_proj",
    )(x2d, gs, sh, w_main, w_small)


def _merge_kernel(of_ref, ob_ref, z_ref, yb_ref, yc_ref, h_ref, gs_ref, sh_ref, gate_ref, nrm_ref,
                  wg_ref, wa_ref, wb_ref, wc_ref, wo_ref, out_ref):
    xn = _modnorm(h_ref[...], gs_ref[0], sh_ref[0]).astype(BF16)
    o = of_ref[...].astype(F32) + ob_ref[...].astype(F32)
    z = z_ref[...]
    heads = []
    for hd in range(DN_HEADS):
        sl = slice(hd * DN_HEAD_DIM, (hd + 1) * DN_HEAD_DIM)
        oh, zh = o[:, sl], z[:, sl]
        r = lax.rsqrt(jnp.mean(oh * oh, axis=-1, keepdims=True) + NORM_EPS)
        heads.append(oh * r * nrm_ref[...] * (zh * _sigmoid(zh)))
    ya = jnp.concatenate(heads, axis=-1)
    merged = None
    for i, (y, w_ref) in enumerate(((ya, wa_ref), (yb_ref[...], wb_ref), (yc_ref[...], wc_ref))):
        g = jnp.dot(xn, wg_ref[:, i * D_MODEL:(i + 1) * D_MODEL], preferred_element_type=F32)
        term = _sigmoid(g) * _bdot(y, w_ref[...])
        merged = term if merged is None else merged + term
    out_ref[...] = h_ref[...] + gate_ref[0] * _bdot(merged, wo_ref[...])


def merge(o_f, o_b, p, y_b, y_c, h2d, gs, sh, gate, dn_out_norm, wg, wa, wb, wc, wo, rows_per_mod, tm):
    m = h2d.shape[0]
    tiles_per_mod = rows_per_mod // tm
    row = lambda w: pl.BlockSpec((tm, w), lambda i: (i, 0))
    full = lambda a: pl.BlockSpec(a.shape, lambda i: (0,) * a.ndim, pipeline_mode=pl.Buffered(1))
    mod_spec = pl.BlockSpec((1, 1, D_MODEL), lambda i: (i // tiles_per_mod, 0, 0))
    nrm = dn_out_norm.reshape(1, DN_HEAD_DIM)
    return pl.pallas_call(
        _merge_kernel,
        grid=(m // tm,),
        in_specs=[row(DN_WIDTH), row(DN_WIDTH),
                  pl.BlockSpec((tm, DN_WIDTH), lambda i: (i, P_Z // DN_WIDTH)),
                  row(FN_WIDTH), row(HY_WIDTH), row(D_MODEL), mod_spec, mod_spec, mod_spec,
                  full(nrm), full(wg), full(wa), full(wb), full(wc), full(wo)],
        out_specs=row(D_MODEL),
        out_shape=jax.ShapeDtypeStruct((m, D_MODEL), F32),
        compiler_params=_cparams("parallel"),
        name="merge",
    )(o_f, o_b, p, y_b, y_c, h2d, gs, sh, gate, nrm, wg, wa, wb, wc, wo)


def _mlp_kernel(h_ref, gs_ref, sh_ref, gate_ref, w1_ref, w2_ref, fin_ref, out_ref, xn_ref, acc_ref, *, final):
    j = pl.program_id(1)

    @pl.when(j == 0)
    def _():
        xn_ref[...] = _modnorm(h_ref[...], gs_ref[0], sh_ref[0]).astype(BF16)
        acc_ref[...] = jnp.zeros_like(acc_ref)

    a = jnp.maximum(jnp.dot(xn_ref[...], w1_ref[...], preferred_element_type=F32), 0.0)
    acc_ref[...] += jnp.dot((a * a).astype(BF16), w2_ref[...], preferred_element_type=F32)

    @pl.when(j == pl.num_programs(1) - 1)
    def _():
        y = h_ref[...] + gate_ref[0] * acc_ref[...]
        if final:
            y = y * lax.rsqrt(jnp.mean(y * y, axis=-1, keepdims=True) + NORM_EPS) * fin_ref[...]
        out_ref[...] = y


def mlp(h2d, gs, sh, gate, w1, w2, rows_per_mod, tm, final_gain=None):
    m = h2d.shape[0]
    tf = 1024
    tiles_per_mod = rows_per_mod // tm
    mod_spec = pl.BlockSpec((1, 1, D_MODEL), lambda i, j: (i // tiles_per_mod, 0, 0))
    final = final_gain is not None
    fin = (final_gain if final else jnp.ones((D_MODEL,), F32)).reshape(1, D_MODEL)
    return pl.pallas_call(
        functools.partial(_mlp_kernel, final=final),
        grid=(m // tm, D_FF // tf),
        in_specs=[pl.BlockSpec((tm, D_MODEL), lambda i, j: (i, 0)), mod_spec, mod_spec, mod_spec,
                  pl.BlockSpec((D_MODEL, tf), lambda i, j: (0, j)),
                  pl.BlockSpec((tf, D_MODEL), lambda i, j: (j, 0)),
                  pl.BlockSpec((1, D_MODEL), lambda i, j: (0, 0))],
        out_specs=pl.BlockSpec((tm, D_MODEL), lambda i, j: (i, 0)),
        out_shape=jax.ShapeDtypeStruct((m, D_MODEL), F32),
        scratch_shapes=[pltpu.VMEM((tm, D_MODEL), BF16), pltpu.VMEM((tm, D_MODEL), F32)],
        compiler_params=_cparams("parallel", "arbitrary"),
        name="mlp",
    )(h2d, gs, sh, gate, w1, w2, fin)


def _dnconv_kernel(prev_ref, cur_ref, next_ref, w_ref, o_ref, *, cols, n_tiles):
    t, j = pl.program_id(1), pl.program_id(2)
    tt = cur_ref.shape[0]
    prev = jnp.where(t == 0, 0.0, prev_ref[...])
    nxt = jnp.where(t == n_tiles - 1, 0.0, next_ref[...])
    ext = jnp.concatenate([prev, cur_ref[...], nxt], axis=0)
    n_ext = tt + 2 * cols
    col = lax.broadcasted_iota(jnp.int32, (n_ext, 1), 0) % cols
    left = jnp.where(col == 0, 0.0, pltpu.roll(ext, 1, axis=0))
    right = jnp.where(col == cols - 1, 0.0, pltpu.roll(ext, n_ext - 1, axis=0))
    acc = jnp.zeros((tt, DN_WIDTH), F32)
    for dr in range(SHORT_CONV):
        base = dr * cols
        acc = (acc + w_ref[3 * dr:3 * dr + 1, :] * left[base:base + tt]
               + w_ref[3 * dr + 1:3 * dr + 2, :] * ext[base:base + tt]
               + w_ref[3 * dr + 2:3 * dr + 3, :] * right[base:base + tt])
    y = acc * _sigmoid(acc)
    q_scale = jnp.where(j == 0, DN_HEAD_DIM ** -0.5, 1.0)
    for hd in range(DN_HEADS):
        sl = slice(hd * DN_HEAD_DIM, (hd + 1) * DN_HEAD_DIM)
        yh = y[:, sl]
        nrm = lax.rsqrt(jnp.sum(yh * yh, axis=-1, keepdims=True) + NORM_EPS) * q_scale
        o_ref[:, sl] = yh * jnp.where(j < 2, nrm, 1.0)


def dn_conv_prep(p, conv_w, b, n, rows, cols):
    tr = min(rows, 32)
    tt = tr * cols
    n_tiles = rows // tr
    nblk = b * n // cols
    c0 = P_QKV // DN_WIDTH
    return pl.pallas_call(
        functools.partial(_dnconv_kernel, cols=cols, n_tiles=n_tiles),
        grid=(b, n_tiles, 3),
        in_specs=[pl.BlockSpec((cols, DN_WIDTH),
                               lambda bi, t, j: (jnp.maximum(bi * rows + t * tr - 1, 0), c0 + j)),
                  pl.BlockSpec((tt, DN_WIDTH), lambda bi, t, j: (bi * n_tiles + t, c0 + j)),
                  pl.BlockSpec((cols, DN_WIDTH),
                               lambda bi, t, j: (jnp.minimum(bi * rows + (t + 1) * tr, nblk - 1), c0 + j)),
                  pl.BlockSpec((SHORT_CONV * SHORT_CONV, DN_WIDTH), lambda bi, t, j: (0, j))],
        out_specs=pl.BlockSpec((tt, DN_WIDTH), lambda bi, t, j: (bi * n_tiles + t, j)),
        out_shape=jax.ShapeDtypeStruct((b * n, 3 * DN_WIDTH), F32),
        compiler_params=_cparams("parallel", "parallel", "parallel"),
        name="dn_conv_prep",
    )(p, p, p, conv_w.reshape(SHORT_CONV * SHORT_CONV, 3 * DN_WIDTH))


PREP_CHUNKS = 8
PREP_GROUP = 4


def _softplus(x):
    return jnp.maximum(x, 0.0) + jnp.log(1.0 + jnp.exp(-jnp.abs(x)))


def _delta_prep_kernel(qkv_ref, sm_ref, smt_ref, prm_ref, prmt_ref, u0_ref, lhs1_ref, lhs2_ref, gl_ref, *, n_chunks):
    cc = DN_CHUNK
    nh = DN_HEADS
    sm = sm_ref[...]
    beta_all = _sigmoid(sm)
    g_all = -jnp.exp(prm_ref[0:1, :]) * _softplus(sm + prm_ref[1:2, :])
    gt_all = -jnp.exp(prmt_ref[:, 0:1]) * _softplus(smt_ref[...] + prmt_ref[:, 1:2])
    lt = 2 * cc
    lane = lax.broadcasted_iota(jnp.int32, (gt_all.shape[0], lt), 1) % cc
    gt_tiles = []
    for t in range(gt_all.shape[1] // lt):
        gt_f = gt_b = gt_all[:, t * lt:(t + 1) * lt]
        s = 1
        while s < cc:
            gt_f = gt_f + jnp.where(lane >= s, pltpu.roll(gt_f, s, axis=1), 0.0)
            gt_b = gt_b + jnp.where(lane < cc - s, pltpu.roll(gt_b, lt - s, axis=1), 0.0)
            s *= 2
        gt_tiles.append((gt_f, gt_b))
    ri = lax.broadcasted_iota(jnp.int32, (cc, cc), 0)
    ci_ = lax.broadcasted_iota(jnp.int32, (cc, cc), 1)
    sub = lax.broadcasted_iota(jnp.int32, (cc, 1), 0)
    blk = lambda s: (ri // s) == (ci_ // s)
    leaf = 8

    heads = []
    for ci in range(n_chunks):
        rows = slice(ci * cc, (ci + 1) * cc)
        gc_f = g_all[rows]
        gc_b = gc_f
        s = 1
        while s < cc:
            gc_f = gc_f + jnp.where(sub >= s, pltpu.roll(gc_f, s, axis=0), 0.0)
            gc_b = gc_b + jnp.where(sub < cc - s, pltpu.roll(gc_b, cc - s, axis=0), 0.0)
            s *= 2
        for h in range(nh):
            q = qkv_ref[rows, h * 128:(h + 1) * 128]
            k = qkv_ref[rows, (nh + h) * 128:(nh + h + 1) * 128]
            v = qkv_ref[rows, (2 * nh + h) * 128:(2 * nh + h + 1) * 128]
            heads.append((ci, rows, h, q, k, v, gc_f, gc_b))
    qkks = [lax.dot_general(jnp.concatenate([q, k], axis=0).astype(BF16), k.astype(BF16),
                            (((1,), (1,)), ((), ())), preferred_element_type=F32)
            for (_, _, _, q, k, _, _, _) in heads]
    per_group = PREP_GROUP * nh
    for g0 in range(0, len(heads), per_group):
        chains = []
        for (ci, rows, h, q, k, v, gc_f, gc_b), qkk in zip(heads[g0:g0 + per_group], qkks[g0:g0 + per_group]):
            qk, kk = qkk[:cc], qkk[cc:]
            for d in range(2):
                cb_, ca_ = d * nh + h, 2 * nh + d * nh + h
                beta = jnp.broadcast_to(beta_all[rows, cb_:cb_ + 1], (cc, DN_HEAD_DIM))
                g_col = jnp.broadcast_to((gc_f if d == 0 else gc_b)[:, ca_:ca_ + 1], (cc, DN_HEAD_DIM))
                lo = (ci % 2) * cc
                g_row = gt_tiles[ci // 2][d][ca_:ca_ + 1, lo:lo + cc]
                incl = (ri >= ci_) if d == 0 else (ri <= ci_)
                strict = (ri > ci_) if d == 0 else (ri < ci_)
                decay = jnp.exp(jnp.where(incl, g_col[:, :cc] - g_row, -1e30))
                a = jnp.where(strict, beta[:, :cc] * kk * decay, 0.0)
                eg = jnp.exp(g_col)
                g_last = g_col[cc - 1:cc] if d == 0 else g_col[0:1]
                chains.append(dict(ci=ci, rows=rows, h=h, d=d, a=a, qkd=qk * decay, g_last=g_last,
                                   rhs=jnp.concatenate([v * beta, k * (beta * eg)], axis=1),
                                   q_d=q * eg, k_d=k * jnp.exp(g_last - g_col)))

        pws = [jnp.where(blk(leaf), c["a"], 0.0) for c in chains]
        devs = [-pw for pw in pws]
        for _ in range(2):
            pws = [_bdot(pw, pw) for pw in pws]
            cross = [_bdot(dev, pw) for dev, pw in zip(devs, pws)]
            devs = [dev + pw + x for dev, pw, x in zip(devs, pws, cross)]
        s = leaf
        while s < cc:
            offs = [jnp.where(blk(2 * s) & jnp.logical_not(blk(s)), c["a"], 0.0) for c in chains]
            xs = [off + _bdot(dev, off) for dev, off in zip(devs, offs)]
            devs = [dev - x - _bdot(x, dev) for dev, x in zip(devs, xs)]
            s *= 2
        uws = [c["rhs"] + _bdot(dev, c["rhs"]) for dev, c in zip(devs, chains)]
        for c, uw in zip(chains, uws):
            d, h, ci = c["d"], c["h"], c["ci"]
            u0_ref[d, h, c["rows"], :] = uw[:, :128]
            lhs1_ref[d, h, ci] = jnp.concatenate([uw[:, 128:], c["q_d"]], axis=0).astype(BF16)
            lhs2_ref[d, h, ci] = jnp.concatenate([c["qkd"], c["k_d"].T], axis=0).astype(BF16)
            gl_ref[d, h, ci] = jnp.broadcast_to(jnp.exp(c["g_last"]), (8, DN_HEAD_DIM))


def delta_prep(qkv, small, small_t, prm, prm_t, b, n):
    nc = n // DN_CHUNK
    cb = min(PREP_CHUNKS, nc)
    nblk = nc // cb
    tt = cb * DN_CHUNK
    nh = DN_HEADS
    return pl.pallas_call(
        functools.partial(_delta_prep_kernel, n_chunks=cb),
        grid=(b, nblk),
        in_specs=[pl.BlockSpec((tt, 3 * DN_WIDTH), lambda bi, c: (bi * nblk + c, 0)),
                  pl.BlockSpec((tt, SMALL_WIDTH), lambda bi, c: (bi * nblk + c, 0)),
                  pl.BlockSpec((4 * nh, tt), lambda bi, c: (0, bi * nblk + c)),
                  pl.BlockSpec((2, SMALL_WIDTH), lambda bi, c: (0, 0)),
                  pl.BlockSpec((4 * nh, 2), lambda bi, c: (0, 0))],
        out_specs=[pl.BlockSpec((2, None, nh, tt, 128), lambda bi, c: (0, bi, 0, c, 0)),
                   pl.BlockSpec((2, None, nh, cb, 128, 128), lambda bi, c: (0, bi, 0, c, 0, 0)),
                   pl.BlockSpec((2, None, nh, cb, 192, 64), lambda bi, c: (0, bi, 0, c, 0, 0)),
                   pl.BlockSpec((2, None, nh, cb, 8, 128), lambda bi, c: (0, bi, 0, c, 0, 0))],
        out_shape=[jax.ShapeDtypeStruct((2, b, nh, n, 128), F32),
                   jax.ShapeDtypeStruct((2, b, nh, nc, 128, 128), BF16),
                   jax.ShapeDtypeStruct((2, b, nh, nc, 192, 64), BF16),
                   jax.ShapeDtypeStruct((2, b, nh, nc, 8, 128), F32)],
        compiler_params=_cparams("parallel", "parallel"),
        name="delta_prep",
    )(qkv, small, small_t, prm, prm_t)


def _delta_scan_kernel(u0f_ref, u0b_ref, l1f_ref, l1b_ref, l2f_ref, l2b_ref, glf_ref, glb_ref, s0_ref,
                       of_ref, ob_ref, sout_ref, st_ref, *, sc, nb):
    n = pl.program_id(0)
    cc = DN_CHUNK

    @pl.when(n == 0)
    def _():
        st_ref[...] = s0_ref[...]

    dirs = ((u0f_ref, l1f_ref, l2f_ref, glf_ref, of_ref), (u0b_ref, l1b_ref, l2b_ref, glb_ref, ob_ref))

    def body(i, carry):
        chains = []
        for d in range(2):
            ci = i if d == 0 else sc - 1 - i
            r0 = pl.multiple_of(ci * cc, cc)
            chains += [(d, bi, h, ci, r0) for bi in range(nb) for h in range(DN_HEADS)]
        sts = [st_ref[d, bi, h] for (d, bi, h, _, _) in chains]
        r1s = [jnp.dot(dirs[d][1][bi, h, ci], st.astype(BF16), preferred_element_type=F32)
               for (d, bi, h, ci, _), st in zip(chains, sts)]
        us = [dirs[d][0][bi, h, pl.ds(r0, cc), :] - r1[:cc] for (d, bi, h, _, r0), r1 in zip(chains, r1s)]
        r2s = [jnp.dot(dirs[d][2][bi, h, ci], u.astype(BF16), preferred_element_type=F32)
               for (d, bi, h, ci, _), u in zip(chains, us)]
        for (d, bi, h, ci, r0), st, r1, r2 in zip(chains, sts, r1s, r2s):
            dirs[d][4][bi, pl.ds(r0, cc), h * 128:(h + 1) * 128] = (r1[cc:] + r2[:cc]).astype(BF16)
            st_ref[d, bi, h] = st * dirs[d][3][bi, h, ci, 0:1, :] + r2[cc:]
        return carry

    lax.fori_loop(0, sc, body, 0)

    @pl.when(n == pl.num_programs(0) - 1)
    def _():
        sout_ref[...] = st_ref[...]


def delta_scan(u0, lhs1, lhs2, gl, s0, b, n):
    nc = n // DN_CHUNK
    sc = min(8, nc)
    nblk = nc // sc
    nh = DN_HEADS
    tt = sc * DN_CHUNK
    fwd = lambda i: i
    bwd = lambda i: nblk - 1 - i

    def specs(d, blk):
        return [pl.BlockSpec((None, b, nh, tt, 128), lambda i: (d, 0, 0, blk(i), 0)),
                pl.BlockSpec((None, b, nh, sc, 128, 128), lambda i: (d, 0, 0, blk(i), 0, 0)),
                pl.BlockSpec((None, b, nh, sc, 192, 64), lambda i: (d, 0, 0, blk(i), 0, 0)),
                pl.BlockSpec((None, b, nh, sc, 8, 128), lambda i: (d, 0, 0, blk(i), 0, 0))]

    sf, sb = specs(0, fwd), specs(1, bwd)
    in_specs = [sf[0], sb[0], sf[1], sb[1], sf[2], sb[2], sf[3], sb[3],
                pl.BlockSpec((2, b, nh, 128, 128), lambda i: (0, 0, 0, 0, 0))]
    return pl.pallas_call(
        functools.partial(_delta_scan_kernel, sc=sc, nb=b),
        grid=(nblk,),
        in_specs=in_specs,
        out_specs=[pl.BlockSpec((b, tt, DN_WIDTH), lambda i: (0, fwd(i), 0)),
                   pl.BlockSpec((b, tt, DN_WIDTH), lambda i: (0, bwd(i), 0)),
                   pl.BlockSpec((2, b, nh, 128, 128), lambda i: (0, 0, 0, 0, 0))],
        out_shape=[jax.ShapeDtypeStruct((b, n, DN_WIDTH), BF16), jax.ShapeDtypeStruct((b, n, DN_WIDTH), BF16),
                   jax.ShapeDtypeStruct((2, b, nh, 128, 128), F32)],
        scratch_shapes=[pltpu.VMEM((2, b, nh, 128, 128), F32)],
        compiler_params=_cparams("arbitrary"),
        name="delta_scan",
    )(u0, u0, lhs1, lhs1, lhs2, lhs2, gl, gl, s0)


def delta_branch(p, small, conv_w, a_log, dt_bias, s0, b, n, rows, cols):
    nh = DN_HEADS
    qkv = dn_conv_prep(p, conv_w, b, n, rows, cols)
    small_t = small[:, :4 * nh].T
    rate = jnp.concatenate([jnp.zeros((2 * nh,), F32), a_log.reshape(-1)])
    bias = jnp.concatenate([jnp.zeros((2 * nh,), F32), dt_bias.reshape(-1)])
    prm_t = jnp.stack([rate, bias], axis=1)
    prm = jnp.pad(prm_t.T, ((0, 0), (0, SMALL_WIDTH - 4 * nh)))
    u0, lhs1, lhs2, gl = delta_prep(qkv, small, small_t, prm, prm_t, b, n)
    o_f, o_b, s_out = delta_scan(u0, lhs1, lhs2, gl, s0, b, n)
    return o_f.reshape(b * n, DN_WIDTH), o_b.reshape(b * n, DN_WIDTH), s_out


def _cos_sin(rows, cols, period):
    ang = 2.0 * np.pi * ((np.arange(rows)[:, None] * np.arange(cols)[None, :]) % period) / period
    return np.cos(ang), np.sin(ang)


def _const_bf16(a):
    return jnp.asarray(a, F32).astype(BF16)


def _stage2_matrices(l2):
    c, s = _cos_sin(l2, l2, l2)
    fwd = np.block([[c, s], [-s, c]])
    inv = np.block([[c, -s], [s, c]])
    return _const_bf16(fwd), _const_bf16(inv)


def _twiddles(l1, l2, kb):
    ang = 2.0 * np.pi * ((np.arange(l2)[:, None] * np.arange(l1)[None, :]) % (l1 * l2)) / (l1 * l2)
    tw = np.stack([np.cos(ang), np.sin(ang)], axis=0).reshape(2, l2, l1 // kb, kb)
    return jnp.asarray(np.transpose(tw, (2, 0, 1, 3)), F32)


def _twiddle_mul(ar, ai, c, s, conj):
    if conj:
        return ar * c - ai * s, ai * c + ar * s
    return ar * c + ai * s, ai * c - ar * s


def _cplx_apply(m_ref, re, im):
    half = re.shape[0]
    out = jnp.dot(m_ref[...], jnp.concatenate([re, im], axis=0).astype(BF16), preferred_element_type=F32)
    return out[:half], out[half:]


def _fnet_mid_kernel(a_ref, tw_ref, m2_ref, cs_ref, o_ref, *, kb, ch):
    ts = [_twiddle_mul(a_ref[0, j], a_ref[1, j], tw_ref[0, :, j:j + 1], tw_ref[1, :, j:j + 1], False)
          for j in range(kb)]
    us = [_cplx_apply(m2_ref, tr, ti) for tr, ti in ts]
    ys = [jnp.dot(jnp.concatenate([ur, ui], axis=1).astype(BF16), cs_ref[...], preferred_element_type=F32)
          for ur, ui in us]
    for j, y in enumerate(ys):
        o_ref[:, j * ch:(j + 1) * ch] = y


def fnet_mid(a, tw, m2, cs, kb):
    bsz, _, l1, l2, ch = a.shape
    return pl.pallas_call(
        functools.partial(_fnet_mid_kernel, kb=kb, ch=ch),
        grid=(bsz, l1 // kb),
        in_specs=[pl.BlockSpec((None, 2, kb, l2, ch), lambda b, k: (b, 0, k, 0, 0)),
                  pl.BlockSpec((None, 2, l2, kb), lambda b, k: (k, 0, 0, 0)),
                  pl.BlockSpec(m2.shape, lambda b, k: (0, 0)),
                  pl.BlockSpec(cs.shape, lambda b, k: (0, 0))],
        out_specs=pl.BlockSpec((None, l2, kb * ch), lambda b, k: (b, 0, k)),
        out_shape=jax.ShapeDtypeStruct((bsz, l2, l1 * ch), F32),
        compiler_params=_cparams("parallel", "parallel"),
        name="fnet_mid",
    )(a, tw, m2, cs)


def _fnet_channel_matrix(n):
    c, s = _cos_sin(FN_GROUP_DIM, FN_GROUP_DIM, FN_GROUP_DIM)
    eye = np.eye(FN_GROUPS)
    scale = 1.0 / math.sqrt(n * FN_GROUP_DIM)
    return _const_bf16(np.concatenate([np.kron(eye, c), np.kron(eye, s)], axis=0) * scale)


FN_SUB = 8


def _fnet_long_kernel(x_ref, kin_ref, tw_ref, m2k_ref, cs_ref, o_ref, a_ref):
    n_sub = x_ref.shape[1] // FN_SUB
    rows, width = x_ref.shape[0] * FN_SUB, x_ref.shape[2]
    half = FN_SUB * FN_SUB
    for j in range(n_sub):
        r = jnp.dot(kin_ref[...], x_ref[:, j * FN_SUB:(j + 1) * FN_SUB, :].reshape(rows, width).astype(BF16),
                    preferred_element_type=F32)
        a_ref[0, j * half:(j + 1) * half, :] = r[:half]
        a_ref[1, j * half:(j + 1) * half, :] = r[half:]
    tr, ti = _twiddle_mul(a_ref[0], a_ref[1], tw_ref[0], tw_ref[1], False)
    ur, ui = _cplx_apply(m2k_ref, tr, ti)
    y = jnp.dot(jnp.concatenate([ur, ui], axis=1).astype(BF16), cs_ref[...], preferred_element_type=F32)
    o_ref[...] = y.reshape(o_ref.shape)


def fnet_long(p, b, n):
    ch, sub = FN_WIDTH, FN_SUB
    l1, l2 = n // 128, 128
    nblk = l1 // sub
    c1, s1 = _cos_sin(l1, l1, l1)
    m1 = np.stack([c1, -s1], axis=0).reshape(2, nblk, sub, l1)
    kin = np.einsum('pbkn,jq->bpjknq', m1, np.eye(sub)).reshape(nblk, 2 * sub * sub, l1 * sub)
    ang = 2.0 * np.pi * ((np.arange(l2)[:, None] * np.arange(l1)[None, :]) % n) / n
    tw = np.stack([np.cos(ang), np.sin(ang)], axis=0).reshape(2, l2, nblk, sub)
    tw = np.transpose(tw, (2, 0, 1, 3)).reshape(nblk, 2, l2 * sub, 1)
    c2, s2 = _cos_sin(l2, l2, l2)
    m2k = _const_bf16(np.kron(np.block([[c2, s2], [-s2, c2]]), np.eye(sub)))
    cs = _fnet_channel_matrix(n)
    const = lambda a: pl.BlockSpec(a.shape, lambda bi, k: (0,) * a.ndim, pipeline_mode=pl.Buffered(1))
    y = pl.pallas_call(
        _fnet_long_kernel,
        grid=(b, nblk),
        in_specs=[pl.BlockSpec((None, l1, l2, ch), lambda bi, k: (bi, 0, 0, P_FN // ch),
                               pipeline_mode=pl.Buffered(1)),
                  pl.BlockSpec((None,) + kin.shape[1:], lambda bi, k: (k, 0, 0)),
                  pl.BlockSpec((None, 2, l2 * sub, 1), lambda bi, k: (k, 0, 0, 0)),
                  const(m2k), const(cs)],
        out_specs=pl.BlockSpec((None, l2, sub, ch), lambda bi, k: (bi, 0, k, 0)),
        out_shape=jax.ShapeDtypeStruct((b, l2, l1, ch), F32),
        scratch_shapes=[pltpu.VMEM((2, l2 * sub, ch), F32)],
        compiler_params=_cparams("parallel", "parallel"),
        name="fnet_long",
    )(p.reshape(b, l1, l2, P_WIDTH), _const_bf16(kin), jnp.asarray(tw, F32), m2k, cs)
    return y.reshape(b * n, ch)


def fnet_branch(p, b, n):
    if n == 8192:
        return fnet_long(p, b, n)
    ch = FN_WIDTH
    xr = lax.slice_in_dim(p, P_FN, P_FN + ch, axis=1).reshape(b, 1, 1, n, ch)
    a = jnp.concatenate([xr, jnp.zeros_like(xr)], axis=1)
    y = fnet_mid(a, _twiddles(1, n, 1), _stage2_matrices(n)[0], _fnet_channel_matrix(n), 1)
    return y.reshape(b * n, ch)


def _seq_conv_kernel(prev_ref, cur_ref, next_ref, w_ref, *o_refs, n_tiles):
    t = pl.program_id(1)
    tt = cur_ref.shape[0]
    cur = cur_ref[...]
    row = lax.broadcasted_iota(jnp.int32, (tt, 1), 0)
    before = jnp.where(t == 0, 0.0, prev_ref[7:8, :])
    after = jnp.where(t == n_tiles - 1, 0.0, next_ref[0:1, :])
    left = jnp.where(row == 0, before, pltpu.roll(cur, 1, axis=0))
    right = jnp.where(row == tt - 1, after, pltpu.roll(cur, tt - 1, axis=0))
    y = w_ref[0:1, :] * left + w_ref[1:2, :] * cur + w_ref[2:3, :] * right
    for part, o_ref in enumerate(o_refs):
        o_ref[...] = y[:, part * HY_WIDTH:(part + 1) * HY_WIDTH]


def seq_conv(p, conv_w, b, n):
    tt = min(n, 1024)
    n_tiles = n // tt
    parts = HY_ORDER + 1
    w = parts * HY_WIDTH
    c0 = P_HY // w
    out_spec = pl.BlockSpec((tt, HY_WIDTH), lambda bi, t: (bi * n_tiles + t, 0))
    return pl.pallas_call(
        functools.partial(_seq_conv_kernel, n_tiles=n_tiles),
        grid=(b, n_tiles),
        in_specs=[pl.BlockSpec((8, w), lambda bi, t: (jnp.maximum((bi * n_tiles + t) * (tt // 8) - 1, 0), c0)),
                  pl.BlockSpec((tt, w), lambda bi, t: (bi * n_tiles + t, c0)),
                  pl.BlockSpec((8, w), lambda bi, t: (jnp.minimum((bi * n_tiles + t + 1) * (tt // 8),
                                                                  b * n // 8 - 1), c0)),
                  pl.BlockSpec((SHORT_CONV, w), lambda bi, t: (0, 0))],
        out_specs=[out_spec] * parts,
        out_shape=[jax.ShapeDtypeStruct((b * n, HY_WIDTH), F32)] * parts,
        compiler_params=_cparams("parallel", "parallel"),
        name="hy_seq_conv",
    )(p, p, p, conv_w)


def _hdot(a, b):
    return jnp.dot(a, b, preferred_element_type=F32, precision=lax.Precision.HIGHEST)


def _hy_filter_kernel(ft_ref, t_ref, w1t_ref, b1_ref, f1_ref, w2t_ref, b2_ref, f2_ref, w3_ref, dl_ref, k_ref, s_ref,
                      *, n, tr):
    i = pl.program_id(0)
    hid = jnp.sin(f1_ref[...] * (_hdot(w1t_ref[...], ft_ref[...]) + b1_ref[...]))
    hid = jnp.sin(f2_ref[...] * (_hdot(w2t_ref[...], hid) + b2_ref[...]))
    filt = _hdot(hid.T, w3_ref[...]) * jnp.exp(-t_ref[...] * dl_ref[...])
    row = i * tr + lax.broadcasted_iota(jnp.int32, (tr, 1), 0)
    filt = jnp.where(row == n, 0.0, filt)
    k_ref[...] = filt

    @pl.when(i == 0)
    def _():
        s_ref[...] = jnp.zeros_like(s_ref)

    s_ref[...] += jnp.sum(jnp.abs(filt), axis=0, keepdims=True)


def hy_filter(n, w1, b1, freq1, w2, b2, freq2, w3):
    pos = jnp.arange(n, dtype=F32)
    t = pos / max(n - 1, 1)
    bands = jnp.linspace(1e-4, HY_BANDS - 1, HY_BANDS, dtype=F32)
    ang = (2.0 * math.pi / n) * pos[:, None] * bands[None, :]
    feats = jnp.concatenate([t[:, None], jnp.cos(ang), -jnp.sin(ang)], axis=-1)
    feats2 = jnp.concatenate([feats, feats[:1], feats[:0:-1]], axis=0)
    kpad = 128
    feats_t = jnp.pad(feats2, ((0, 0), (0, kpad - HY_EMB_DIM))).T
    w1t = jnp.pad(w1, ((0, kpad - HY_EMB_DIM), (0, 0))).T
    min_decay = math.log(HY_DECAY_TARGET) / HY_SLOW_DECAY_PCT
    max_decay = math.log(HY_DECAY_TARGET) / HY_FAST_DECAY_PCT
    cw = HY_ORDER * HY_WIDTH
    deltas = jnp.abs(jnp.linspace(min_decay, max_decay, cw, dtype=F32)).reshape(1, cw)
    tr = min(n, 1024)
    half = n // tr
    hd = HY_FILTER_HIDDEN
    vec = lambda v: v.reshape(hd, 1)
    full = lambda shp: pl.BlockSpec(shp, lambda i: (0, 0))
    return pl.pallas_call(
        functools.partial(_hy_filter_kernel, n=n, tr=tr),
        grid=(2 * half,),
        in_specs=[pl.BlockSpec((kpad, tr), lambda i: (0, i)), pl.BlockSpec((tr, 1), lambda i: (i, 0)),
                  full((hd, kpad)), full((hd, 1)), full((hd, 1)), full((hd, hd)), full((hd, 1)), full((hd, 1)),
                  pl.BlockSpec((hd, cw), lambda i: (0, i // half)), full((1, cw))],
        out_specs=[pl.BlockSpec((tr, cw), lambda i: (i, 0)), full((1, cw))],
        out_shape=[jax.ShapeDtypeStruct((2 * n, cw), F32), jax.ShapeDtypeStruct((1, cw), F32)],
        compiler_params=_cparams("arbitrary"),
        name="hy_filter",
    )(feats_t, feats2[:, 0:1], w1t, vec(b1), vec(freq1), w2.T, vec(b2), vec(freq2), w3, deltas)


def _hy_spec_kernel(a_ref, tw_ref, m2_ref, s_ref, o_ref, *, kb):
    inv = 1.0 / s_ref[...]
    ts = [_twiddle_mul(a_ref[0, j], a_ref[1, j], tw_ref[0, :, j:j + 1], tw_ref[1, :, j:j + 1], False)
          for j in range(kb)]
    xs = [_cplx_apply(m2_ref, tr, ti) for tr, ti in ts]
    for j, (xr, xi) in enumerate(xs):
        o_ref[0, j] = xr * inv
        o_ref[1, j] = xi * inv


def hy_spec(a, tw, m2, abs_sum, kb):
    _, l1, l2, cw = a.shape
    w = HY_WIDTH
    return pl.pallas_call(
        functools.partial(_hy_spec_kernel, kb=kb),
        grid=(l1 // kb, cw // w),
        in_specs=[pl.BlockSpec((2, kb, l2, w), lambda k, c: (0, k, 0, c)),
                  pl.BlockSpec((None, 2, l2, kb), lambda k, c: (k, 0, 0, 0)),
                  pl.BlockSpec(m2.shape, lambda k, c: (0, 0)),
                  pl.BlockSpec((1, w), lambda k, c: (0, c))],
        out_specs=pl.BlockSpec((2, kb, l2, w), lambda k, c: (0, k, 0, c)),
        out_shape=jax.ShapeDtypeStruct((2, l1, l2, cw), F32),
        compiler_params=_cparams("parallel", "parallel"),
        name="hy_spec",
    )(a, tw, m2, abs_sum)


def _hy_mid_kernel(a_ref, tw_ref, m2_ref, m2c_ref, kf_ref, o_ref, *, kb):
    cs = [(tw_ref[0, :, j:j + 1], tw_ref[1, :, j:j + 1]) for j in range(kb)]
    ts = [_twiddle_mul(a_ref[0, j], a_ref[1, j], c, s, False) for j, (c, s) in enumerate(cs)]
    xs = [_cplx_apply(m2_ref, tr, ti) for tr, ti in ts]
    ps = [(xr * kf_ref[0, j] - xi * kf_ref[1, j], xr * kf_ref[1, j] + xi * kf_ref[0, j])
          for j, (xr, xi) in enumerate(xs)]
    bs = [_cplx_apply(m2c_ref, pr, pi) for pr, pi in ps]
    for j, ((br, bi), (c, s)) in enumerate(zip(bs, cs)):
        o_ref[0, j], o_ref[1, j] = _twiddle_mul(br, bi, c, s, True)


def hy_mid(a, tw, m2, m2c, kf, order, kb):
    bsz, _, l1, l2, w = a.shape
    return pl.pallas_call(
        functools.partial(_hy_mid_kernel, kb=kb),
        grid=(bsz, l1 // kb),
        in_specs=[pl.BlockSpec((None, 2, kb, l2, w), lambda b, k: (b, 0, k, 0, 0)),
                  pl.BlockSpec((None, 2, l2, kb), lambda b, k: (k, 0, 0, 0)),
                  pl.BlockSpec(m2.shape, lambda b, k: (0, 0)),
                  pl.BlockSpec(m2c.shape, lambda b, k: (0, 0)),
                  pl.BlockSpec((2, kb, l2, w), lambda b, k: (0, k, 0, order))],
        out_specs=pl.BlockSpec((None, 2, kb, l2, w), lambda b, k: (b, 0, k, 0, 0)),
        out_shape=jax.ShapeDtypeStruct((bsz, 2, l1, l2, w), F32),
        compiler_params=_cparams("parallel", "parallel"),
        name="hy_mid",
    )(a, tw, m2, m2c, kf)


def _hy_out_kernel(m_ref, bp_ref, xo_ref, z_ref, bias_ref, o_ref):
    y = jnp.dot(m_ref[...], bp_ref[...].astype(BF16), preferred_element_type=F32)
    o_ref[...] = xo_ref[...] * (y + bias_ref[...] * z_ref[...])


def hy_out(m, bp, xo, z, bias_row, tn):
    bsz, k, n = bp.shape
    r = m.shape[0]
    blk = pl.BlockSpec((None, r, tn), lambda b, j: (b, 0, j))
    return pl.pallas_call(
        _hy_out_kernel,
        grid=(bsz, n // tn),
        in_specs=[pl.BlockSpec((r, k), lambda b, j: (0, 0)),
                  pl.BlockSpec((None, k, tn), lambda b, j: (b, 0, j)), blk, blk,
                  pl.BlockSpec((1, tn), lambda b, j: (0, j))],
        out_specs=blk,
        out_shape=jax.ShapeDtypeStruct((bsz, r, n), F32),
        compiler_params=_cparams("parallel", "parallel"),
        name="hy_out",
    )(m, bp, xo, z, bias_row)


SUB = 8
HY_KB = 16


def _half_spectrum_blocks(l1, kb):
    return -(-(l1 // 2 + 1) // kb)


def _kron_stage1(l1, n1_used, kb):
    c, s = _cos_sin(l1, n1_used, l1)
    m = np.stack([c, -s], axis=0).reshape(2, l1 // kb, kb, n1_used)
    m = np.transpose(m, (1, 0, 2, 3)).reshape(l1 // kb, 2 * kb, n1_used)[:_half_spectrum_blocks(l1, kb)]
    return _const_bf16(np.stack([np.kron(blk, np.eye(SUB)) for blk in m]))


def _kron_stage_out(l1, n1_used, kb):
    c, s = _cos_sin(n1_used, l1, l1)
    k1 = np.arange(l1)
    mult = np.where((k1 == 0) | (k1 == l1 // 2), 1.0, np.where(k1 < l1 // 2, 2.0, 0.0))
    m = np.stack([c * mult, -s * mult], axis=1).reshape(n1_used, 2, l1 // kb, kb) / (l1 * l1)
    m = np.transpose(m, (2, 0, 1, 3)).reshape(l1 // kb, n1_used, 2 * kb)[:_half_spectrum_blocks(l1, kb)]
    return _const_bf16(np.stack([np.kron(blk, np.eye(SUB)) for blk in m]))


def _strided_stage_in(kin_ref, src_ref, a_ref):
    n_sub, width = src_ref.shape[1] // SUB, src_ref.shape[2]
    rows = src_ref.shape[0] * SUB
    for j in range(n_sub):
        r = jnp.dot(kin_ref[...], src_ref[:, j * SUB:(j + 1) * SUB, :].reshape(rows, width).astype(BF16),
                    preferred_element_type=F32)
        a_ref[:, j * SUB:(j + 1) * SUB, :] = r.reshape(a_ref.shape[0], SUB, width)


def _hy_conv_kernel(z_ref, xo_ref, kin_ref, tw_ref, m2_ref, m2c_ref, kf_ref, kout_ref, bias_ref, o_ref,
                    a_ref, b_ref, *, kb):
    k = pl.program_id(1)

    @pl.when(k == 0)
    def _():
        o_ref[...] = jnp.zeros_like(o_ref)

    _strided_stage_in(kin_ref, z_ref, a_ref)
    grp = kb
    for g0 in range(0, kb, grp):
        js = range(g0, g0 + grp)
        cs = [(tw_ref[0, :, j:j + 1], tw_ref[1, :, j:j + 1]) for j in js]
        ts = [_twiddle_mul(a_ref[j], a_ref[kb + j], c, s, False) for j, (c, s) in zip(js, cs)]
        xs = [_cplx_apply(m2_ref, tr, ti) for tr, ti in ts]
        ps = [(xr * kf_ref[0, j] - xi * kf_ref[1, j], xr * kf_ref[1, j] + xi * kf_ref[0, j])
              for j, (xr, xi) in zip(js, xs)]
        bs = [_cplx_apply(m2c_ref, pr, pi) for pr, pi in ps]
        for j, (br, bi), (c, s) in zip(js, bs, cs):
            b_ref[j], b_ref[kb + j] = _twiddle_mul(br, bi, c, s, True)
    n_sub, width = o_ref.shape[1] // SUB, o_ref.shape[2]
    for j in range(n_sub):
        r = jnp.dot(kout_ref[...], b_ref[:, j * SUB:(j + 1) * SUB, :].reshape(2 * kb * SUB, width).astype(BF16),
                    preferred_element_type=F32)
        o_ref[:, j * SUB:(j + 1) * SUB, :] += r.reshape(o_ref.shape[0], SUB, width)

    @pl.when(k == pl.num_programs(1) - 1)
    def _():
        o_ref[...] = xo_ref[...] * (o_ref[...] + bias_ref[...] * z_ref[...])


def hy_conv_long(z, xo, kf, bias, order, b, n):
    w, kb = HY_WIDTH, HY_KB
    l1 = l2 = 128
    n1 = n // l2
    nblk = _half_spectrum_blocks(l1, kb)
    tw = _twiddles(l1, l2, kb)[:nblk]
    m2, m2c = _stage2_matrices(l2)
    kin, kout = _kron_stage1(l1, n1, kb), _kron_stage_out(l1, n1, kb)
    tok = pl.BlockSpec((None, n1, l2, w), lambda bi, k: (bi, 0, 0, 0), pipeline_mode=pl.Buffered(1))
    const = lambda a: pl.BlockSpec(a.shape, lambda bi, k: (0,) * a.ndim)
    out = pl.pallas_call(
        functools.partial(_hy_conv_kernel, kb=kb),
        grid=(b, nblk),
        in_specs=[tok, tok,
                  pl.BlockSpec((None,) + kin.shape[1:], lambda bi, k: (k, 0, 0)),
                  pl.BlockSpec((None, 2, l2, kb), lambda bi, k: (k, 0, 0, 0)),
                  const(m2), const(m2c),
                  pl.BlockSpec((2, kb, l2, w), lambda bi, k: (0, k, 0, order)),
                  pl.BlockSpec((None,) + kout.shape[1:], lambda bi, k: (k, 0, 0)),
                  pl.BlockSpec((1, w), lambda bi, k: (0, 0))],
        out_specs=pl.BlockSpec((None, n1, l2, w), lambda bi, k: (bi, 0, 0, 0), pipeline_mode=pl.Buffered(1)),
        out_shape=jax.ShapeDtypeStruct((b, n1, l2, w), F32),
        scratch_shapes=[pltpu.VMEM((2 * kb, l2, w), F32), pltpu.VMEM((2 * kb, l2, w), F32)],
        compiler_params=_cparams("parallel", "arbitrary"),
        name="hy_conv_long",
    )(z.reshape(b, n1, l2, w), xo.reshape(b, n1, l2, w), kin, tw, m2, m2c, kf, kout, bias.reshape(1, w))
    return out.reshape(b * n, w)


def _hy_spec_long_kernel(kern_ref, kin_ref, tw_ref, m2_ref, s_ref, o_ref, a_ref, *, kb):
    _strided_stage_in(kin_ref, kern_ref, a_ref)
    inv = 1.0 / s_ref[...]
    ts = [_twiddle_mul(a_ref[j], a_ref[kb + j], tw_ref[0, :, j:j + 1], tw_ref[1, :, j:j + 1], False)
          for j in range(kb)]
    xs = [_cplx_apply(m2_ref, tr, ti) for tr, ti in ts]
    for j, (xr, xi) in enumerate(xs):
        o_ref[0, j] = xr * inv
        o_ref[1, j] = xi * inv


def hy_spec_long(kern, abs_sum):
    w, kb = HY_WIDTH, HY_KB
    l1 = l2 = 128
    cw = kern.shape[1]
    nblk = _half_spectrum_blocks(l1, kb)
    tw = _twiddles(l1, l2, kb)[:nblk]
    m2, _ = _stage2_matrices(l2)
    kin = _kron_stage1(l1, l1, kb)
    return pl.pallas_call(
        functools.partial(_hy_spec_long_kernel, kb=kb),
        grid=(cw // w, nblk),
        in_specs=[pl.BlockSpec((l1, l2, w), lambda c, k: (0, 0, c), pipeline_mode=pl.Buffered(1)),
                  pl.BlockSpec((None,) + kin.shape[1:], lambda c, k: (k, 0, 0)),
                  pl.BlockSpec((None, 2, l2, kb), lambda c, k: (k, 0, 0, 0)),
                  pl.BlockSpec(m2.shape, lambda c, k: (0, 0)),
                  pl.BlockSpec((1, w), lambda c, k: (0, c))],
        out_specs=pl.BlockSpec((2, kb, l2, w), lambda c, k: (0, k, 0, c)),
        out_shape=jax.ShapeDtypeStruct((2, nblk * kb, l2, cw), F32),
        scratch_shapes=[pltpu.VMEM((2 * kb, l2, w), F32)],
        compiler_params=_cparams("parallel", "arbitrary"),
        name="hy_spec_long",
    )(kern.reshape(l1, l2, cw), kin, tw, m2, abs_sum)


def hyena_pallas(p, conv_w, w1, b1, freq1, w2, b2, freq2, w3, bias, b, n):
    w = HY_WIDTH
    x0, x1, v = seq_conv(p, conv_w, b, n)
    kern, abs_sum = hy_filter(n, w1, b1, freq1, w2, b2, freq2, w3)
    z = v
    if n == 8192:
        kf = hy_spec_long(kern, abs_sum)
        for order, xo in enumerate((x0, x1)):
            z = hy_conv_long(z, xo, kf, bias[order], order, b, n)
        return z
    l2 = 2 * n
    tw = _twiddles(1, l2, 1)
    m2, m2c = _stage2_matrices(l2)
    ak = jnp.stack([kern, jnp.zeros_like(kern)], axis=0).reshape(2, 1, l2, HY_ORDER * w)
    m_out = _const_bf16(np.eye(n, 2 * l2) / l2)
    kf = hy_spec(ak, tw, m2, abs_sum, 1)
    for order, xo in enumerate((x0, x1)):
        zp = jnp.pad(z.reshape(b, 1, 1, n, w), ((0, 0), (0, 0), (0, 0), (0, n), (0, 0)))
        a = jnp.concatenate([zp, jnp.zeros_like(zp)], axis=1)
        bp = hy_mid(a, tw, m2, m2c, kf, order, 1)
        z = hy_out(m_out, bp.reshape(b, 2 * l2, w), xo.reshape(b, n, w), z.reshape(b, n, w),
                   bias[order].reshape(1, w), w).reshape(b * n, w)
    return z


def _prep_w_in(w):
    main = jnp.concatenate([w[:, OFF_Q:OFF_Z], w[:, OFF_HY:OFF_GATE], w[:, OFF_FN:OFF_HY], w[:, OFF_Z:OFF_BETA]],
                           axis=1).astype(BF16)
    small = jnp.pad(w[:, OFF_BETA:OFF_FN], ((0, 0), (0, SMALL_WIDTH - 4 * DN_HEADS))).astype(BF16)
    return main, small, w[:, OFF_GATE:IN_WIDTH].astype(BF16)


def kernel(x, c, ctx, c_ctx, w_mod, b_mod, norm1, norm2, w_in, dn_conv, dn_a_log, dn_dt_bias,
           dn_out_norm, hy_conv, hy_w1, hy_b1, hy_freq1, hy_w2, hy_b2, hy_freq2, hy_w3, hy_bias,
           w_branch_a, w_branch_b, w_branch_c, w_out, w_ff1, w_ff2, final_norm):
    b, n_lat, d = x.shape
    rows = n_lat // GRID_W
    n_ctx = ctx.shape[1]
    tm_x, tm_c = 1024, n_ctx

    c_rows = jnp.concatenate([c, c_ctx[None], jnp.zeros((8 - b - 1, d), F32)], axis=0)
    mods = mod_vectors(c_rows, w_mod, b_mod)
    s_zero = jnp.zeros((2, b, DN_HEADS, DN_HEAD_DIM, DN_HEAD_DIM), F32)
    h, hc = x.reshape(b * n_lat, d), ctx.reshape(b * n_ctx, d)

    for l in range(DEPTH):
        last = l == DEPTH - 1
        mv = mods[l].reshape(8, N_MOD, 1, d)
        mx = [mv[:b, i] for i in range(N_MOD)]
        mc = [mv[b:b + 1, i] for i in range(N_MOD)]
        w_main, w_small, w_gate = _prep_w_in(w_in[l])
        wa, wb, wc, wo = (w.astype(BF16) for w in (w_branch_a[l], w_branch_b[l], w_branch_c[l], w_out[l]))
        w1, w2 = w_ff1[l].astype(BF16), w_ff2[l].astype(BF16)
        n1, n2 = norm1[l][None, None, :], norm2[l][None, None, :]

        p_c, small_c = in_proj(hc, n1 * (1.0 + mc[1]), mc[0], w_main, w_small, b * n_ctx, tm_c)
        p_x, small_x = in_proj(h, n1 * (1.0 + mx[1]), mx[0], w_main, w_small, n_lat, tm_x)

        def mix(p, n):
            y_c = hyena_pallas(p, hy_conv[l], hy_w1[l], hy_b1[l], hy_freq1[l], hy_w2[l], hy_b2[l], hy_freq2[l],
                               hy_w3[l], hy_bias[l], b, n)
            return fnet_branch(p, b, n), y_c

        ocf, ocb, s_ctx = delta_branch(p_c, small_c, dn_conv[l], dn_a_log[l], dn_dt_bias[l], s_zero,
                                       b, n_ctx, 1, n_ctx)
        oxf, oxb, _ = delta_branch(p_x, small_x, dn_conv[l], dn_a_log[l], dn_dt_bias[l], s_ctx,
                                   b, n_lat, rows, GRID_W)

        y_b, y_c = mix(p_x, n_lat)
        h = merge(oxf, oxb, p_x, y_b, y_c, h, n1 * (1.0 + mx[1]), mx[0], mx[2], dn_out_norm[l],
                  w_gate, wa, wb, wc, wo, n_lat, 512)
        h = mlp(h, n2 * (1.0 + mx[4]), mx[3], mx[5], w1, w2, n_lat, tm_x, final_norm if last else None)

        if not last:
            y_b, y_c = mix(p_c, n_ctx)
            hc = merge(ocf, ocb, p_c, y_b, y_c, hc, n1 * (1.0 + mc[1]), mc[0], mc[2], dn_out_norm[l],
                       w_gate, wa, wb, wc, wo, b * n_ctx, tm_c)
            hc = mlp(hc, n2 * (1.0 + mc[4]), mc[3], mc[5], w1, w2, b * n_ctx, tm_c)

    return h.reshape(b, n_lat, d)
```

```python
import functools
import math

import jax
import jax.numpy as jnp
import numpy as np
from jax import lax
from jax.experimental import pallas as pl
from jax.experimental.pallas import tpu as pltpu

D_MODEL = 1024
DEPTH = 2
GRID_W = 64
NORM_EPS = 1e-6
N_MOD = 6

DN_HEADS = 4
DN_HEAD_DIM = 128
DN_WIDTH = DN_HEADS * DN_HEAD_DIM
DN_CHUNK = 64
SHORT_CONV = 3

FN_GROUPS = 4
FN_GROUP_DIM = 64
FN_WIDTH = FN_GROUPS * FN_GROUP_DIM

HY_WIDTH = 256
HY_ORDER = 2
HY_EMB_DIM = 33
HY_BANDS = (HY_EMB_DIM - 1) // 2
HY_FILTER_HIDDEN = 64
HY_FAST_DECAY_PCT = 0.3
HY_SLOW_DECAY_PCT = 1.5
HY_DECAY_TARGET = 1e-2

N_BRANCHES = 3
D_FF = 4 * D_MODEL

OFF_Q = 0
OFF_Z = 3 * DN_WIDTH
OFF_BETA = OFF_Z + DN_WIDTH
OFF_A = OFF_BETA + 2 * DN_HEADS
OFF_FN = OFF_A + 2 * DN_HEADS
OFF_HY = OFF_FN + FN_WIDTH
OFF_GATE = OFF_HY + (HY_ORDER + 1) * HY_WIDTH
IN_WIDTH = OFF_GATE + N_BRANCHES * D_MODEL

P_QKV = 0
P_HY = P_QKV + 3 * DN_WIDTH
P_FN = P_HY + (HY_ORDER + 1) * HY_WIDTH
P_Z = P_FN + FN_WIDTH
P_WIDTH = P_Z + DN_WIDTH
SMALL_WIDTH = 128

F32 = jnp.float32
BF16 = jnp.bfloat16
VMEM_LIMIT = 56 * 1024 * 1024


def _cparams(*sem):
    return pltpu.CompilerParams(dimension_semantics=sem, vmem_limit_bytes=VMEM_LIMIT)


def _bdot(a, b):
    return jnp.dot(a.astype(BF16), b.astype(BF16), preferred_element_type=F32)


def _sigmoid(x):
    return 0.5 * jnp.tanh(0.5 * x) + 0.5


def _modnorm(xf, gs, sh):
    r = lax.rsqrt(jnp.mean(xf * xf, axis=-1, keepdims=True) + NORM_EPS)
    return xf * r * gs + sh


def _mod_kernel(c_ref, w_ref, b_ref, o_ref):
    c = c_ref[...]
    o_ref[...] = _bdot(c * _sigmoid(c), w_ref[...]) + b_ref[...]


def mod_vectors(c_rows, w_mod, b_mod):
    tn = 1536
    n = N_MOD * D_MODEL
    return pl.pallas_call(
        _mod_kernel,
        grid=(DEPTH, n // tn),
        in_specs=[pl.BlockSpec((8, D_MODEL), lambda l, j: (0, 0)),
                  pl.BlockSpec((None, D_MODEL, tn), lambda l, j: (l, 0, j)),
                  pl.BlockSpec((None, 1, tn), lambda l, j: (l, 0, j))],
        out_specs=pl.BlockSpec((None, 8, tn), lambda l, j: (l, 0, j)),
        out_shape=jax.ShapeDtypeStruct((DEPTH, 8, n), F32),
        compiler_params=_cparams("parallel", "parallel"),
        name="mod_vectors",
    )(c_rows, w_mod, b_mod.reshape(DEPTH, 1, n))


def _in_proj_kernel(x_ref, gs_ref, sh_ref, w_ref, ws_ref, p_ref, small_ref, *, tn):
    xn = _modnorm(x_ref[...], gs_ref[0], sh_ref[0]).astype(BF16)
    small_ref[...] = jnp.dot(xn, ws_ref[...], preferred_element_type=F32)
    for j in range(p_ref.shape[1] // tn):
        p_ref[:, j * tn:(j + 1) * tn] = jnp.dot(xn, w_ref[:, j * tn:(j + 1) * tn], preferred_element_type=F32)


def in_proj(x2d, gs, sh, w_main, w_small, rows_per_mod, tm):
    m = x2d.shape[0]
    tiles_per_mod = rows_per_mod // tm
    mod_spec = pl.BlockSpec((1, 1, D_MODEL), lambda i: (i // tiles_per_mod, 0, 0))
    resident = lambda a: pl.BlockSpec(a.shape, lambda i: (0, 0), pipeline_mode=pl.Buffered(1))
    return pl.pallas_call(
        functools.partial(_in_proj_kernel, tn=1024),
        grid=(m // tm,),
        in_specs=[pl.BlockSpec((tm, D_MODEL), lambda i: (i, 0)), mod_spec, mod_spec,
                  resident(w_main), resident(w_small)],
        out_specs=[pl.BlockSpec((tm, P_WIDTH), lambda i: (i, 0)),
                   pl.BlockSpec((tm, SMALL_WIDTH), lambda i: (i, 0))],
        out_shape=[jax.ShapeDtypeStruct((m, P_WIDTH), F32), jax.ShapeDtypeStruct((m, SMALL_WIDTH), F32)],
        compiler_params=_cparams("parallel"),
        name="in_proj",
    )(x2d, gs, sh, w_main, w_small)


def _merge_kernel(of_ref, ob_ref, z_ref, yb_ref, yc_ref, h_ref, gs_ref, sh_ref, gate_ref, nrm_ref,
                  wg_ref, wa_ref, wb_ref, wc_ref, wo_ref, out_ref):
    xn = _modnorm(h_ref[...], gs_ref[0], sh_ref[0]).astype(BF16)
    o = of_ref[...].astype(F32) + ob_ref[...].astype(F32)
    z = z_ref[...]
    heads = []
    for hd in range(DN_HEADS):
        sl = slice(hd * DN_HEAD_DIM, (hd + 1) * DN_HEAD_DIM)
        oh, zh = o[:, sl], z[:, sl]
        r = lax.rsqrt(jnp.mean(oh * oh, axis=-1, keepdims=True) + NORM_EPS)
        heads.append(oh * r * nrm_ref[...] * (zh * _sigmoid(zh)))
    ya = jnp.concatenate(heads, axis=-1)
    merged = None
    for i, (y, w_ref) in enumerate(((ya, wa_ref), (yb_ref[...], wb_ref), (yc_ref[...], wc_ref))):
        g = jnp.dot(xn, wg_ref[:, i * D_MODEL:(i + 1) * D_MODEL], preferred_element_type=F32)
        term = _sigmoid(g) * _bdot(y, w_ref[...])
        merged = term if merged is None else merged + term
    out_ref[...] = h_ref[...] + gate_ref[0] * _bdot(merged, wo_ref[...])


def merge(o_f, o_b, p, y_b, y_c, h2d, gs, sh, gate, dn_out_norm, wg, wa, wb, wc, wo, rows_per_mod, tm):
    m = h2d.shape[0]
    tiles_per_mod = rows_per_mod // tm
    row = lambda w: pl.BlockSpec((tm, w), lambda i: (i, 0))
    full = lambda a: pl.BlockSpec(a.shape, lambda i: (0,) * a.ndim, pipeline_mode=pl.Buffered(1))
    mod_spec = pl.BlockSpec((1, 1, D_MODEL), lambda i: (i // tiles_per_mod, 0, 0))
    nrm = dn_out_norm.reshape(1, DN_HEAD_DIM)
    return pl.pallas_call(
        _merge_kernel,
        grid=(m // tm,),
        in_specs=[row(DN_WIDTH), row(DN_WIDTH),
                  pl.BlockSpec((tm, DN_WIDTH), lambda i: (i, P_Z // DN_WIDTH)),
                  row(FN_WIDTH), row(HY_WIDTH), row(D_MODEL), mod_spec, mod_spec, mod_spec,
                  full(nrm), full(wg), full(wa), full(wb), full(wc), full(wo)],
        out_specs=row(D_MODEL),
        out_shape=jax.ShapeDtypeStruct((m, D_MODEL), F32),
        compiler_params=_cparams("parallel"),
        name="merge",
    )(o_f, o_b, p, y_b, y_c, h2d, gs, sh, gate, nrm, wg, wa, wb, wc, wo)


def _mlp_kernel(h_ref, gs_ref, sh_ref, gate_ref, w1_ref, w2_ref, fin_ref, out_ref, acc_ref, *, final, tf):
    xn = _modnorm(h_ref[...], gs_ref[0], sh_ref[0]).astype(BF16)
    for j in range(w1_ref.shape[1] // tf):
        a = jnp.maximum(jnp.dot(xn, w1_ref[:, j * tf:(j + 1) * tf], preferred_element_type=F32), 0.0)
        part = jnp.dot((a * a).astype(BF16), w2_ref[j * tf:(j + 1) * tf, :], preferred_element_type=F32)
        if j == 0:
            acc_ref[...] = part
        else:
            acc_ref[...] += part
    y = h_ref[...] + gate_ref[0] * acc_ref[...]
    if final:
        y = y * lax.rsqrt(jnp.mean(y * y, axis=-1, keepdims=True) + NORM_EPS) * fin_ref[...]
    out_ref[...] = y


def mlp(h2d, gs, sh, gate, w1, w2, rows_per_mod, tm, final_gain=None):
    m = h2d.shape[0]
    tiles_per_mod = rows_per_mod // tm
    mod_spec = pl.BlockSpec((1, 1, D_MODEL), lambda i: (i // tiles_per_mod, 0, 0))
    resident = lambda a: pl.BlockSpec(a.shape, lambda i: (0, 0), pipeline_mode=pl.Buffered(1))
    final = final_gain is not None
    fin = (final_gain if final else jnp.ones((D_MODEL,), F32)).reshape(1, D_MODEL)
    return pl.pallas_call(
        functools.partial(_mlp_kernel, final=final, tf=1024),
        grid=(m // tm,),
        in_specs=[pl.BlockSpec((tm, D_MODEL), lambda i: (i, 0)), mod_spec, mod_spec, mod_spec,
                  resident(w1), resident(w2), pl.BlockSpec((1, D_MODEL), lambda i: (0, 0))],
        out_specs=pl.BlockSpec((tm, D_MODEL), lambda i: (i, 0)),
        out_shape=jax.ShapeDtypeStruct((m, D_MODEL), F32),
        scratch_shapes=[pltpu.VMEM((tm, D_MODEL), F32)],
        compiler_params=_cparams("parallel"),
        name="mlp",
    )(h2d, gs, sh, gate, w1, w2, fin)


def _dnconv_kernel(prev_ref, cur_ref, next_ref, w_ref, o_ref, *, cols, n_tiles):
    t, j = pl.program_id(1), pl.program_id(2)
    tt = cur_ref.shape[0]
    prev = jnp.where(t == 0, 0.0, prev_ref[...])
    nxt = jnp.where(t == n_tiles - 1, 0.0, next_ref[...])
    ext = jnp.concatenate([prev, cur_ref[...], nxt], axis=0)
    n_ext = tt + 2 * cols
    col = lax.broadcasted_iota(jnp.int32, (n_ext, 1), 0) % cols
    left = jnp.where(col == 0, 0.0, pltpu.roll(ext, 1, axis=0))
    right = jnp.where(col == cols - 1, 0.0, pltpu.roll(ext, n_ext - 1, axis=0))
    acc = jnp.zeros((tt, DN_WIDTH), F32)
    for dr in range(SHORT_CONV):
        base = dr * cols
        acc = (acc + w_ref[3 * dr:3 * dr + 1, :] * left[base:base + tt]
               + w_ref[3 * dr + 1:3 * dr + 2, :] * ext[base:base + tt]
               + w_ref[3 * dr + 2:3 * dr + 3, :] * right[base:base + tt])
    y = acc * _sigmoid(acc)
    q_scale = jnp.where(j == 0, DN_HEAD_DIM ** -0.5, 1.0)
    for hd in range(DN_HEADS):
        sl = slice(hd * DN_HEAD_DIM, (hd + 1) * DN_HEAD_DIM)
        yh = y[:, sl]
        nrm = lax.rsqrt(jnp.sum(yh * yh, axis=-1, keepdims=True) + NORM_EPS) * q_scale
        o_ref[:, sl] = yh * jnp.where(j < 2, nrm, 1.0)


def dn_conv_prep(p, conv_w, b, n, rows, cols):
    tr = min(rows, 32)
    tt = tr * cols
    n_tiles = rows // tr
    nblk = b * n // cols
    c0 = P_QKV // DN_WIDTH
    return pl.pallas_call(
        functools.partial(_dnconv_kernel, cols=cols, n_tiles=n_tiles),
        grid=(b, n_tiles, 3),
        in_specs=[pl.BlockSpec((cols, DN_WIDTH),
                               lambda bi, t, j: (jnp.maximum(bi * rows + t * tr - 1, 0), c0 + j)),
                  pl.BlockSpec((tt, DN_WIDTH), lambda bi, t, j: (bi * n_tiles + t, c0 + j)),
                  pl.BlockSpec((cols, DN_WIDTH),
                               lambda bi, t, j: (jnp.minimum(bi * rows + (t + 1) * tr, nblk - 1), c0 + j)),
                  pl.BlockSpec((SHORT_CONV * SHORT_CONV, DN_WIDTH), lambda bi, t, j: (0, j))],
        out_specs=pl.BlockSpec((tt, DN_WIDTH), lambda bi, t, j: (bi * n_tiles + t, j)),
        out_shape=jax.ShapeDtypeStruct((b * n, 3 * DN_WIDTH), F32),
        compiler_params=_cparams("parallel", "parallel", "parallel"),
        name="dn_conv_prep",
    )(p, p, p, conv_w.reshape(SHORT_CONV * SHORT_CONV, 3 * DN_WIDTH))


PREP_CHUNKS = 8
PREP_GROUP = 4


def _softplus(x):
    return jnp.maximum(x, 0.0) + jnp.log(1.0 + jnp.exp(-jnp.abs(x)))


def _delta_prep_kernel(qkv_ref, sm_ref, smt_ref, prm_ref, prmt_ref, u0_ref, lhs1_ref, lhs2_ref, gl_ref, *, n_chunks):
    cc = DN_CHUNK
    nh = DN_HEADS
    sm = sm_ref[...]
    beta_all = _sigmoid(sm)
    g_all = -jnp.exp(prm_ref[0:1, :]) * _softplus(sm + prm_ref[1:2, :])
    gt_all = -jnp.exp(prmt_ref[:, 0:1]) * _softplus(smt_ref[...] + prmt_ref[:, 1:2])
    lt = 2 * cc
    lane = lax.broadcasted_iota(jnp.int32, (gt_all.shape[0], lt), 1) % cc
    gt_tiles = []
    for t in range(gt_all.shape[1] // lt):
        gt_f = gt_b = gt_all[:, t * lt:(t + 1) * lt]
        s = 1
        while s < cc:
            gt_f = gt_f + jnp.where(lane >= s, pltpu.roll(gt_f, s, axis=1), 0.0)
            gt_b = gt_b + jnp.where(lane < cc - s, pltpu.roll(gt_b, lt - s, axis=1), 0.0)
            s *= 2
        gt_tiles.append((gt_f, gt_b))
    ri = lax.broadcasted_iota(jnp.int32, (cc, cc), 0)
    ci_ = lax.broadcasted_iota(jnp.int32, (cc, cc), 1)
    sub = lax.broadcasted_iota(jnp.int32, (cc, 1), 0)
    blk = lambda s: (ri // s) == (ci_ // s)
    leaf = 8

    heads = []
    for ci in range(n_chunks):
        rows = slice(ci * cc, (ci + 1) * cc)
        gc_f = g_all[rows]
        gc_b = gc_f
        s = 1
        while s < cc:
            gc_f = gc_f + jnp.where(sub >= s, pltpu.roll(gc_f, s, axis=0), 0.0)
            gc_b = gc_b + jnp.where(sub < cc - s, pltpu.roll(gc_b, cc - s, axis=0), 0.0)
            s *= 2
        for h in range(nh):
            q = qkv_ref[rows, h * 128:(h + 1) * 128]
            k = qkv_ref[rows, (nh + h) * 128:(nh + h + 1) * 128]
            v = qkv_ref[rows, (2 * nh + h) * 128:(2 * nh + h + 1) * 128]
            heads.append((ci, rows, h, q, k, v, gc_f, gc_b))
    qkks = [lax.dot_general(jnp.concatenate([q, k], axis=0).astype(BF16), k.astype(BF16),
                            (((1,), (1,)), ((), ())), preferred_element_type=F32)
            for (_, _, _, q, k, _, _, _) in heads]
    per_group = PREP_GROUP * nh
    for g0 in range(0, len(heads), per_group):
        chains = []
        for (ci, rows, h, q, k, v, gc_f, gc_b), qkk in zip(heads[g0:g0 + per_group], qkks[g0:g0 + per_group]):
            qk, kk = qkk[:cc], qkk[cc:]
            for d in range(2):
                cb_, ca_ = d * nh + h, 2 * nh + d * nh + h
                beta = jnp.broadcast_to(beta_all[rows, cb_:cb_ + 1], (cc, DN_HEAD_DIM))
                g_col = jnp.broadcast_to((gc_f if d == 0 else gc_b)[:, ca_:ca_ + 1], (cc, DN_HEAD_DIM))
                lo = (ci % 2) * cc
                g_row = gt_tiles[ci // 2][d][ca_:ca_ + 1, lo:lo + cc]
                incl = (ri >= ci_) if d == 0 else (ri <= ci_)
                strict = (ri > ci_) if d == 0 else (ri < ci_)
                decay = jnp.exp(jnp.where(incl, g_col[:, :cc] - g_row, -1e30))
                a = jnp.where(strict, beta[:, :cc] * kk * decay, 0.0)
                eg = jnp.exp(g_col)
                g_last = g_col[cc - 1:cc] if d == 0 else g_col[0:1]
                chains.append(dict(ci=ci, rows=rows, h=h, d=d, a=a, qkd=qk * decay, g_last=g_last,
                                   rhs=jnp.concatenate([v * beta, k * (beta * eg)], axis=1),
                                   q_d=q * eg, k_d=k * jnp.exp(g_last - g_col)))

        pws = [jnp.where(blk(leaf), c["a"], 0.0) for c in chains]
        devs = [-pw for pw in pws]
        for _ in range(2):
            pws = [_bdot(pw, pw) for pw in pws]
            cross = [_bdot(dev, pw) for dev, pw in zip(devs, pws)]
            devs = [dev + pw + x for dev, pw, x in zip(devs, pws, cross)]
        s = leaf
        while s < cc:
            offs = [jnp.where(blk(2 * s) & jnp.logical_not(blk(s)), c["a"], 0.0) for c in chains]
            xs = [off + _bdot(dev, off) for dev, off in zip(devs, offs)]
            devs = [dev - x - _bdot(x, dev) for dev, x in zip(devs, xs)]
            s *= 2
        uws = [c["rhs"] + _bdot(dev, c["rhs"]) for dev, c in zip(devs, chains)]
        for c, uw in zip(chains, uws):
            d, h, ci = c["d"], c["h"], c["ci"]
            u0_ref[d, h, c["rows"], :] = uw[:, :128]
            lhs1_ref[d, h, ci] = jnp.concatenate([uw[:, 128:], c["q_d"]], axis=0).astype(BF16)
            lhs2_ref[d, h, ci] = jnp.concatenate([c["qkd"], c["k_d"].T], axis=0).astype(BF16)
            gl_ref[d, h, ci] = jnp.broadcast_to(jnp.exp(c["g_last"]), (8, DN_HEAD_DIM))


def delta_prep(qkv, small, small_t, prm, prm_t, b, n):
    nc = n // DN_CHUNK
    cb = min(PREP_CHUNKS, nc)
    nblk = nc // cb
    tt = cb * DN_CHUNK
    nh = DN_HEADS
    return pl.pallas_call(
        functools.partial(_delta_prep_kernel, n_chunks=cb),
        grid=(b, nblk),
        in_specs=[pl.BlockSpec((tt, 3 * DN_WIDTH), lambda bi, c: (bi * nblk + c, 0)),
                  pl.BlockSpec((tt, SMALL_WIDTH), lambda bi, c: (bi * nblk + c, 0)),
                  pl.BlockSpec((4 * nh, tt), lambda bi, c: (0, bi * nblk + c)),
                  pl.BlockSpec((2, SMALL_WIDTH), lambda bi, c: (0, 0)),
                  pl.BlockSpec((4 * nh, 2), lambda bi, c: (0, 0))],
        out_specs=[pl.BlockSpec((2, None, nh, tt, 128), lambda bi, c: (0, bi, 0, c, 0)),
                   pl.BlockSpec((2, None, nh, cb, 128, 128), lambda bi, c: (0, bi, 0, c, 0, 0)),
                   pl.BlockSpec((2, None, nh, cb, 192, 64), lambda bi, c: (0, bi, 0, c, 0, 0)),
                   pl.BlockSpec((2, None, nh, cb, 8, 128), lambda bi, c: (0, bi, 0, c, 0, 0))],
        out_shape=[jax.ShapeDtypeStruct((2, b, nh, n, 128), F32),
                   jax.ShapeDtypeStruct((2, b, nh, nc, 128, 128), BF16),
                   jax.ShapeDtypeStruct((2, b, nh, nc, 192, 64), BF16),
                   jax.ShapeDtypeStruct((2, b, nh, nc, 8, 128), F32)],
        compiler_params=_cparams("parallel", "parallel"),
        name="delta_prep",
    )(qkv, small, small_t, prm, prm_t)


def _delta_scan_kernel(u0f_ref, u0b_ref, l1f_ref, l1b_ref, l2f_ref, l2b_ref, glf_ref, glb_ref, s0_ref,
                       of_ref, ob_ref, sout_ref, st_ref, *, sc, nb):
    n = pl.program_id(0)
    cc = DN_CHUNK

    @pl.when(n == 0)
    def _():
        st_ref[...] = s0_ref[...]

    dirs = ((u0f_ref, l1f_ref, l2f_ref, glf_ref, of_ref), (u0b_ref, l1b_ref, l2b_ref, glb_ref, ob_ref))

    def body(i, carry):
        chains = []
        for d in range(2):
            ci = i if d == 0 else sc - 1 - i
            r0 = pl.multiple_of(ci * cc, cc)
            chains += [(d, bi, h, ci, r0) for bi in range(nb) for h in range(DN_HEADS)]
        sts = [st_ref[d, bi, h] for (d, bi, h, _, _) in chains]
        r1s = [jnp.dot(dirs[d][1][bi, h, ci], st.astype(BF16), preferred_element_type=F32)
               for (d, bi, h, ci, _), st in zip(chains, sts)]
        us = [dirs[d][0][bi, h, pl.ds(r0, cc), :] - r1[:cc] for (d, bi, h, _, r0), r1 in zip(chains, r1s)]
        r2s = [jnp.dot(dirs[d][2][bi, h, ci], u.astype(BF16), preferred_element_type=F32)
               for (d, bi, h, ci, _), u in zip(chains, us)]
        for (d, bi, h, ci, r0), st, r1, r2 in zip(chains, sts, r1s, r2s):
            dirs[d][4][bi, pl.ds(r0, cc), h * 128:(h + 1) * 128] = (r1[cc:] + r2[:cc]).astype(BF16)
            st_ref[d, bi, h] = st * dirs[d][3][bi, h, ci, 0:1, :] + r2[cc:]
        return carry

    lax.fori_loop(0, sc, body, 0)

    @pl.when(n == pl.num_programs(0) - 1)
    def _():
        sout_ref[...] = st_ref[...]


def delta_scan(u0, lhs1, lhs2, gl, s0, b, n):
    nc = n // DN_CHUNK
    sc = min(8, nc)
    nblk = nc // sc
    nh = DN_HEADS
    tt = sc * DN_CHUNK
    fwd = lambda i: i
    bwd = lambda i: nblk - 1 - i

    def specs(d, blk):
        return [pl.BlockSpec((None, b, nh, tt, 128), lambda i: (d, 0, 0, blk(i), 0)),
                pl.BlockSpec((None, b, nh, sc, 128, 128), lambda i: (d, 0, 0, blk(i), 0, 0)),
                pl.BlockSpec((None, b, nh, sc, 192, 64), lambda i: (d, 0, 0, blk(i), 0, 0)),
                pl.BlockSpec((None, b, nh, sc, 8, 128), lambda i: (d, 0, 0, blk(i), 0, 0))]

    sf, sb = specs(0, fwd), specs(1, bwd)
    in_specs = [sf[0], sb[0], sf[1], sb[1], sf[2], sb[2], sf[3], sb[3],
                pl.BlockSpec((2, b, nh, 128, 128), lambda i: (0, 0, 0, 0, 0))]
    return pl.pallas_call(
        functools.partial(_delta_scan_kernel, sc=sc, nb=b),
        grid=(nblk,),
        in_specs=in_specs,
        out_specs=[pl.BlockSpec((b, tt, DN_WIDTH), lambda i: (0, fwd(i), 0)),
                   pl.BlockSpec((b, tt, DN_WIDTH), lambda i: (0, bwd(i), 0)),
                   pl.BlockSpec((2, b, nh, 128, 128), lambda i: (0, 0, 0, 0, 0))],
        out_shape=[jax.ShapeDtypeStruct((b, n, DN_WIDTH), BF16), jax.ShapeDtypeStruct((b, n, DN_WIDTH), BF16),
                   jax.ShapeDtypeStruct((2, b, nh, 128, 128), F32)],
        scratch_shapes=[pltpu.VMEM((2, b, nh, 128, 128), F32)],
        compiler_params=_cparams("arbitrary"),
        name="delta_scan",
    )(u0, u0, lhs1, lhs1, lhs2, lhs2, gl, gl, s0)


def delta_branch(p, small, conv_w, a_log, dt_bias, s0, b, n, rows, cols):
    nh = DN_HEADS
    qkv = dn_conv_prep(p, conv_w, b, n, rows, cols)
    small_t = small[:, :4 * nh].T
    rate = jnp.concatenate([jnp.zeros((2 * nh,), F32), a_log.reshape(-1)])
    bias = jnp.concatenate([jnp.zeros((2 * nh,), F32), dt_bias.reshape(-1)])
    prm_t = jnp.stack([rate, bias], axis=1)
    prm = jnp.pad(prm_t.T, ((0, 0), (0, SMALL_WIDTH - 4 * nh)))
    u0, lhs1, lhs2, gl = delta_prep(qkv, small, small_t, prm, prm_t, b, n)
    o_f, o_b, s_out = delta_scan(u0, lhs1, lhs2, gl, s0, b, n)
    return o_f.reshape(b * n, DN_WIDTH), o_b.reshape(b * n, DN_WIDTH), s_out


def _cos_sin(rows, cols, period):
    ang = 2.0 * np.pi * ((np.arange(rows)[:, None] * np.arange(cols)[None, :]) % period) / period
    return np.cos(ang), np.sin(ang)


def _const_bf16(a):
    return jnp.asarray(a, F32).astype(BF16)


def _stage2_matrices(l2):
    c, s = _cos_sin(l2, l2, l2)
    fwd = np.block([[c, s], [-s, c]])
    inv = np.block([[c, -s], [s, c]])
    return _const_bf16(fwd), _const_bf16(inv)


def _twiddles(l1, l2, kb):
    ang = 2.0 * np.pi * ((np.arange(l2)[:, None] * np.arange(l1)[None, :]) % (l1 * l2)) / (l1 * l2)
    tw = np.stack([np.cos(ang), np.sin(ang)], axis=0).reshape(2, l2, l1 // kb, kb)
    return jnp.asarray(np.transpose(tw, (2, 0, 1, 3)), F32)


def _twiddle_mul(ar, ai, c, s, conj):
    if conj:
        return ar * c - ai * s, ai * c + ar * s
    return ar * c + ai * s, ai * c - ar * s


def _cplx_apply(m_ref, re, im):
    half = re.shape[0]
    out = jnp.dot(m_ref[...], jnp.concatenate([re, im], axis=0).astype(BF16), preferred_element_type=F32)
    return out[:half], out[half:]


def _fnet_mid_kernel(a_ref, tw_ref, m2_ref, cs_ref, o_ref, *, kb, ch):
    ts = [_twiddle_mul(a_ref[0, j], a_ref[1, j], tw_ref[0, :, j:j + 1], tw_ref[1, :, j:j + 1], False)
          for j in range(kb)]
    us = [_cplx_apply(m2_ref, tr, ti) for tr, ti in ts]
    ys = [jnp.dot(jnp.concatenate([ur, ui], axis=1).astype(BF16), cs_ref[...], preferred_element_type=F32)
          for ur, ui in us]
    for j, y in enumerate(ys):
        o_ref[:, j * ch:(j + 1) * ch] = y


def fnet_mid(a, tw, m2, cs, kb):
    bsz, _, l1, l2, ch = a.shape
    return pl.pallas_call(
        functools.partial(_fnet_mid_kernel, kb=kb, ch=ch),
        grid=(bsz, l1 // kb),
        in_specs=[pl.BlockSpec((None, 2, kb, l2, ch), lambda b, k: (b, 0, k, 0, 0)),
                  pl.BlockSpec((None, 2, l2, kb), lambda b, k: (k, 0, 0, 0)),
                  pl.BlockSpec(m2.shape, lambda b, k: (0, 0)),
                  pl.BlockSpec(cs.shape, lambda b, k: (0, 0))],
        out_specs=pl.BlockSpec((None, l2, kb * ch), lambda b, k: (b, 0, k)),
        out_shape=jax.ShapeDtypeStruct((bsz, l2, l1 * ch), F32),
        compiler_params=_cparams("parallel", "parallel"),
        name="fnet_mid",
    )(a, tw, m2, cs)


def _fnet_channel_matrix(n):
    c, s = _cos_sin(FN_GROUP_DIM, FN_GROUP_DIM, FN_GROUP_DIM)
    eye = np.eye(FN_GROUPS)
    scale = 1.0 / math.sqrt(n * FN_GROUP_DIM)
    return _const_bf16(np.concatenate([np.kron(eye, c), np.kron(eye, s)], axis=0) * scale)


FN_SUB = 8


def _fnet_long_kernel(x_ref, kin_ref, tw_ref, m2k_ref, cs_ref, o_ref, a_ref):
    n_sub = x_ref.shape[1] // FN_SUB
    rows, width = x_ref.shape[0] * FN_SUB, x_ref.shape[2]
    half = FN_SUB * FN_SUB
    for j in range(n_sub):
        r = jnp.dot(kin_ref[...], x_ref[:, j * FN_SUB:(j + 1) * FN_SUB, :].reshape(rows, width).astype(BF16),
                    preferred_element_type=F32)
        a_ref[0, j * half:(j + 1) * half, :] = r[:half]
        a_ref[1, j * half:(j + 1) * half, :] = r[half:]
    tr, ti = _twiddle_mul(a_ref[0], a_ref[1], tw_ref[0], tw_ref[1], False)
    ur, ui = _cplx_apply(m2k_ref, tr, ti)
    y = jnp.dot(jnp.concatenate([ur, ui], axis=1).astype(BF16), cs_ref[...], preferred_element_type=F32)
    o_ref[...] = y.reshape(o_ref.shape)


def fnet_long(p, b, n):
    ch, sub = FN_WIDTH, FN_SUB
    l1, l2 = n // 128, 128
    nblk = l1 // sub
    c1, s1 = _cos_sin(l1, l1, l1)
    m1 = np.stack([c1, -s1], axis=0).reshape(2, nblk, sub, l1)
    kin = np.einsum('pbkn,jq->bpjknq', m1, np.eye(sub)).reshape(nblk, 2 * sub * sub, l1 * sub)
    ang = 2.0 * np.pi * ((np.arange(l2)[:, None] * np.arange(l1)[None, :]) % n) / n
    tw = np.stack([np.cos(ang), np.sin(ang)], axis=0).reshape(2, l2, nblk, sub)
    tw = np.transpose(tw, (2, 0, 1, 3)).reshape(nblk, 2, l2 * sub, 1)
    c2, s2 = _cos_sin(l2, l2, l2)
    m2k = _const_bf16(np.kron(np.block([[c2, s2], [-s2, c2]]), np.eye(sub)))
    cs = _fnet_channel_matrix(n)
    const = lambda a: pl.BlockSpec(a.shape, lambda bi, k: (0,) * a.ndim, pipeline_mode=pl.Buffered(1))
    y = pl.pallas_call(
        _fnet_long_kernel,
        grid=(b, nblk),
        in_specs=[pl.BlockSpec((None, l1, l2, ch), lambda bi, k: (bi, 0, 0, P_FN // ch),
                               pipeline_mode=pl.Buffered(1)),
                  pl.BlockSpec((None,) + kin.shape[1:], lambda bi, k: (k, 0, 0)),
                  pl.BlockSpec((None, 2, l2 * sub, 1), lambda bi, k: (k, 0, 0, 0)),
                  const(m2k), const(cs)],
        out_specs=pl.BlockSpec((None, l2, sub, ch), lambda bi, k: (bi, 0, k, 0)),
        out_shape=jax.ShapeDtypeStruct((b, l2, l1, ch), F32),
        scratch_shapes=[pltpu.VMEM((2, l2 * sub, ch), F32)],
        compiler_params=_cparams("parallel", "parallel"),
        name="fnet_long",
    )(p.reshape(b, l1, l2, P_WIDTH), _const_bf16(kin), jnp.asarray(tw, F32), m2k, cs)
    return y.reshape(b * n, ch)


def fnet_branch(p, b, n):
    if n == 8192:
        return fnet_long(p, b, n)
    ch = FN_WIDTH
    xr = lax.slice_in_dim(p, P_FN, P_FN + ch, axis=1).reshape(b, 1, 1, n, ch)
    a = jnp.concatenate([xr, jnp.zeros_like(xr)], axis=1)
    y = fnet_mid(a, _twiddles(1, n, 1), _stage2_matrices(n)[0], _fnet_channel_matrix(n), 1)
    return y.reshape(b * n, ch)


def _seq_conv_kernel(prev_ref, cur_ref, next_ref, w_ref, *o_refs, n_tiles):
    t = pl.program_id(1)
    tt = cur_ref.shape[0]
    cur = cur_ref[...]
    row = lax.broadcasted_iota(jnp.int32, (tt, 1), 0)
    before = jnp.where(t == 0, 0.0, prev_ref[7:8, :])
    after = jnp.where(t == n_tiles - 1, 0.0, next_ref[0:1, :])
    left = jnp.where(row == 0, before, pltpu.roll(cur, 1, axis=0))
    right = jnp.where(row == tt - 1, after, pltpu.roll(cur, tt - 1, axis=0))
    y = w_ref[0:1, :] * left + w_ref[1:2, :] * cur + w_ref[2:3, :] * right
    for part, o_ref in enumerate(o_refs):
        o_ref[...] = y[:, part * HY_WIDTH:(part + 1) * HY_WIDTH]


def seq_conv(p, conv_w, b, n):
    tt = min(n, 1024)
    n_tiles = n // tt
    parts = HY_ORDER + 1
    w = parts * HY_WIDTH
    c0 = P_HY // w
    out_spec = pl.BlockSpec((tt, HY_WIDTH), lambda bi, t: (bi * n_tiles + t, 0))
    return pl.pallas_call(
        functools.partial(_seq_conv_kernel, n_tiles=n_tiles),
        grid=(b, n_tiles),
        in_specs=[pl.BlockSpec((8, w), lambda bi, t: (jnp.maximum((bi * n_tiles + t) * (tt // 8) - 1, 0), c0)),
                  pl.BlockSpec((tt, w), lambda bi, t: (bi * n_tiles + t, c0)),
                  pl.BlockSpec((8, w), lambda bi, t: (jnp.minimum((bi * n_tiles + t + 1) * (tt // 8),
                                                                  b * n // 8 - 1), c0)),
                  pl.BlockSpec((SHORT_CONV, w), lambda bi, t: (0, 0))],
        out_specs=[out_spec] * parts,
        out_shape=[jax.ShapeDtypeStruct((b * n, HY_WIDTH), F32)] * parts,
        compiler_params=_cparams("parallel", "parallel"),
        name="hy_seq_conv",
    )(p, p, p, conv_w)


def _hdot(a, b):
    return jnp.dot(a, b, preferred_element_type=F32, precision=lax.Precision.HIGHEST)


def _hy_filter_kernel(ft_ref, t_ref, w1t_ref, b1_ref, f1_ref, w2t_ref, b2_ref, f2_ref, w3_ref, dl_ref, k_ref, s_ref,
                      *, n, tr):
    i = pl.program_id(0)
    hid = jnp.sin(f1_ref[...] * (_hdot(w1t_ref[...], ft_ref[...]) + b1_ref[...]))
    hid = jnp.sin(f2_ref[...] * (_hdot(w2t_ref[...], hid) + b2_ref[...]))
    filt = _hdot(hid.T, w3_ref[...]) * jnp.exp(-t_ref[...] * dl_ref[...])
    row = i * tr + lax.broadcasted_iota(jnp.int32, (tr, 1), 0)
    filt = jnp.where(row == n, 0.0, filt)
    k_ref[...] = filt

    @pl.when(i == 0)
    def _():
        s_ref[...] = jnp.zeros_like(s_ref)

    s_ref[...] += jnp.sum(jnp.abs(filt), axis=0, keepdims=True)


def hy_filter(n, w1, b1, freq1, w2, b2, freq2, w3):
    pos = jnp.arange(n, dtype=F32)
    t = pos / max(n - 1, 1)
    bands = jnp.linspace(1e-4, HY_BANDS - 1, HY_BANDS, dtype=F32)
    ang = (2.0 * math.pi / n) * pos[:, None] * bands[None, :]
    feats = jnp.concatenate([t[:, None], jnp.cos(ang), -jnp.sin(ang)], axis=-1)
    feats2 = jnp.concatenate([feats, feats[:1], feats[:0:-1]], axis=0)
    kpad = 128
    feats_t = jnp.pad(feats2, ((0, 0), (0, kpad - HY_EMB_DIM))).T
    w1t = jnp.pad(w1, ((0, kpad - HY_EMB_DIM), (0, 0))).T
    min_decay = math.log(HY_DECAY_TARGET) / HY_SLOW_DECAY_PCT
    max_decay = math.log(HY_DECAY_TARGET) / HY_FAST_DECAY_PCT
    cw = HY_ORDER * HY_WIDTH
    deltas = jnp.abs(jnp.linspace(min_decay, max_decay, cw, dtype=F32)).reshape(1, cw)
    tr = min(n, 1024)
    half = n // tr
    hd = HY_FILTER_HIDDEN
    vec = lambda v: v.reshape(hd, 1)
    full = lambda shp: pl.BlockSpec(shp, lambda i: (0, 0))
    return pl.pallas_call(
        functools.partial(_hy_filter_kernel, n=n, tr=tr),
        grid=(2 * half,),
        in_specs=[pl.BlockSpec((kpad, tr), lambda i: (0, i)), pl.BlockSpec((tr, 1), lambda i: (i, 0)),
                  full((hd, kpad)), full((hd, 1)), full((hd, 1)), full((hd, hd)), full((hd, 1)), full((hd, 1)),
                  pl.BlockSpec((hd, cw), lambda i: (0, i // half)), full((1, cw))],
        out_specs=[pl.BlockSpec((tr, cw), lambda i: (i, 0)), full((1, cw))],
        out_shape=[jax.ShapeDtypeStruct((2 * n, cw), F32), jax.ShapeDtypeStruct((1, cw), F32)],
        compiler_params=_cparams("arbitrary"),
        name="hy_filter",
    )(feats_t, feats2[:, 0:1], w1t, vec(b1), vec(freq1), w2.T, vec(b2), vec(freq2), w3, deltas)


def _hy_spec_kernel(a_ref, tw_ref, m2_ref, s_ref, o_ref, *, kb):
    inv = 1.0 / s_ref[...]
    ts = [_twiddle_mul(a_ref[0, j], a_ref[1, j], tw_ref[0, :, j:j + 1], tw_ref[1, :, j:j + 1], False)
          for j in range(kb)]
    xs = [_cplx_apply(m2_ref, tr, ti) for tr, ti in ts]
    for j, (xr, xi) in enumerate(xs):
        o_ref[0, j] = xr * inv
        o_ref[1, j] = xi * inv


def hy_spec(a, tw, m2, abs_sum, kb):
    _, l1, l2, cw = a.shape
    w = HY_WIDTH
    return pl.pallas_call(
        functools.partial(_hy_spec_kernel, kb=kb),
        grid=(l1 // kb, cw // w),
        in_specs=[pl.BlockSpec((2, kb, l2, w), lambda k, c: (0, k, 0, c)),
                  pl.BlockSpec((None, 2, l2, kb), lambda k, c: (k, 0, 0, 0)),
                  pl.BlockSpec(m2.shape, lambda k, c: (0, 0)),
                  pl.BlockSpec((1, w), lambda k, c: (0, c))],
        out_specs=pl.BlockSpec((2, kb, l2, w), lambda k, c: (0, k, 0, c)),
        out_shape=jax.ShapeDtypeStruct((2, l1, l2, cw), F32),
        compiler_params=_cparams("parallel", "parallel"),
        name="hy_spec",
    )(a, tw, m2, abs_sum)


def _hy_mid_kernel(a_ref, tw_ref, m2_ref, m2c_ref, kf_ref, o_ref, *, kb):
    cs = [(tw_ref[0, :, j:j + 1], tw_ref[1, :, j:j + 1]) for j in range(kb)]
    ts = [_twiddle_mul(a_ref[0, j], a_ref[1, j], c, s, False) for j, (c, s) in enumerate(cs)]
    xs = [_cplx_apply(m2_ref, tr, ti) for tr, ti in ts]
    ps = [(xr * kf_ref[0, j] - xi * kf_ref[1, j], xr * kf_ref[1, j] + xi * kf_ref[0, j])
          for j, (xr, xi) in enumerate(xs)]
    bs = [_cplx_apply(m2c_ref, pr, pi) for pr, pi in ps]
    for j, ((br, bi), (c, s)) in enumerate(zip(bs, cs)):
        o_ref[0, j], o_ref[1, j] = _twiddle_mul(br, bi, c, s, True)


def hy_mid(a, tw, m2, m2c, kf, order, kb):
    bsz, _, l1, l2, w = a.shape
    return pl.pallas_call(
        functools.partial(_hy_mid_kernel, kb=kb),
        grid=(bsz, l1 // kb),
        in_specs=[pl.BlockSpec((None, 2, kb, l2, w), lambda b, k: (b, 0, k, 0, 0)),
                  pl.BlockSpec((None, 2, l2, kb), lambda b, k: (k, 0, 0, 0)),
                  pl.BlockSpec(m2.shape, lambda b, k: (0, 0)),
                  pl.BlockSpec(m2c.shape, lambda b, k: (0, 0)),
                  pl.BlockSpec((2, kb, l2, w), lambda b, k: (0, k, 0, order))],
        out_specs=pl.BlockSpec((None, 2, kb, l2, w), lambda b, k: (b, 0, k, 0, 0)),
        out_shape=jax.ShapeDtypeStruct((bsz, 2, l1, l2, w), F32),
        compiler_params=_cparams("parallel", "parallel"),
        name="hy_mid",
    )(a, tw, m2, m2c, kf)


def _hy_out_kernel(m_ref, bp_ref, xo_ref, z_ref, bias_ref, o_ref):
    y = jnp.dot(m_ref[...], bp_ref[...].astype(BF16), preferred_element_type=F32)
    o_ref[...] = xo_ref[...] * (y + bias_ref[...] * z_ref[...])


def hy_out(m, bp, xo, z, bias_row, tn):
    bsz, k, n = bp.shape
    r = m.shape[0]
    blk = pl.BlockSpec((None, r, tn), lambda b, j: (b, 0, j))
    return pl.pallas_call(
        _hy_out_kernel,
        grid=(bsz, n // tn),
        in_specs=[pl.BlockSpec((r, k), lambda b, j: (0, 0)),
                  pl.BlockSpec((None, k, tn), lambda b, j: (b, 0, j)), blk, blk,
                  pl.BlockSpec((1, tn), lambda b, j: (0, j))],
        out_specs=blk,
        out_shape=jax.ShapeDtypeStruct((bsz, r, n), F32),
        compiler_params=_cparams("parallel", "parallel"),
        name="hy_out",
    )(m, bp, xo, z, bias_row)


SUB = 8
HY_KB = 16


def _half_spectrum_blocks(l1, kb):
    return -(-(l1 // 2 + 1) // kb)


def _kron_stage1(l1, n1_used, kb):
    c, s = _cos_sin(l1, n1_used, l1)
    m = np.stack([c, -s], axis=0).reshape(2, l1 // kb, kb, n1_used)
    m = np.transpose(m, (1, 0, 2, 3)).reshape(l1 // kb, 2 * kb, n1_used)[:_half_spectrum_blocks(l1, kb)]
    return _const_bf16(np.stack([np.kron(blk, np.eye(SUB)) for blk in m]))


def _kron_stage_out(l1, n1_used, kb):
    c, s = _cos_sin(n1_used, l1, l1)
    k1 = np.arange(l1)
    mult = np.where((k1 == 0) | (k1 == l1 // 2), 1.0, np.where(k1 < l1 // 2, 2.0, 0.0))
    m = np.stack([c * mult, -s * mult], axis=1).reshape(n1_used, 2, l1 // kb, kb) / (l1 * l1)
    m = np.transpose(m, (2, 0, 1, 3)).reshape(l1 // kb, n1_used, 2 * kb)[:_half_spectrum_blocks(l1, kb)]
    return _const_bf16(np.stack([np.kron(blk, np.eye(SUB)) for blk in m]))


def _strided_stage_in(kin_ref, src_ref, a_ref):
    n_sub, width = src_ref.shape[1] // SUB, src_ref.shape[2]
    rows = src_ref.shape[0] * SUB
    for j in range(n_sub):
        r = jnp.dot(kin_ref[...], src_ref[:, j * SUB:(j + 1) * SUB, :].reshape(rows, width).astype(BF16),
                    preferred_element_type=F32)
        a_ref[:, j * SUB:(j + 1) * SUB, :] = r.reshape(a_ref.shape[0], SUB, width)


def _hy_conv_kernel(z_ref, xo_ref, kin_ref, tw_ref, m2_ref, m2c_ref, kf_ref, kout_ref, bias_ref, o_ref,
                    a_ref, b_ref, *, kb):
    k = pl.program_id(1)

    @pl.when(k == 0)
    def _():
        o_ref[...] = jnp.zeros_like(o_ref)

    _strided_stage_in(kin_ref, z_ref, a_ref)
    grp = kb
    for g0 in range(0, kb, grp):
        js = range(g0, g0 + grp)
        cs = [(tw_ref[0, :, j:j + 1], tw_ref[1, :, j:j + 1]) for j in js]
        ts = [_twiddle_mul(a_ref[j], a_ref[kb + j], c, s, False) for j, (c, s) in zip(js, cs)]
        xs = [_cplx_apply(m2_ref, tr, ti) for tr, ti in ts]
        ps = [(xr * kf_ref[0, j] - xi * kf_ref[1, j], xr * kf_ref[1, j] + xi * kf_ref[0, j])
              for j, (xr, xi) in zip(js, xs)]
        bs = [_cplx_apply(m2c_ref, pr, pi) for pr, pi in ps]
        for j, (br, bi), (c, s) in zip(js, bs, cs):
            b_ref[j], b_ref[kb + j] = _twiddle_mul(br, bi, c, s, True)
    n_sub, width = o_ref.shape[1] // SUB, o_ref.shape[2]
    for j in range(n_sub):
        r = jnp.dot(kout_ref[...], b_ref[:, j * SUB:(j + 1) * SUB, :].reshape(2 * kb * SUB, width).astype(BF16),
                    preferred_element_type=F32)
        o_ref[:, j * SUB:(j + 1) * SUB, :] += r.reshape(o_ref.shape[0], SUB, width)

    @pl.when(k == pl.num_programs(1) - 1)
    def _():
        o_ref[...] = xo_ref[...] * (o_ref[...] + bias_ref[...] * z_ref[...])


def hy_conv_long(z, xo, kf, bias, order, b, n):
    w, kb = HY_WIDTH, HY_KB
    l1 = l2 = 128
    n1 = n // l2
    nblk = _half_spectrum_blocks(l1, kb)
    tw = _twiddles(l1, l2, kb)[:nblk]
    m2, m2c = _stage2_matrices(l2)
    kin, kout = _kron_stage1(l1, n1, kb), _kron_stage_out(l1, n1, kb)
    tok = pl.BlockSpec((None, n1, l2, w), lambda bi, k: (bi, 0, 0, 0), pipeline_mode=pl.Buffered(1))
    const = lambda a: pl.BlockSpec(a.shape, lambda bi, k: (0,) * a.ndim)
    out = pl.pallas_call(
        functools.partial(_hy_conv_kernel, kb=kb),
        grid=(b, nblk),
        in_specs=[tok, tok,
                  pl.BlockSpec((None,) + kin.shape[1:], lambda bi, k: (k, 0, 0)),
                  pl.BlockSpec((None, 2, l2, kb), lambda bi, k: (k, 0, 0, 0)),
                  const(m2), const(m2c),
                  pl.BlockSpec((2, kb, l2, w), lambda bi, k: (0, k, 0, order)),
                  pl.BlockSpec((None,) + kout.shape[1:], lambda bi, k: (k, 0, 0)),
                  pl.BlockSpec((1, w), lambda bi, k: (0, 0))],
        out_specs=pl.BlockSpec((None, n1, l2, w), lambda bi, k: (bi, 0, 0, 0), pipeline_mode=pl.Buffered(1)),
        out_shape=jax.ShapeDtypeStruct((b, n1, l2, w), F32),
        scratch_shapes=[pltpu.VMEM((2 * kb, l2, w), F32), pltpu.VMEM((2 * kb, l2, w), F32)],
        compiler_params=_cparams("parallel", "arbitrary"),
        name="hy_conv_long",
    )(z.reshape(b, n1, l2, w), xo.reshape(b, n1, l2, w), kin, tw, m2, m2c, kf, kout, bias.reshape(1, w))
    return out.reshape(b * n, w)


def _hy_spec_long_kernel(kern_ref, kin_ref, tw_ref, m2_ref, s_ref, o_ref, a_ref, *, kb):
    _strided_stage_in(kin_ref, kern_ref, a_ref)
    inv = 1.0 / s_ref[...]
    ts = [_twiddle_mul(a_ref[j], a_ref[kb + j], tw_ref[0, :, j:j + 1], tw_ref[1, :, j:j + 1], False)
          for j in range(kb)]
    xs = [_cplx_apply(m2_ref, tr, ti) for tr, ti in ts]
    for j, (xr, xi) in enumerate(xs):
        o_ref[0, j] = xr * inv
        o_ref[1, j] = xi * inv


def hy_spec_long(kern, abs_sum):
    w, kb = HY_WIDTH, HY_KB
    l1 = l2 = 128
    cw = kern.shape[1]
    nblk = _half_spectrum_blocks(l1, kb)
    tw = _twiddles(l1, l2, kb)[:nblk]
    m2, _ = _stage2_matrices(l2)
    kin = _kron_stage1(l1, l1, kb)
    return pl.pallas_call(
        functools.partial(_hy_spec_long_kernel, kb=kb),
        grid=(cw // w, nblk),
        in_specs=[pl.BlockSpec((l1, l2, w), lambda c, k: (0, 0, c), pipeline_mode=pl.Buffered(1)),
                  pl.BlockSpec((None,) + kin.shape[1:], lambda c, k: (k, 0, 0)),
                  pl.BlockSpec((None, 2, l2, kb), lambda c, k: (k, 0, 0, 0)),
                  pl.BlockSpec(m2.shape, lambda c, k: (0, 0)),
                  pl.BlockSpec((1, w), lambda c, k: (0, c))],
        out_specs=pl.BlockSpec((2, kb, l2, w), lambda c, k: (0, k, 0, c)),
        out_shape=jax.ShapeDtypeStruct((2, nblk * kb, l2, cw), F32),
        scratch_shapes=[pltpu.VMEM((2 * kb, l2, w), F32)],
        compiler_params=_cparams("parallel", "arbitrary"),
        name="hy_spec_long",
    )(kern.reshape(l1, l2, cw), kin, tw, m2, abs_sum)


def hyena_pallas(p, conv_w, w1, b1, freq1, w2, b2, freq2, w3, bias, b, n):
    w = HY_WIDTH
    x0, x1, v = seq_conv(p, conv_w, b, n)
    kern, abs_sum = hy_filter(n, w1, b1, freq1, w2, b2, freq2, w3)
    z = v
    if n == 8192:
        kf = hy_spec_long(kern, abs_sum)
        for order, xo in enumerate((x0, x1)):
            z = hy_conv_long(z, xo, kf, bias[order], order, b, n)
        return z
    l2 = 2 * n
    tw = _twiddles(1, l2, 1)
    m2, m2c = _stage2_matrices(l2)
    ak = jnp.stack([kern, jnp.zeros_like(kern)], axis=0).reshape(2, 1, l2, HY_ORDER * w)
    m_out = _const_bf16(np.eye(n, 2 * l2) / l2)
    kf = hy_spec(ak, tw, m2, abs_sum, 1)
    for order, xo in enumerate((x0, x1)):
        zp = jnp.pad(z.reshape(b, 1, 1, n, w), ((0, 0), (0, 0), (0, 0), (0, n), (0, 0)))
        a = jnp.concatenate([zp, jnp.zeros_like(zp)], axis=1)
        bp = hy_mid(a, tw, m2, m2c, kf, order, 1)
        z = hy_out(m_out, bp.reshape(b, 2 * l2, w), xo.reshape(b, n, w), z.reshape(b, n, w),
                   bias[order].reshape(1, w), w).reshape(b * n, w)
    return z


def _prep_w_in(w):
    main = jnp.concatenate([w[:, OFF_Q:OFF_Z], w[:, OFF_HY:OFF_GATE], w[:, OFF_FN:OFF_HY], w[:, OFF_Z:OFF_BETA]],
                           axis=1).astype(BF16)
    small = jnp.pad(w[:, OFF_BETA:OFF_FN], ((0, 0), (0, SMALL_WIDTH - 4 * DN_HEADS))).astype(BF16)
    return main, small, w[:, OFF_GATE:IN_WIDTH].astype(BF16)


def kernel(x, c, ctx, c_ctx, w_mod, b_mod, norm1, norm2, w_in, dn_conv, dn_a_log, dn_dt_bias,
           dn_out_norm, hy_conv, hy_w1, hy_b1, hy_freq1, hy_w2, hy_b2, hy_freq2, hy_w3, hy_bias,
           w_branch_a, w_branch_b, w_branch_c, w_out, w_ff1, w_ff2, final_norm):
    b, n_lat, d = x.shape
    rows = n_lat // GRID_W
    n_ctx = ctx.shape[1]
    tm_x, tm_c = 1024, n_ctx

    c_rows = jnp.concatenate([c, c_ctx[None], jnp.zeros((8 - b - 1, d), F32)], axis=0)
    mods = mod_vectors(c_rows, w_mod, b_mod)
    s_zero = jnp.zeros((2, b, DN_HEADS, DN_HEAD_DIM, DN_HEAD_DIM), F32)
    h, hc = x.reshape(b * n_lat, d), ctx.reshape(b * n_ctx, d)

    for l in range(DEPTH):
        last = l == DEPTH - 1
        mv = mods[l].reshape(8, N_MOD, 1, d)
        mx = [mv[:b, i] for i in range(N_MOD)]
        mc = [mv[b:b + 1, i] for i in range(N_MOD)]
        w_main, w_small, w_gate = _prep_w_in(w_in[l])
        wa, wb, wc, wo = (w.astype(BF16) for w in (w_branch_a[l], w_branch_b[l], w_branch_c[l], w_out[l]))
        w1, w2 = w_ff1[l].astype(BF16), w_ff2[l].astype(BF16)
        n1, n2 = norm1[l][None, None, :], norm2[l][None, None, :]

        p_c, small_c = in_proj(hc, n1 * (1.0 + mc[1]), mc[0], w_main, w_small, b * n_ctx, tm_c)
        p_x, small_x = in_proj(h, n1 * (1.0 + mx[1]), mx[0], w_main, w_small, n_lat, tm_x)

        def mix(p, n):
            y_c = hyena_pallas(p, hy_conv[l], hy_w1[l], hy_b1[l], hy_freq1[l], hy_w2[l], hy_b2[l], hy_freq2[l],
                               hy_w3[l], hy_bias[l], b, n)
            return fnet_branch(p, b, n), y_c

        ocf, ocb, s_ctx = delta_branch(p_c, small_c, dn_conv[l], dn_a_log[l], dn_dt_bias[l], s_zero,
                                       b, n_ctx, 1, n_ctx)
        oxf, oxb, _ = delta_branch(p_x, small_x, dn_conv[l], dn_a_log[l], dn_dt_bias[l], s_ctx,
                                   b, n_lat, rows, GRID_W)

        y_b, y_c = mix(p_x, n_lat)
        h = merge(oxf, oxb, p_x, y_b, y_c, h, n1 * (1.0 + mx[1]), mx[0], mx[2], dn_out_norm[l],
                  w_gate, wa, wb, wc, wo, n_lat, 512)
        h = mlp(h, n2 * (1.0 + mx[4]), mx[3], mx[5], w1, w2, n_lat, tm_x, final_norm if last else None)

        if not last:
            y_b, y_c = mix(p_c, n_ctx)
            hc = merge(ocf, ocb, p_c, y_b, y_c, hc, n1 * (1.0 + mc[1]), mc[0], mc[2], dn_out_norm[l],
                       w_gate, wa, wb, wc, wo, b * n_ctx, tm_c)
            hc = mlp(hc, n2 * (1.0 + mc[4]), mc[3], mc[5], w1, w2, b * n_ctx, tm_c)

    return h.reshape(b, n_lat, d)
```

```python
import functools
import math

import jax
import jax.numpy as jnp
import numpy as np
from jax import lax
from jax.experimental import pallas as pl
from jax.experimental.pallas import tpu as pltpu

D_MODEL = 1024
DEPTH = 2
GRID_W = 64
NORM_EPS = 1e-6
N_MOD = 6

DN_HEADS = 4
DN_HEAD_DIM = 128
DN_WIDTH = DN_HEADS * DN_HEAD_DIM
DN_CHUNK = 64
SHORT_CONV = 3

FN_GROUPS = 4
FN_GROUP_DIM = 64
FN_WIDTH = FN_GROUPS * FN_GROUP_DIM

HY_WIDTH = 256
HY_ORDER = 2
HY_EMB_DIM = 33
HY_BANDS = (HY_EMB_DIM - 1) // 2
HY_FILTER_HIDDEN = 64
HY_FAST_DECAY_PCT = 0.3
HY_SLOW_DECAY_PCT = 1.5
HY_DECAY_TARGET = 1e-2

N_BRANCHES = 3
D_FF = 4 * D_MODEL

OFF_Q = 0
OFF_Z = 3 * DN_WIDTH
OFF_BETA = OFF_Z + DN_WIDTH
OFF_A = OFF_BETA + 2 * DN_HEADS
OFF_FN = OFF_A + 2 * DN_HEADS
OFF_HY = OFF_FN + FN_WIDTH
OFF_GATE = OFF_HY + (HY_ORDER + 1) * HY_WIDTH
IN_WIDTH = OFF_GATE + N_BRANCHES * D_MODEL

P_QKV = 0
P_HY = P_QKV + 3 * DN_WIDTH
P_FN = P_HY + (HY_ORDER + 1) * HY_WIDTH
P_Z = P_FN + FN_WIDTH
P_WIDTH = P_Z + DN_WIDTH
SMALL_WIDTH = 128

F32 = jnp.float32
BF16 = jnp.bfloat16
VMEM_LIMIT = 56 * 1024 * 1024


def _cparams(*sem):
    return pltpu.CompilerParams(dimension_semantics=sem, vmem_limit_bytes=VMEM_LIMIT)


def _bdot(a, b):
    return jnp.dot(a.astype(BF16), b.astype(BF16), preferred_element_type=F32)


def _sigmoid(x):
    return 0.5 * jnp.tanh(0.5 * x) + 0.5


def _modnorm(xf, gs, sh):
    r = lax.rsqrt(jnp.mean(xf * xf, axis=-1, keepdims=True) + NORM_EPS)
    return xf * r * gs + sh


def _mod_kernel(c_ref, w_ref, b_ref, o_ref):
    c = c_ref[...]
    o_ref[...] = _bdot(c * _sigmoid(c), w_ref[...]) + b_ref[...]


def mod_vectors(c_rows, w_mod, b_mod):
    tn = 1536
    n = N_MOD * D_MODEL
    return pl.pallas_call(
        _mod_kernel,
        grid=(DEPTH, n // tn),
        in_specs=[pl.BlockSpec((8, D_MODEL), lambda l, j: (0, 0)),
                  pl.BlockSpec((None, D_MODEL, tn), lambda l, j: (l, 0, j)),
                  pl.BlockSpec((None, 1, tn), lambda l, j: (l, 0, j))],
        out_specs=pl.BlockSpec((None, 8, tn), lambda l, j: (l, 0, j)),
        out_shape=jax.ShapeDtypeStruct((DEPTH, 8, n), F32),
        compiler_params=_cparams("parallel", "parallel"),
        name="mod_vectors",
    )(c_rows, w_mod, b_mod.reshape(DEPTH, 1, n))


def _in_proj_kernel(x_ref, gs_ref, sh_ref, w_ref, ws_ref, p_ref, small_ref, *, tn):
    xn = _modnorm(x_ref[...], gs_ref[0], sh_ref[0]).astype(BF16)
    small_ref[...] = jnp.dot(xn, ws_ref[...], preferred_element_type=F32)
    for j in range(p_ref.shape[1] // tn):
        p_ref[:, j * tn:(j + 1) * tn] = jnp.dot(xn, w_ref[:, j * tn:(j + 1) * tn], preferred_element_type=F32)


def in_proj(x2d, gs, sh, w_main, w_small, rows_per_mod, tm):
    m = x2d.shape[0]
    tiles_per_mod = rows_per_mod // tm
    mod_spec = pl.BlockSpec((1, 1, D_MODEL), lambda i: (i // tiles_per_mod, 0, 0))
    resident = lambda a: pl.BlockSpec(a.shape, lambda i: (0, 0), pipeline_mode=pl.Buffered(1))
    return pl.pallas_call(
        functools.partial(_in_proj_kernel, tn=1024),
        grid=(m // tm,),
        in_specs=[pl.BlockSpec((tm, D_MODEL), lambda i: (i, 0)), mod_spec, mod_spec,
                  resident(w_main), resident(w_small)],
        out_specs=[pl.BlockSpec((tm, P_WIDTH), lambda i: (i, 0)),
                   pl.BlockSpec((tm, SMALL_WIDTH), lambda i: (i, 0))],
        out_shape=[jax.ShapeDtypeStruct((m, P_WIDTH), F32), jax.ShapeDtypeStruct((m, SMALL_WIDTH), F32)],
        compiler_params=_cparams("parallel"),
        name="in_proj",
    )(x2d, gs, sh, w_main, w_small)


def _merge_kernel(of_ref, ob_ref, z_ref, yb_ref, yc_ref, h_ref, gs_ref, sh_ref, gate_ref, nrm_ref,
                  wg_ref, wa_ref, wb_ref, wc_ref, wo_ref, out_ref):
    xn = _modnorm(h_ref[...], gs_ref[0], sh_ref[0]).astype(BF16)
    o = of_ref[...].astype(F32) + ob_ref[...].astype(F32)
    z = z_ref[...]
    heads = []
    for hd in range(DN_HEADS):
        sl = slice(hd * DN_HEAD_DIM, (hd + 1) * DN_HEAD_DIM)
        oh, zh = o[:, sl], z[:, sl]
        r = lax.rsqrt(jnp.mean(oh * oh, axis=-1, keepdims=True) + NORM_EPS)
        heads.append(oh * r * nrm_ref[...] * (zh * _sigmoid(zh)))
    ya = jnp.concatenate(heads, axis=-1)
    merged = None
    for i, (y, w_ref) in enumerate(((ya, wa_ref), (yb_ref[...], wb_ref), (yc_ref[...], wc_ref))):
        g = jnp.dot(xn, wg_ref[:, i * D_MODEL:(i + 1) * D_MODEL], preferred_element_type=F32)
        term = _sigmoid(g) * _bdot(y, w_ref[...])
        merged = term if merged is None else merged + term
    out_ref[...] = h_ref[...] + gate_ref[0] * _bdot(merged, wo_ref[...])


def merge(o_f, o_b, p, y_b, y_c, h2d, gs, sh, gate, dn_out_norm, wg, wa, wb, wc, wo, rows_per_mod, tm):
    m = h2d.shape[0]
    tiles_per_mod = rows_per_mod // tm
    row = lambda w: pl.BlockSpec((tm, w), lambda i: (i, 0))
    full = lambda a: pl.BlockSpec(a.shape, lambda i: (0,) * a.ndim, pipeline_mode=pl.Buffered(1))
    mod_spec = pl.BlockSpec((1, 1, D_MODEL), lambda i: (i // tiles_per_mod, 0, 0))
    nrm = dn_out_norm.reshape(1, DN_HEAD_DIM)
    return pl.pallas_call(
        _merge_kernel,
        grid=(m // tm,),
        in_specs=[row(DN_WIDTH), row(DN_WIDTH),
                  pl.BlockSpec((tm, DN_WIDTH), lambda i: (i, P_Z // DN_WIDTH)),
                  row(FN_WIDTH), row(HY_WIDTH), row(D_MODEL), mod_spec, mod_spec, mod_spec,
                  full(nrm), full(wg), full(wa), full(wb), full(wc), full(wo)],
        out_specs=row(D_MODEL),
        out_shape=jax.ShapeDtypeStruct((m, D_MODEL), F32),
        compiler_params=_cparams("parallel"),
        name="merge",
    )(o_f, o_b, p, y_b, y_c, h2d, gs, sh, gate, nrm, wg, wa, wb, wc, wo)


def _mlp_kernel(h_ref, gs_ref, sh_ref, gate_ref, w1_ref, w2_ref, fin_ref, out_ref, acc_ref, *, final, tf):
    xn = _modnorm(h_ref[...], gs_ref[0], sh_ref[0]).astype(BF16)
    for j in range(w1_ref.shape[1] // tf):
        a = jnp.maximum(jnp.dot(xn, w1_ref[:, j * tf:(j + 1) * tf], preferred_element_type=F32), 0.0)
        part = jnp.dot((a * a).astype(BF16), w2_ref[j * tf:(j + 1) * tf, :], preferred_element_type=F32)
        if j == 0:
            acc_ref[...] = part
        else:
            acc_ref[...] += part
    y = h_ref[...] + gate_ref[0] * acc_ref[...]
    if final:
        y = y * lax.rsqrt(jnp.mean(y * y, axis=-1, keepdims=True) + NORM_EPS) * fin_ref[...]
    out_ref[...] = y


def mlp(h2d, gs, sh, gate, w1, w2, rows_per_mod, tm, final_gain=None):
    m = h2d.shape[0]
    tiles_per_mod = rows_per_mod // tm
    mod_spec = pl.BlockSpec((1, 1, D_MODEL), lambda i: (i // tiles_per_mod, 0, 0))
    resident = lambda a: pl.BlockSpec(a.shape, lambda i: (0, 0), pipeline_mode=pl.Buffered(1))
    final = final_gain is not None
    fin = (final_gain if final else jnp.ones((D_MODEL,), F32)).reshape(1, D_MODEL)
    return pl.pallas_call(
        functools.partial(_mlp_kernel, final=final, tf=1024),
        grid=(m // tm,),
        in_specs=[pl.BlockSpec((tm, D_MODEL), lambda i: (i, 0)), mod_spec, mod_spec, mod_spec,
                  resident(w1), resident(w2), pl.BlockSpec((1, D_MODEL), lambda i: (0, 0))],
        out_specs=pl.BlockSpec((tm, D_MODEL), lambda i: (i, 0)),
        out_shape=jax.ShapeDtypeStruct((m, D_MODEL), F32),
        scratch_shapes=[pltpu.VMEM((tm, D_MODEL), F32)],
        compiler_params=_cparams("parallel"),
        name="mlp",
    )(h2d, gs, sh, gate, w1, w2, fin)


def _dnconv_kernel(prev_ref, cur_ref, next_ref, w_ref, o_ref, *, cols, n_tiles):
    t, j = pl.program_id(1), pl.program_id(2)
    tt = cur_ref.shape[0]
    prev = jnp.where(t == 0, 0.0, prev_ref[...])
    nxt = jnp.where(t == n_tiles - 1, 0.0, next_ref[...])
    ext = jnp.concatenate([prev, cur_ref[...], nxt], axis=0)
    n_ext = tt + 2 * cols
    col = lax.broadcasted_iota(jnp.int32, (n_ext, 1), 0) % cols
    left = jnp.where(col == 0, 0.0, pltpu.roll(ext, 1, axis=0))
    right = jnp.where(col == cols - 1, 0.0, pltpu.roll(ext, n_ext - 1, axis=0))
    acc = jnp.zeros((tt, DN_WIDTH), F32)
    for dr in range(SHORT_CONV):
        base = dr * cols
        acc = (acc + w_ref[3 * dr:3 * dr + 1, :] * left[base:base + tt]
               + w_ref[3 * dr + 1:3 * dr + 2, :] * ext[base:base + tt]
               + w_ref[3 * dr + 2:3 * dr + 3, :] * right[base:base + tt])
    y = acc * _sigmoid(acc)
    q_scale = jnp.where(j == 0, DN_HEAD_DIM ** -0.5, 1.0)
    for hd in range(DN_HEADS):
        sl = slice(hd * DN_HEAD_DIM, (hd + 1) * DN_HEAD_DIM)
        yh = y[:, sl]
        nrm = lax.rsqrt(jnp.sum(yh * yh, axis=-1, keepdims=True) + NORM_EPS) * q_scale
        o_ref[:, sl] = yh * jnp.where(j < 2, nrm, 1.0)


def dn_conv_prep(p, conv_w, b, n, rows, cols):
    tr = min(rows, 32)
    tt = tr * cols
    n_tiles = rows // tr
    nblk = b * n // cols
    c0 = P_QKV // DN_WIDTH
    return pl.pallas_call(
        functools.partial(_dnconv_kernel, cols=cols, n_tiles=n_tiles),
        grid=(b, n_tiles, 3),
        in_specs=[pl.BlockSpec((cols, DN_WIDTH),
                               lambda bi, t, j: (jnp.maximum(bi * rows + t * tr - 1, 0), c0 + j)),
                  pl.BlockSpec((tt, DN_WIDTH), lambda bi, t, j: (bi * n_tiles + t, c0 + j)),
                  pl.BlockSpec((cols, DN_WIDTH),
                               lambda bi, t, j: (jnp.minimum(bi * rows + (t + 1) * tr, nblk - 1), c0 + j)),
                  pl.BlockSpec((SHORT_CONV * SHORT_CONV, DN_WIDTH), lambda bi, t, j: (0, j))],
        out_specs=pl.BlockSpec((tt, DN_WIDTH), lambda bi, t, j: (bi * n_tiles + t, j)),
        out_shape=jax.ShapeDtypeStruct((b * n, 3 * DN_WIDTH), F32),
        compiler_params=_cparams("parallel", "parallel", "parallel"),
        name="dn_conv_prep",
    )(p, p, p, conv_w.reshape(SHORT_CONV * SHORT_CONV, 3 * DN_WIDTH))


PREP_CHUNKS = 8
PREP_GROUP = 4


def _softplus(x):
    return jnp.maximum(x, 0.0) + jnp.log(1.0 + jnp.exp(-jnp.abs(x)))


def _delta_prep_kernel(qkv_ref, sm_ref, smt_ref, prm_ref, prmt_ref, u0_ref, lhs1_ref, lhs2_ref, gl_ref, *, n_chunks):
    cc = DN_CHUNK
    nh = DN_HEADS
    sm = sm_ref[...]
    beta_all = _sigmoid(sm)
    g_all = -jnp.exp(prm_ref[0:1, :]) * _softplus(sm + prm_ref[1:2, :])
    gt_all = -jnp.exp(prmt_ref[:, 0:1]) * _softplus(smt_ref[...] + prmt_ref[:, 1:2])
    lt = 2 * cc
    lane = lax.broadcasted_iota(jnp.int32, (gt_all.shape[0], lt), 1) % cc
    gt_tiles = []
    for t in range(gt_all.shape[1] // lt):
        gt_f = gt_b = gt_all[:, t * lt:(t + 1) * lt]
        s = 1
        while s < cc:
            gt_f = gt_f + jnp.where(lane >= s, pltpu.roll(gt_f, s, axis=1), 0.0)
            gt_b = gt_b + jnp.where(lane < cc - s, pltpu.roll(gt_b, lt - s, axis=1), 0.0)
            s *= 2
        gt_tiles.append((gt_f, gt_b))
    ri = lax.broadcasted_iota(jnp.int32, (cc, cc), 0)
    ci_ = lax.broadcasted_iota(jnp.int32, (cc, cc), 1)
    sub = lax.broadcasted_iota(jnp.int32, (cc, 1), 0)
    blk = lambda s: (ri // s) == (ci_ // s)
    leaf = 8

    heads = []
    for ci in range(n_chunks):
        rows = slice(ci * cc, (ci + 1) * cc)
        gc_f = g_all[rows]
        gc_b = gc_f
        s = 1
        while s < cc:
            gc_f = gc_f + jnp.where(sub >= s, pltpu.roll(gc_f, s, axis=0), 0.0)
            gc_b = gc_b + jnp.where(sub < cc - s, pltpu.roll(gc_b, cc - s, axis=0), 0.0)
            s *= 2
        for h in range(nh):
            q = qkv_ref[rows, h * 128:(h + 1) * 128]
            k = qkv_ref[rows, (nh + h) * 128:(nh + h + 1) * 128]
            v = qkv_ref[rows, (2 * nh + h) * 128:(2 * nh + h + 1) * 128]
            heads.append((ci, rows, h, q, k, v, gc_f, gc_b))
    qkks = [lax.dot_general(jnp.concatenate([q, k], axis=0).astype(BF16), k.astype(BF16),
                            (((1,), (1,)), ((), ())), preferred_element_type=F32)
            for (_, _, _, q, k, _, _, _) in heads]
    per_group = PREP_GROUP * nh
    for g0 in range(0, len(heads), per_group):
        group = list(zip(heads[g0:g0 + per_group], qkks[g0:g0 + per_group]))

        def token_scalars(ci, rows, h, d, gc_f, gc_b):
            beta = jnp.broadcast_to(beta_all[rows, d * nh + h:d * nh + h + 1], (cc, DN_HEAD_DIM))
            ca_ = 2 * nh + d * nh + h
            g_col = jnp.broadcast_to((gc_f if d == 0 else gc_b)[:, ca_:ca_ + 1], (cc, DN_HEAD_DIM))
            g_last = g_col[cc - 1:cc] if d == 0 else g_col[0:1]
            return beta, g_col, g_last

        mats = []
        for (ci, rows, h, q, k, v, gc_f, gc_b), qkk in group:
            qk, kk = qkk[:cc], qkk[cc:]
            for d in range(2):
                beta, g_col, g_last = token_scalars(ci, rows, h, d, gc_f, gc_b)
                lo = (ci % 2) * cc
                g_row = gt_tiles[ci // 2][d][2 * nh + d * nh + h:2 * nh + d * nh + h + 1, lo:lo + cc]
                incl = (ri >= ci_) if d == 0 else (ri <= ci_)
                strict = (ri > ci_) if d == 0 else (ri < ci_)
                decay = jnp.exp(jnp.where(incl, g_col[:, :cc] - g_row, -1e30))
                mats.append(jnp.where(strict, beta[:, :cc] * kk * decay, 0.0))
                lhs1_ref[d, h, ci, cc:2 * cc, :] = (q * jnp.exp(g_col)).astype(BF16)
                lhs2_ref[d, h, ci, 0:cc, :] = (qk * decay).astype(BF16)
                lhs2_ref[d, h, ci, cc:3 * cc, :] = (k * jnp.exp(g_last - g_col)).T.astype(BF16)
                gl_ref[d, h, ci] = jnp.broadcast_to(jnp.exp(g_last), (8, DN_HEAD_DIM))

        pws = [jnp.where(blk(leaf), a, 0.0) for a in mats]
        devs = [-pw for pw in pws]
        for _ in range(2):
            pws = [_bdot(pw, pw) for pw in pws]
            cross = [_bdot(dev, pw) for dev, pw in zip(devs, pws)]
            devs = [dev + pw + x for dev, pw, x in zip(devs, pws, cross)]
        s = leaf
        while s < cc:
            offs = [jnp.where(blk(2 * s) & jnp.logical_not(blk(s)), a, 0.0) for a in mats]
            xs = [off + _bdot(dev, off) for dev, off in zip(devs, offs)]
            devs = [dev - x - _bdot(x, dev) for dev, x in zip(devs, xs)]
            s *= 2
        chains, rhss = [], []
        for (ci, rows, h, q, k, v, gc_f, gc_b), _ in group:
            for d in range(2):
                beta, g_col, _ = token_scalars(ci, rows, h, d, gc_f, gc_b)
                chains.append((ci, rows, h, d))
                rhss.append(jnp.concatenate([v * beta, k * (beta * jnp.exp(g_col))], axis=1))
        uws = [rhs + _bdot(dev, rhs) for dev, rhs in zip(devs, rhss)]
        for (ci, rows, h, d), uw in zip(chains, uws):
            u0_ref[d, h, rows, :] = uw[:, :128]
            lhs1_ref[d, h, ci, 0:cc, :] = uw[:, 128:].astype(BF16)


def delta_prep(qkv, small, small_t, prm, prm_t, b, n):
    nc = n // DN_CHUNK
    cb = min(PREP_CHUNKS, nc)
    nblk = nc // cb
    tt = cb * DN_CHUNK
    nh = DN_HEADS
    return pl.pallas_call(
        functools.partial(_delta_prep_kernel, n_chunks=cb),
        grid=(b, nblk),
        in_specs=[pl.BlockSpec((tt, 3 * DN_WIDTH), lambda bi, c: (bi * nblk + c, 0)),
                  pl.BlockSpec((tt, SMALL_WIDTH), lambda bi, c: (bi * nblk + c, 0)),
                  pl.BlockSpec((4 * nh, tt), lambda bi, c: (0, bi * nblk + c)),
                  pl.BlockSpec((2, SMALL_WIDTH), lambda bi, c: (0, 0)),
                  pl.BlockSpec((4 * nh, 2), lambda bi, c: (0, 0))],
        out_specs=[pl.BlockSpec((2, None, nh, tt, 128), lambda bi, c: (0, bi, 0, c, 0)),
                   pl.BlockSpec((2, None, nh, cb, 128, 128), lambda bi, c: (0, bi, 0, c, 0, 0)),
                   pl.BlockSpec((2, None, nh, cb, 192, 64), lambda bi, c: (0, bi, 0, c, 0, 0)),
                   pl.BlockSpec((2, None, nh, cb, 8, 128), lambda bi, c: (0, bi, 0, c, 0, 0))],
        out_shape=[jax.ShapeDtypeStruct((2, b, nh, n, 128), F32),
                   jax.ShapeDtypeStruct((2, b, nh, nc, 128, 128), BF16),
                   jax.ShapeDtypeStruct((2, b, nh, nc, 192, 64), BF16),
                   jax.ShapeDtypeStruct((2, b, nh, nc, 8, 128), F32)],
        compiler_params=_cparams("parallel", "parallel"),
        name="delta_prep",
    )(qkv, small, small_t, prm, prm_t)


def _delta_scan_kernel(u0f_ref, u0b_ref, l1f_ref, l1b_ref, l2f_ref, l2b_ref, glf_ref, glb_ref, s0_ref,
                       of_ref, ob_ref, sout_ref, st_ref, *, sc, nb):
    n = pl.program_id(0)
    cc = DN_CHUNK

    @pl.when(n == 0)
    def _():
        st_ref[...] = s0_ref[...]

    dirs = ((u0f_ref, l1f_ref, l2f_ref, glf_ref, of_ref), (u0b_ref, l1b_ref, l2b_ref, glb_ref, ob_ref))

    def body(i, carry):
        chains = []
        for d in range(2):
            ci = i if d == 0 else sc - 1 - i
            r0 = pl.multiple_of(ci * cc, cc)
            chains += [(d, bi, h, ci, r0) for bi in range(nb) for h in range(DN_HEADS)]
        sts = [st_ref[d, bi, h] for (d, bi, h, _, _) in chains]
        r1s = [jnp.dot(dirs[d][1][bi, h, ci], st.astype(BF16), preferred_element_type=F32)
               for (d, bi, h, ci, _), st in zip(chains, sts)]
        us = [dirs[d][0][bi, h, pl.ds(r0, cc), :] - r1[:cc] for (d, bi, h, _, r0), r1 in zip(chains, r1s)]
        r2s = [jnp.dot(dirs[d][2][bi, h, ci], u.astype(BF16), preferred_element_type=F32)
               for (d, bi, h, ci, _), u in zip(chains, us)]
        for (d, bi, h, ci, r0), st, r1, r2 in zip(chains, sts, r1s, r2s):
            dirs[d][4][bi, pl.ds(r0, cc), h * 128:(h + 1) * 128] = (r1[cc:] + r2[:cc]).astype(BF16)
            st_ref[d, bi, h] = st * dirs[d][3][bi, h, ci, 0:1, :] + r2[cc:]
        return carry

    lax.fori_loop(0, sc, body, 0, unroll=4)

    @pl.when(n == pl.num_programs(0) - 1)
    def _():
        sout_ref[...] = st_ref[...]


def delta_scan(u0, lhs1, lhs2, gl, s0, b, n):
    nc = n // DN_CHUNK
    sc = min(8, nc)
    nblk = nc // sc
    nh = DN_HEADS
    tt = sc * DN_CHUNK
    fwd = lambda i: i
    bwd = lambda i: nblk - 1 - i

    def specs(d, blk):
        return [pl.BlockSpec((None, b, nh, tt, 128), lambda i: (d, 0, 0, blk(i), 0)),
                pl.BlockSpec((None, b, nh, sc, 128, 128), lambda i: (d, 0, 0, blk(i), 0, 0)),
                pl.BlockSpec((None, b, nh, sc, 192, 64), lambda i: (d, 0, 0, blk(i), 0, 0)),
                pl.BlockSpec((None, b, nh, sc, 8, 128), lambda i: (d, 0, 0, blk(i), 0, 0))]

    sf, sb = specs(0, fwd), specs(1, bwd)
    in_specs = [sf[0], sb[0], sf[1], sb[1], sf[2], sb[2], sf[3], sb[3],
                pl.BlockSpec((2, b, nh, 128, 128), lambda i: (0, 0, 0, 0, 0))]
    return pl.pallas_call(
        functools.partial(_delta_scan_kernel, sc=sc, nb=b),
        grid=(nblk,),
        in_specs=in_specs,
        out_specs=[pl.BlockSpec((b, tt, DN_WIDTH), lambda i: (0, fwd(i), 0)),
                   pl.BlockSpec((b, tt, DN_WIDTH), lambda i: (0, bwd(i), 0)),
                   pl.BlockSpec((2, b, nh, 128, 128), lambda i: (0, 0, 0, 0, 0))],
        out_shape=[jax.ShapeDtypeStruct((b, n, DN_WIDTH), BF16), jax.ShapeDtypeStruct((b, n, DN_WIDTH), BF16),
                   jax.ShapeDtypeStruct((2, b, nh, 128, 128), F32)],
        scratch_shapes=[pltpu.VMEM((2, b, nh, 128, 128), F32)],
        compiler_params=_cparams("arbitrary"),
        name="delta_scan",
    )(u0, u0, lhs1, lhs1, lhs2, lhs2, gl, gl, s0)


def delta_branch(p, small, conv_w, a_log, dt_bias, s0, b, n, rows, cols):
    nh = DN_HEADS
    qkv = dn_conv_prep(p, conv_w, b, n, rows, cols)
    small_t = small[:, :4 * nh].T
    rate = jnp.concatenate([jnp.zeros((2 * nh,), F32), a_log.reshape(-1)])
    bias = jnp.concatenate([jnp.zeros((2 * nh,), F32), dt_bias.reshape(-1)])
    prm_t = jnp.stack([rate, bias], axis=1)
    prm = jnp.pad(prm_t.T, ((0, 0), (0, SMALL_WIDTH - 4 * nh)))
    u0, lhs1, lhs2, gl = delta_prep(qkv, small, small_t, prm, prm_t, b, n)
    o_f, o_b, s_out = delta_scan(u0, lhs1, lhs2, gl, s0, b, n)
    return o_f.reshape(b * n, DN_WIDTH), o_b.reshape(b * n, DN_WIDTH), s_out


def _cos_sin(rows, cols, period):
    ang = 2.0 * np.pi * ((np.arange(rows)[:, None] * np.arange(cols)[None, :]) % period) / period
    return np.cos(ang), np.sin(ang)


def _const_bf16(a):
    return jnp.asarray(a, F32).astype(BF16)


def _stage2_matrices(l2):
    c, s = _cos_sin(l2, l2, l2)
    fwd = np.block([[c, s], [-s, c]])
    inv = np.block([[c, -s], [s, c]])
    return _const_bf16(fwd), _const_bf16(inv)


def _twiddles(l1, l2, kb):
    ang = 2.0 * np.pi * ((np.arange(l2)[:, None] * np.arange(l1)[None, :]) % (l1 * l2)) / (l1 * l2)
    tw = np.stack([np.cos(ang), np.sin(ang)], axis=0).reshape(2, l2, l1 // kb, kb)
    return jnp.asarray(np.transpose(tw, (2, 0, 1, 3)), F32)


def _twiddle_mul(ar, ai, c, s, conj):
    if conj:
        return ar * c - ai * s, ai * c + ar * s
    return ar * c + ai * s, ai * c - ar * s


def _cplx_apply(m_ref, re, im):
    half = re.shape[0]
    out = jnp.dot(m_ref[...], jnp.concatenate([re, im], axis=0).astype(BF16), preferred_element_type=F32)
    return out[:half], out[half:]


def _fnet_mid_kernel(a_ref, tw_ref, m2_ref, cs_ref, o_ref, *, kb, ch):
    ts = [_twiddle_mul(a_ref[0, j], a_ref[1, j], tw_ref[0, :, j:j + 1], tw_ref[1, :, j:j + 1], False)
          for j in range(kb)]
    us = [_cplx_apply(m2_ref, tr, ti) for tr, ti in ts]
    ys = [jnp.dot(jnp.concatenate([ur, ui], axis=1).astype(BF16), cs_ref[...], preferred_element_type=F32)
          for ur, ui in us]
    for j, y in enumerate(ys):
        o_ref[:, j * ch:(j + 1) * ch] = y


def fnet_mid(a, tw, m2, cs, kb):
    bsz, _, l1, l2, ch = a.shape
    return pl.pallas_call(
        functools.partial(_fnet_mid_kernel, kb=kb, ch=ch),
        grid=(bsz, l1 // kb),
        in_specs=[pl.BlockSpec((None, 2, kb, l2, ch), lambda b, k: (b, 0, k, 0, 0)),
                  pl.BlockSpec((None, 2, l2, kb), lambda b, k: (k, 0, 0, 0)),
                  pl.BlockSpec(m2.shape, lambda b, k: (0, 0)),
                  pl.BlockSpec(cs.shape, lambda b, k: (0, 0))],
        out_specs=pl.BlockSpec((None, l2, kb * ch), lambda b, k: (b, 0, k)),
        out_shape=jax.ShapeDtypeStruct((bsz, l2, l1 * ch), F32),
        compiler_params=_cparams("parallel", "parallel"),
        name="fnet_mid",
    )(a, tw, m2, cs)


def _fnet_channel_matrix(n):
    c, s = _cos_sin(FN_GROUP_DIM, FN_GROUP_DIM, FN_GROUP_DIM)
    eye = np.eye(FN_GROUPS)
    scale = 1.0 / math.sqrt(n * FN_GROUP_DIM)
    return _const_bf16(np.concatenate([np.kron(eye, c), np.kron(eye, s)], axis=0) * scale)


FN_SUB = 8


def _fnet_long_kernel(x_ref, kin_ref, tw_ref, m2k_ref, cs_ref, o_ref, a_ref):
    n_sub = x_ref.shape[1] // FN_SUB
    rows, width = x_ref.shape[0] * FN_SUB, x_ref.shape[2]
    half = FN_SUB * FN_SUB
    for j in range(n_sub):
        r = jnp.dot(kin_ref[...], x_ref[:, j * FN_SUB:(j + 1) * FN_SUB, :].reshape(rows, width).astype(BF16),
                    preferred_element_type=F32)
        a_ref[0, j * half:(j + 1) * half, :] = r[:half]
        a_ref[1, j * half:(j + 1) * half, :] = r[half:]
    tr, ti = _twiddle_mul(a_ref[0], a_ref[1], tw_ref[0], tw_ref[1], False)
    ur, ui = _cplx_apply(m2k_ref, tr, ti)
    y = jnp.dot(jnp.concatenate([ur, ui], axis=1).astype(BF16), cs_ref[...], preferred_element_type=F32)
    o_ref[...] = y.reshape(o_ref.shape)


def fnet_long(p, b, n):
    ch, sub = FN_WIDTH, FN_SUB
    l1, l2 = n // 128, 128
    nblk = l1 // sub
    c1, s1 = _cos_sin(l1, l1, l1)
    m1 = np.stack([c1, -s1], axis=0).reshape(2, nblk, sub, l1)
    kin = np.einsum('pbkn,jq->bpjknq', m1, np.eye(sub)).reshape(nblk, 2 * sub * sub, l1 * sub)
    ang = 2.0 * np.pi * ((np.arange(l2)[:, None] * np.arange(l1)[None, :]) % n) / n
    tw = np.stack([np.cos(ang), np.sin(ang)], axis=0).reshape(2, l2, nblk, sub)
    tw = np.transpose(tw, (2, 0, 1, 3)).reshape(nblk, 2, l2 * sub, 1)
    c2, s2 = _cos_sin(l2, l2, l2)
    m2k = _const_bf16(np.kron(np.block([[c2, s2], [-s2, c2]]), np.eye(sub)))
    cs = _fnet_channel_matrix(n)
    const = lambda a: pl.BlockSpec(a.shape, lambda bi, k: (0,) * a.ndim, pipeline_mode=pl.Buffered(1))
    y = pl.pallas_call(
        _fnet_long_kernel,
        grid=(b, nblk),
        in_specs=[pl.BlockSpec((None, l1, l2, ch), lambda bi, k: (bi, 0, 0, P_FN // ch),
                               pipeline_mode=pl.Buffered(1)),
                  pl.BlockSpec((None,) + kin.shape[1:], lambda bi, k: (k, 0, 0)),
                  pl.BlockSpec((None, 2, l2 * sub, 1), lambda bi, k: (k, 0, 0, 0)),
                  const(m2k), const(cs)],
        out_specs=pl.BlockSpec((None, l2, sub, ch), lambda bi, k: (bi, 0, k, 0)),
        out_shape=jax.ShapeDtypeStruct((b, l2, l1, ch), F32),
        scratch_shapes=[pltpu.VMEM((2, l2 * sub, ch), F32)],
        compiler_params=_cparams("parallel", "parallel"),
        name="fnet_long",
    )(p.reshape(b, l1, l2, P_WIDTH), _const_bf16(kin), jnp.asarray(tw, F32), m2k, cs)
    return y.reshape(b * n, ch)


def fnet_branch(p, b, n):
    if n == 8192:
        return fnet_long(p, b, n)
    ch = FN_WIDTH
    xr = lax.slice_in_dim(p, P_FN, P_FN + ch, axis=1).reshape(b, 1, 1, n, ch)
    a = jnp.concatenate([xr, jnp.zeros_like(xr)], axis=1)
    y = fnet_mid(a, _twiddles(1, n, 1), _stage2_matrices(n)[0], _fnet_channel_matrix(n), 1)
    return y.reshape(b * n, ch)


def _seq_conv_kernel(prev_ref, cur_ref, next_ref, w_ref, *o_refs, n_tiles):
    t = pl.program_id(1)
    tt = cur_ref.shape[0]
    cur = cur_ref[...]
    row = lax.broadcasted_iota(jnp.int32, (tt, 1), 0)
    before = jnp.where(t == 0, 0.0, prev_ref[7:8, :])
    after = jnp.where(t == n_tiles - 1, 0.0, next_ref[0:1, :])
    left = jnp.where(row == 0, before, pltpu.roll(cur, 1, axis=0))
    right = jnp.where(row == tt - 1, after, pltpu.roll(cur, tt - 1, axis=0))
    y = w_ref[0:1, :] * left + w_ref[1:2, :] * cur + w_ref[2:3, :] * right
    for part, o_ref in enumerate(o_refs):
        o_ref[...] = y[:, part * HY_WIDTH:(part + 1) * HY_WIDTH]


def seq_conv(p, conv_w, b, n):
    tt = min(n, 1024)
    n_tiles = n // tt
    parts = HY_ORDER + 1
    w = parts * HY_WIDTH
    c0 = P_HY // w
    out_spec = pl.BlockSpec((tt, HY_WIDTH), lambda bi, t: (bi * n_tiles + t, 0))
    return pl.pallas_call(
        functools.partial(_seq_conv_kernel, n_tiles=n_tiles),
        grid=(b, n_tiles),
        in_specs=[pl.BlockSpec((8, w), lambda bi, t: (jnp.maximum((bi * n_tiles + t) * (tt // 8) - 1, 0), c0)),
                  pl.BlockSpec((tt, w), lambda bi, t: (bi * n_tiles + t, c0)),
                  pl.BlockSpec((8, w), lambda bi, t: (jnp.minimum((bi * n_tiles + t + 1) * (tt // 8),
                                                                  b * n // 8 - 1), c0)),
                  pl.BlockSpec((SHORT_CONV, w), lambda bi, t: (0, 0))],
        out_specs=[out_spec] * parts,
        out_shape=[jax.ShapeDtypeStruct((b * n, HY_WIDTH), F32)] * parts,
        compiler_params=_cparams("parallel", "parallel"),
        name="hy_seq_conv",
    )(p, p, p, conv_w)


def _hdot(a, b):
    return jnp.dot(a, b, preferred_element_type=F32, precision=lax.Precision.HIGHEST)


def _hy_filter_kernel(ft_ref, t_ref, w1t_ref, b1_ref, f1_ref, w2t_ref, b2_ref, f2_ref, w3_ref, dl_ref, k_ref, s_ref,
                      *, n, tr):
    i = pl.program_id(0)
    hid = jnp.sin(f1_ref[...] * (_hdot(w1t_ref[...], ft_ref[...]) + b1_ref[...]))
    hid = jnp.sin(f2_ref[...] * (_hdot(w2t_ref[...], hid) + b2_ref[...]))
    filt = _hdot(hid.T, w3_ref[...]) * jnp.exp(-t_ref[...] * dl_ref[...])
    row = i * tr + lax.broadcasted_iota(jnp.int32, (tr, 1), 0)
    filt = jnp.where(row == n, 0.0, filt)
    k_ref[...] = filt

    @pl.when(i == 0)
    def _():
        s_ref[...] = jnp.zeros_like(s_ref)

    s_ref[...] += jnp.sum(jnp.abs(filt), axis=0, keepdims=True)


def hy_filter(n, w1, b1, freq1, w2, b2, freq2, w3):
    pos = jnp.arange(n, dtype=F32)
    t = pos / max(n - 1, 1)
    bands = jnp.linspace(1e-4, HY_BANDS - 1, HY_BANDS, dtype=F32)
    ang = (2.0 * math.pi / n) * pos[:, None] * bands[None, :]
    feats = jnp.concatenate([t[:, None], jnp.cos(ang), -jnp.sin(ang)], axis=-1)
    feats2 = jnp.concatenate([feats, feats[:1], feats[:0:-1]], axis=0)
    kpad = 128
    feats_t = jnp.pad(feats2, ((0, 0), (0, kpad - HY_EMB_DIM))).T
    w1t = jnp.pad(w1, ((0, kpad - HY_EMB_DIM), (0, 0))).T
    min_decay = math.log(HY_DECAY_TARGET) / HY_SLOW_DECAY_PCT
    max_decay = math.log(HY_DECAY_TARGET) / HY_FAST_DECAY_PCT
    cw = HY_ORDER * HY_WIDTH
    deltas = jnp.abs(jnp.linspace(min_decay, max_decay, cw, dtype=F32)).reshape(1, cw)
    tr = min(n, 1024)
    half = n // tr
    hd = HY_FILTER_HIDDEN
    vec = lambda v: v.reshape(hd, 1)
    full = lambda shp: pl.BlockSpec(shp, lambda i: (0, 0))
    return pl.pallas_call(
        functools.partial(_hy_filter_kernel, n=n, tr=tr),
        grid=(2 * half,),
        in_specs=[pl.BlockSpec((kpad, tr), lambda i: (0, i)), pl.BlockSpec((tr, 1), lambda i: (i, 0)),
                  full((hd, kpad)), full((hd, 1)), full((hd, 1)), full((hd, hd)), full((hd, 1)), full((hd, 1)),
                  pl.BlockSpec((hd, cw), lambda i: (0, i // half)), full((1, cw))],
        out_specs=[pl.BlockSpec((tr, cw), lambda i: (i, 0)), full((1, cw))],
        out_shape=[jax.ShapeDtypeStruct((2 * n, cw), F32), jax.ShapeDtypeStruct((1, cw), F32)],
        compiler_params=_cparams("arbitrary"),
        name="hy_filter",
    )(feats_t, feats2[:, 0:1], w1t, vec(b1), vec(freq1), w2.T, vec(b2), vec(freq2), w3, deltas)


def _hy_spec_kernel(a_ref, tw_ref, m2_ref, s_ref, o_ref, *, kb):
    inv = 1.0 / s_ref[...]
    ts = [_twiddle_mul(a_ref[0, j], a_ref[1, j], tw_ref[0, :, j:j + 1], tw_ref[1, :, j:j + 1], False)
          for j in range(kb)]
    xs = [_cplx_apply(m2_ref, tr, ti) for tr, ti in ts]
    for j, (xr, xi) in enumerate(xs):
        o_ref[0, j] = xr * inv
        o_ref[1, j] = xi * inv


def hy_spec(a, tw, m2, abs_sum, kb):
    _, l1, l2, cw = a.shape
    w = HY_WIDTH
    return pl.pallas_call(
        functools.partial(_hy_spec_kernel, kb=kb),
        grid=(l1 // kb, cw // w),
        in_specs=[pl.BlockSpec((2, kb, l2, w), lambda k, c: (0, k, 0, c)),
                  pl.BlockSpec((None, 2, l2, kb), lambda k, c: (k, 0, 0, 0)),
                  pl.BlockSpec(m2.shape, lambda k, c: (0, 0)),
                  pl.BlockSpec((1, w), lambda k, c: (0, c))],
        out_specs=pl.BlockSpec((2, kb, l2, w), lambda k, c: (0, k, 0, c)),
        out_shape=jax.ShapeDtypeStruct((2, l1, l2, cw), F32),
        compiler_params=_cparams("parallel", "parallel"),
        name="hy_spec",
    )(a, tw, m2, abs_sum)


def _hy_mid_kernel(a_ref, tw_ref, m2_ref, m2c_ref, kf_ref, o_ref, *, kb):
    cs = [(tw_ref[0, :, j:j + 1], tw_ref[1, :, j:j + 1]) for j in range(kb)]
    ts = [_twiddle_mul(a_ref[0, j], a_ref[1, j], c, s, False) for j, (c, s) in enumerate(cs)]
    xs = [_cplx_apply(m2_ref, tr, ti) for tr, ti in ts]
    ps = [(xr * kf_ref[0, j] - xi * kf_ref[1, j], xr * kf_ref[1, j] + xi * kf_ref[0, j])
          for j, (xr, xi) in enumerate(xs)]
    bs = [_cplx_apply(m2c_ref, pr, pi) for pr, pi in ps]
    for j, ((br, bi), (c, s)) in enumerate(zip(bs, cs)):
        o_ref[0, j], o_ref[1, j] = _twiddle_mul(br, bi, c, s, True)


def hy_mid(a, tw, m2, m2c, kf, order, kb):
    bsz, _, l1, l2, w = a.shape
    return pl.pallas_call(
        functools.partial(_hy_mid_kernel, kb=kb),
        grid=(bsz, l1 // kb),
        in_specs=[pl.BlockSpec((None, 2, kb, l2, w), lambda b, k: (b, 0, k, 0, 0)),
                  pl.BlockSpec((None, 2, l2, kb), lambda b, k: (k, 0, 0, 0)),
                  pl.BlockSpec(m2.shape, lambda b, k: (0, 0)),
                  pl.BlockSpec(m2c.shape, lambda b, k: (0, 0)),
                  pl.BlockSpec((2, kb, l2, w), lambda b, k: (0, k, 0, order))],
        out_specs=pl.BlockSpec((None, 2, kb, l2, w), lambda b, k: (b, 0, k, 0, 0)),
        out_shape=jax.ShapeDtypeStruct((bsz, 2, l1, l2, w), F32),
        compiler_params=_cparams("parallel", "parallel"),
        name="hy_mid",
    )(a, tw, m2, m2c, kf)


def _hy_out_kernel(m_ref, bp_ref, xo_ref, z_ref, bias_ref, o_ref):
    y = jnp.dot(m_ref[...], bp_ref[...].astype(BF16), preferred_element_type=F32)
    o_ref[...] = xo_ref[...] * (y + bias_ref[...] * z_ref[...])


def hy_out(m, bp, xo, z, bias_row, tn):
    bsz, k, n = bp.shape
    r = m.shape[0]
    blk = pl.BlockSpec((None, r, tn), lambda b, j: (b, 0, j))
    return pl.pallas_call(
        _hy_out_kernel,
        grid=(bsz, n // tn),
        in_specs=[pl.BlockSpec((r, k), lambda b, j: (0, 0)),
                  pl.BlockSpec((None, k, tn), lambda b, j: (b, 0, j)), blk, blk,
                  pl.BlockSpec((1, tn), lambda b, j: (0, j))],
        out_specs=blk,
        out_shape=jax.ShapeDtypeStruct((bsz, r, n), F32),
        compiler_params=_cparams("parallel", "parallel"),
        name="hy_out",
    )(m, bp, xo, z, bias_row)


SUB = 8
HY_KB = 16


def _half_spectrum_blocks(l1, kb):
    return -(-(l1 // 2 + 1) // kb)


def _kron_stage1(l1, n1_used, kb):
    c, s = _cos_sin(l1, n1_used, l1)
    m = np.stack([c, -s], axis=0).reshape(2, l1 // kb, kb, n1_used)
    m = np.transpose(m, (1, 0, 2, 3)).reshape(l1 // kb, 2 * kb, n1_used)[:_half_spectrum_blocks(l1, kb)]
    return _const_bf16(np.stack([np.kron(blk, np.eye(SUB)) for blk in m]))


def _kron_stage_out(l1, n1_used, kb):
    c, s = _cos_sin(n1_used, l1, l1)
    k1 = np.arange(l1)
    mult = np.where((k1 == 0) | (k1 == l1 // 2), 1.0, np.where(k1 < l1 // 2, 2.0, 0.0))
    m = np.stack([c * mult, -s * mult], axis=1).reshape(n1_used, 2, l1 // kb, kb) / (l1 * l1)
    m = np.transpose(m, (2, 0, 1, 3)).reshape(l1 // kb, n1_used, 2 * kb)[:_half_spectrum_blocks(l1, kb)]
    return _const_bf16(np.stack([np.kron(blk, np.eye(SUB)) for blk in m]))


def _strided_stage_in(kin_ref, src_ref, a_ref):
    n_sub, width = src_ref.shape[1] // SUB, src_ref.shape[2]
    rows = src_ref.shape[0] * SUB
    for j in range(n_sub):
        r = jnp.dot(kin_ref[...], src_ref[:, j * SUB:(j + 1) * SUB, :].reshape(rows, width).astype(BF16),
                    preferred_element_type=F32)
        a_ref[:, j * SUB:(j + 1) * SUB, :] = r.reshape(a_ref.shape[0], SUB, width)


def _hy_conv_kernel(z_ref, xo_ref, kin_ref, tw_ref, m2_ref, m2c_ref, kf_ref, kout_ref, bias_ref, o_ref,
                    a_ref, b_ref, *, kb):
    k = pl.program_id(1)

    @pl.when(k == 0)
    def _():
        o_ref[...] = jnp.zeros_like(o_ref)

    _strided_stage_in(kin_ref, z_ref, a_ref)
    grp = kb
    for g0 in range(0, kb, grp):
        js = range(g0, g0 + grp)
        cs = [(tw_ref[0, :, j:j + 1], tw_ref[1, :, j:j + 1]) for j in js]
        ts = [_twiddle_mul(a_ref[j], a_ref[kb + j], c, s, False) for j, (c, s) in zip(js, cs)]
        xs = [_cplx_apply(m2_ref, tr, ti) for tr, ti in ts]
        ps = [(xr * kf_ref[0, j] - xi * kf_ref[1, j], xr * kf_ref[1, j] + xi * kf_ref[0, j])
              for j, (xr, xi) in zip(js, xs)]
        bs = [_cplx_apply(m2c_ref, pr, pi) for pr, pi in ps]
        for j, (br, bi), (c, s) in zip(js, bs, cs):
            b_ref[j], b_ref[kb + j] = _twiddle_mul(br, bi, c, s, True)
    n_sub, width = o_ref.shape[1] // SUB, o_ref.shape[2]
    for j in range(n_sub):
        r = jnp.dot(kout_ref[...], b_ref[:, j * SUB:(j + 1) * SUB, :].reshape(2 * kb * SUB, width).astype(BF16),
                    preferred_element_type=F32)
        o_ref[:, j * SUB:(j + 1) * SUB, :] += r.reshape(o_ref.shape[0], SUB, width)

    @pl.when(k == pl.num_programs(1) - 1)
    def _():
        o_ref[...] = xo_ref[...] * (o_ref[...] + bias_ref[...] * z_ref[...])


def hy_conv_long(z, xo, kf, bias, order, b, n):
    w, kb = HY_WIDTH, HY_KB
    l1 = l2 = 128
    n1 = n // l2
    nblk = _half_spectrum_blocks(l1, kb)
    tw = _twiddles(l1, l2, kb)[:nblk]
    m2, m2c = _stage2_matrices(l2)
    kin, kout = _kron_stage1(l1, n1, kb), _kron_stage_out(l1, n1, kb)
    tok = pl.BlockSpec((None, n1, l2, w), lambda bi, k: (bi, 0, 0, 0), pipeline_mode=pl.Buffered(1))
    const = lambda a: pl.BlockSpec(a.shape, lambda bi, k: (0,) * a.ndim)
    out = pl.pallas_call(
        functools.partial(_hy_conv_kernel, kb=kb),
        grid=(b, nblk),
        in_specs=[tok, tok,
                  pl.BlockSpec((None,) + kin.shape[1:], lambda bi, k: (k, 0, 0)),
                  pl.BlockSpec((None, 2, l2, kb), lambda bi, k: (k, 0, 0, 0)),
                  const(m2), const(m2c),
                  pl.BlockSpec((2, kb, l2, w), lambda bi, k: (0, k, 0, order)),
                  pl.BlockSpec((None,) + kout.shape[1:], lambda bi, k: (k, 0, 0)),
                  pl.BlockSpec((1, w), lambda bi, k: (0, 0))],
        out_specs=pl.BlockSpec((None, n1, l2, w), lambda bi, k: (bi, 0, 0, 0), pipeline_mode=pl.Buffered(1)),
        out_shape=jax.ShapeDtypeStruct((b, n1, l2, w), F32),
        scratch_shapes=[pltpu.VMEM((2 * kb, l2, w), F32), pltpu.VMEM((2 * kb, l2, w), F32)],
        compiler_params=_cparams("parallel", "arbitrary"),
        name="hy_conv_long",
    )(z.reshape(b, n1, l2, w), xo.reshape(b, n1, l2, w), kin, tw, m2, m2c, kf, kout, bias.reshape(1, w))
    return out.reshape(b * n, w)


def _hy_spec_long_kernel(kern_ref, kin_ref, tw_ref, m2_ref, s_ref, o_ref, a_ref, *, kb):
    _strided_stage_in(kin_ref, kern_ref, a_ref)
    inv = 1.0 / s_ref[...]
    ts = [_twiddle_mul(a_ref[j], a_ref[kb + j], tw_ref[0, :, j:j + 1], tw_ref[1, :, j:j + 1], False)
          for j in range(kb)]
    xs = [_cplx_apply(m2_ref, tr, ti) for tr, ti in ts]
    for j, (xr, xi) in enumerate(xs):
        o_ref[0, j] = xr * inv
        o_ref[1, j] = xi * inv


def hy_spec_long(kern, abs_sum):
    w, kb = HY_WIDTH, HY_KB
    l1 = l2 = 128
    cw = kern.shape[1]
    nblk = _half_spectrum_blocks(l1, kb)
    tw = _twiddles(l1, l2, kb)[:nblk]
    m2, _ = _stage2_matrices(l2)
    kin = _kron_stage1(l1, l1, kb)
    return pl.pallas_call(
        functools.partial(_hy_spec_long_kernel, kb=kb),
        grid=(cw // w, nblk),
        in_specs=[pl.BlockSpec((l1, l2, w), lambda c, k: (0, 0, c), pipeline_mode=pl.Buffered(1)),
                  pl.BlockSpec((None,) + kin.shape[1:], lambda c, k: (k, 0, 0)),
                  pl.BlockSpec((None, 2, l2, kb), lambda c, k: (k, 0, 0, 0)),
                  pl.BlockSpec(m2.shape, lambda c, k: (0, 0)),
                  pl.BlockSpec((1, w), lambda c, k: (0, c))],
        out_specs=pl.BlockSpec((2, kb, l2, w), lambda c, k: (0, k, 0, c)),
        out_shape=jax.ShapeDtypeStruct((2, nblk * kb, l2, cw), F32),
        scratch_shapes=[pltpu.VMEM((2 * kb, l2, w), F32)],
        compiler_params=_cparams("parallel", "arbitrary"),
        name="hy_spec_long",
    )(kern.reshape(l1, l2, cw), kin, tw, m2, abs_sum)


def hyena_pallas(p, conv_w, w1, b1, freq1, w2, b2, freq2, w3, bias, b, n):
    w = HY_WIDTH
    x0, x1, v = seq_conv(p, conv_w, b, n)
    kern, abs_sum = hy_filter(n, w1, b1, freq1, w2, b2, freq2, w3)
    z = v
    if n == 8192:
        kf = hy_spec_long(kern, abs_sum)
        for order, xo in enumerate((x0, x1)):
            z = hy_conv_long(z, xo, kf, bias[order], order, b, n)
        return z
    l2 = 2 * n
    tw = _twiddles(1, l2, 1)
    m2, m2c = _stage2_matrices(l2)
    ak = jnp.stack([kern, jnp.zeros_like(kern)], axis=0).reshape(2, 1, l2, HY_ORDER * w)
    m_out = _const_bf16(np.eye(n, 2 * l2) / l2)
    kf = hy_spec(ak, tw, m2, abs_sum, 1)
    for order, xo in enumerate((x0, x1)):
        zp = jnp.pad(z.reshape(b, 1, 1, n, w), ((0, 0), (0, 0), (0, 0), (0, n), (0, 0)))
        a = jnp.concatenate([zp, jnp.zeros_like(zp)], axis=1)
        bp = hy_mid(a, tw, m2, m2c, kf, order, 1)
        z = hy_out(m_out, bp.reshape(b, 2 * l2, w), xo.reshape(b, n, w), z.reshape(b, n, w),
                   bias[order].reshape(1, w), w).reshape(b * n, w)
    return z


def _prep_w_in(w):
    main = jnp.concatenate([w[:, OFF_Q:OFF_Z], w[:, OFF_HY:OFF_GATE], w[:, OFF_FN:OFF_HY], w[:, OFF_Z:OFF_BETA]],
                           axis=1).astype(BF16)
    small = jnp.pad(w[:, OFF_BETA:OFF_FN], ((0, 0), (0, SMALL_WIDTH - 4 * DN_HEADS))).astype(BF16)
    return main, small, w[:, OFF_GATE:IN_WIDTH].astype(BF16)


def kernel(x, c, ctx, c_ctx, w_mod, b_mod, norm1, norm2, w_in, dn_conv, dn_a_log, dn_dt_bias,
           dn_out_norm, hy_conv, hy_w1, hy_b1, hy_freq1, hy_w2, hy_b2, hy_freq2, hy_w3, hy_bias,
           w_branch_a, w_branch_b, w_branch_c, w_out, w_ff1, w_ff2, final_norm):
    b, n_lat, d = x.shape
    rows = n_lat // GRID_W
    n_ctx = ctx.shape[1]
    tm_x, tm_c = 1024, n_ctx

    c_rows = jnp.concatenate([c, c_ctx[None], jnp.zeros((8 - b - 1, d), F32)], axis=0)
    mods = mod_vectors(c_rows, w_mod, b_mod)
    s_zero = jnp.zeros((2, b, DN_HEADS, DN_HEAD_DIM, DN_HEAD_DIM), F32)
    h, hc = x.reshape(b * n_lat, d), ctx.reshape(b * n_ctx, d)

    for l in range(DEPTH):
        last = l == DEPTH - 1
        mv = mods[l].reshape(8, N_MOD, 1, d)
        mx = [mv[:b, i] for i in range(N_MOD)]
        mc = [mv[b:b + 1, i] for i in range(N_MOD)]
        w_main, w_small, w_gate = _prep_w_in(w_in[l])
        wa, wb, wc, wo = (w.astype(BF16) for w in (w_branch_a[l], w_branch_b[l], w_branch_c[l], w_out[l]))
        w1, w2 = w_ff1[l].astype(BF16), w_ff2[l].astype(BF16)
        n1, n2 = norm1[l][None, None, :], norm2[l][None, None, :]

        p_c, small_c = in_proj(hc, n1 * (1.0 + mc[1]), mc[0], w_main, w_small, b * n_ctx, tm_c)
        p_x, small_x = in_proj(h, n1 * (1.0 + mx[1]), mx[0], w_main, w_small, n_lat, tm_x)

        def mix(p, n):
            y_c = hyena_pallas(p, hy_conv[l], hy_w1[l], hy_b1[l], hy_freq1[l], hy_w2[l], hy_b2[l], hy_freq2[l],
                               hy_w3[l], hy_bias[l], b, n)
            return fnet_branch(p, b, n), y_c

        ocf, ocb, s_ctx = delta_branch(p_c, small_c, dn_conv[l], dn_a_log[l], dn_dt_bias[l], s_zero,
                                       b, n_ctx, 1, n_ctx)
        oxf, oxb, _ = delta_branch(p_x, small_x, dn_conv[l], dn_a_log[l], dn_dt_bias[l], s_ctx,
                                   b, n_lat, rows, GRID_W)

        y_b, y_c = mix(p_x, n_lat)
        h = merge(oxf, oxb, p_x, y_b, y_c, h, n1 * (1.0 + mx[1]), mx[0], mx[2], dn_out_norm[l],
                  w_gate, wa, wb, wc, wo, n_lat, 512)
        h = mlp(h, n2 * (1.0 + mx[4]), mx[3], mx[5], w1, w2, n_lat, tm_x, final_norm if last else None)

        if not last:
            y_b, y_c = mix(p_c, n_ctx)
            hc = merge(ocf, ocb, p_c, y_b, y_c, hc, n1 * (1.0 + mc[1]), mc[0], mc[2], dn_out_norm[l],
                       w_gate, wa, wb, wc, wo, b * n_ctx, tm_c)
            hc = mlp(hc, n2 * (1.0 + mc[4]), mc[3], mc[5], w1, w2, b * n_ctx, tm_c)

    return h.reshape(b, n_lat, d)
```

```python
import functools
import math

import jax
import jax.numpy as jnp
import numpy as np
from jax import lax
from jax.experimental import pallas as pl
from jax.experimental.pallas import tpu as pltpu

D_MODEL = 1024
DEPTH = 2
GRID_W = 64
NORM_EPS = 1e-6
N_MOD = 6

DN_HEADS = 4
DN_HEAD_DIM = 128
DN_WIDTH = DN_HEADS * DN_HEAD_DIM
DN_CHUNK = 64
SHORT_CONV = 3

FN_GROUPS = 4
FN_GROUP_DIM = 64
FN_WIDTH = FN_GROUPS * FN_GROUP_DIM

HY_WIDTH = 256
HY_ORDER = 2
HY_EMB_DIM = 33
HY_BANDS = (HY_EMB_DIM - 1) // 2
HY_FILTER_HIDDEN = 64
HY_FAST_DECAY_PCT = 0.3
HY_SLOW_DECAY_PCT = 1.5
HY_DECAY_TARGET = 1e-2

N_BRANCHES = 3
D_FF = 4 * D_MODEL

OFF_Q = 0
OFF_Z = 3 * DN_WIDTH
OFF_BETA = OFF_Z + DN_WIDTH
OFF_A = OFF_BETA + 2 * DN_HEADS
OFF_FN = OFF_A + 2 * DN_HEADS
OFF_HY = OFF_FN + FN_WIDTH
OFF_GATE = OFF_HY + (HY_ORDER + 1) * HY_WIDTH
IN_WIDTH = OFF_GATE + N_BRANCHES * D_MODEL

P_QKV = 0
P_HY = P_QKV + 3 * DN_WIDTH
P_FN = P_HY + (HY_ORDER + 1) * HY_WIDTH
P_Z = P_FN + FN_WIDTH
P_WIDTH = P_Z + DN_WIDTH
SMALL_WIDTH = 128

F32 = jnp.float32
BF16 = jnp.bfloat16
VMEM_LIMIT = 56 * 1024 * 1024


def _cparams(*sem):
    return pltpu.CompilerParams(dimension_semantics=sem, vmem_limit_bytes=VMEM_LIMIT)


def _bdot(a, b):
    return jnp.dot(a.astype(BF16), b.astype(BF16), preferred_element_type=F32)


def _sigmoid(x):
    return 0.5 * jnp.tanh(0.5 * x) + 0.5


def _modnorm(xf, gs, sh):
    r = lax.rsqrt(jnp.mean(xf * xf, axis=-1, keepdims=True) + NORM_EPS)
    return xf * r * gs + sh


def _mod_kernel(c_ref, w_ref, b_ref, o_ref):
    c = c_ref[...]
    o_ref[...] = _bdot(c * _sigmoid(c), w_ref[...]) + b_ref[...]


def mod_vectors(c_rows, w_mod, b_mod):
    tn = 1536
    n = N_MOD * D_MODEL
    return pl.pallas_call(
        _mod_kernel,
        grid=(DEPTH, n // tn),
        in_specs=[pl.BlockSpec((8, D_MODEL), lambda l, j: (0, 0)),
                  pl.BlockSpec((None, D_MODEL, tn), lambda l, j: (l, 0, j)),
                  pl.BlockSpec((None, 1, tn), lambda l, j: (l, 0, j))],
        out_specs=pl.BlockSpec((None, 8, tn), lambda l, j: (l, 0, j)),
        out_shape=jax.ShapeDtypeStruct((DEPTH, 8, n), F32),
        compiler_params=_cparams("parallel", "parallel"),
        name="mod_vectors",
    )(c_rows, w_mod, b_mod.reshape(DEPTH, 1, n))


def _in_proj_kernel(x_ref, gs_ref, sh_ref, w_ref, ws_ref, p_ref, small_ref, *, tn):
    xn = _modnorm(x_ref[...], gs_ref[0], sh_ref[0]).astype(BF16)
    small_ref[...] = jnp.dot(xn, ws_ref[...], preferred_element_type=F32)
    for j in range(p_ref.shape[1] // tn):
        p_ref[:, j * tn:(j + 1) * tn] = jnp.dot(xn, w_ref[:, j * tn:(j + 1) * tn], preferred_element_type=F32)


def in_proj(x2d, gs, sh, w_main, w_small, rows_per_mod, tm):
    m = x2d.shape[0]
    tiles_per_mod = rows_per_mod // tm
    mod_spec = pl.BlockSpec((1, 1, D_MODEL), lambda i: (i // tiles_per_mod, 0, 0))
    resident = lambda a: pl.BlockSpec(a.shape, lambda i: (0, 0), pipeline_mode=pl.Buffered(1))
    return pl.pallas_call(
        functools.partial(_in_proj_kernel, tn=1024),
        grid=(m // tm,),
        in_specs=[pl.BlockSpec((tm, D_MODEL), lambda i: (i, 0)), mod_spec, mod_spec,
                  resident(w_main), resident(w_small)],
        out_specs=[pl.BlockSpec((tm, P_WIDTH), lambda i: (i, 0)),
                   pl.BlockSpec((tm, SMALL_WIDTH), lambda i: (i, 0))],
        out_shape=[jax.ShapeDtypeStruct((m, P_WIDTH), F32), jax.ShapeDtypeStruct((m, SMALL_WIDTH), F32)],
        compiler_params=_cparams("parallel"),
        name="in_proj",
    )(x2d, gs, sh, w_main, w_small)


def _merge_kernel(of_ref, ob_ref, z_ref, yb_ref, yc_ref, h_ref, gs_ref, sh_ref, gate_ref, nrm_ref,
                  wg_ref, wa_ref, wb_ref, wc_ref, wo_ref, out_ref):
    xn = _modnorm(h_ref[...], gs_ref[0], sh_ref[0]).astype(BF16)
    o = of_ref[...].astype(F32) + ob_ref[...].astype(F32)
    z = z_ref[...]
    heads = []
    for hd in range(DN_HEADS):
        sl = slice(hd * DN_HEAD_DIM, (hd + 1) * DN_HEAD_DIM)
        oh, zh = o[:, sl], z[:, sl]
        r = lax.rsqrt(jnp.mean(oh * oh, axis=-1, keepdims=True) + NORM_EPS)
        heads.append(oh * r * nrm_ref[...] * (zh * _sigmoid(zh)))
    ya = jnp.concatenate(heads, axis=-1)
    merged = None
    for i, (y, w_ref) in enumerate(((ya, wa_ref), (yb_ref[...], wb_ref), (yc_ref[...], wc_ref))):
        g = jnp.dot(xn, wg_ref[:, i * D_MODEL:(i + 1) * D_MODEL], preferred_element_type=F32)
        term = _sigmoid(g) * _bdot(y, w_ref[...])
        merged = term if merged is None else merged + term
    out_ref[...] = h_ref[...] + gate_ref[0] * _bdot(merged, wo_ref[...])


def merge(o_f, o_b, p, y_b, y_c, h2d, gs, sh, gate, dn_out_norm, wg, wa, wb, wc, wo, rows_per_mod, tm):
    m = h2d.shape[0]
    tiles_per_mod = rows_per_mod // tm
    row = lambda w: pl.BlockSpec((tm, w), lambda i: (i, 0))
    full = lambda a: pl.BlockSpec(a.shape, lambda i: (0,) * a.ndim, pipeline_mode=pl.Buffered(1))
    mod_spec = pl.BlockSpec((1, 1, D_MODEL), lambda i: (i // tiles_per_mod, 0, 0))
    nrm = dn_out_norm.reshape(1, DN_HEAD_DIM)
    return pl.pallas_call(
        _merge_kernel,
        grid=(m // tm,),
        in_specs=[row(DN_WIDTH), row(DN_WIDTH),
                  pl.BlockSpec((tm, DN_WIDTH), lambda i: (i, P_Z // DN_WIDTH)),
                  row(FN_WIDTH), row(HY_WIDTH), row(D_MODEL), mod_spec, mod_spec, mod_spec,
                  full(nrm), full(wg), full(wa), full(wb), full(wc), full(wo)],
        out_specs=row(D_MODEL),
        out_shape=jax.ShapeDtypeStruct((m, D_MODEL), F32),
        compiler_params=_cparams("parallel"),
        name="merge",
    )(o_f, o_b, p, y_b, y_c, h2d, gs, sh, gate, nrm, wg, wa, wb, wc, wo)


def _mlp_kernel(h_ref, gs_ref, sh_ref, gate_ref, w1_ref, w2_ref, fin_ref, out_ref, acc_ref, *, final, tf):
    xn = _modnorm(h_ref[...], gs_ref[0], sh_ref[0]).astype(BF16)
    for j in range(w1_ref.shape[1] // tf):
        a = jnp.maximum(jnp.dot(xn, w1_ref[:, j * tf:(j + 1) * tf], preferred_element_type=F32), 0.0)
        part = jnp.dot((a * a).astype(BF16), w2_ref[j * tf:(j + 1) * tf, :], preferred_element_type=F32)
        if j == 0:
            acc_ref[...] = part
        else:
            acc_ref[...] += part
    y = h_ref[...] + gate_ref[0] * acc_ref[...]
    if final:
        y = y * lax.rsqrt(jnp.mean(y * y, axis=-1, keepdims=True) + NORM_EPS) * fin_ref[...]
    out_ref[...] = y


def mlp(h2d, gs, sh, gate, w1, w2, rows_per_mod, tm, final_gain=None):
    m = h2d.shape[0]
    tiles_per_mod = rows_per_mod // tm
    mod_spec = pl.BlockSpec((1, 1, D_MODEL), lambda i: (i // tiles_per_mod, 0, 0))
    resident = lambda a: pl.BlockSpec(a.shape, lambda i: (0, 0), pipeline_mode=pl.Buffered(1))
    final = final_gain is not None
    fin = (final_gain if final else jnp.ones((D_MODEL,), F32)).reshape(1, D_MODEL)
    return pl.pallas_call(
        functools.partial(_mlp_kernel, final=final, tf=1024),
        grid=(m // tm,),
        in_specs=[pl.BlockSpec((tm, D_MODEL), lambda i: (i, 0)), mod_spec, mod_spec, mod_spec,
                  resident(w1), resident(w2), pl.BlockSpec((1, D_MODEL), lambda i: (0, 0))],
        out_specs=pl.BlockSpec((tm, D_MODEL), lambda i: (i, 0)),
        out_shape=jax.ShapeDtypeStruct((m, D_MODEL), F32),
        scratch_shapes=[pltpu.VMEM((tm, D_MODEL), F32)],
        compiler_params=_cparams("parallel"),
        name="mlp",
    )(h2d, gs, sh, gate, w1, w2, fin)


def _dnconv_kernel(prev_ref, cur_ref, next_ref, w_ref, o_ref, *, cols, n_tiles):
    t, j = pl.program_id(1), pl.program_id(2)
    tt = cur_ref.shape[0]
    prev = jnp.where(t == 0, 0.0, prev_ref[...])
    nxt = jnp.where(t == n_tiles - 1, 0.0, next_ref[...])
    ext = jnp.concatenate([prev, cur_ref[...], nxt], axis=0)
    n_ext = tt + 2 * cols
    col = lax.broadcasted_iota(jnp.int32, (n_ext, 1), 0) % cols
    left = jnp.where(col == 0, 0.0, pltpu.roll(ext, 1, axis=0))
    right = jnp.where(col == cols - 1, 0.0, pltpu.roll(ext, n_ext - 1, axis=0))
    acc = jnp.zeros((tt, DN_WIDTH), F32)
    for dr in range(SHORT_CONV):
        base = dr * cols
        acc = (acc + w_ref[3 * dr:3 * dr + 1, :] * left[base:base + tt]
               + w_ref[3 * dr + 1:3 * dr + 2, :] * ext[base:base + tt]
               + w_ref[3 * dr + 2:3 * dr + 3, :] * right[base:base + tt])
    y = acc * _sigmoid(acc)
    q_scale = jnp.where(j == 0, DN_HEAD_DIM ** -0.5, 1.0)
    for hd in range(DN_HEADS):
        sl = slice(hd * DN_HEAD_DIM, (hd + 1) * DN_HEAD_DIM)
        yh = y[:, sl]
        nrm = lax.rsqrt(jnp.sum(yh * yh, axis=-1, keepdims=True) + NORM_EPS) * q_scale
        o_ref[:, sl] = yh * jnp.where(j < 2, nrm, 1.0)


def dn_conv_prep(p, conv_w, b, n, rows, cols):
    tr = min(rows, 32)
    tt = tr * cols
    n_tiles = rows // tr
    nblk = b * n // cols
    c0 = P_QKV // DN_WIDTH
    return pl.pallas_call(
        functools.partial(_dnconv_kernel, cols=cols, n_tiles=n_tiles),
        grid=(b, n_tiles, 3),
        in_specs=[pl.BlockSpec((cols, DN_WIDTH),
                               lambda bi, t, j: (jnp.maximum(bi * rows + t * tr - 1, 0), c0 + j)),
                  pl.BlockSpec((tt, DN_WIDTH), lambda bi, t, j: (bi * n_tiles + t, c0 + j)),
                  pl.BlockSpec((cols, DN_WIDTH),
                               lambda bi, t, j: (jnp.minimum(bi * rows + (t + 1) * tr, nblk - 1), c0 + j)),
                  pl.BlockSpec((SHORT_CONV * SHORT_CONV, DN_WIDTH), lambda bi, t, j: (0, j))],
        out_specs=pl.BlockSpec((tt, DN_WIDTH), lambda bi, t, j: (bi * n_tiles + t, j)),
        out_shape=jax.ShapeDtypeStruct((b * n, 3 * DN_WIDTH), F32),
        compiler_params=_cparams("parallel", "parallel", "parallel"),
        name="dn_conv_prep",
    )(p, p, p, conv_w.reshape(SHORT_CONV * SHORT_CONV, 3 * DN_WIDTH))


PREP_CHUNKS = 8
PREP_GROUP = 4


def _softplus(x):
    return jnp.maximum(x, 0.0) + jnp.log(1.0 + jnp.exp(-jnp.abs(x)))


def _delta_prep_kernel(qkv_ref, sm_ref, smt_ref, prm_ref, prmt_ref, u0_ref, lhs1_ref, lhs2_ref, gl_ref, *, n_chunks):
    cc = DN_CHUNK
    nh = DN_HEADS
    sm = sm_ref[...]
    beta_all = _sigmoid(sm)
    g_all = -jnp.exp(prm_ref[0:1, :]) * _softplus(sm + prm_ref[1:2, :])
    gt_all = -jnp.exp(prmt_ref[:, 0:1]) * _softplus(smt_ref[...] + prmt_ref[:, 1:2])
    lt = 2 * cc
    lane = lax.broadcasted_iota(jnp.int32, (gt_all.shape[0], lt), 1) % cc
    gt_tiles = []
    for t in range(gt_all.shape[1] // lt):
        gt_f = gt_b = gt_all[:, t * lt:(t + 1) * lt]
        s = 1
        while s < cc:
            gt_f = gt_f + jnp.where(lane >= s, pltpu.roll(gt_f, s, axis=1), 0.0)
            gt_b = gt_b + jnp.where(lane < cc - s, pltpu.roll(gt_b, lt - s, axis=1), 0.0)
            s *= 2
        gt_tiles.append((gt_f, gt_b))
    ri = lax.broadcasted_iota(jnp.int32, (cc, cc), 0)
    ci_ = lax.broadcasted_iota(jnp.int32, (cc, cc), 1)
    sub = lax.broadcasted_iota(jnp.int32, (cc, 1), 0)
    blk = lambda s: (ri // s) == (ci_ // s)
    leaf = 8

    heads = []
    for ci in range(n_chunks):
        rows = slice(ci * cc, (ci + 1) * cc)
        gc_f = g_all[rows]
        gc_b = gc_f
        s = 1
        while s < cc:
            gc_f = gc_f + jnp.where(sub >= s, pltpu.roll(gc_f, s, axis=0), 0.0)
            gc_b = gc_b + jnp.where(sub < cc - s, pltpu.roll(gc_b, cc - s, axis=0), 0.0)
            s *= 2
        for h in range(nh):
            q = qkv_ref[rows, h * 128:(h + 1) * 128]
            k = qkv_ref[rows, (nh + h) * 128:(nh + h + 1) * 128]
            v = qkv_ref[rows, (2 * nh + h) * 128:(2 * nh + h + 1) * 128]
            heads.append((ci, rows, h, q, k, v, gc_f, gc_b))
    qkks = [lax.dot_general(jnp.concatenate([q, k], axis=0).astype(BF16), k.astype(BF16),
                            (((1,), (1,)), ((), ())), preferred_element_type=F32)
            for (_, _, _, q, k, _, _, _) in heads]
    per_group = PREP_GROUP * nh
    for g0 in range(0, len(heads), per_group):
        group = list(zip(heads[g0:g0 + per_group], qkks[g0:g0 + per_group]))

        def token_scalars(ci, rows, h, d, gc_f, gc_b):
            beta = jnp.broadcast_to(beta_all[rows, d * nh + h:d * nh + h + 1], (cc, DN_HEAD_DIM))
            ca_ = 2 * nh + d * nh + h
            g_col = jnp.broadcast_to((gc_f if d == 0 else gc_b)[:, ca_:ca_ + 1], (cc, DN_HEAD_DIM))
            g_last = g_col[cc - 1:cc] if d == 0 else g_col[0:1]
            return beta, g_col, g_last

        mats = []
        for (ci, rows, h, q, k, v, gc_f, gc_b), qkk in group:
            qk, kk = qkk[:cc], qkk[cc:]
            for d in range(2):
                beta, g_col, g_last = token_scalars(ci, rows, h, d, gc_f, gc_b)
                lo = (ci % 2) * cc
                g_row = gt_tiles[ci // 2][d][2 * nh + d * nh + h:2 * nh + d * nh + h + 1, lo:lo + cc]
                incl = (ri >= ci_) if d == 0 else (ri <= ci_)
                strict = (ri > ci_) if d == 0 else (ri < ci_)
                decay = jnp.exp(jnp.where(incl, g_col[:, :cc] - g_row, -1e30))
                mats.append(jnp.where(strict, beta[:, :cc] * kk * decay, 0.0))
                lhs1_ref[d, h, ci, cc:2 * cc, :] = (q * jnp.exp(g_col)).astype(BF16)
                lhs2_ref[d, h, ci, 0:cc, :] = (qk * decay).astype(BF16)
                lhs2_ref[d, h, ci, cc:3 * cc, :] = (k * jnp.exp(g_last - g_col)).T.astype(BF16)
                gl_ref[d, h, ci] = jnp.broadcast_to(jnp.exp(g_last), (8, DN_HEAD_DIM))

        pws = [jnp.where(blk(leaf), a, 0.0) for a in mats]
        devs = [-pw for pw in pws]
        for _ in range(2):
            pws = [_bdot(pw, pw) for pw in pws]
            cross = [_bdot(dev, pw) for dev, pw in zip(devs, pws)]
            devs = [dev + pw + x for dev, pw, x in zip(devs, pws, cross)]
        s = leaf
        while s < cc:
            offs = [jnp.where(blk(2 * s) & jnp.logical_not(blk(s)), a, 0.0) for a in mats]
            xs = [off + _bdot(dev, off) for dev, off in zip(devs, offs)]
            devs = [dev - x - _bdot(x, dev) for dev, x in zip(devs, xs)]
            s *= 2
        chains, rhss = [], []
        for (ci, rows, h, q, k, v, gc_f, gc_b), _ in group:
            for d in range(2):
                beta, g_col, _ = token_scalars(ci, rows, h, d, gc_f, gc_b)
                chains.append((ci, rows, h, d))
                rhss.append(jnp.concatenate([v * beta, k * (beta * jnp.exp(g_col))], axis=1))
        uws = [rhs + _bdot(dev, rhs) for dev, rhs in zip(devs, rhss)]
        for (ci, rows, h, d), uw in zip(chains, uws):
            u0_ref[d, h, rows, :] = uw[:, :128]
            lhs1_ref[d, h, ci, 0:cc, :] = uw[:, 128:].astype(BF16)


def delta_prep(qkv, small, small_t, prm, prm_t, b, n):
    nc = n // DN_CHUNK
    cb = min(PREP_CHUNKS, nc)
    nblk = nc // cb
    tt = cb * DN_CHUNK
    nh = DN_HEADS
    return pl.pallas_call(
        functools.partial(_delta_prep_kernel, n_chunks=cb),
        grid=(b, nblk),
        in_specs=[pl.BlockSpec((tt, 3 * DN_WIDTH), lambda bi, c: (bi * nblk + c, 0)),
                  pl.BlockSpec((tt, SMALL_WIDTH), lambda bi, c: (bi * nblk + c, 0)),
                  pl.BlockSpec((4 * nh, tt), lambda bi, c: (0, bi * nblk + c)),
                  pl.BlockSpec((2, SMALL_WIDTH), lambda bi, c: (0, 0)),
                  pl.BlockSpec((4 * nh, 2), lambda bi, c: (0, 0))],
        out_specs=[pl.BlockSpec((2, None, nh, tt, 128), lambda bi, c: (0, bi, 0, c, 0)),
                   pl.BlockSpec((2, None, nh, cb, 128, 128), lambda bi, c: (0, bi, 0, c, 0, 0)),
                   pl.BlockSpec((2, None, nh, cb, 192, 64), lambda bi, c: (0, bi, 0, c, 0, 0)),
                   pl.BlockSpec((2, None, nh, cb, 8, 128), lambda bi, c: (0, bi, 0, c, 0, 0))],
        out_shape=[jax.ShapeDtypeStruct((2, b, nh, n, 128), F32),
                   jax.ShapeDtypeStruct((2, b, nh, nc, 128, 128), BF16),
                   jax.ShapeDtypeStruct((2, b, nh, nc, 192, 64), BF16),
                   jax.ShapeDtypeStruct((2, b, nh, nc, 8, 128), F32)],
        compiler_params=_cparams("parallel", "parallel"),
        name="delta_prep",
    )(qkv, small, small_t, prm, prm_t)


def _delta_scan_kernel(u0f_ref, u0b_ref, l1f_ref, l1b_ref, l2f_ref, l2b_ref, glf_ref, glb_ref, s0_ref,
                       of_ref, ob_ref, sout_ref, st_ref, *, sc, nb):
    n = pl.program_id(0)
    cc = DN_CHUNK

    @pl.when(n == 0)
    def _():
        st_ref[...] = s0_ref[...]

    dirs = ((u0f_ref, l1f_ref, l2f_ref, glf_ref, of_ref), (u0b_ref, l1b_ref, l2b_ref, glb_ref, ob_ref))

    def body(i, carry):
        chains = []
        for d in range(2):
            ci = i if d == 0 else sc - 1 - i
            r0 = pl.multiple_of(ci * cc, cc)
            chains += [(d, bi, h, ci, r0) for bi in range(nb) for h in range(DN_HEADS)]
        sts = [st_ref[d, bi, h] for (d, bi, h, _, _) in chains]
        r1s = [jnp.dot(dirs[d][1][bi, h, ci], st.astype(BF16), preferred_element_type=F32)
               for (d, bi, h, ci, _), st in zip(chains, sts)]
        us = [dirs[d][0][bi, h, pl.ds(r0, cc), :] - r1[:cc] for (d, bi, h, _, r0), r1 in zip(chains, r1s)]
        r2s = [jnp.dot(dirs[d][2][bi, h, ci], u.astype(BF16), preferred_element_type=F32)
               for (d, bi, h, ci, _), u in zip(chains, us)]
        for (d, bi, h, ci, r0), st, r1, r2 in zip(chains, sts, r1s, r2s):
            dirs[d][4][bi, pl.ds(r0, cc), h * 128:(h + 1) * 128] = (r1[cc:] + r2[:cc]).astype(BF16)
            st_ref[d, bi, h] = st * dirs[d][3][bi, h, ci, 0:1, :] + r2[cc:]
        return carry

    lax.fori_loop(0, sc, body, 0, unroll=4)

    @pl.when(n == pl.num_programs(0) - 1)
    def _():
        sout_ref[...] = st_ref[...]


def delta_scan(u0, lhs1, lhs2, gl, s0, b, n):
    nc = n // DN_CHUNK
    sc = min(8, nc)
    nblk = nc // sc
    nh = DN_HEADS
    tt = sc * DN_CHUNK
    fwd = lambda i: i
    bwd = lambda i: nblk - 1 - i

    def specs(d, blk):
        return [pl.BlockSpec((None, b, nh, tt, 128), lambda i: (d, 0, 0, blk(i), 0)),
                pl.BlockSpec((None, b, nh, sc, 128, 128), lambda i: (d, 0, 0, blk(i), 0, 0)),
                pl.BlockSpec((None, b, nh, sc, 192, 64), lambda i: (d, 0, 0, blk(i), 0, 0)),
                pl.BlockSpec((None, b, nh, sc, 8, 128), lambda i: (d, 0, 0, blk(i), 0, 0))]

    sf, sb = specs(0, fwd), specs(1, bwd)
    in_specs = [sf[0], sb[0], sf[1], sb[1], sf[2], sb[2], sf[3], sb[3],
                pl.BlockSpec((2, b, nh, 128, 128), lambda i: (0, 0, 0, 0, 0))]
    return pl.pallas_call(
        functools.partial(_delta_scan_kernel, sc=sc, nb=b),
        grid=(nblk,),
        in_specs=in_specs,
        out_specs=[pl.BlockSpec((b, tt, DN_WIDTH), lambda i: (0, fwd(i), 0)),
                   pl.BlockSpec((b, tt, DN_WIDTH), lambda i: (0, bwd(i), 0)),
                   pl.BlockSpec((2, b, nh, 128, 128), lambda i: (0, 0, 0, 0, 0))],
        out_shape=[jax.ShapeDtypeStruct((b, n, DN_WIDTH), BF16), jax.ShapeDtypeStruct((b, n, DN_WIDTH), BF16),
                   jax.ShapeDtypeStruct((2, b, nh, 128, 128), F32)],
        scratch_shapes=[pltpu.VMEM((2, b, nh, 128, 128), F32)],
        compiler_params=_cparams("arbitrary"),
        name="delta_scan",
    )(u0, u0, lhs1, lhs1, lhs2, lhs2, gl, gl, s0)


def delta_branch(p, small, conv_w, a_log, dt_bias, s0, b, n, rows, cols):
    nh = DN_HEADS
    qkv = dn_conv_prep(p, conv_w, b, n, rows, cols)
    small_t = small[:, :4 * nh].T
    rate = jnp.concatenate([jnp.zeros((2 * nh,), F32), a_log.reshape(-1)])
    bias = jnp.concatenate([jnp.zeros((2 * nh,), F32), dt_bias.reshape(-1)])
    prm_t = jnp.stack([rate, bias], axis=1)
    prm = jnp.pad(prm_t.T, ((0, 0), (0, SMALL_WIDTH - 4 * nh)))
    u0, lhs1, lhs2, gl = delta_prep(qkv, small, small_t, prm, prm_t, b, n)
    o_f, o_b, s_out = delta_scan(u0, lhs1, lhs2, gl, s0, b, n)
    return o_f.reshape(b * n, DN_WIDTH), o_b.reshape(b * n, DN_WIDTH), s_out


def _cos_sin(rows, cols, period):
    ang = 2.0 * np.pi * ((np.arange(rows)[:, None] * np.arange(cols)[None, :]) % period) / period
    return np.cos(ang), np.sin(ang)


def _const_bf16(a):
    return jnp.asarray(a, F32).astype(BF16)


def _stage2_matrices(l2):
    c, s = _cos_sin(l2, l2, l2)
    fwd = np.block([[c, s], [-s, c]])
    inv = np.block([[c, -s], [s, c]])
    return _const_bf16(fwd), _const_bf16(inv)


def _twiddles(l1, l2, kb, n_k1=None):
    n_k1 = l1 if n_k1 is None else n_k1
    ang = 2.0 * np.pi * ((np.arange(l2)[:, None] * np.arange(n_k1)[None, :]) % (l1 * l2)) / (l1 * l2)
    tw = np.stack([np.cos(ang), np.sin(ang)], axis=0).reshape(2, l2, n_k1 // kb, kb)
    return jnp.asarray(np.transpose(tw, (2, 0, 1, 3)), F32)


def _twiddle_mul(ar, ai, c, s, conj):
    if conj:
        return ar * c - ai * s, ai * c + ar * s
    return ar * c + ai * s, ai * c - ar * s


def _cplx_apply(m_ref, re, im):
    half = re.shape[0]
    out = jnp.dot(m_ref[...], jnp.concatenate([re, im], axis=0).astype(BF16), preferred_element_type=F32)
    return out[:half], out[half:]


def _fnet_mid_kernel(a_ref, tw_ref, m2_ref, cs_ref, o_ref, *, kb, ch):
    ts = [_twiddle_mul(a_ref[0, j], a_ref[1, j], tw_ref[0, :, j:j + 1], tw_ref[1, :, j:j + 1], False)
          for j in range(kb)]
    us = [_cplx_apply(m2_ref, tr, ti) for tr, ti in ts]
    ys = [jnp.dot(jnp.concatenate([ur, ui], axis=1).astype(BF16), cs_ref[...], preferred_element_type=F32)
          for ur, ui in us]
    for j, y in enumerate(ys):
        o_ref[:, j * ch:(j + 1) * ch] = y


def fnet_mid(a, tw, m2, cs, kb):
    bsz, _, l1, l2, ch = a.shape
    return pl.pallas_call(
        functools.partial(_fnet_mid_kernel, kb=kb, ch=ch),
        grid=(bsz, l1 // kb),
        in_specs=[pl.BlockSpec((None, 2, kb, l2, ch), lambda b, k: (b, 0, k, 0, 0)),
                  pl.BlockSpec((None, 2, l2, kb), lambda b, k: (k, 0, 0, 0)),
                  pl.BlockSpec(m2.shape, lambda b, k: (0, 0)),
                  pl.BlockSpec(cs.shape, lambda b, k: (0, 0))],
        out_specs=pl.BlockSpec((None, l2, kb * ch), lambda b, k: (b, 0, k)),
        out_shape=jax.ShapeDtypeStruct((bsz, l2, l1 * ch), F32),
        compiler_params=_cparams("parallel", "parallel"),
        name="fnet_mid",
    )(a, tw, m2, cs)


def _fnet_channel_matrix(n):
    c, s = _cos_sin(FN_GROUP_DIM, FN_GROUP_DIM, FN_GROUP_DIM)
    eye = np.eye(FN_GROUPS)
    scale = 1.0 / math.sqrt(n * FN_GROUP_DIM)
    return _const_bf16(np.concatenate([np.kron(eye, c), np.kron(eye, s)], axis=0) * scale)


FN_SUB = 8


def _fnet_long_kernel(x_ref, kin_ref, tw_ref, m2k_ref, cs_ref, o_ref, a_ref):
    n_sub = x_ref.shape[1] // FN_SUB
    rows, width = x_ref.shape[0] * FN_SUB, x_ref.shape[2]
    half = FN_SUB * FN_SUB
    for j in range(n_sub):
        r = jnp.dot(kin_ref[...], x_ref[:, j * FN_SUB:(j + 1) * FN_SUB, :].reshape(rows, width).astype(BF16),
                    preferred_element_type=F32)
        a_ref[0, j * half:(j + 1) * half, :] = r[:half]
        a_ref[1, j * half:(j + 1) * half, :] = r[half:]
    tr, ti = _twiddle_mul(a_ref[0], a_ref[1], tw_ref[0], tw_ref[1], False)
    ur, ui = _cplx_apply(m2k_ref, tr, ti)
    y = jnp.dot(jnp.concatenate([ur, ui], axis=1).astype(BF16), cs_ref[...], preferred_element_type=F32)
    o_ref[...] = y.reshape(o_ref.shape)


def fnet_long(p, b, n):
    ch, sub = FN_WIDTH, FN_SUB
    l1, l2 = n // 128, 128
    nblk = l1 // sub
    c1, s1 = _cos_sin(l1, l1, l1)
    m1 = np.stack([c1, -s1], axis=0).reshape(2, nblk, sub, l1)
    kin = np.einsum('pbkn,jq->bpjknq', m1, np.eye(sub)).reshape(nblk, 2 * sub * sub, l1 * sub)
    ang = 2.0 * np.pi * ((np.arange(l2)[:, None] * np.arange(l1)[None, :]) % n) / n
    tw = np.stack([np.cos(ang), np.sin(ang)], axis=0).reshape(2, l2, nblk, sub)
    tw = np.transpose(tw, (2, 0, 1, 3)).reshape(nblk, 2, l2 * sub, 1)
    c2, s2 = _cos_sin(l2, l2, l2)
    m2k = _const_bf16(np.kron(np.block([[c2, s2], [-s2, c2]]), np.eye(sub)))
    cs = _fnet_channel_matrix(n)
    const = lambda a: pl.BlockSpec(a.shape, lambda bi, k: (0,) * a.ndim, pipeline_mode=pl.Buffered(1))
    y = pl.pallas_call(
        _fnet_long_kernel,
        grid=(b, nblk),
        in_specs=[pl.BlockSpec((None, l1, l2, ch), lambda bi, k: (bi, 0, 0, P_FN // ch),
                               pipeline_mode=pl.Buffered(1)),
                  pl.BlockSpec((None,) + kin.shape[1:], lambda bi, k: (k, 0, 0)),
                  pl.BlockSpec((None, 2, l2 * sub, 1), lambda bi, k: (k, 0, 0, 0)),
                  const(m2k), const(cs)],
        out_specs=pl.BlockSpec((None, l2, sub, ch), lambda bi, k: (bi, 0, k, 0)),
        out_shape=jax.ShapeDtypeStruct((b, l2, l1, ch), F32),
        scratch_shapes=[pltpu.VMEM((2, l2 * sub, ch), F32)],
        compiler_params=_cparams("parallel", "parallel"),
        name="fnet_long",
    )(p.reshape(b, l1, l2, P_WIDTH), _const_bf16(kin), jnp.asarray(tw, F32), m2k, cs)
    return y.reshape(b * n, ch)


def fnet_branch(p, b, n):
    if n == 8192:
        return fnet_long(p, b, n)
    ch = FN_WIDTH
    xr = lax.slice_in_dim(p, P_FN, P_FN + ch, axis=1).reshape(b, 1, 1, n, ch)
    a = jnp.concatenate([xr, jnp.zeros_like(xr)], axis=1)
    y = fnet_mid(a, _twiddles(1, n, 1), _stage2_matrices(n)[0], _fnet_channel_matrix(n), 1)
    return y.reshape(b * n, ch)


def _seq_conv_kernel(prev_ref, cur_ref, next_ref, w_ref, *o_refs, n_tiles):
    t = pl.program_id(1)
    tt = cur_ref.shape[0]
    cur = cur_ref[...]
    row = lax.broadcasted_iota(jnp.int32, (tt, 1), 0)
    before = jnp.where(t == 0, 0.0, prev_ref[7:8, :])
    after = jnp.where(t == n_tiles - 1, 0.0, next_ref[0:1, :])
    left = jnp.where(row == 0, before, pltpu.roll(cur, 1, axis=0))
    right = jnp.where(row == tt - 1, after, pltpu.roll(cur, tt - 1, axis=0))
    y = w_ref[0:1, :] * left + w_ref[1:2, :] * cur + w_ref[2:3, :] * right
    for part, o_ref in enumerate(o_refs):
        o_ref[...] = y[:, part * HY_WIDTH:(part + 1) * HY_WIDTH]


def seq_conv(p, conv_w, b, n):
    tt = min(n, 1024)
    n_tiles = n // tt
    parts = HY_ORDER + 1
    w = parts * HY_WIDTH
    c0 = P_HY // w
    out_spec = pl.BlockSpec((tt, HY_WIDTH), lambda bi, t: (bi * n_tiles + t, 0))
    return pl.pallas_call(
        functools.partial(_seq_conv_kernel, n_tiles=n_tiles),
        grid=(b, n_tiles),
        in_specs=[pl.BlockSpec((8, w), lambda bi, t: (jnp.maximum((bi * n_tiles + t) * (tt // 8) - 1, 0), c0)),
                  pl.BlockSpec((tt, w), lambda bi, t: (bi * n_tiles + t, c0)),
                  pl.BlockSpec((8, w), lambda bi, t: (jnp.minimum((bi * n_tiles + t + 1) * (tt // 8),
                                                                  b * n // 8 - 1), c0)),
                  pl.BlockSpec((SHORT_CONV, w), lambda bi, t: (0, 0))],
        out_specs=[out_spec] * parts,
        out_shape=[jax.ShapeDtypeStruct((b * n, HY_WIDTH), F32)] * parts,
        compiler_params=_cparams("parallel", "parallel"),
        name="hy_seq_conv",
    )(p, p, p, conv_w)


def _hdot(a, b):
    return jnp.dot(a, b, preferred_element_type=F32, precision=lax.Precision.HIGHEST)


def _hy_filter_kernel(ft_ref, t_ref, w1t_ref, b1_ref, f1_ref, w2t_ref, b2_ref, f2_ref, w3_ref, dl_ref, k_ref, s_ref,
                      *, n, tr):
    i = pl.program_id(0)
    hid = jnp.sin(f1_ref[...] * (_hdot(w1t_ref[...], ft_ref[...]) + b1_ref[...]))
    hid = jnp.sin(f2_ref[...] * (_hdot(w2t_ref[...], hid) + b2_ref[...]))
    filt = _hdot(hid.T, w3_ref[...]) * jnp.exp(-t_ref[...] * dl_ref[...])
    row = i * tr + lax.broadcasted_iota(jnp.int32, (tr, 1), 0)
    filt = jnp.where(row == n, 0.0, filt)
    k_ref[...] = filt

    @pl.when(i == 0)
    def _():
        s_ref[...] = jnp.zeros_like(s_ref)

    s_ref[...] += jnp.sum(jnp.abs(filt), axis=0, keepdims=True)


def hy_filter(n, w1, b1, freq1, w2, b2, freq2, w3):
    pos = jnp.arange(n, dtype=F32)
    t = pos / max(n - 1, 1)
    bands = jnp.linspace(1e-4, HY_BANDS - 1, HY_BANDS, dtype=F32)
    ang = (2.0 * math.pi / n) * pos[:, None] * bands[None, :]
    feats = jnp.concatenate([t[:, None], jnp.cos(ang), -jnp.sin(ang)], axis=-1)
    feats2 = jnp.concatenate([feats, feats[:1], feats[:0:-1]], axis=0)
    kpad = 128
    feats_t = jnp.pad(feats2, ((0, 0), (0, kpad - HY_EMB_DIM))).T
    w1t = jnp.pad(w1, ((0, kpad - HY_EMB_DIM), (0, 0))).T
    min_decay = math.log(HY_DECAY_TARGET) / HY_SLOW_DECAY_PCT
    max_decay = math.log(HY_DECAY_TARGET) / HY_FAST_DECAY_PCT
    cw = HY_ORDER * HY_WIDTH
    deltas = jnp.abs(jnp.linspace(min_decay, max_decay, cw, dtype=F32)).reshape(1, cw)
    tr = min(n, 1024)
    half = n // tr
    hd = HY_FILTER_HIDDEN
    vec = lambda v: v.reshape(hd, 1)
    full = lambda shp: pl.BlockSpec(shp, lambda i: (0, 0))
    return pl.pallas_call(
        functools.partial(_hy_filter_kernel, n=n, tr=tr),
        grid=(2 * half,),
        in_specs=[pl.BlockSpec((kpad, tr), lambda i: (0, i)), pl.BlockSpec((tr, 1), lambda i: (i, 0)),
                  full((hd, kpad)), full((hd, 1)), full((hd, 1)), full((hd, hd)), full((hd, 1)), full((hd, 1)),
                  pl.BlockSpec((hd, cw), lambda i: (0, i // half)), full((1, cw))],
        out_specs=[pl.BlockSpec((tr, cw), lambda i: (i, 0)), full((1, cw))],
        out_shape=[jax.ShapeDtypeStruct((2 * n, cw), F32), jax.ShapeDtypeStruct((1, cw), F32)],
        compiler_params=_cparams("arbitrary"),
        name="hy_filter",
    )(feats_t, feats2[:, 0:1], w1t, vec(b1), vec(freq1), w2.T, vec(b2), vec(freq2), w3, deltas)


def _hy_spec_kernel(a_ref, tw_ref, m2_ref, s_ref, o_ref, *, kb):
    inv = 1.0 / s_ref[...]
    ts = [_twiddle_mul(a_ref[0, j], a_ref[1, j], tw_ref[0, :, j:j + 1], tw_ref[1, :, j:j + 1], False)
          for j in range(kb)]
    xs = [_cplx_apply(m2_ref, tr, ti) for tr, ti in ts]
    for j, (xr, xi) in enumerate(xs):
        o_ref[0, j] = xr * inv
        o_ref[1, j] = xi * inv


def hy_spec(a, tw, m2, abs_sum, kb):
    _, l1, l2, cw = a.shape
    w = HY_WIDTH
    return pl.pallas_call(
        functools.partial(_hy_spec_kernel, kb=kb),
        grid=(l1 // kb, cw // w),
        in_specs=[pl.BlockSpec((2, kb, l2, w), lambda k, c: (0, k, 0, c)),
                  pl.BlockSpec((None, 2, l2, kb), lambda k, c: (k, 0, 0, 0)),
                  pl.BlockSpec(m2.shape, lambda k, c: (0, 0)),
                  pl.BlockSpec((1, w), lambda k, c: (0, c))],
        out_specs=pl.BlockSpec((2, kb, l2, w), lambda k, c: (0, k, 0, c)),
        out_shape=jax.ShapeDtypeStruct((2, l1, l2, cw), F32),
        compiler_params=_cparams("parallel", "parallel"),
        name="hy_spec",
    )(a, tw, m2, abs_sum)


def _hy_mid_kernel(a_ref, tw_ref, m2_ref, m2c_ref, kf_ref, o_ref, *, kb):
    cs = [(tw_ref[0, :, j:j + 1], tw_ref[1, :, j:j + 1]) for j in range(kb)]
    ts = [_twiddle_mul(a_ref[0, j], a_ref[1, j], c, s, False) for j, (c, s) in enumerate(cs)]
    xs = [_cplx_apply(m2_ref, tr, ti) for tr, ti in ts]
    ps = [(xr * kf_ref[0, j] - xi * kf_ref[1, j], xr * kf_ref[1, j] + xi * kf_ref[0, j])
          for j, (xr, xi) in enumerate(xs)]
    bs = [_cplx_apply(m2c_ref, pr, pi) for pr, pi in ps]
    for j, ((br, bi), (c, s)) in enumerate(zip(bs, cs)):
        o_ref[0, j], o_ref[1, j] = _twiddle_mul(br, bi, c, s, True)


def hy_mid(a, tw, m2, m2c, kf, order, kb):
    bsz, _, l1, l2, w = a.shape
    return pl.pallas_call(
        functools.partial(_hy_mid_kernel, kb=kb),
        grid=(bsz, l1 // kb),
        in_specs=[pl.BlockSpec((None, 2, kb, l2, w), lambda b, k: (b, 0, k, 0, 0)),
                  pl.BlockSpec((None, 2, l2, kb), lambda b, k: (k, 0, 0, 0)),
                  pl.BlockSpec(m2.shape, lambda b, k: (0, 0)),
                  pl.BlockSpec(m2c.shape, lambda b, k: (0, 0)),
                  pl.BlockSpec((2, kb, l2, w), lambda b, k: (0, k, 0, order))],
        out_specs=pl.BlockSpec((None, 2, kb, l2, w), lambda b, k: (b, 0, k, 0, 0)),
        out_shape=jax.ShapeDtypeStruct((bsz, 2, l1, l2, w), F32),
        compiler_params=_cparams("parallel", "parallel"),
        name="hy_mid",
    )(a, tw, m2, m2c, kf)


def _hy_out_kernel(m_ref, bp_ref, xo_ref, z_ref, bias_ref, o_ref):
    y = jnp.dot(m_ref[...], bp_ref[...].astype(BF16), preferred_element_type=F32)
    o_ref[...] = xo_ref[...] * (y + bias_ref[...] * z_ref[...])


def hy_out(m, bp, xo, z, bias_row, tn):
    bsz, k, n = bp.shape
    r = m.shape[0]
    blk = pl.BlockSpec((None, r, tn), lambda b, j: (b, 0, j))
    return pl.pallas_call(
        _hy_out_kernel,
        grid=(bsz, n // tn),
        in_specs=[pl.BlockSpec((r, k), lambda b, j: (0, 0)),
                  pl.BlockSpec((None, k, tn), lambda b, j: (b, 0, j)), blk, blk,
                  pl.BlockSpec((1, tn), lambda b, j: (0, j))],
        out_specs=blk,
        out_shape=jax.ShapeDtypeStruct((bsz, r, n), F32),
        compiler_params=_cparams("parallel", "parallel"),
        name="hy_out",
    )(m, bp, xo, z, bias_row)


SUB = 8
HY_KB = 13
HY_SLOTS = 16


def _half_spectrum_blocks(l1, kb):
    return -(-(l1 // 2 + 1) // kb)


def _kron_stage1(l1, n1_used, kb):
    nblk = _half_spectrum_blocks(l1, kb)
    c, s = _cos_sin(nblk * kb, n1_used, l1)
    m = np.stack([c, -s], axis=0).reshape(2, nblk, kb, n1_used)
    m = np.transpose(m, (1, 0, 2, 3)).reshape(nblk, 2 * kb, n1_used)
    return _const_bf16(np.stack([np.kron(blk, np.eye(SUB)) for blk in m]))


def _kron_stage_out(l1, n1_used, kb, slots):
    nblk = _half_spectrum_blocks(l1, kb)
    c, s = _cos_sin(n1_used, nblk * kb, l1)
    k1 = np.arange(nblk * kb)
    mult = np.where((k1 == 0) | (k1 == l1 // 2), 1.0, np.where(k1 < l1 // 2, 2.0, 0.0))
    m = np.stack([c * mult, -s * mult], axis=1).reshape(n1_used, 2, nblk, kb) / (l1 * l1)
    m = np.pad(m, ((0, 0), (0, 0), (0, 0), (0, slots - kb)))
    m = np.transpose(m, (2, 0, 1, 3)).reshape(nblk, n1_used, 2 * slots)
    return _const_bf16(np.stack([np.kron(blk, np.eye(SUB)) for blk in m]))


def _strided_stage_in(kin_ref, src_ref, a_ref):
    n_sub, width = src_ref.shape[1] // SUB, src_ref.shape[2]
    rows = src_ref.shape[0] * SUB
    for j in range(n_sub):
        r = jnp.dot(kin_ref[...], src_ref[:, j * SUB:(j + 1) * SUB, :].reshape(rows, width).astype(BF16),
                    preferred_element_type=F32)
        a_ref[:, j * SUB:(j + 1) * SUB, :] = r.reshape(a_ref.shape[0], SUB, width)


def _hy_conv_kernel(z_ref, xo_ref, kin_ref, tw_ref, m2_ref, m2c_ref, kf_ref, kout_ref, bias_ref, o_ref,
                    a_ref, b_ref, *, kb):
    k = pl.program_id(1)
    slots = b_ref.shape[0] // 2

    @pl.when(k == 0)
    def _():
        o_ref[...] = jnp.zeros_like(o_ref)
        b_ref[...] = jnp.zeros_like(b_ref)

    _strided_stage_in(kin_ref, z_ref, a_ref)
    grp = kb
    for g0 in range(0, kb, grp):
        js = range(g0, g0 + grp)
        cs = [(tw_ref[0, :, j:j + 1], tw_ref[1, :, j:j + 1]) for j in js]
        ts = [_twiddle_mul(a_ref[j], a_ref[kb + j], c, s, False) for j, (c, s) in zip(js, cs)]
        xs = [_cplx_apply(m2_ref, tr, ti) for tr, ti in ts]
        ps = [(xr * kf_ref[0, j] - xi * kf_ref[1, j], xr * kf_ref[1, j] + xi * kf_ref[0, j])
              for j, (xr, xi) in zip(js, xs)]
        bs = [_cplx_apply(m2c_ref, pr, pi) for pr, pi in ps]
        for j, (br, bi), (c, s) in zip(js, bs, cs):
            b_ref[j], b_ref[slots + j] = _twiddle_mul(br, bi, c, s, True)
    n_sub, width = o_ref.shape[1] // SUB, o_ref.shape[2]
    for j in range(n_sub):
        r = jnp.dot(kout_ref[...], b_ref[:, j * SUB:(j + 1) * SUB, :].reshape(2 * slots * SUB, width).astype(BF16),
                    preferred_element_type=F32)
        o_ref[:, j * SUB:(j + 1) * SUB, :] += r.reshape(o_ref.shape[0], SUB, width)

    @pl.when(k == pl.num_programs(1) - 1)
    def _():
        o_ref[...] = xo_ref[...] * (o_ref[...] + bias_ref[...] * z_ref[...])


def hy_conv_long(z, xo, kf, bias, order, b, n):
    w, kb = HY_WIDTH, HY_KB
    l1 = l2 = 128
    n1 = n // l2
    nblk = _half_spectrum_blocks(l1, kb)
    tw = _twiddles(l1, l2, kb, nblk * kb)
    m2, m2c = _stage2_matrices(l2)
    kin, kout = _kron_stage1(l1, n1, kb), _kron_stage_out(l1, n1, kb, HY_SLOTS)
    tok = pl.BlockSpec((None, n1, l2, w), lambda bi, k: (bi, 0, 0, 0), pipeline_mode=pl.Buffered(1))
    const = lambda a: pl.BlockSpec(a.shape, lambda bi, k: (0,) * a.ndim)
    out = pl.pallas_call(
        functools.partial(_hy_conv_kernel, kb=kb),
        grid=(b, nblk),
        in_specs=[tok, tok,
                  pl.BlockSpec((None,) + kin.shape[1:], lambda bi, k: (k, 0, 0)),
                  pl.BlockSpec((None, 2, l2, kb), lambda bi, k: (k, 0, 0, 0)),
                  const(m2), const(m2c),
                  pl.BlockSpec((2, kb, l2, w), lambda bi, k: (0, k, 0, order)),
                  pl.BlockSpec((None,) + kout.shape[1:], lambda bi, k: (k, 0, 0)),
                  pl.BlockSpec((1, w), lambda bi, k: (0, 0))],
        out_specs=pl.BlockSpec((None, n1, l2, w), lambda bi, k: (bi, 0, 0, 0), pipeline_mode=pl.Buffered(1)),
        out_shape=jax.ShapeDtypeStruct((b, n1, l2, w), F32),
        scratch_shapes=[pltpu.VMEM((2 * kb, l2, w), F32), pltpu.VMEM((2 * HY_SLOTS, l2, w), F32)],
        compiler_params=_cparams("parallel", "arbitrary"),
        name="hy_conv_long",
    )(z.reshape(b, n1, l2, w), xo.reshape(b, n1, l2, w), kin, tw, m2, m2c, kf, kout, bias.reshape(1, w))
    return out.reshape(b * n, w)


def _hy_spec_long_kernel(kern_ref, kin_ref, tw_ref, m2_ref, s_ref, o_ref, a_ref, *, kb):
    _strided_stage_in(kin_ref, kern_ref, a_ref)
    inv = 1.0 / s_ref[...]
    ts = [_twiddle_mul(a_ref[j], a_ref[kb + j], tw_ref[0, :, j:j + 1], tw_ref[1, :, j:j + 1], False)
          for j in range(kb)]
    xs = [_cplx_apply(m2_ref, tr, ti) for tr, ti in ts]
    for j, (xr, xi) in enumerate(xs):
        o_ref[0, j] = xr * inv
        o_ref[1, j] = xi * inv


def hy_spec_long(kern, abs_sum):
    w, kb = HY_WIDTH, HY_KB
    l1 = l2 = 128
    cw = kern.shape[1]
    nblk = _half_spectrum_blocks(l1, kb)
    tw = _twiddles(l1, l2, kb, nblk * kb)
    m2, _ = _stage2_matrices(l2)
    kin = _kron_stage1(l1, l1, kb)
    return pl.pallas_call(
        functools.partial(_hy_spec_long_kernel, kb=kb),
        grid=(cw // w, nblk),
        in_specs=[pl.BlockSpec((l1, l2, w), lambda c, k: (0, 0, c), pipeline_mode=pl.Buffered(1)),
                  pl.BlockSpec((None,) + kin.shape[1:], lambda c, k: (k, 0, 0)),
                  pl.BlockSpec((None, 2, l2, kb), lambda c, k: (k, 0, 0, 0)),
                  pl.BlockSpec(m2.shape, lambda c, k: (0, 0)),
                  pl.BlockSpec((1, w), lambda c, k: (0, c))],
        out_specs=pl.BlockSpec((2, kb, l2, w), lambda c, k: (0, k, 0, c)),
        out_shape=jax.ShapeDtypeStruct((2, nblk * kb, l2, cw), F32),
        scratch_shapes=[pltpu.VMEM((2 * kb, l2, w), F32)],
        compiler_params=_cparams("parallel", "arbitrary"),
        name="hy_spec_long",
    )(kern.reshape(l1, l2, cw), kin, tw, m2, abs_sum)


def hyena_pallas(p, conv_w, w1, b1, freq1, w2, b2, freq2, w3, bias, b, n):
    w = HY_WIDTH
    x0, x1, v = seq_conv(p, conv_w, b, n)
    kern, abs_sum = hy_filter(n, w1, b1, freq1, w2, b2, freq2, w3)
    z = v
    if n == 8192:
        kf = hy_spec_long(kern, abs_sum)
        for order, xo in enumerate((x0, x1)):
            z = hy_conv_long(z, xo, kf, bias[order], order, b, n)
        return z
    l2 = 2 * n
    tw = _twiddles(1, l2, 1)
    m2, m2c = _stage2_matrices(l2)
    ak = jnp.stack([kern, jnp.zeros_like(kern)], axis=0).reshape(2, 1, l2, HY_ORDER * w)
    m_out = _const_bf16(np.eye(n, 2 * l2) / l2)
    kf = hy_spec(ak, tw, m2, abs_sum, 1)
    for order, xo in enumerate((x0, x1)):
        zp = jnp.pad(z.reshape(b, 1, 1, n, w), ((0, 0), (0, 0), (0, 0), (0, n), (0, 0)))
        a = jnp.concatenate([zp, jnp.zeros_like(zp)], axis=1)
        bp = hy_mid(a, tw, m2, m2c, kf, order, 1)
        z = hy_out(m_out, bp.reshape(b, 2 * l2, w), xo.reshape(b, n, w), z.reshape(b, n, w),
                   bias[order].reshape(1, w), w).reshape(b * n, w)
    return z


def _prep_w_in(w):
    main = jnp.concatenate([w[:, OFF_Q:OFF_Z], w[:, OFF_HY:OFF_GATE], w[:, OFF_FN:OFF_HY], w[:, OFF_Z:OFF_BETA]],
                           axis=1).astype(BF16)
    small = jnp.pad(w[:, OFF_BETA:OFF_FN], ((0, 0), (0, SMALL_WIDTH - 4 * DN_HEADS))).astype(BF16)
    return main, small, w[:, OFF_GATE:IN_WIDTH].astype(BF16)


def kernel(x, c, ctx, c_ctx, w_mod, b_mod, norm1, norm2, w_in, dn_conv, dn_a_log, dn_dt_bias,
           dn_out_norm, hy_conv, hy_w1, hy_b1, hy_freq1, hy_w2, hy_b2, hy_freq2, hy_w3, hy_bias,
           w_branch_a, w_branch_b, w_branch_c, w_out, w_ff1, w_ff2, final_norm):
    b, n_lat, d = x.shape
    rows = n_lat // GRID_W
    n_ctx = ctx.shape[1]
    tm_x, tm_c = 1024, n_ctx

    c_rows = jnp.concatenate([c, c_ctx[None], jnp.zeros((8 - b - 1, d), F32)], axis=0)
    mods = mod_vectors(c_rows, w_mod, b_mod)
    s_zero = jnp.zeros((2, b, DN_HEADS, DN_HEAD_DIM, DN_HEAD_DIM), F32)
    h, hc = x.reshape(b * n_lat, d), ctx.reshape(b * n_ctx, d)

    for l in range(DEPTH):
        last = l == DEPTH - 1
        mv = mods[l].reshape(8, N_MOD, 1, d)
        mx = [mv[:b, i] for i in range(N_MOD)]
        mc = [mv[b:b + 1, i] for i in range(N_MOD)]
        w_main, w_small, w_gate = _prep_w_in(w_in[l])
        wa, wb, wc, wo = (w.astype(BF16) for w in (w_branch_a[l], w_branch_b[l], w_branch_c[l], w_out[l]))
        w1, w2 = w_ff1[l].astype(BF16), w_ff2[l].astype(BF16)
        n1, n2 = norm1[l][None, None, :], norm2[l][None, None, :]

        p_c, small_c = in_proj(hc, n1 * (1.0 + mc[1]), mc[0], w_main, w_small, b * n_ctx, tm_c)
        p_x, small_x = in_proj(h, n1 * (1.0 + mx[1]), mx[0], w_main, w_small, n_lat, tm_x)

        def mix(p, n):
            y_c = hyena_pallas(p, hy_conv[l], hy_w1[l], hy_b1[l], hy_freq1[l], hy_w2[l], hy_b2[l], hy_freq2[l],
                               hy_w3[l], hy_bias[l], b, n)
            return fnet_branch(p, b, n), y_c

        ocf, ocb, s_ctx = delta_branch(p_c, small_c, dn_conv[l], dn_a_log[l], dn_dt_bias[l], s_zero,
                                       b, n_ctx, 1, n_ctx)
        oxf, oxb, _ = delta_branch(p_x, small_x, dn_conv[l], dn_a_log[l], dn_dt_bias[l], s_ctx,
                                   b, n_lat, rows, GRID_W)

        y_b, y_c = mix(p_x, n_lat)
        h = merge(oxf, oxb, p_x, y_b, y_c, h, n1 * (1.0 + mx[1]), mx[0], mx[2], dn_out_norm[l],
                  w_gate, wa, wb, wc, wo, n_lat, 512)
        h = mlp(h, n2 * (1.0 + mx[4]), mx[3], mx[5], w1, w2, n_lat, tm_x, final_norm if last else None)

        if not last:
            y_b, y_c = mix(p_c, n_ctx)
            hc = merge(ocf, ocb, p_c, y_b, y_c, hc, n1 * (1.0 + mc[1]), mc[0], mc[2], dn_out_norm[l],
                       w_gate, wa, wb, wc, wo, b * n_ctx, tm_c)
            hc = mlp(hc, n2 * (1.0 + mc[4]), mc[3], mc[5], w1, w2, b * n_ctx, tm_c)

    return h.reshape(b, n_lat, d)
```

```python
import functools
import math

import jax
import jax.numpy as jnp
import numpy as np
from jax import lax
from jax.experimental import pallas as pl
from jax.experimental.pallas import tpu as pltpu

D_MODEL = 1024
DEPTH = 2
GRID_W = 64
NORM_EPS = 1e-6
N_MOD = 6

DN_HEADS = 4
DN_HEAD_DIM = 128
DN_WIDTH = DN_HEADS * DN_HEAD_DIM
DN_CHUNK = 64
SHORT_CONV = 3

FN_GROUPS = 4
FN_GROUP_DIM = 64
FN_WIDTH = FN_GROUPS * FN_GROUP_DIM

HY_WIDTH = 256
HY_ORDER = 2
HY_EMB_DIM = 33
HY_BANDS = (HY_EMB_DIM - 1) // 2
HY_FILTER_HIDDEN = 64
HY_FAST_DECAY_PCT = 0.3
HY_SLOW_DECAY_PCT = 1.5
HY_DECAY_TARGET = 1e-2

N_BRANCHES = 3
D_FF = 4 * D_MODEL

OFF_Q = 0
OFF_Z = 3 * DN_WIDTH
OFF_BETA = OFF_Z + DN_WIDTH
OFF_A = OFF_BETA + 2 * DN_HEADS
OFF_FN = OFF_A + 2 * DN_HEADS
OFF_HY = OFF_FN + FN_WIDTH
OFF_GATE = OFF_HY + (HY_ORDER + 1) * HY_WIDTH
IN_WIDTH = OFF_GATE + N_BRANCHES * D_MODEL

P_QKV = 0
P_HY = P_QKV + 3 * DN_WIDTH
P_FN = P_HY + (HY_ORDER + 1) * HY_WIDTH
P_Z = P_FN + FN_WIDTH
P_WIDTH = P_Z + DN_WIDTH
SMALL_WIDTH = 128

F32 = jnp.float32
BF16 = jnp.bfloat16
VMEM_LIMIT = 56 * 1024 * 1024
SUB = 8


def _cparams(*sem):
    return pltpu.CompilerParams(dimension_semantics=sem, vmem_limit_bytes=VMEM_LIMIT)


def _bdot(a, b):
    return jnp.dot(a.astype(BF16), b.astype(BF16), preferred_element_type=F32)


def _sigmoid(x):
    return 0.5 * jnp.tanh(0.5 * x) + 0.5


def _modnorm(xf, gs, sh):
    r = lax.rsqrt(jnp.mean(xf * xf, axis=-1, keepdims=True) + NORM_EPS)
    return xf * r * gs + sh


def _mod_kernel(c_ref, w_ref, b_ref, o_ref):
    c = c_ref[...]
    o_ref[...] = _bdot(c * _sigmoid(c), w_ref[...]) + b_ref[...]


def mod_vectors(c_rows, w_mod, b_mod):
    tn = 1536
    n = N_MOD * D_MODEL
    return pl.pallas_call(
        _mod_kernel,
        grid=(DEPTH, n // tn),
        in_specs=[pl.BlockSpec((8, D_MODEL), lambda l, j: (0, 0)),
                  pl.BlockSpec((None, D_MODEL, tn), lambda l, j: (l, 0, j)),
                  pl.BlockSpec((None, 1, tn), lambda l, j: (l, 0, j))],
        out_specs=pl.BlockSpec((None, 8, tn), lambda l, j: (l, 0, j)),
        out_shape=jax.ShapeDtypeStruct((DEPTH, 8, n), F32),
        compiler_params=_cparams("parallel", "parallel"),
        name="mod_vectors",
    )(c_rows, w_mod, b_mod.reshape(DEPTH, 1, n))


def _in_proj_kernel(x_ref, gs_ref, sh_ref, w_ref, ws_ref, p_ref, small_ref, *, tn):
    xn = _modnorm(x_ref[...], gs_ref[0], sh_ref[0]).astype(BF16)
    small_ref[...] = jnp.dot(xn, ws_ref[...], preferred_element_type=F32)
    for j in range(p_ref.shape[1] // tn):
        p_ref[:, j * tn:(j + 1) * tn] = jnp.dot(xn, w_ref[:, j * tn:(j + 1) * tn], preferred_element_type=F32)


def in_proj(x2d, gs, sh, w_main, w_small, rows_per_mod, tm):
    m = x2d.shape[0]
    tiles_per_mod = rows_per_mod // tm
    mod_spec = pl.BlockSpec((1, 1, D_MODEL), lambda i: (i // tiles_per_mod, 0, 0))
    resident = lambda a: pl.BlockSpec(a.shape, lambda i: (0, 0), pipeline_mode=pl.Buffered(1))
    return pl.pallas_call(
        functools.partial(_in_proj_kernel, tn=1024),
        grid=(m // tm,),
        in_specs=[pl.BlockSpec((tm, D_MODEL), lambda i: (i, 0)), mod_spec, mod_spec,
                  resident(w_main), resident(w_small)],
        out_specs=[pl.BlockSpec((tm, P_WIDTH), lambda i: (i, 0)),
                   pl.BlockSpec((tm, SMALL_WIDTH), lambda i: (i, 0))],
        out_shape=[jax.ShapeDtypeStruct((m, P_WIDTH), F32), jax.ShapeDtypeStruct((m, SMALL_WIDTH), F32)],
        compiler_params=_cparams("parallel"),
        name="in_proj",
    )(x2d, gs, sh, w_main, w_small)


def _merge_kernel(of_ref, ob_ref, z_ref, yb_ref, yc_ref, h_ref, gs_ref, sh_ref, gate_ref, nrm_ref,
                  wg_ref, wa_ref, wb_ref, wc_ref, wo_ref, out_ref):
    xn = _modnorm(h_ref[...], gs_ref[0], sh_ref[0]).astype(BF16)
    o = of_ref[...].astype(F32) + ob_ref[...].astype(F32)
    z = z_ref[...]
    heads = []
    for hd in range(DN_HEADS):
        sl = slice(hd * DN_HEAD_DIM, (hd + 1) * DN_HEAD_DIM)
        oh, zh = o[:, sl], z[:, sl]
        r = lax.rsqrt(jnp.mean(oh * oh, axis=-1, keepdims=True) + NORM_EPS)
        heads.append(oh * r * nrm_ref[...] * (zh * _sigmoid(zh)))
    ya = jnp.concatenate(heads, axis=-1)
    merged = None
    for i, (y, w_ref) in enumerate(((ya, wa_ref), (yb_ref[...], wb_ref), (yc_ref[...], wc_ref))):
        g = jnp.dot(xn, wg_ref[:, i * D_MODEL:(i + 1) * D_MODEL], preferred_element_type=F32)
        term = _sigmoid(g) * _bdot(y, w_ref[...])
        merged = term if merged is None else merged + term
    out_ref[...] = h_ref[...] + gate_ref[0] * _bdot(merged, wo_ref[...])


def merge(o_f, o_b, p, y_b, y_c, h2d, gs, sh, gate, dn_out_norm, wg, wa, wb, wc, wo, rows_per_mod, tm):
    m = h2d.shape[0]
    tiles_per_mod = rows_per_mod // tm
    row = lambda w: pl.BlockSpec((tm, w), lambda i: (i, 0))
    full = lambda a: pl.BlockSpec(a.shape, lambda i: (0,) * a.ndim, pipeline_mode=pl.Buffered(1))
    mod_spec = pl.BlockSpec((1, 1, D_MODEL), lambda i: (i // tiles_per_mod, 0, 0))
    nrm = dn_out_norm.reshape(1, DN_HEAD_DIM)
    return pl.pallas_call(
        _merge_kernel,
        grid=(m // tm,),
        in_specs=[row(DN_WIDTH), row(DN_WIDTH),
                  pl.BlockSpec((tm, DN_WIDTH), lambda i: (i, P_Z // DN_WIDTH)),
                  row(FN_WIDTH), row(HY_WIDTH), row(D_MODEL), mod_spec, mod_spec, mod_spec,
                  full(nrm), full(wg), full(wa), full(wb), full(wc), full(wo)],
        out_specs=row(D_MODEL),
        out_shape=jax.ShapeDtypeStruct((m, D_MODEL), F32),
        compiler_params=_cparams("parallel"),
        name="merge",
    )(o_f, o_b, p, y_b, y_c, h2d, gs, sh, gate, nrm, wg, wa, wb, wc, wo)


def _mlp_kernel(h_ref, gs_ref, sh_ref, gate_ref, w1_ref, w2_ref, fin_ref, out_ref, acc_ref, *, final, tf):
    xn = _modnorm(h_ref[...], gs_ref[0], sh_ref[0]).astype(BF16)
    for j in range(w1_ref.shape[1] // tf):
        a = jnp.maximum(jnp.dot(xn, w1_ref[:, j * tf:(j + 1) * tf], preferred_element_type=F32), 0.0)
        part = jnp.dot((a * a).astype(BF16), w2_ref[j * tf:(j + 1) * tf, :], preferred_element_type=F32)
        if j == 0:
            acc_ref[...] = part
        else:
            acc_ref[...] += part
    y = h_ref[...] + gate_ref[0] * acc_ref[...]
    if final:
        y = y * lax.rsqrt(jnp.mean(y * y, axis=-1, keepdims=True) + NORM_EPS) * fin_ref[...]
    out_ref[...] = y


def mlp(h2d, gs, sh, gate, w1, w2, rows_per_mod, tm, final_gain=None):
    m = h2d.shape[0]
    tiles_per_mod = rows_per_mod // tm
    mod_spec = pl.BlockSpec((1, 1, D_MODEL), lambda i: (i // tiles_per_mod, 0, 0))
    resident = lambda a: pl.BlockSpec(a.shape, lambda i: (0, 0), pipeline_mode=pl.Buffered(1))
    final = final_gain is not None
    fin = (final_gain if final else jnp.ones((D_MODEL,), F32)).reshape(1, D_MODEL)
    return pl.pallas_call(
        functools.partial(_mlp_kernel, final=final, tf=1024),
        grid=(m // tm,),
        in_specs=[pl.BlockSpec((tm, D_MODEL), lambda i: (i, 0)), mod_spec, mod_spec, mod_spec,
                  resident(w1), resident(w2), pl.BlockSpec((1, D_MODEL), lambda i: (0, 0))],
        out_specs=pl.BlockSpec((tm, D_MODEL), lambda i: (i, 0)),
        out_shape=jax.ShapeDtypeStruct((m, D_MODEL), F32),
        scratch_shapes=[pltpu.VMEM((tm, D_MODEL), F32)],
        compiler_params=_cparams("parallel"),
        name="mlp",
    )(h2d, gs, sh, gate, w1, w2, fin)


def _dnconv_kernel(prev_ref, cur_ref, next_ref, w_ref, o_ref, *, cols, n_tiles):
    t, j = pl.program_id(1), pl.program_id(2)
    tt = cur_ref.shape[0]
    prev = jnp.where(t == 0, 0.0, prev_ref[...])
    nxt = jnp.where(t == n_tiles - 1, 0.0, next_ref[...])
    ext = jnp.concatenate([prev, cur_ref[...], nxt], axis=0)
    n_ext = tt + 2 * cols
    col = lax.broadcasted_iota(jnp.int32, (n_ext, 1), 0) % cols
    left = jnp.where(col == 0, 0.0, pltpu.roll(ext, 1, axis=0))
    right = jnp.where(col == cols - 1, 0.0, pltpu.roll(ext, n_ext - 1, axis=0))
    acc = jnp.zeros((tt, DN_WIDTH), F32)
    for dr in range(SHORT_CONV):
        base = dr * cols
        acc = (acc + w_ref[3 * dr:3 * dr + 1, :] * left[base:base + tt]
               + w_ref[3 * dr + 1:3 * dr + 2, :] * ext[base:base + tt]
               + w_ref[3 * dr + 2:3 * dr + 3, :] * right[base:base + tt])
    y = acc * _sigmoid(acc)
    q_scale = jnp.where(j == 0, DN_HEAD_DIM ** -0.5, 1.0)
    for hd in range(DN_HEADS):
        sl = slice(hd * DN_HEAD_DIM, (hd + 1) * DN_HEAD_DIM)
        yh = y[:, sl]
        nrm = lax.rsqrt(jnp.sum(yh * yh, axis=-1, keepdims=True) + NORM_EPS) * q_scale
        o_ref[:, sl] = yh * jnp.where(j < 2, nrm, 1.0)


def dn_conv_prep(p, conv_w, b, n, rows, cols):
    tr = min(rows, 32)
    tt = tr * cols
    n_tiles = rows // tr
    nblk = b * n // cols
    c0 = P_QKV // DN_WIDTH
    return pl.pallas_call(
        functools.partial(_dnconv_kernel, cols=cols, n_tiles=n_tiles),
        grid=(b, n_tiles, 3),
        in_specs=[pl.BlockSpec((cols, DN_WIDTH),
                               lambda bi, t, j: (jnp.maximum(bi * rows + t * tr - 1, 0), c0 + j)),
                  pl.BlockSpec((tt, DN_WIDTH), lambda bi, t, j: (bi * n_tiles + t, c0 + j)),
                  pl.BlockSpec((cols, DN_WIDTH),
                               lambda bi, t, j: (jnp.minimum(bi * rows + (t + 1) * tr, nblk - 1), c0 + j)),
                  pl.BlockSpec((SHORT_CONV * SHORT_CONV, DN_WIDTH), lambda bi, t, j: (0, j))],
        out_specs=pl.BlockSpec((tt, DN_WIDTH), lambda bi, t, j: (bi * n_tiles + t, j)),
        out_shape=jax.ShapeDtypeStruct((b * n, 3 * DN_WIDTH), F32),
        compiler_params=_cparams("parallel", "parallel", "parallel"),
        name="dn_conv_prep",
    )(p, p, p, conv_w.reshape(SHORT_CONV * SHORT_CONV, 3 * DN_WIDTH))


PREP_CHUNKS = 8
PREP_GROUP = 4


def _softplus(x):
    return jnp.maximum(x, 0.0) + jnp.log(1.0 + jnp.exp(-jnp.abs(x)))


def _delta_prep_kernel(qkv_ref, sm_ref, smt_ref, prm_ref, prmt_ref, u0_ref, lhs1_ref, lhs2_ref, gl_ref, *, n_chunks):
    cc = DN_CHUNK
    nh = DN_HEADS
    sm = sm_ref[...]
    beta_all = _sigmoid(sm)
    g_all = -jnp.exp(prm_ref[0:1, :]) * _softplus(sm + prm_ref[1:2, :])
    gt_all = -jnp.exp(prmt_ref[:, 0:1]) * _softplus(smt_ref[...] + prmt_ref[:, 1:2])
    lt = 2 * cc
    lane = lax.broadcasted_iota(jnp.int32, (gt_all.shape[0], lt), 1) % cc
    gt_tiles = []
    for t in range(gt_all.shape[1] // lt):
        gt_f = gt_b = gt_all[:, t * lt:(t + 1) * lt]
        s = 1
        while s < cc:
            gt_f = gt_f + jnp.where(lane >= s, pltpu.roll(gt_f, s, axis=1), 0.0)
            gt_b = gt_b + jnp.where(lane < cc - s, pltpu.roll(gt_b, lt - s, axis=1), 0.0)
            s *= 2
        gt_tiles.append((gt_f, gt_b))
    ri = lax.broadcasted_iota(jnp.int32, (cc, cc), 0)
    ci_ = lax.broadcasted_iota(jnp.int32, (cc, cc), 1)
    sub = lax.broadcasted_iota(jnp.int32, (cc, 1), 0)
    blk = lambda s: (ri // s) == (ci_ // s)
    leaf = 8

    heads = []
    for ci in range(n_chunks):
        rows = slice(ci * cc, (ci + 1) * cc)
        gc_f = g_all[rows]
        gc_b = gc_f
        s = 1
        while s < cc:
            gc_f = gc_f + jnp.where(sub >= s, pltpu.roll(gc_f, s, axis=0), 0.0)
            gc_b = gc_b + jnp.where(sub < cc - s, pltpu.roll(gc_b, cc - s, axis=0), 0.0)
            s *= 2
        for h in range(nh):
            q = qkv_ref[rows, h * 128:(h + 1) * 128]
            k = qkv_ref[rows, (nh + h) * 128:(nh + h + 1) * 128]
            v = qkv_ref[rows, (2 * nh + h) * 128:(2 * nh + h + 1) * 128]
            heads.append((ci, rows, h, q, k, v, gc_f, gc_b))
    qkks = [lax.dot_general(jnp.concatenate([q, k], axis=0).astype(BF16), k.astype(BF16),
                            (((1,), (1,)), ((), ())), preferred_element_type=F32)
            for (_, _, _, q, k, _, _, _) in heads]
    per_group = PREP_GROUP * nh
    for g0 in range(0, len(heads), per_group):
        group = list(zip(heads[g0:g0 + per_group], qkks[g0:g0 + per_group]))

        def token_scalars(ci, rows, h, d, gc_f, gc_b):
            beta = jnp.broadcast_to(beta_all[rows, d * nh + h:d * nh + h + 1], (cc, DN_HEAD_DIM))
            ca_ = 2 * nh + d * nh + h
            g_col = jnp.broadcast_to((gc_f if d == 0 else gc_b)[:, ca_:ca_ + 1], (cc, DN_HEAD_DIM))
            g_last = g_col[cc - 1:cc] if d == 0 else g_col[0:1]
            return beta, g_col, g_last

        mats = []
        for (ci, rows, h, q, k, v, gc_f, gc_b), qkk in group:
            qk, kk = qkk[:cc], qkk[cc:]
            for d in range(2):
                beta, g_col, g_last = token_scalars(ci, rows, h, d, gc_f, gc_b)
                lo = (ci % 2) * cc
                g_row = gt_tiles[ci // 2][d][2 * nh + d * nh + h:2 * nh + d * nh + h + 1, lo:lo + cc]
                incl = (ri >= ci_) if d == 0 else (ri <= ci_)
                strict = (ri > ci_) if d == 0 else (ri < ci_)
                decay = jnp.exp(jnp.where(incl, g_col[:, :cc] - g_row, -1e30))
                mats.append(jnp.where(strict, beta[:, :cc] * kk * decay, 0.0))
                lhs1_ref[d, h, ci, cc:2 * cc, :] = (q * jnp.exp(g_col)).astype(BF16)
                lhs2_ref[d, h, ci, 0:cc, :] = (qk * decay).astype(BF16)
                lhs2_ref[d, h, ci, cc:3 * cc, :] = (k * jnp.exp(g_last - g_col)).T.astype(BF16)
                gl_ref[d, h, ci] = jnp.broadcast_to(jnp.exp(g_last), (8, DN_HEAD_DIM))

        pws = [jnp.where(blk(leaf), a, 0.0) for a in mats]
        devs = [-pw for pw in pws]
        for _ in range(2):
            pws = [_bdot(pw, pw) for pw in pws]
            cross = [_bdot(dev, pw) for dev, pw in zip(devs, pws)]
            devs = [dev + pw + x for dev, pw, x in zip(devs, pws, cross)]
        s = leaf
        while s < cc:
            offs = [jnp.where(blk(2 * s) & jnp.logical_not(blk(s)), a, 0.0) for a in mats]
            xs = [off + _bdot(dev, off) for dev, off in zip(devs, offs)]
            devs = [dev - x - _bdot(x, dev) for dev, x in zip(devs, xs)]
            s *= 2
        chains, rhss = [], []
        for (ci, rows, h, q, k, v, gc_f, gc_b), _ in group:
            for d in range(2):
                beta, g_col, _ = token_scalars(ci, rows, h, d, gc_f, gc_b)
                chains.append((ci, rows, h, d))
                rhss.append(jnp.concatenate([v * beta, k * (beta * jnp.exp(g_col))], axis=1))
        uws = [rhs + _bdot(dev, rhs) for dev, rhs in zip(devs, rhss)]
        for (ci, rows, h, d), uw in zip(chains, uws):
            u0_ref[d, h, rows, :] = uw[:, :128]
            lhs1_ref[d, h, ci, 0:cc, :] = uw[:, 128:].astype(BF16)


def delta_prep(qkv, small, small_t, prm, prm_t, b, n):
    nc = n // DN_CHUNK
    cb = min(PREP_CHUNKS, nc)
    nblk = nc // cb
    tt = cb * DN_CHUNK
    nh = DN_HEADS
    return pl.pallas_call(
        functools.partial(_delta_prep_kernel, n_chunks=cb),
        grid=(b, nblk),
        in_specs=[pl.BlockSpec((tt, 3 * DN_WIDTH), lambda bi, c: (bi * nblk + c, 0)),
                  pl.BlockSpec((tt, SMALL_WIDTH), lambda bi, c: (bi * nblk + c, 0)),
                  pl.BlockSpec((4 * nh, tt), lambda bi, c: (0, bi * nblk + c)),
                  pl.BlockSpec((2, SMALL_WIDTH), lambda bi, c: (0, 0)),
                  pl.BlockSpec((4 * nh, 2), lambda bi, c: (0, 0))],
        out_specs=[pl.BlockSpec((2, None, nh, tt, 128), lambda bi, c: (0, bi, 0, c, 0)),
                   pl.BlockSpec((2, None, nh, cb, 128, 128), lambda bi, c: (0, bi, 0, c, 0, 0)),
                   pl.BlockSpec((2, None, nh, cb, 192, 64), lambda bi, c: (0, bi, 0, c, 0, 0)),
                   pl.BlockSpec((2, None, nh, cb, 8, 128), lambda bi, c: (0, bi, 0, c, 0, 0))],
        out_shape=[jax.ShapeDtypeStruct((2, b, nh, n, 128), F32),
                   jax.ShapeDtypeStruct((2, b, nh, nc, 128, 128), BF16),
                   jax.ShapeDtypeStruct((2, b, nh, nc, 192, 64), BF16),
                   jax.ShapeDtypeStruct((2, b, nh, nc, 8, 128), F32)],
        compiler_params=_cparams("parallel", "parallel"),
        name="delta_prep",
    )(qkv, small, small_t, prm, prm_t)


def _delta_scan_kernel(u0f_ref, u0b_ref, l1f_ref, l1b_ref, l2f_ref, l2b_ref, glf_ref, glb_ref, s0_ref,
                       of_ref, ob_ref, sout_ref, st_ref, *, sc, nb):
    n = pl.program_id(0)
    cc = DN_CHUNK

    @pl.when(n == 0)
    def _():
        st_ref[...] = s0_ref[...]

    dirs = ((u0f_ref, l1f_ref, l2f_ref, glf_ref, of_ref), (u0b_ref, l1b_ref, l2b_ref, glb_ref, ob_ref))

    def body(i, carry):
        chains = []
        for d in range(2):
            ci = i if d == 0 else sc - 1 - i
            r0 = pl.multiple_of(ci * cc, cc)
            chains += [(d, bi, h, ci, r0) for bi in range(nb) for h in range(DN_HEADS)]
        sts = [st_ref[d, bi, h] for (d, bi, h, _, _) in chains]
        r1s = [jnp.dot(dirs[d][1][bi, h, ci], st.astype(BF16), preferred_element_type=F32)
               for (d, bi, h, ci, _), st in zip(chains, sts)]
        us = [dirs[d][0][bi, h, pl.ds(r0, cc), :] - r1[:cc] for (d, bi, h, _, r0), r1 in zip(chains, r1s)]
        r2s = [jnp.dot(dirs[d][2][bi, h, ci], u.astype(BF16), preferred_element_type=F32)
               for (d, bi, h, ci, _), u in zip(chains, us)]
        for (d, bi, h, ci, r0), st, r1, r2 in zip(chains, sts, r1s, r2s):
            dirs[d][4][bi, pl.ds(r0, cc), h * 128:(h + 1) * 128] = (r1[cc:] + r2[:cc]).astype(BF16)
            st_ref[d, bi, h] = st * dirs[d][3][bi, h, ci, 0:1, :] + r2[cc:]
        return carry

    lax.fori_loop(0, sc, body, 0, unroll=4)

    @pl.when(n == pl.num_programs(0) - 1)
    def _():
        sout_ref[...] = st_ref[...]


def delta_scan(u0, lhs1, lhs2, gl, s0, b, n):
    nc = n // DN_CHUNK
    sc = min(8, nc)
    nblk = nc // sc
    nh = DN_HEADS
    tt = sc * DN_CHUNK
    fwd = lambda i: i
    bwd = lambda i: nblk - 1 - i

    def specs(d, blk):
        return [pl.BlockSpec((None, b, nh, tt, 128), lambda i: (d, 0, 0, blk(i), 0)),
                pl.BlockSpec((None, b, nh, sc, 128, 128), lambda i: (d, 0, 0, blk(i), 0, 0)),
                pl.BlockSpec((None, b, nh, sc, 192, 64), lambda i: (d, 0, 0, blk(i), 0, 0)),
                pl.BlockSpec((None, b, nh, sc, 8, 128), lambda i: (d, 0, 0, blk(i), 0, 0))]

    sf, sb = specs(0, fwd), specs(1, bwd)
    in_specs = [sf[0], sb[0], sf[1], sb[1], sf[2], sb[2], sf[3], sb[3],
                pl.BlockSpec((2, b, nh, 128, 128), lambda i: (0, 0, 0, 0, 0))]
    return pl.pallas_call(
        functools.partial(_delta_scan_kernel, sc=sc, nb=b),
        grid=(nblk,),
        in_specs=in_specs,
        out_specs=[pl.BlockSpec((b, tt, DN_WIDTH), lambda i: (0, fwd(i), 0)),
                   pl.BlockSpec((b, tt, DN_WIDTH), lambda i: (0, bwd(i), 0)),
                   pl.BlockSpec((2, b, nh, 128, 128), lambda i: (0, 0, 0, 0, 0))],
        out_shape=[jax.ShapeDtypeStruct((b, n, DN_WIDTH), BF16), jax.ShapeDtypeStruct((b, n, DN_WIDTH), BF16),
                   jax.ShapeDtypeStruct((2, b, nh, 128, 128), F32)],
        scratch_shapes=[pltpu.VMEM((2, b, nh, 128, 128), F32)],
        compiler_params=_cparams("arbitrary"),
        name="delta_scan",
    )(u0, u0, lhs1, lhs1, lhs2, lhs2, gl, gl, s0)


def delta_branch(p, small, conv_w, a_log, dt_bias, s0, b, n, rows, cols):
    nh = DN_HEADS
    qkv = dn_conv_prep(p, conv_w, b, n, rows, cols)
    small_t = small[:, :4 * nh].T
    rate = jnp.concatenate([jnp.zeros((2 * nh,), F32), a_log.reshape(-1)])
    bias = jnp.concatenate([jnp.zeros((2 * nh,), F32), dt_bias.reshape(-1)])
    prm_t = jnp.stack([rate, bias], axis=1)
    prm = jnp.pad(prm_t.T, ((0, 0), (0, SMALL_WIDTH - 4 * nh)))
    u0, lhs1, lhs2, gl = delta_prep(qkv, small, small_t, prm, prm_t, b, n)
    o_f, o_b, s_out = delta_scan(u0, lhs1, lhs2, gl, s0, b, n)
    return o_f.reshape(b * n, DN_WIDTH), o_b.reshape(b * n, DN_WIDTH), s_out


def _cos_sin(rows, cols, period):
    ang = 2.0 * np.pi * ((np.arange(rows)[:, None] * np.arange(cols)[None, :]) % period) / period
    return np.cos(ang), np.sin(ang)


def _const_bf16(a):
    return jnp.asarray(a, F32).astype(BF16)


def _stage2_matrices(l2):
    c, s = _cos_sin(l2, l2, l2)
    fwd = np.block([[c, s], [-s, c]])
    inv = np.block([[c, -s], [s, c]])
    return _const_bf16(fwd), _const_bf16(inv)


def _twiddles(l1, l2, kb, n_k1=None):
    n_k1 = l1 if n_k1 is None else n_k1
    ang = 2.0 * np.pi * ((np.arange(l2)[:, None] * np.arange(n_k1)[None, :]) % (l1 * l2)) / (l1 * l2)
    tw = np.stack([np.cos(ang), np.sin(ang)], axis=0).reshape(2, l2, n_k1 // kb, kb)
    return jnp.asarray(np.transpose(tw, (2, 0, 1, 3)), F32)


def _twiddle_mul(ar, ai, c, s, conj):
    if conj:
        return ar * c - ai * s, ai * c + ar * s
    return ar * c + ai * s, ai * c - ar * s


def _cplx_apply(m_ref, re, im):
    half = re.shape[0]
    out = jnp.dot(m_ref[...], jnp.concatenate([re, im], axis=0).astype(BF16), preferred_element_type=F32)
    return out[:half], out[half:]


def _fnet_mid_kernel(a_ref, tw_ref, m2_ref, cs_ref, o_ref, *, kb, ch):
    ts = [_twiddle_mul(a_ref[0, j], a_ref[1, j], tw_ref[0, :, j:j + 1], tw_ref[1, :, j:j + 1], False)
          for j in range(kb)]
    us = [_cplx_apply(m2_ref, tr, ti) for tr, ti in ts]
    ys = [jnp.dot(jnp.concatenate([ur, ui], axis=1).astype(BF16), cs_ref[...], preferred_element_type=F32)
          for ur, ui in us]
    for j, y in enumerate(ys):
        o_ref[:, j * ch:(j + 1) * ch] = y


def fnet_mid(a, tw, m2, cs, kb):
    bsz, _, l1, l2, ch = a.shape
    return pl.pallas_call(
        functools.partial(_fnet_mid_kernel, kb=kb, ch=ch),
        grid=(bsz, l1 // kb),
        in_specs=[pl.BlockSpec((None, 2, kb, l2, ch), lambda b, k: (b, 0, k, 0, 0)),
                  pl.BlockSpec((None, 2, l2, kb), lambda b, k: (k, 0, 0, 0)),
                  pl.BlockSpec(m2.shape, lambda b, k: (0, 0)),
                  pl.BlockSpec(cs.shape, lambda b, k: (0, 0))],
        out_specs=pl.BlockSpec((None, l2, kb * ch), lambda b, k: (b, 0, k)),
        out_shape=jax.ShapeDtypeStruct((bsz, l2, l1 * ch), F32),
        compiler_params=_cparams("parallel", "parallel"),
        name="fnet_mid",
    )(a, tw, m2, cs)


def _fnet_channel_matrix(n):
    c, s = _cos_sin(FN_GROUP_DIM, FN_GROUP_DIM, FN_GROUP_DIM)
    eye = np.eye(FN_GROUPS)
    scale = 1.0 / math.sqrt(n * FN_GROUP_DIM)
    return _const_bf16(np.concatenate([np.kron(eye, c), np.kron(eye, s)], axis=0) * scale)


def _fnet_long_kernel(x_ref, kin_ref, tw_ref, m2k_ref, cs_ref, o_ref, a_ref):
    n_sub = x_ref.shape[1] // SUB
    rows, width = x_ref.shape[0] * SUB, x_ref.shape[2]
    half = SUB * SUB
    for j in range(n_sub):
        r = jnp.dot(kin_ref[...], x_ref[:, j * SUB:(j + 1) * SUB, :].reshape(rows, width).astype(BF16),
                    preferred_element_type=F32)
        a_ref[0, j * half:(j + 1) * half, :] = r[:half]
        a_ref[1, j * half:(j + 1) * half, :] = r[half:]
    tr, ti = _twiddle_mul(a_ref[0], a_ref[1], tw_ref[0], tw_ref[1], False)
    ur, ui = _cplx_apply(m2k_ref, tr, ti)
    y = jnp.dot(jnp.concatenate([ur, ui], axis=1).astype(BF16), cs_ref[...], preferred_element_type=F32)
    o_ref[...] = y.reshape(o_ref.shape)


def fnet_long(p, b, n):
    ch, sub = FN_WIDTH, SUB
    l1, l2 = n // 128, 128
    nblk = l1 // sub
    c1, s1 = _cos_sin(l1, l1, l1)
    m1 = np.stack([c1, -s1], axis=0).reshape(2, nblk, sub, l1)
    kin = np.einsum('pbkn,jq->bpjknq', m1, np.eye(sub)).reshape(nblk, 2 * sub * sub, l1 * sub)
    ang = 2.0 * np.pi * ((np.arange(l2)[:, None] * np.arange(l1)[None, :]) % n) / n
    tw = np.stack([np.cos(ang), np.sin(ang)], axis=0).reshape(2, l2, nblk, sub)
    tw = np.transpose(tw, (2, 0, 1, 3)).reshape(nblk, 2, l2 * sub, 1)
    c2, s2 = _cos_sin(l2, l2, l2)
    m2k = _const_bf16(np.kron(np.block([[c2, s2], [-s2, c2]]), np.eye(sub)))
    cs = _fnet_channel_matrix(n)
    const = lambda a: pl.BlockSpec(a.shape, lambda bi, k: (0,) * a.ndim, pipeline_mode=pl.Buffered(1))
    y = pl.pallas_call(
        _fnet_long_kernel,
        grid=(b, nblk),
        in_specs=[pl.BlockSpec((None, l1, l2, ch), lambda bi, k: (bi, 0, 0, P_FN // ch),
                               pipeline_mode=pl.Buffered(1)),
                  pl.BlockSpec((None,) + kin.shape[1:], lambda bi, k: (k, 0, 0)),
                  pl.BlockSpec((None, 2, l2 * sub, 1), lambda bi, k: (k, 0, 0, 0)),
                  const(m2k), const(cs)],
        out_specs=pl.BlockSpec((None, l2, sub, ch), lambda bi, k: (bi, 0, k, 0)),
        out_shape=jax.ShapeDtypeStruct((b, l2, l1, ch), F32),
        scratch_shapes=[pltpu.VMEM((2, l2 * sub, ch), F32)],
        compiler_params=_cparams("parallel", "parallel"),
        name="fnet_long",
    )(p.reshape(b, l1, l2, P_WIDTH), _const_bf16(kin), jnp.asarray(tw, F32), m2k, cs)
    return y.reshape(b * n, ch)


def fnet_branch(p, b, n):
    if n == 8192:
        return fnet_long(p, b, n)
    ch = FN_WIDTH
    xr = lax.slice_in_dim(p, P_FN, P_FN + ch, axis=1).reshape(b, 1, 1, n, ch)
    a = jnp.concatenate([xr, jnp.zeros_like(xr)], axis=1)
    y = fnet_mid(a, _twiddles(1, n, 1), _stage2_matrices(n)[0], _fnet_channel_matrix(n), 1)
    return y.reshape(b * n, ch)


def _seq_conv_kernel(prev_ref, cur_ref, next_ref, w_ref, *o_refs, n_tiles):
    t = pl.program_id(1)
    tt = cur_ref.shape[0]
    cur = cur_ref[...]
    row = lax.broadcasted_iota(jnp.int32, (tt, 1), 0)
    before = jnp.where(t == 0, 0.0, prev_ref[7:8, :])
    after = jnp.where(t == n_tiles - 1, 0.0, next_ref[0:1, :])
    left = jnp.where(row == 0, before, pltpu.roll(cur, 1, axis=0))
    right = jnp.where(row == tt - 1, after, pltpu.roll(cur, tt - 1, axis=0))
    y = w_ref[0:1, :] * left + w_ref[1:2, :] * cur + w_ref[2:3, :] * right
    for part, o_ref in enumerate(o_refs):
        o_ref[...] = y[:, part * HY_WIDTH:(part + 1) * HY_WIDTH]


def seq_conv(p, conv_w, b, n):
    tt = min(n, 1024)
    n_tiles = n // tt
    parts = HY_ORDER + 1
    w = parts * HY_WIDTH
    c0 = P_HY // w
    out_spec = pl.BlockSpec((tt, HY_WIDTH), lambda bi, t: (bi * n_tiles + t, 0))
    return pl.pallas_call(
        functools.partial(_seq_conv_kernel, n_tiles=n_tiles),
        grid=(b, n_tiles),
        in_specs=[pl.BlockSpec((8, w), lambda bi, t: (jnp.maximum((bi * n_tiles + t) * (tt // 8) - 1, 0), c0)),
                  pl.BlockSpec((tt, w), lambda bi, t: (bi * n_tiles + t, c0)),
                  pl.BlockSpec((8, w), lambda bi, t: (jnp.minimum((bi * n_tiles + t + 1) * (tt // 8),
                                                                  b * n // 8 - 1), c0)),
                  pl.BlockSpec((SHORT_CONV, w), lambda bi, t: (0, 0))],
        out_specs=[out_spec] * parts,
        out_shape=[jax.ShapeDtypeStruct((b * n, HY_WIDTH), F32)] * parts,
        compiler_params=_cparams("parallel", "parallel"),
        name="hy_seq_conv",
    )(p, p, p, conv_w)


def _hdot(a, b):
    return jnp.dot(a, b, preferred_element_type=F32, precision=lax.Precision.HIGHEST)


def _hy_filter_kernel(ft_ref, t_ref, w1t_ref, b1_ref, f1_ref, w2t_ref, b2_ref, f2_ref, w3_ref, dl_ref, k_ref, s_ref,
                      *, n, tr):
    i = pl.program_id(0)
    hid = jnp.sin(f1_ref[...] * (_hdot(w1t_ref[...], ft_ref[...]) + b1_ref[...]))
    hid = jnp.sin(f2_ref[...] * (_hdot(w2t_ref[...], hid) + b2_ref[...]))
    filt = _hdot(hid.T, w3_ref[...]) * jnp.exp(-t_ref[...] * dl_ref[...])
    row = i * tr + lax.broadcasted_iota(jnp.int32, (tr, 1), 0)
    filt = jnp.where(row == n, 0.0, filt)
    k_ref[...] = filt

    @pl.when(i == 0)
    def _():
        s_ref[...] = jnp.zeros_like(s_ref)

    s_ref[...] += jnp.sum(jnp.abs(filt), axis=0, keepdims=True)


def hy_filter(n, w1, b1, freq1, w2, b2, freq2, w3):
    pos = jnp.arange(n, dtype=F32)
    t = pos / max(n - 1, 1)
    bands = jnp.linspace(1e-4, HY_BANDS - 1, HY_BANDS, dtype=F32)
    ang = (2.0 * math.pi / n) * pos[:, None] * bands[None, :]
    feats = jnp.concatenate([t[:, None], jnp.cos(ang), -jnp.sin(ang)], axis=-1)
    feats2 = jnp.concatenate([feats, feats[:1], feats[:0:-1]], axis=0)
    kpad = 128
    feats_t = jnp.pad(feats2, ((0, 0), (0, kpad - HY_EMB_DIM))).T
    w1t = jnp.pad(w1, ((0, kpad - HY_EMB_DIM), (0, 0))).T
    min_decay = math.log(HY_DECAY_TARGET) / HY_SLOW_DECAY_PCT
    max_decay = math.log(HY_DECAY_TARGET) / HY_FAST_DECAY_PCT
    cw = HY_ORDER * HY_WIDTH
    deltas = jnp.abs(jnp.linspace(min_decay, max_decay, cw, dtype=F32)).reshape(1, cw)
    tr = min(n, 1024)
    half = n // tr
    hd = HY_FILTER_HIDDEN
    vec = lambda v: v.reshape(hd, 1)
    full = lambda shp: pl.BlockSpec(shp, lambda i: (0, 0))
    return pl.pallas_call(
        functools.partial(_hy_filter_kernel, n=n, tr=tr),
        grid=(2 * half,),
        in_specs=[pl.BlockSpec((kpad, tr), lambda i: (0, i)), pl.BlockSpec((tr, 1), lambda i: (i, 0)),
                  full((hd, kpad)), full((hd, 1)), full((hd, 1)), full((hd, hd)), full((hd, 1)), full((hd, 1)),
                  pl.BlockSpec((hd, cw), lambda i: (0, i // half)), full((1, cw))],
        out_specs=[pl.BlockSpec((tr, cw), lambda i: (i, 0)), full((1, cw))],
        out_shape=[jax.ShapeDtypeStruct((2 * n, cw), F32), jax.ShapeDtypeStruct((1, cw), F32)],
        compiler_params=_cparams("arbitrary"),
        name="hy_filter",
    )(feats_t, feats2[:, 0:1], w1t, vec(b1), vec(freq1), w2.T, vec(b2), vec(freq2), w3, deltas)


def _hy_spec_kernel(a_ref, tw_ref, m2_ref, s_ref, o_ref, *, kb):
    inv = 1.0 / s_ref[...]
    ts = [_twiddle_mul(a_ref[0, j], a_ref[1, j], tw_ref[0, :, j:j + 1], tw_ref[1, :, j:j + 1], False)
          for j in range(kb)]
    xs = [_cplx_apply(m2_ref, tr, ti) for tr, ti in ts]
    for j, (xr, xi) in enumerate(xs):
        o_ref[0, j] = xr * inv
        o_ref[1, j] = xi * inv


def hy_spec(a, tw, m2, abs_sum, kb):
    _, l1, l2, cw = a.shape
    w = HY_WIDTH
    return pl.pallas_call(
        functools.partial(_hy_spec_kernel, kb=kb),
        grid=(l1 // kb, cw // w),
        in_specs=[pl.BlockSpec((2, kb, l2, w), lambda k, c: (0, k, 0, c)),
                  pl.BlockSpec((None, 2, l2, kb), lambda k, c: (k, 0, 0, 0)),
                  pl.BlockSpec(m2.shape, lambda k, c: (0, 0)),
                  pl.BlockSpec((1, w), lambda k, c: (0, c))],
        out_specs=pl.BlockSpec((2, kb, l2, w), lambda k, c: (0, k, 0, c)),
        out_shape=jax.ShapeDtypeStruct((2, l1, l2, cw), F32),
        compiler_params=_cparams("parallel", "parallel"),
        name="hy_spec",
    )(a, tw, m2, abs_sum)


def _hy_mid_kernel(a_ref, tw_ref, m2_ref, m2c_ref, kf_ref, o_ref, *, kb):
    cs = [(tw_ref[0, :, j:j + 1], tw_ref[1, :, j:j + 1]) for j in range(kb)]
    ts = [_twiddle_mul(a_ref[0, j], a_ref[1, j], c, s, False) for j, (c, s) in enumerate(cs)]
    xs = [_cplx_apply(m2_ref, tr, ti) for tr, ti in ts]
    ps = [(xr * kf_ref[0, j] - xi * kf_ref[1, j], xr * kf_ref[1, j] + xi * kf_ref[0, j])
          for j, (xr, xi) in enumerate(xs)]
    bs = [_cplx_apply(m2c_ref, pr, pi) for pr, pi in ps]
    for j, ((br, bi), (c, s)) in enumerate(zip(bs, cs)):
        o_ref[0, j], o_ref[1, j] = _twiddle_mul(br, bi, c, s, True)


def hy_mid(a, tw, m2, m2c, kf, order, kb):
    bsz, _, l1, l2, w = a.shape
    return pl.pallas_call(
        functools.partial(_hy_mid_kernel, kb=kb),
        grid=(bsz, l1 // kb),
        in_specs=[pl.BlockSpec((None, 2, kb, l2, w), lambda b, k: (b, 0, k, 0, 0)),
                  pl.BlockSpec((None, 2, l2, kb), lambda b, k: (k, 0, 0, 0)),
                  pl.BlockSpec(m2.shape, lambda b, k: (0, 0)),
                  pl.BlockSpec(m2c.shape, lambda b, k: (0, 0)),
                  pl.BlockSpec((2, kb, l2, w), lambda b, k: (0, k, 0, order))],
        out_specs=pl.BlockSpec((None, 2, kb, l2, w), lambda b, k: (b, 0, k, 0, 0)),
        out_shape=jax.ShapeDtypeStruct((bsz, 2, l1, l2, w), F32),
        compiler_params=_cparams("parallel", "parallel"),
        name="hy_mid",
    )(a, tw, m2, m2c, kf)


def _hy_out_kernel(m_ref, bp_ref, xo_ref, z_ref, bias_ref, o_ref):
    y = jnp.dot(m_ref[...], bp_ref[...].astype(BF16), preferred_element_type=F32)
    o_ref[...] = xo_ref[...] * (y + bias_ref[...] * z_ref[...])


def hy_out(m, bp, xo, z, bias_row, tn):
    bsz, k, n = bp.shape
    r = m.shape[0]
    blk = pl.BlockSpec((None, r, tn), lambda b, j: (b, 0, j))
    return pl.pallas_call(
        _hy_out_kernel,
        grid=(bsz, n // tn),
        in_specs=[pl.BlockSpec((r, k), lambda b, j: (0, 0)),
                  pl.BlockSpec((None, k, tn), lambda b, j: (b, 0, j)), blk, blk,
                  pl.BlockSpec((1, tn), lambda b, j: (0, j))],
        out_specs=blk,
        out_shape=jax.ShapeDtypeStruct((bsz, r, n), F32),
        compiler_params=_cparams("parallel", "parallel"),
        name="hy_out",
    )(m, bp, xo, z, bias_row)


HY_KB = 13
HY_SLOTS = 16


def _half_spectrum_blocks(l1, kb):
    return -(-(l1 // 2 + 1) // kb)


def _kron_stage1(l1, n1_used, kb):
    nblk = _half_spectrum_blocks(l1, kb)
    c, s = _cos_sin(nblk * kb, n1_used, l1)
    m = np.stack([c, -s], axis=0).reshape(2, nblk, kb, n1_used)
    m = np.transpose(m, (1, 0, 2, 3)).reshape(nblk, 2 * kb, n1_used)
    return _const_bf16(np.stack([np.kron(blk, np.eye(SUB)) for blk in m]))


def _kron_stage_out(l1, n1_used, kb, slots):
    nblk = _half_spectrum_blocks(l1, kb)
    c, s = _cos_sin(n1_used, nblk * kb, l1)
    k1 = np.arange(nblk * kb)
    mult = np.where((k1 == 0) | (k1 == l1 // 2), 1.0, np.where(k1 < l1 // 2, 2.0, 0.0))
    m = np.stack([c * mult, -s * mult], axis=1).reshape(n1_used, 2, nblk, kb) / (l1 * l1)
    m = np.pad(m, ((0, 0), (0, 0), (0, 0), (0, slots - kb)))
    m = np.transpose(m, (2, 0, 1, 3)).reshape(nblk, n1_used, 2 * slots)
    return _const_bf16(np.stack([np.kron(blk, np.eye(SUB)) for blk in m]))


def _strided_stage_in(kin_ref, src_ref, a_ref):
    n_sub, width = src_ref.shape[1] // SUB, src_ref.shape[2]
    rows = src_ref.shape[0] * SUB
    for j in range(n_sub):
        r = jnp.dot(kin_ref[...], src_ref[:, j * SUB:(j + 1) * SUB, :].reshape(rows, width).astype(BF16),
                    preferred_element_type=F32)
        a_ref[:, j * SUB:(j + 1) * SUB, :] = r.reshape(a_ref.shape[0], SUB, width)


def _hy_conv_kernel(z_ref, xo_ref, kin_ref, tw_ref, m2_ref, m2c_ref, kf_ref, kout_ref, bias_ref, o_ref,
                    a_ref, b_ref, *, kb):
    k = pl.program_id(1)
    slots = b_ref.shape[0] // 2

    @pl.when(k == 0)
    def _():
        o_ref[...] = jnp.zeros_like(o_ref)
        b_ref[...] = jnp.zeros_like(b_ref)

    _strided_stage_in(kin_ref, z_ref, a_ref)
    grp = kb
    for g0 in range(0, kb, grp):
        js = range(g0, g0 + grp)
        cs = [(tw_ref[0, :, j:j + 1], tw_ref[1, :, j:j + 1]) for j in js]
        ts = [_twiddle_mul(a_ref[j], a_ref[kb + j], c, s, False) for j, (c, s) in zip(js, cs)]
        xs = [_cplx_apply(m2_ref, tr, ti) for tr, ti in ts]
        ps = [(xr * kf_ref[0, j] - xi * kf_ref[1, j], xr * kf_ref[1, j] + xi * kf_ref[0, j])
              for j, (xr, xi) in zip(js, xs)]
        bs = [_cplx_apply(m2c_ref, pr, pi) for pr, pi in ps]
        for j, (br, bi), (c, s) in zip(js, bs, cs):
            b_ref[j], b_ref[slots + j] = _twiddle_mul(br, bi, c, s, True)
    n_sub, width = o_ref.shape[1] // SUB, o_ref.shape[2]
    for j in range(n_sub):
        r = jnp.dot(kout_ref[...], b_ref[:, j * SUB:(j + 1) * SUB, :].reshape(2 * slots * SUB, width).astype(BF16),
                    preferred_element_type=F32)
        o_ref[:, j * SUB:(j + 1) * SUB, :] += r.reshape(o_ref.shape[0], SUB, width)

    @pl.when(k == pl.num_programs(1) - 1)
    def _():
        o_ref[...] = xo_ref[...] * (o_ref[...] + bias_ref[...] * z_ref[...])


def hy_conv_long(z, xo, kf, bias, order, b, n):
    w, kb = HY_WIDTH, HY_KB
    l1 = l2 = 128
    n1 = n // l2
    nblk = _half_spectrum_blocks(l1, kb)
    tw = _twiddles(l1, l2, kb, nblk * kb)
    m2, m2c = _stage2_matrices(l2)
    kin, kout = _kron_stage1(l1, n1, kb), _kron_stage_out(l1, n1, kb, HY_SLOTS)
    tok = pl.BlockSpec((None, n1, l2, w), lambda bi, k: (bi, 0, 0, 0), pipeline_mode=pl.Buffered(1))
    const = lambda a: pl.BlockSpec(a.shape, lambda bi, k: (0,) * a.ndim)
    out = pl.pallas_call(
        functools.partial(_hy_conv_kernel, kb=kb),
        grid=(b, nblk),
        in_specs=[tok, tok,
                  pl.BlockSpec((None,) + kin.shape[1:], lambda bi, k: (k, 0, 0)),
                  pl.BlockSpec((None, 2, l2, kb), lambda bi, k: (k, 0, 0, 0)),
                  const(m2), const(m2c),
                  pl.BlockSpec((2, kb, l2, w), lambda bi, k: (0, k, 0, order)),
                  pl.BlockSpec((None,) + kout.shape[1:], lambda bi, k: (k, 0, 0)),
                  pl.BlockSpec((1, w), lambda bi, k: (0, 0))],
        out_specs=pl.BlockSpec((None, n1, l2, w), lambda bi, k: (bi, 0, 0, 0), pipeline_mode=pl.Buffered(1)),
        out_shape=jax.ShapeDtypeStruct((b, n1, l2, w), F32),
        scratch_shapes=[pltpu.VMEM((2 * kb, l2, w), F32), pltpu.VMEM((2 * HY_SLOTS, l2, w), F32)],
        compiler_params=_cparams("parallel", "arbitrary"),
        name="hy_conv_long",
    )(z.reshape(b, n1, l2, w), xo.reshape(b, n1, l2, w), kin, tw, m2, m2c, kf, kout, bias.reshape(1, w))
    return out.reshape(b * n, w)


def _hy_spec_long_kernel(kern_ref, kin_ref, tw_ref, m2_ref, s_ref, o_ref, a_ref, *, kb):
    _strided_stage_in(kin_ref, kern_ref, a_ref)
    inv = 1.0 / s_ref[...]
    ts = [_twiddle_mul(a_ref[j], a_ref[kb + j], tw_ref[0, :, j:j + 1], tw_ref[1, :, j:j + 1], False)
          for j in range(kb)]
    xs = [_cplx_apply(m2_ref, tr, ti) for tr, ti in ts]
    for j, (xr, xi) in enumerate(xs):
        o_ref[0, j] = xr * inv
        o_ref[1, j] = xi * inv


def hy_spec_long(kern, abs_sum):
    w, kb = HY_WIDTH, HY_KB
    l1 = l2 = 128
    cw = kern.shape[1]
    nblk = _half_spectrum_blocks(l1, kb)
    tw = _twiddles(l1, l2, kb, nblk * kb)
    m2, _ = _stage2_matrices(l2)
    kin = _kron_stage1(l1, l1, kb)
    return pl.pallas_call(
        functools.partial(_hy_spec_long_kernel, kb=kb),
        grid=(cw // w, nblk),
        in_specs=[pl.BlockSpec((l1, l2, w), lambda c, k: (0, 0, c), pipeline_mode=pl.Buffered(1)),
                  pl.BlockSpec((None,) + kin.shape[1:], lambda c, k: (k, 0, 0)),
                  pl.BlockSpec((None, 2, l2, kb), lambda c, k: (k, 0, 0, 0)),
                  pl.BlockSpec(m2.shape, lambda c, k: (0, 0)),
                  pl.BlockSpec((1, w), lambda c, k: (0, c))],
        out_specs=pl.BlockSpec((2, kb, l2, w), lambda c, k: (0, k, 0, c)),
        out_shape=jax.ShapeDtypeStruct((2, nblk * kb, l2, cw), F32),
        scratch_shapes=[pltpu.VMEM((2 * kb, l2, w), F32)],
        compiler_params=_cparams("parallel", "arbitrary"),
        name="hy_spec_long",
    )(kern.reshape(l1, l2, cw), kin, tw, m2, abs_sum)


def hyena_pallas(p, conv_w, w1, b1, freq1, w2, b2, freq2, w3, bias, b, n):
    w = HY_WIDTH
    x0, x1, v = seq_conv(p, conv_w, b, n)
    kern, abs_sum = hy_filter(n, w1, b1, freq1, w2, b2, freq2, w3)
    z = v
    if n == 8192:
        kf = hy_spec_long(kern, abs_sum)
        for order, xo in enumerate((x0, x1)):
            z = hy_conv_long(z, xo, kf, bias[order], order, b, n)
        return z
    l2 = 2 * n
    tw = _twiddles(1, l2, 1)
    m2, m2c = _stage2_matrices(l2)
    ak = jnp.stack([kern, jnp.zeros_like(kern)], axis=0).reshape(2, 1, l2, HY_ORDER * w)
    m_out = _const_bf16(np.eye(n, 2 * l2) / l2)
    kf = hy_spec(ak, tw, m2, abs_sum, 1)
    for order, xo in enumerate((x0, x1)):
        zp = jnp.pad(z.reshape(b, 1, 1, n, w), ((0, 0), (0, 0), (0, 0), (0, n), (0, 0)))
        a = jnp.concatenate([zp, jnp.zeros_like(zp)], axis=1)
        bp = hy_mid(a, tw, m2, m2c, kf, order, 1)
        z = hy_out(m_out, bp.reshape(b, 2 * l2, w), xo.reshape(b, n, w), z.reshape(b, n, w),
                   bias[order].reshape(1, w), w).reshape(b * n, w)
    return z


def _prep_w_in(w):
    main = jnp.concatenate([w[:, OFF_Q:OFF_Z], w[:, OFF_HY:OFF_GATE], w[:, OFF_FN:OFF_HY], w[:, OFF_Z:OFF_BETA]],
                           axis=1).astype(BF16)
    small = jnp.pad(w[:, OFF_BETA:OFF_FN], ((0, 0), (0, SMALL_WIDTH - 4 * DN_HEADS))).astype(BF16)
    return main, small, w[:, OFF_GATE:IN_WIDTH].astype(BF16)


def kernel(x, c, ctx, c_ctx, w_mod, b_mod, norm1, norm2, w_in, dn_conv, dn_a_log, dn_dt_bias,
           dn_out_norm, hy_conv, hy_w1, hy_b1, hy_freq1, hy_w2, hy_b2, hy_freq2, hy_w3, hy_bias,
           w_branch_a, w_branch_b, w_branch_c, w_out, w_ff1, w_ff2, final_norm):
    b, n_lat, d = x.shape
    rows = n_lat // GRID_W
    n_ctx = ctx.shape[1]
    tm_x, tm_c = 1024, n_ctx

    c_rows = jnp.concatenate([c, c_ctx[None], jnp.zeros((8 - b - 1, d), F32)], axis=0)
    mods = mod_vectors(c_rows, w_mod, b_mod)
    s_zero = jnp.zeros((2, b, DN_HEADS, DN_HEAD_DIM, DN_HEAD_DIM), F32)
    h, hc = x.reshape(b * n_lat, d), ctx.reshape(b * n_ctx, d)

    for l in range(DEPTH):
        last = l == DEPTH - 1
        mv = mods[l].reshape(8, N_MOD, 1, d)
        mx = [mv[:b, i] for i in range(N_MOD)]
        mc = [mv[b:b + 1, i] for i in range(N_MOD)]
        w_main, w_small, w_gate = _prep_w_in(w_in[l])
        wa, wb, wc, wo = (w.astype(BF16) for w in (w_branch_a[l], w_branch_b[l], w_branch_c[l], w_out[l]))
        w1, w2 = w_ff1[l].astype(BF16), w_ff2[l].astype(BF16)
        n1, n2 = norm1[l][None, None, :], norm2[l][None, None, :]

        p_c, small_c = in_proj(hc, n1 * (1.0 + mc[1]), mc[0], w_main, w_small, b * n_ctx, tm_c)
        p_x, small_x = in_proj(h, n1 * (1.0 + mx[1]), mx[0], w_main, w_small, n_lat, tm_x)

        def mix(p, n):
            y_c = hyena_pallas(p, hy_conv[l], hy_w1[l], hy_b1[l], hy_freq1[l], hy_w2[l], hy_b2[l], hy_freq2[l],
                               hy_w3[l], hy_bias[l], b, n)
            return fnet_branch(p, b, n), y_c

        ocf, ocb, s_ctx = delta_branch(p_c, small_c, dn_conv[l], dn_a_log[l], dn_dt_bias[l], s_zero,
                                       b, n_ctx, 1, n_ctx)
        oxf, oxb, _ = delta_branch(p_x, small_x, dn_conv[l], dn_a_log[l], dn_dt_bias[l], s_ctx,
                                   b, n_lat, rows, GRID_W)

        y_b, y_c = mix(p_x, n_lat)
        h = merge(oxf, oxb, p_x, y_b, y_c, h, n1 * (1.0 + mx[1]), mx[0], mx[2], dn_out_norm[l],
                  w_gate, wa, wb, wc, wo, n_lat, 512)
        h = mlp(h, n2 * (1.0 + mx[4]), mx[3], mx[5], w1, w2, n_lat, tm_x, final_norm if last else None)

        if not last:
            y_b, y_c = mix(p_c, n_ctx)
            hc = merge(ocf, ocb, p_c, y_b, y_c, hc, n1 * (1.0 + mc[1]), mc[0], mc[2], dn_out_norm[l],
                       w_gate, wa, wb, wc, wo, b * n_ctx, tm_c)
            hc = mlp(hc, n2 * (1.0 + mc[4]), mc[3], mc[5], w1, w2, b * n_ctx, tm_c)

    return h.reshape(b, n_lat, d)
```

```python
import functools
import math

import jax
import jax.numpy as jnp
import numpy as np
from jax import lax
from jax.experimental import pallas as pl
from jax.experimental.pallas import tpu as pltpu

D_MODEL = 1024
DEPTH = 2
GRID_W = 64
NORM_EPS = 1e-6
N_MOD = 6

DN_HEADS = 4
DN_HEAD_DIM = 128
DN_WIDTH = DN_HEADS * DN_HEAD_DIM
DN_CHUNK = 64
SHORT_CONV = 3

FN_GROUPS = 4
FN_GROUP_DIM = 64
FN_WIDTH = FN_GROUPS * FN_GROUP_DIM

HY_WIDTH = 256
HY_ORDER = 2
HY_EMB_DIM = 33
HY_BANDS = (HY_EMB_DIM - 1) // 2
HY_FILTER_HIDDEN = 64
HY_FAST_DECAY_PCT = 0.3
HY_SLOW_DECAY_PCT = 1.5
HY_DECAY_TARGET = 1e-2

N_BRANCHES = 3
D_FF = 4 * D_MODEL

OFF_Q = 0
OFF_Z = 3 * DN_WIDTH
OFF_BETA = OFF_Z + DN_WIDTH
OFF_A = OFF_BETA + 2 * DN_HEADS
OFF_FN = OFF_A + 2 * DN_HEADS
OFF_HY = OFF_FN + FN_WIDTH
OFF_GATE = OFF_HY + (HY_ORDER + 1) * HY_WIDTH
IN_WIDTH = OFF_GATE + N_BRANCHES * D_MODEL

P_QKV = 0
P_HY = P_QKV + 3 * DN_WIDTH
P_FN = P_HY + (HY_ORDER + 1) * HY_WIDTH
P_Z = P_FN + FN_WIDTH
P_WIDTH = P_Z + DN_WIDTH
SMALL_WIDTH = 128

F32 = jnp.float32
BF16 = jnp.bfloat16
VMEM_LIMIT = 56 * 1024 * 1024
SUB = 8


def _cparams(*sem):
    return pltpu.CompilerParams(dimension_semantics=sem, vmem_limit_bytes=VMEM_LIMIT)


def _bdot(a, b):
    return jnp.dot(a.astype(BF16), b.astype(BF16), preferred_element_type=F32)


def _sigmoid(x):
    return 0.5 * jnp.tanh(0.5 * x) + 0.5


def _modnorm(xf, gs, sh):
    r = lax.rsqrt(jnp.mean(xf * xf, axis=-1, keepdims=True) + NORM_EPS)
    return xf * r * gs + sh


def _mod_kernel(c_ref, w_ref, b_ref, o_ref):
    c = c_ref[...]
    o_ref[...] = _bdot(c * _sigmoid(c), w_ref[...]) + b_ref[...]


def mod_vectors(c_rows, w_mod, b_mod):
    tn = 1536
    n = N_MOD * D_MODEL
    return pl.pallas_call(
        _mod_kernel,
        grid=(DEPTH, n // tn),
        in_specs=[pl.BlockSpec((8, D_MODEL), lambda l, j: (0, 0)),
                  pl.BlockSpec((None, D_MODEL, tn), lambda l, j: (l, 0, j)),
                  pl.BlockSpec((None, 1, tn), lambda l, j: (l, 0, j))],
        out_specs=pl.BlockSpec((None, 8, tn), lambda l, j: (l, 0, j)),
        out_shape=jax.ShapeDtypeStruct((DEPTH, 8, n), F32),
        compiler_params=_cparams("parallel", "parallel"),
        name="mod_vectors",
    )(c_rows, w_mod, b_mod.reshape(DEPTH, 1, n))


def _in_proj_kernel(x_ref, gs_ref, sh_ref, w_ref, ws_ref, p_ref, small_ref, *, tn):
    xn = _modnorm(x_ref[...], gs_ref[0], sh_ref[0]).astype(BF16)
    small_ref[...] = jnp.dot(xn, ws_ref[...], preferred_element_type=F32)
    for j in range(p_ref.shape[1] // tn):
        p_ref[:, j * tn:(j + 1) * tn] = jnp.dot(xn, w_ref[:, j * tn:(j + 1) * tn], preferred_element_type=F32)


def in_proj(x2d, gs, sh, w_main, w_small, rows_per_mod, tm):
    m = x2d.shape[0]
    tiles_per_mod = rows_per_mod // tm
    mod_spec = pl.BlockSpec((1, 1, D_MODEL), lambda i: (i // tiles_per_mod, 0, 0))
    resident = lambda a: pl.BlockSpec(a.shape, lambda i: (0, 0), pipeline_mode=pl.Buffered(1))
    return pl.pallas_call(
        functools.partial(_in_proj_kernel, tn=1024),
        grid=(m // tm,),
        in_specs=[pl.BlockSpec((tm, D_MODEL), lambda i: (i, 0)), mod_spec, mod_spec,
                  resident(w_main), resident(w_small)],
        out_specs=[pl.BlockSpec((tm, P_WIDTH), lambda i: (i, 0)),
                   pl.BlockSpec((tm, SMALL_WIDTH), lambda i: (i, 0))],
        out_shape=[jax.ShapeDtypeStruct((m, P_WIDTH), F32), jax.ShapeDtypeStruct((m, SMALL_WIDTH), F32)],
        compiler_params=_cparams("parallel"),
        name="in_proj",
    )(x2d, gs, sh, w_main, w_small)


def _merge_kernel(of_ref, ob_ref, z_ref, yb_ref, yc_ref, h_ref, gs_ref, sh_ref, gate_ref, nrm_ref,
                  wg_ref, wa_ref, wb_ref, wc_ref, wo_ref, out_ref):
    xn = _modnorm(h_ref[...], gs_ref[0], sh_ref[0]).astype(BF16)
    o = of_ref[...].astype(F32) + ob_ref[...].astype(F32)
    z = z_ref[...]
    heads = []
    for hd in range(DN_HEADS):
        sl = slice(hd * DN_HEAD_DIM, (hd + 1) * DN_HEAD_DIM)
        oh, zh = o[:, sl], z[:, sl]
        r = lax.rsqrt(jnp.mean(oh * oh, axis=-1, keepdims=True) + NORM_EPS)
        heads.append(oh * r * nrm_ref[...] * (zh * _sigmoid(zh)))
    ya = jnp.concatenate(heads, axis=-1)
    merged = None
    for i, (y, w_ref) in enumerate(((ya, wa_ref), (yb_ref[...], wb_ref), (yc_ref[...], wc_ref))):
        g = jnp.dot(xn, wg_ref[:, i * D_MODEL:(i + 1) * D_MODEL], preferred_element_type=F32)
        term = _sigmoid(g) * _bdot(y, w_ref[...])
        merged = term if merged is None else merged + term
    out_ref[...] = h_ref[...] + gate_ref[0] * _bdot(merged, wo_ref[...])


def merge(o_f, o_b, p, y_b, y_c, h2d, gs, sh, gate, dn_out_norm, wg, wa, wb, wc, wo, rows_per_mod, tm):
    m = h2d.shape[0]
    tiles_per_mod = rows_per_mod // tm
    row = lambda w: pl.BlockSpec((tm, w), lambda i: (i, 0))
    full = lambda a: pl.BlockSpec(a.shape, lambda i: (0,) * a.ndim, pipeline_mode=pl.Buffered(1))
    mod_spec = pl.BlockSpec((1, 1, D_MODEL), lambda i: (i // tiles_per_mod, 0, 0))
    nrm = dn_out_norm.reshape(1, DN_HEAD_DIM)
    return pl.pallas_call(
        _merge_kernel,
        grid=(m // tm,),
        in_specs=[row(DN_WIDTH), row(DN_WIDTH),
                  pl.BlockSpec((tm, DN_WIDTH), lambda i: (i, P_Z // DN_WIDTH)),
                  row(FN_WIDTH), row(HY_WIDTH), row(D_MODEL), mod_spec, mod_spec, mod_spec,
                  full(nrm), full(wg), full(wa), full(wb), full(wc), full(wo)],
        out_specs=row(D_MODEL),
        out_shape=jax.ShapeDtypeStruct((m, D_MODEL), F32),
        compiler_params=_cparams("parallel"),
        name="merge",
    )(o_f, o_b, p, y_b, y_c, h2d, gs, sh, gate, nrm, wg, wa, wb, wc, wo)


def _mlp_kernel(h_ref, gs_ref, sh_ref, gate_ref, w1_ref, w2_ref, fin_ref, out_ref, acc_ref, *, final, tf):
    xn = _modnorm(h_ref[...], gs_ref[0], sh_ref[0]).astype(BF16)
    for j in range(w1_ref.shape[1] // tf):
        a = jnp.maximum(jnp.dot(xn, w1_ref[:, j * tf:(j + 1) * tf], preferred_element_type=F32), 0.0)
        part = jnp.dot((a * a).astype(BF16), w2_ref[j * tf:(j + 1) * tf, :], preferred_element_type=F32)
        if j == 0:
            acc_ref[...] = part
        else:
            acc_ref[...] += part
    y = h_ref[...] + gate_ref[0] * acc_ref[...]
    if final:
        y = y * lax.rsqrt(jnp.mean(y * y, axis=-1, keepdims=True) + NORM_EPS) * fin_ref[...]
    out_ref[...] = y


def mlp(h2d, gs, sh, gate, w1, w2, rows_per_mod, tm, final_gain=None):
    m = h2d.shape[0]
    tiles_per_mod = rows_per_mod // tm
    mod_spec = pl.BlockSpec((1, 1, D_MODEL), lambda i: (i // tiles_per_mod, 0, 0))
    resident = lambda a: pl.BlockSpec(a.shape, lambda i: (0, 0), pipeline_mode=pl.Buffered(1))
    final = final_gain is not None
    fin = (final_gain if final else jnp.ones((D_MODEL,), F32)).reshape(1, D_MODEL)
    return pl.pallas_call(
        functools.partial(_mlp_kernel, final=final, tf=1024),
        grid=(m // tm,),
        in_specs=[pl.BlockSpec((tm, D_MODEL), lambda i: (i, 0)), mod_spec, mod_spec, mod_spec,
                  resident(w1), resident(w2), pl.BlockSpec((1, D_MODEL), lambda i: (0, 0))],
        out_specs=pl.BlockSpec((tm, D_MODEL), lambda i: (i, 0)),
        out_shape=jax.ShapeDtypeStruct((m, D_MODEL), F32),
        scratch_shapes=[pltpu.VMEM((tm, D_MODEL), F32)],
        compiler_params=_cparams("parallel"),
        name="mlp",
    )(h2d, gs, sh, gate, w1, w2, fin)


def _dnconv_kernel(prev_ref, cur_ref, next_ref, w_ref, o_ref, *, cols, n_tiles):
    t, j = pl.program_id(1), pl.program_id(2)
    tt = cur_ref.shape[0]
    prev = jnp.where(t == 0, 0.0, prev_ref[...])
    nxt = jnp.where(t == n_tiles - 1, 0.0, next_ref[...])
    ext = jnp.concatenate([prev, cur_ref[...], nxt], axis=0)
    n_ext = tt + 2 * cols
    col = lax.broadcasted_iota(jnp.int32, (n_ext, 1), 0) % cols
    left = jnp.where(col == 0, 0.0, pltpu.roll(ext, 1, axis=0))
    right = jnp.where(col == cols - 1, 0.0, pltpu.roll(ext, n_ext - 1, axis=0))
    acc = jnp.zeros((tt, DN_WIDTH), F32)
    for dr in range(SHORT_CONV):
        base = dr * cols
        acc = (acc + w_ref[3 * dr:3 * dr + 1, :] * left[base:base + tt]
               + w_ref[3 * dr + 1:3 * dr + 2, :] * ext[base:base + tt]
               + w_ref[3 * dr + 2:3 * dr + 3, :] * right[base:base + tt])
    y = acc * _sigmoid(acc)
    q_scale = jnp.where(j == 0, DN_HEAD_DIM ** -0.5, 1.0)
    for hd in range(DN_HEADS):
        sl = slice(hd * DN_HEAD_DIM, (hd + 1) * DN_HEAD_DIM)
        yh = y[:, sl]
        nrm = lax.rsqrt(jnp.sum(yh * yh, axis=-1, keepdims=True) + NORM_EPS) * q_scale
        o_ref[:, sl] = yh * jnp.where(j < 2, nrm, 1.0)


def dn_conv_prep(p, conv_w, b, n, rows, cols):
    tr = min(rows, 32)
    tt = tr * cols
    n_tiles = rows // tr
    nblk = b * n // cols
    c0 = P_QKV // DN_WIDTH
    return pl.pallas_call(
        functools.partial(_dnconv_kernel, cols=cols, n_tiles=n_tiles),
        grid=(b, n_tiles, 3),
        in_specs=[pl.BlockSpec((cols, DN_WIDTH),
                               lambda bi, t, j: (jnp.maximum(bi * rows + t * tr - 1, 0), c0 + j)),
                  pl.BlockSpec((tt, DN_WIDTH), lambda bi, t, j: (bi * n_tiles + t, c0 + j)),
                  pl.BlockSpec((cols, DN_WIDTH),
                               lambda bi, t, j: (jnp.minimum(bi * rows + (t + 1) * tr, nblk - 1), c0 + j)),
                  pl.BlockSpec((SHORT_CONV * SHORT_CONV, DN_WIDTH), lambda bi, t, j: (0, j))],
        out_specs=pl.BlockSpec((tt, DN_WIDTH), lambda bi, t, j: (bi * n_tiles + t, j)),
        out_shape=jax.ShapeDtypeStruct((b * n, 3 * DN_WIDTH), F32),
        compiler_params=_cparams("parallel", "parallel", "parallel"),
        name="dn_conv_prep",
    )(p, p, p, conv_w.reshape(SHORT_CONV * SHORT_CONV, 3 * DN_WIDTH))


PREP_CHUNKS = 8
PREP_GROUP = 4


def _softplus(x):
    return jnp.maximum(x, 0.0) + jnp.log(1.0 + jnp.exp(-jnp.abs(x)))


def _delta_prep_kernel(qkv_ref, sm_ref, smt_ref, prm_ref, prmt_ref, u0_ref, lhs1_ref, lhs2_ref, gl_ref, *, n_chunks):
    cc = DN_CHUNK
    nh = DN_HEADS
    sm = sm_ref[...]
    beta_all = _sigmoid(sm)
    g_all = -jnp.exp(prm_ref[0:1, :]) * _softplus(sm + prm_ref[1:2, :])
    gt_all = -jnp.exp(prmt_ref[:, 0:1]) * _softplus(smt_ref[...] + prmt_ref[:, 1:2])
    lt = 2 * cc
    lane = lax.broadcasted_iota(jnp.int32, (gt_all.shape[0], lt), 1) % cc
    gt_tiles = []
    for t in range(gt_all.shape[1] // lt):
        gt_f = gt_b = gt_all[:, t * lt:(t + 1) * lt]
        s = 1
        while s < cc:
            gt_f = gt_f + jnp.where(lane >= s, pltpu.roll(gt_f, s, axis=1), 0.0)
            gt_b = gt_b + jnp.where(lane < cc - s, pltpu.roll(gt_b, lt - s, axis=1), 0.0)
            s *= 2
        gt_tiles.append((gt_f, gt_b))
    ri = lax.broadcasted_iota(jnp.int32, (cc, cc), 0)
    ci_ = lax.broadcasted_iota(jnp.int32, (cc, cc), 1)
    sub = lax.broadcasted_iota(jnp.int32, (cc, 1), 0)
    blk = lambda s: (ri // s) == (ci_ // s)
    leaf = 8

    heads = []
    for ci in range(n_chunks):
        rows = slice(ci * cc, (ci + 1) * cc)
        gc_f = g_all[rows]
        gc_b = gc_f
        s = 1
        while s < cc:
            gc_f = gc_f + jnp.where(sub >= s, pltpu.roll(gc_f, s, axis=0), 0.0)
            gc_b = gc_b + jnp.where(sub < cc - s, pltpu.roll(gc_b, cc - s, axis=0), 0.0)
            s *= 2
        for h in range(nh):
            q = qkv_ref[rows, h * 128:(h + 1) * 128]
            k = qkv_ref[rows, (nh + h) * 128:(nh + h + 1) * 128]
            v = qkv_ref[rows, (2 * nh + h) * 128:(2 * nh + h + 1) * 128]
            heads.append((ci, rows, h, q, k, v, gc_f, gc_b))
    qkks = [lax.dot_general(jnp.concatenate([q, k], axis=0).astype(BF16), k.astype(BF16),
                            (((1,), (1,)), ((), ())), preferred_element_type=F32)
            for (_, _, _, q, k, _, _, _) in heads]
    per_group = PREP_GROUP * nh
    for g0 in range(0, len(heads), per_group):
        group = list(zip(heads[g0:g0 + per_group], qkks[g0:g0 + per_group]))

        def token_scalars(ci, rows, h, d, gc_f, gc_b):
            beta = jnp.broadcast_to(beta_all[rows, d * nh + h:d * nh + h + 1], (cc, DN_HEAD_DIM))
            ca_ = 2 * nh + d * nh + h
            g_col = jnp.broadcast_to((gc_f if d == 0 else gc_b)[:, ca_:ca_ + 1], (cc, DN_HEAD_DIM))
            g_last = g_col[cc - 1:cc] if d == 0 else g_col[0:1]
            return beta, g_col, g_last

        mats = []
        for (ci, rows, h, q, k, v, gc_f, gc_b), qkk in group:
            qk, kk = qkk[:cc], qkk[cc:]
            for d in range(2):
                beta, g_col, g_last = token_scalars(ci, rows, h, d, gc_f, gc_b)
                lo = (ci % 2) * cc
                g_row = gt_tiles[ci // 2][d][2 * nh + d * nh + h:2 * nh + d * nh + h + 1, lo:lo + cc]
                incl = (ri >= ci_) if d == 0 else (ri <= ci_)
                strict = (ri > ci_) if d == 0 else (ri < ci_)
                decay = jnp.exp(jnp.where(incl, g_col[:, :cc] - g_row, -1e30))
                mats.append(jnp.where(strict, beta[:, :cc] * kk * decay, 0.0))
                lhs1_ref[d, h, ci, cc:2 * cc, :] = (q * jnp.exp(g_col)).astype(BF16)
                lhs2_ref[d, h, ci, 0:cc, :] = (qk * decay).astype(BF16)
                lhs2_ref[d, h, ci, cc:3 * cc, :] = (k * jnp.exp(g_last - g_col)).T.astype(BF16)
                gl_ref[d, h, ci] = jnp.broadcast_to(jnp.exp(g_last), (8, DN_HEAD_DIM))

        pws = [jnp.where(blk(leaf), a, 0.0) for a in mats]
        devs = [-pw for pw in pws]
        for _ in range(2):
            pws = [_bdot(pw, pw) for pw in pws]
            cross = [_bdot(dev, pw) for dev, pw in zip(devs, pws)]
            devs = [dev + pw + x for dev, pw, x in zip(devs, pws, cross)]
        s = leaf
        while s < cc:
            offs = [jnp.where(blk(2 * s) & jnp.logical_not(blk(s)), a, 0.0) for a in mats]
            xs = [off + _bdot(dev, off) for dev, off in zip(devs, offs)]
            devs = [dev - x - _bdot(x, dev) for dev, x in zip(devs, xs)]
            s *= 2
        chains, rhss = [], []
        for (ci, rows, h, q, k, v, gc_f, gc_b), _ in group:
            for d in range(2):
                beta, g_col, _ = token_scalars(ci, rows, h, d, gc_f, gc_b)
                chains.append((ci, rows, h, d))
                rhss.append(jnp.concatenate([v * beta, k * (beta * jnp.exp(g_col))], axis=1))
        uws = [rhs + _bdot(dev, rhs) for dev, rhs in zip(devs, rhss)]
        for (ci, rows, h, d), uw in zip(chains, uws):
            u0_ref[d, h, rows, :] = uw[:, :128].astype(BF16)
            lhs1_ref[d, h, ci, 0:cc, :] = uw[:, 128:].astype(BF16)


def delta_prep(qkv, small, small_t, prm, prm_t, b, n):
    nc = n // DN_CHUNK
    cb = min(PREP_CHUNKS, nc)
    nblk = nc // cb
    tt = cb * DN_CHUNK
    nh = DN_HEADS
    return pl.pallas_call(
        functools.partial(_delta_prep_kernel, n_chunks=cb),
        grid=(b, nblk),
        in_specs=[pl.BlockSpec((tt, 3 * DN_WIDTH), lambda bi, c: (bi * nblk + c, 0)),
                  pl.BlockSpec((tt, SMALL_WIDTH), lambda bi, c: (bi * nblk + c, 0)),
                  pl.BlockSpec((4 * nh, tt), lambda bi, c: (0, bi * nblk + c)),
                  pl.BlockSpec((2, SMALL_WIDTH), lambda bi, c: (0, 0)),
                  pl.BlockSpec((4 * nh, 2), lambda bi, c: (0, 0))],
        out_specs=[pl.BlockSpec((2, None, nh, tt, 128), lambda bi, c: (0, bi, 0, c, 0)),
                   pl.BlockSpec((2, None, nh, cb, 128, 128), lambda bi, c: (0, bi, 0, c, 0, 0)),
                   pl.BlockSpec((2, None, nh, cb, 192, 64), lambda bi, c: (0, bi, 0, c, 0, 0)),
                   pl.BlockSpec((2, None, nh, cb, 8, 128), lambda bi, c: (0, bi, 0, c, 0, 0))],
        out_shape=[jax.ShapeDtypeStruct((2, b, nh, n, 128), BF16),
                   jax.ShapeDtypeStruct((2, b, nh, nc, 128, 128), BF16),
                   jax.ShapeDtypeStruct((2, b, nh, nc, 192, 64), BF16),
                   jax.ShapeDtypeStruct((2, b, nh, nc, 8, 128), F32)],
        compiler_params=_cparams("parallel", "parallel"),
        name="delta_prep",
    )(qkv, small, small_t, prm, prm_t)


def _delta_scan_kernel(u0f_ref, u0b_ref, l1f_ref, l1b_ref, l2f_ref, l2b_ref, glf_ref, glb_ref, s0_ref,
                       of_ref, ob_ref, sout_ref, st_ref, *, sc, nb):
    n = pl.program_id(0)
    cc = DN_CHUNK

    @pl.when(n == 0)
    def _():
        st_ref[...] = s0_ref[...]

    dirs = ((u0f_ref, l1f_ref, l2f_ref, glf_ref, of_ref), (u0b_ref, l1b_ref, l2b_ref, glb_ref, ob_ref))

    def body(i, carry):
        chains = []
        for d in range(2):
            ci = i if d == 0 else sc - 1 - i
            r0 = pl.multiple_of(ci * cc, cc)
            chains += [(d, bi, h, ci, r0) for bi in range(nb) for h in range(DN_HEADS)]
        sts = [st_ref[d, bi, h] for (d, bi, h, _, _) in chains]
        r1s = [jnp.dot(dirs[d][1][bi, h, ci], st.astype(BF16), preferred_element_type=F32)
               for (d, bi, h, ci, _), st in zip(chains, sts)]
        us = [dirs[d][0][bi, h, pl.ds(r0, cc), :].astype(F32) - r1[:cc]
              for (d, bi, h, _, r0), r1 in zip(chains, r1s)]
        r2s = [jnp.dot(dirs[d][2][bi, h, ci], u.astype(BF16), preferred_element_type=F32)
               for (d, bi, h, ci, _), u in zip(chains, us)]
        for (d, bi, h, ci, r0), st, r1, r2 in zip(chains, sts, r1s, r2s):
            dirs[d][4][bi, pl.ds(r0, cc), h * 128:(h + 1) * 128] = (r1[cc:] + r2[:cc]).astype(BF16)
            st_ref[d, bi, h] = st * dirs[d][3][bi, h, ci, 0:1, :] + r2[cc:]
        return carry

    lax.fori_loop(0, sc, body, 0, unroll=4)

    @pl.when(n == pl.num_programs(0) - 1)
    def _():
        sout_ref[...] = st_ref[...]


def delta_scan(u0, lhs1, lhs2, gl, s0, b, n):
    nc = n // DN_CHUNK
    sc = min(8, nc)
    nblk = nc // sc
    nh = DN_HEADS
    tt = sc * DN_CHUNK
    fwd = lambda i: i
    bwd = lambda i: nblk - 1 - i

    def specs(d, blk):
        return [pl.BlockSpec((None, b, nh, tt, 128), lambda i: (d, 0, 0, blk(i), 0)),
                pl.BlockSpec((None, b, nh, sc, 128, 128), lambda i: (d, 0, 0, blk(i), 0, 0)),
                pl.BlockSpec((None, b, nh, sc, 192, 64), lambda i: (d, 0, 0, blk(i), 0, 0)),
                pl.BlockSpec((None, b, nh, sc, 8, 128), lambda i: (d, 0, 0, blk(i), 0, 0))]

    sf, sb = specs(0, fwd), specs(1, bwd)
    in_specs = [sf[0], sb[0], sf[1], sb[1], sf[2], sb[2], sf[3], sb[3],
                pl.BlockSpec((2, b, nh, 128, 128), lambda i: (0, 0, 0, 0, 0))]
    return pl.pallas_call(
        functools.partial(_delta_scan_kernel, sc=sc, nb=b),
        grid=(nblk,),
        in_specs=in_specs,
        out_specs=[pl.BlockSpec((b, tt, DN_WIDTH), lambda i: (0, fwd(i), 0)),
                   pl.BlockSpec((b, tt, DN_WIDTH), lambda i: (0, bwd(i), 0)),
                   pl.BlockSpec((2, b, nh, 128, 128), lambda i: (0, 0, 0, 0, 0))],
        out_shape=[jax.ShapeDtypeStruct((b, n, DN_WIDTH), BF16), jax.ShapeDtypeStruct((b, n, DN_WIDTH), BF16),
                   jax.ShapeDtypeStruct((2, b, nh, 128, 128), F32)],
        scratch_shapes=[pltpu.VMEM((2, b, nh, 128, 128), F32)],
        compiler_params=_cparams("arbitrary"),
        name="delta_scan",
    )(u0, u0, lhs1, lhs1, lhs2, lhs2, gl, gl, s0)


def delta_branch(p, small, conv_w, a_log, dt_bias, s0, b, n, rows, cols):
    nh = DN_HEADS
    qkv = dn_conv_prep(p, conv_w, b, n, rows, cols)
    small_t = small[:, :4 * nh].T
    rate = jnp.concatenate([jnp.zeros((2 * nh,), F32), a_log.reshape(-1)])
    bias = jnp.concatenate([jnp.zeros((2 * nh,), F32), dt_bias.reshape(-1)])
    prm_t = jnp.stack([rate, bias], axis=1)
    prm = jnp.pad(prm_t.T, ((0, 0), (0, SMALL_WIDTH - 4 * nh)))
    u0, lhs1, lhs2, gl = delta_prep(qkv, small, small_t, prm, prm_t, b, n)
    o_f, o_b, s_out = delta_scan(u0, lhs1, lhs2, gl, s0, b, n)
    return o_f.reshape(b * n, DN_WIDTH), o_b.reshape(b * n, DN_WIDTH), s_out


def _cos_sin(rows, cols, period):
    ang = 2.0 * np.pi * ((np.arange(rows)[:, None] * np.arange(cols)[None, :]) % period) / period
    return np.cos(ang), np.sin(ang)


def _const_bf16(a):
    return jnp.asarray(a, F32).astype(BF16)


def _stage2_matrices(l2):
    c, s = _cos_sin(l2, l2, l2)
    fwd = np.block([[c, s], [-s, c]])
    inv = np.block([[c, -s], [s, c]])
    return _const_bf16(fwd), _const_bf16(inv)


def _twiddles(l1, l2, kb, n_k1=None):
    n_k1 = l1 if n_k1 is None else n_k1
    ang = 2.0 * np.pi * ((np.arange(l2)[:, None] * np.arange(n_k1)[None, :]) % (l1 * l2)) / (l1 * l2)
    tw = np.stack([np.cos(ang), np.sin(ang)], axis=0).reshape(2, l2, n_k1 // kb, kb)
    return jnp.asarray(np.transpose(tw, (2, 0, 1, 3)), F32)


def _twiddle_mul(ar, ai, c, s, conj):
    if conj:
        return ar * c - ai * s, ai * c + ar * s
    return ar * c + ai * s, ai * c - ar * s


def _cplx_apply(m_ref, re, im):
    half = re.shape[0]
    out = jnp.dot(m_ref[...], jnp.concatenate([re, im], axis=0).astype(BF16), preferred_element_type=F32)
    return out[:half], out[half:]


def _fnet_mid_kernel(a_ref, tw_ref, m2_ref, cs_ref, o_ref, *, kb, ch):
    ts = [_twiddle_mul(a_ref[0, j], a_ref[1, j], tw_ref[0, :, j:j + 1], tw_ref[1, :, j:j + 1], False)
          for j in range(kb)]
    us = [_cplx_apply(m2_ref, tr, ti) for tr, ti in ts]
    ys = [jnp.dot(jnp.concatenate([ur, ui], axis=1).astype(BF16), cs_ref[...], preferred_element_type=F32)
          for ur, ui in us]
    for j, y in enumerate(ys):
        o_ref[:, j * ch:(j + 1) * ch] = y


def fnet_mid(a, tw, m2, cs, kb):
    bsz, _, l1, l2, ch = a.shape
    return pl.pallas_call(
        functools.partial(_fnet_mid_kernel, kb=kb, ch=ch),
        grid=(bsz, l1 // kb),
        in_specs=[pl.BlockSpec((None, 2, kb, l2, ch), lambda b, k: (b, 0, k, 0, 0)),
                  pl.BlockSpec((None, 2, l2, kb), lambda b, k: (k, 0, 0, 0)),
                  pl.BlockSpec(m2.shape, lambda b, k: (0, 0)),
                  pl.BlockSpec(cs.shape, lambda b, k: (0, 0))],
        out_specs=pl.BlockSpec((None, l2, kb * ch), lambda b, k: (b, 0, k)),
        out_shape=jax.ShapeDtypeStruct((bsz, l2, l1 * ch), F32),
        compiler_params=_cparams("parallel", "parallel"),
        name="fnet_mid",
    )(a, tw, m2, cs)


def _fnet_channel_matrix(n):
    c, s = _cos_sin(FN_GROUP_DIM, FN_GROUP_DIM, FN_GROUP_DIM)
    eye = np.eye(FN_GROUPS)
    scale = 1.0 / math.sqrt(n * FN_GROUP_DIM)
    return _const_bf16(np.concatenate([np.kron(eye, c), np.kron(eye, s)], axis=0) * scale)


def _fnet_long_kernel(x_ref, kin_ref, tw_ref, m2k_ref, cs_ref, o_ref, a_ref):
    n_sub = x_ref.shape[1] // SUB
    rows, width = x_ref.shape[0] * SUB, x_ref.shape[2]
    half = SUB * SUB
    for j in range(n_sub):
        r = jnp.dot(kin_ref[...], x_ref[:, j * SUB:(j + 1) * SUB, :].reshape(rows, width).astype(BF16),
                    preferred_element_type=F32)
        a_ref[0, j * half:(j + 1) * half, :] = r[:half]
        a_ref[1, j * half:(j + 1) * half, :] = r[half:]
    tr, ti = _twiddle_mul(a_ref[0], a_ref[1], tw_ref[0], tw_ref[1], False)
    ur, ui = _cplx_apply(m2k_ref, tr, ti)
    y = jnp.dot(jnp.concatenate([ur, ui], axis=1).astype(BF16), cs_ref[...], preferred_element_type=F32)
    o_ref[...] = y.reshape(o_ref.shape)


def fnet_long(p, b, n):
    ch, sub = FN_WIDTH, SUB
    l1, l2 = n // 128, 128
    nblk = l1 // sub
    c1, s1 = _cos_sin(l1, l1, l1)
    m1 = np.stack([c1, -s1], axis=0).reshape(2, nblk, sub, l1)
    kin = np.einsum('pbkn,jq->bpjknq', m1, np.eye(sub)).reshape(nblk, 2 * sub * sub, l1 * sub)
    ang = 2.0 * np.pi * ((np.arange(l2)[:, None] * np.arange(l1)[None, :]) % n) / n
    tw = np.stack([np.cos(ang), np.sin(ang)], axis=0).reshape(2, l2, nblk, sub)
    tw = np.transpose(tw, (2, 0, 1, 3)).reshape(nblk, 2, l2 * sub, 1)
    c2, s2 = _cos_sin(l2, l2, l2)
    m2k = _const_bf16(np.kron(np.block([[c2, s2], [-s2, c2]]), np.eye(sub)))
    cs = _fnet_channel_matrix(n)
    const = lambda a: pl.BlockSpec(a.shape, lambda bi, k: (0,) * a.ndim, pipeline_mode=pl.Buffered(1))
    y = pl.pallas_call(
        _fnet_long_kernel,
        grid=(b, nblk),
        in_specs=[pl.BlockSpec((None, l1, l2, ch), lambda bi, k: (bi, 0, 0, P_FN // ch),
                               pipeline_mode=pl.Buffered(1)),
                  pl.BlockSpec((None,) + kin.shape[1:], lambda bi, k: (k, 0, 0)),
                  pl.BlockSpec((None, 2, l2 * sub, 1), lambda bi, k: (k, 0, 0, 0)),
                  const(m2k), const(cs)],
        out_specs=pl.BlockSpec((None, l2, sub, ch), lambda bi, k: (bi, 0, k, 0)),
        out_shape=jax.ShapeDtypeStruct((b, l2, l1, ch), F32),
        scratch_shapes=[pltpu.VMEM((2, l2 * sub, ch), F32)],
        compiler_params=_cparams("parallel", "parallel"),
        name="fnet_long",
    )(p.reshape(b, l1, l2, P_WIDTH), _const_bf16(kin), jnp.asarray(tw, F32), m2k, cs)
    return y.reshape(b * n, ch)


def fnet_branch(p, b, n):
    if n == 8192:
        return fnet_long(p, b, n)
    ch = FN_WIDTH
    xr = lax.slice_in_dim(p, P_FN, P_FN + ch, axis=1).reshape(b, 1, 1, n, ch)
    a = jnp.concatenate([xr, jnp.zeros_like(xr)], axis=1)
    y = fnet_mid(a, _twiddles(1, n, 1), _stage2_matrices(n)[0], _fnet_channel_matrix(n), 1)
    return y.reshape(b * n, ch)


def _seq_conv_kernel(prev_ref, cur_ref, next_ref, w_ref, *o_refs, n_tiles):
    t = pl.program_id(1)
    tt = cur_ref.shape[0]
    cur = cur_ref[...]
    row = lax.broadcasted_iota(jnp.int32, (tt, 1), 0)
    before = jnp.where(t == 0, 0.0, prev_ref[7:8, :])
    after = jnp.where(t == n_tiles - 1, 0.0, next_ref[0:1, :])
    left = jnp.where(row == 0, before, pltpu.roll(cur, 1, axis=0))
    right = jnp.where(row == tt - 1, after, pltpu.roll(cur, tt - 1, axis=0))
    y = w_ref[0:1, :] * left + w_ref[1:2, :] * cur + w_ref[2:3, :] * right
    for part, o_ref in enumerate(o_refs):
        o_ref[...] = y[:, part * HY_WIDTH:(part + 1) * HY_WIDTH]


def seq_conv(p, conv_w, b, n):
    tt = min(n, 1024)
    n_tiles = n // tt
    parts = HY_ORDER + 1
    w = parts * HY_WIDTH
    c0 = P_HY // w
    out_spec = pl.BlockSpec((tt, HY_WIDTH), lambda bi, t: (bi * n_tiles + t, 0))
    return pl.pallas_call(
        functools.partial(_seq_conv_kernel, n_tiles=n_tiles),
        grid=(b, n_tiles),
        in_specs=[pl.BlockSpec((8, w), lambda bi, t: (jnp.maximum((bi * n_tiles + t) * (tt // 8) - 1, 0), c0)),
                  pl.BlockSpec((tt, w), lambda bi, t: (bi * n_tiles + t, c0)),
                  pl.BlockSpec((8, w), lambda bi, t: (jnp.minimum((bi * n_tiles + t + 1) * (tt // 8),
                                                                  b * n // 8 - 1), c0)),
                  pl.BlockSpec((SHORT_CONV, w), lambda bi, t: (0, 0))],
        out_specs=[out_spec] * parts,
        out_shape=[jax.ShapeDtypeStruct((b * n, HY_WIDTH), F32)] * parts,
        compiler_params=_cparams("parallel", "parallel"),
        name="hy_seq_conv",
    )(p, p, p, conv_w)


def _hdot(a, b):
    return jnp.dot(a, b, preferred_element_type=F32, precision=lax.Precision.HIGHEST)


def _hy_filter_kernel(ft_ref, t_ref, w1t_ref, b1_ref, f1_ref, w2t_ref, b2_ref, f2_ref, w3_ref, dl_ref, k_ref, s_ref,
                      *, n, tr):
    i = pl.program_id(0)
    hid = jnp.sin(f1_ref[...] * (_hdot(w1t_ref[...], ft_ref[...]) + b1_ref[...]))
    hid = jnp.sin(f2_ref[...] * (_hdot(w2t_ref[...], hid) + b2_ref[...]))
    filt = _hdot(hid.T, w3_ref[...]) * jnp.exp(-t_ref[...] * dl_ref[...])
    row = i * tr + lax.broadcasted_iota(jnp.int32, (tr, 1), 0)
    filt = jnp.where(row == n, 0.0, filt)
    k_ref[...] = filt

    @pl.when(i == 0)
    def _():
        s_ref[...] = jnp.zeros_like(s_ref)

    s_ref[...] += jnp.sum(jnp.abs(filt), axis=0, keepdims=True)


def hy_filter(n, w1, b1, freq1, w2, b2, freq2, w3):
    pos = jnp.arange(n, dtype=F32)
    t = pos / max(n - 1, 1)
    bands = jnp.linspace(1e-4, HY_BANDS - 1, HY_BANDS, dtype=F32)
    ang = (2.0 * math.pi / n) * pos[:, None] * bands[None, :]
    feats = jnp.concatenate([t[:, None], jnp.cos(ang), -jnp.sin(ang)], axis=-1)
    feats2 = jnp.concatenate([feats, feats[:1], feats[:0:-1]], axis=0)
    kpad = 128
    feats_t = jnp.pad(feats2, ((0, 0), (0, kpad - HY_EMB_DIM))).T
    w1t = jnp.pad(w1, ((0, kpad - HY_EMB_DIM), (0, 0))).T
    min_decay = math.log(HY_DECAY_TARGET) / HY_SLOW_DECAY_PCT
    max_decay = math.log(HY_DECAY_TARGET) / HY_FAST_DECAY_PCT
    cw = HY_ORDER * HY_WIDTH
    deltas = jnp.abs(jnp.linspace(min_decay, max_decay, cw, dtype=F32)).reshape(1, cw)
    tr = min(n, 1024)
    half = n // tr
    hd = HY_FILTER_HIDDEN
    vec = lambda v: v.reshape(hd, 1)
    full = lambda shp: pl.BlockSpec(shp, lambda i: (0, 0))
    return pl.pallas_call(
        functools.partial(_hy_filter_kernel, n=n, tr=tr),
        grid=(2 * half,),
        in_specs=[pl.BlockSpec((kpad, tr), lambda i: (0, i)), pl.BlockSpec((tr, 1), lambda i: (i, 0)),
                  full((hd, kpad)), full((hd, 1)), full((hd, 1)), full((hd, hd)), full((hd, 1)), full((hd, 1)),
                  pl.BlockSpec((hd, cw), lambda i: (0, i // half)), full((1, cw))],
        out_specs=[pl.BlockSpec((tr, cw), lambda i: (i, 0)), full((1, cw))],
        out_shape=[jax.ShapeDtypeStruct((2 * n, cw), F32), jax.ShapeDtypeStruct((1, cw), F32)],
        compiler_params=_cparams("arbitrary"),
        name="hy_filter",
    )(feats_t, feats2[:, 0:1], w1t, vec(b1), vec(freq1), w2.T, vec(b2), vec(freq2), w3, deltas)


def _hy_spec_kernel(a_ref, tw_ref, m2_ref, s_ref, o_ref, *, kb):
    inv = 1.0 / s_ref[...]
    ts = [_twiddle_mul(a_ref[0, j], a_ref[1, j], tw_ref[0, :, j:j + 1], tw_ref[1, :, j:j + 1], False)
          for j in range(kb)]
    xs = [_cplx_apply(m2_ref, tr, ti) for tr, ti in ts]
    for j, (xr, xi) in enumerate(xs):
        o_ref[0, j] = xr * inv
        o_ref[1, j] = xi * inv


def hy_spec(a, tw, m2, abs_sum, kb):
    _, l1, l2, cw = a.shape
    w = HY_WIDTH
    return pl.pallas_call(
        functools.partial(_hy_spec_kernel, kb=kb),
        grid=(l1 // kb, cw // w),
        in_specs=[pl.BlockSpec((2, kb, l2, w), lambda k, c: (0, k, 0, c)),
                  pl.BlockSpec((None, 2, l2, kb), lambda k, c: (k, 0, 0, 0)),
                  pl.BlockSpec(m2.shape, lambda k, c: (0, 0)),
                  pl.BlockSpec((1, w), lambda k, c: (0, c))],
        out_specs=pl.BlockSpec((2, kb, l2, w), lambda k, c: (0, k, 0, c)),
        out_shape=jax.ShapeDtypeStruct((2, l1, l2, cw), F32),
        compiler_params=_cparams("parallel", "parallel"),
        name="hy_spec",
    )(a, tw, m2, abs_sum)


def _hy_mid_kernel(a_ref, tw_ref, m2_ref, m2c_ref, kf_ref, o_ref, *, kb):
    cs = [(tw_ref[0, :, j:j + 1], tw_ref[1, :, j:j + 1]) for j in range(kb)]
    ts = [_twiddle_mul(a_ref[0, j], a_ref[1, j], c, s, False) for j, (c, s) in enumerate(cs)]
    xs = [_cplx_apply(m2_ref, tr, ti) for tr, ti in ts]
    ps = [(xr * kf_ref[0, j] - xi * kf_ref[1, j], xr * kf_ref[1, j] + xi * kf_ref[0, j])
          for j, (xr, xi) in enumerate(xs)]
    bs = [_cplx_apply(m2c_ref, pr, pi) for pr, pi in ps]
    for j, ((br, bi), (c, s)) in enumerate(zip(bs, cs)):
        o_ref[0, j], o_ref[1, j] = _twiddle_mul(br, bi, c, s, True)


def hy_mid(a, tw, m2, m2c, kf, order, kb):
    bsz, _, l1, l2, w = a.shape
    return pl.pallas_call(
        functools.partial(_hy_mid_kernel, kb=kb),
        grid=(bsz, l1 // kb),
        in_specs=[pl.BlockSpec((None, 2, kb, l2, w), lambda b, k: (b, 0, k, 0, 0)),
                  pl.BlockSpec((None, 2, l2, kb), lambda b, k: (k, 0, 0, 0)),
                  pl.BlockSpec(m2.shape, lambda b, k: (0, 0)),
                  pl.BlockSpec(m2c.shape, lambda b, k: (0, 0)),
                  pl.BlockSpec((2, kb, l2, w), lambda b, k: (0, k, 0, order))],
        out_specs=pl.BlockSpec((None, 2, kb, l2, w), lambda b, k: (b, 0, k, 0, 0)),
        out_shape=jax.ShapeDtypeStruct((bsz, 2, l1, l2, w), F32),
        compiler_params=_cparams("parallel", "parallel"),
        name="hy_mid",
    )(a, tw, m2, m2c, kf)


def _hy_out_kernel(m_ref, bp_ref, xo_ref, z_ref, bias_ref, o_ref):
    y = jnp.dot(m_ref[...], bp_ref[...].astype(BF16), preferred_element_type=F32)
    o_ref[...] = xo_ref[...] * (y + bias_ref[...] * z_ref[...])


def hy_out(m, bp, xo, z, bias_row, tn):
    bsz, k, n = bp.shape
    r = m.shape[0]
    blk = pl.BlockSpec((None, r, tn), lambda b, j: (b, 0, j))
    return pl.pallas_call(
        _hy_out_kernel,
        grid=(bsz, n // tn),
        in_specs=[pl.BlockSpec((r, k), lambda b, j: (0, 0)),
                  pl.BlockSpec((None, k, tn), lambda b, j: (b, 0, j)), blk, blk,
                  pl.BlockSpec((1, tn), lambda b, j: (0, j))],
        out_specs=blk,
        out_shape=jax.ShapeDtypeStruct((bsz, r, n), F32),
        compiler_params=_cparams("parallel", "parallel"),
        name="hy_out",
    )(m, bp, xo, z, bias_row)


HY_KB = 13
HY_SLOTS = 16


def _half_spectrum_blocks(l1, kb):
    return -(-(l1 // 2 + 1) // kb)


def _kron_stage1(l1, n1_used, kb):
    nblk = _half_spectrum_blocks(l1, kb)
    c, s = _cos_sin(nblk * kb, n1_used, l1)
    m = np.stack([c, -s], axis=0).reshape(2, nblk, kb, n1_used)
    m = np.transpose(m, (1, 0, 2, 3)).reshape(nblk, 2 * kb, n1_used)
    return _const_bf16(np.stack([np.kron(blk, np.eye(SUB)) for blk in m]))


def _kron_stage_out(l1, n1_used, kb, slots):
    nblk = _half_spectrum_blocks(l1, kb)
    c, s = _cos_sin(n1_used, nblk * kb, l1)
    k1 = np.arange(nblk * kb)
    mult = np.where((k1 == 0) | (k1 == l1 // 2), 1.0, np.where(k1 < l1 // 2, 2.0, 0.0))
    m = np.stack([c * mult, -s * mult], axis=1).reshape(n1_used, 2, nblk, kb) / (l1 * l1)
    m = np.pad(m, ((0, 0), (0, 0), (0, 0), (0, slots - kb)))
    m = np.transpose(m, (2, 0, 1, 3)).reshape(nblk, n1_used, 2 * slots)
    return _const_bf16(np.stack([np.kron(blk, np.eye(SUB)) for blk in m]))


def _strided_stage_in(kin_ref, src_ref, a_ref):
    n_sub, width = src_ref.shape[1] // SUB, src_ref.shape[2]
    rows = src_ref.shape[0] * SUB
    for j in range(n_sub):
        r = jnp.dot(kin_ref[...], src_ref[:, j * SUB:(j + 1) * SUB, :].reshape(rows, width).astype(BF16),
                    preferred_element_type=F32)
        a_ref[:, j * SUB:(j + 1) * SUB, :] = r.reshape(a_ref.shape[0], SUB, width)


def _hy_conv_kernel(z_ref, xo_ref, kin_ref, tw_ref, m2_ref, m2c_ref, kf_ref, kout_ref, bias_ref, o_ref,
                    a_ref, b_ref, *, kb):
    k = pl.program_id(1)
    slots = b_ref.shape[0] // 2

    @pl.when(k == 0)
    def _():
        o_ref[...] = jnp.zeros_like(o_ref)
        b_ref[...] = jnp.zeros_like(b_ref)

    _strided_stage_in(kin_ref, z_ref, a_ref)
    grp = kb
    for g0 in range(0, kb, grp):
        js = range(g0, g0 + grp)
        cs = [(tw_ref[0, :, j:j + 1], tw_ref[1, :, j:j + 1]) for j in js]
        ts = [_twiddle_mul(a_ref[j], a_ref[kb + j], c, s, False) for j, (c, s) in zip(js, cs)]
        xs = [_cplx_apply(m2_ref, tr, ti) for tr, ti in ts]
        ps = [(xr * kf_ref[0, j] - xi * kf_ref[1, j], xr * kf_ref[1, j] + xi * kf_ref[0, j])
              for j, (xr, xi) in zip(js, xs)]
        bs = [_cplx_apply(m2c_ref, pr, pi) for pr, pi in ps]
        for j, (br, bi), (c, s) in zip(js, bs, cs):
            b_ref[j], b_ref[slots + j] = _twiddle_mul(br, bi, c, s, True)
    n_sub, width = o_ref.shape[1] // SUB, o_ref.shape[2]
    for j in range(n_sub):
        r = jnp.dot(kout_ref[...], b_ref[:, j * SUB:(j + 1) * SUB, :].reshape(2 * slots * SUB, width).astype(BF16),
                    preferred_element_type=F32)
        o_ref[:, j * SUB:(j + 1) * SUB, :] += r.reshape(o_ref.shape[0], SUB, width)

    @pl.when(k == pl.num_programs(1) - 1)
    def _():
        o_ref[...] = xo_ref[...] * (o_ref[...] + bias_ref[...] * z_ref[...])


def hy_conv_long(z, xo, kf, bias, order, b, n):
    w, kb = HY_WIDTH, HY_KB
    l1 = l2 = 128
    n1 = n // l2
    nblk = _half_spectrum_blocks(l1, kb)
    tw = _twiddles(l1, l2, kb, nblk * kb)
    m2, m2c = _stage2_matrices(l2)
    kin, kout = _kron_stage1(l1, n1, kb), _kron_stage_out(l1, n1, kb, HY_SLOTS)
    tok = pl.BlockSpec((None, n1, l2, w), lambda bi, k: (bi, 0, 0, 0), pipeline_mode=pl.Buffered(1))
    const = lambda a: pl.BlockSpec(a.shape, lambda bi, k: (0,) * a.ndim)
    out = pl.pallas_call(
        functools.partial(_hy_conv_kernel, kb=kb),
        grid=(b, nblk),
        in_specs=[tok, tok,
                  pl.BlockSpec((None,) + kin.shape[1:], lambda bi, k: (k, 0, 0)),
                  pl.BlockSpec((None, 2, l2, kb), lambda bi, k: (k, 0, 0, 0)),
                  const(m2), const(m2c),
                  pl.BlockSpec((2, kb, l2, w), lambda bi, k: (0, k, 0, order)),
                  pl.BlockSpec((None,) + kout.shape[1:], lambda bi, k: (k, 0, 0)),
                  pl.BlockSpec((1, w), lambda bi, k: (0, 0))],
        out_specs=pl.BlockSpec((None, n1, l2, w), lambda bi, k: (bi, 0, 0, 0), pipeline_mode=pl.Buffered(1)),
        out_shape=jax.ShapeDtypeStruct((b, n1, l2, w), F32),
        scratch_shapes=[pltpu.VMEM((2 * kb, l2, w), F32), pltpu.VMEM((2 * HY_SLOTS, l2, w), F32)],
        compiler_params=_cparams("parallel", "arbitrary"),
        name="hy_conv_long",
    )(z.reshape(b, n1, l2, w), xo.reshape(b, n1, l2, w), kin, tw, m2, m2c, kf, kout, bias.reshape(1, w))
    return out.reshape(b * n, w)


def _hy_spec_long_kernel(kern_ref, kin_ref, tw_ref, m2_ref, s_ref, o_ref, a_ref, *, kb):
    _strided_stage_in(kin_ref, kern_ref, a_ref)
    inv = 1.0 / s_ref[...]
    ts = [_twiddle_mul(a_ref[j], a_ref[kb + j], tw_ref[0, :, j:j + 1], tw_ref[1, :, j:j + 1], False)
          for j in range(kb)]
    xs = [_cplx_apply(m2_ref, tr, ti) for tr, ti in ts]
    for j, (xr, xi) in enumerate(xs):
        o_ref[0, j] = xr * inv
        o_ref[1, j] = xi * inv


def hy_spec_long(kern, abs_sum):
    w, kb = HY_WIDTH, HY_KB
    l1 = l2 = 128
    cw = kern.shape[1]
    nblk = _half_spectrum_blocks(l1, kb)
    tw = _twiddles(l1, l2, kb, nblk * kb)
    m2, _ = _stage2_matrices(l2)
    kin = _kron_stage1(l1, l1, kb)
    return pl.pallas_call(
        functools.partial(_hy_spec_long_kernel, kb=kb),
        grid=(cw // w, nblk),
        in_specs=[pl.BlockSpec((l1, l2, w), lambda c, k: (0, 0, c), pipeline_mode=pl.Buffered(1)),
                  pl.BlockSpec((None,) + kin.shape[1:], lambda c, k: (k, 0, 0)),
                  pl.BlockSpec((None, 2, l2, kb), lambda c, k: (k, 0, 0, 0)),
                  pl.BlockSpec(m2.shape, lambda c, k: (0, 0)),
                  pl.BlockSpec((1, w), lambda c, k: (0, c))],
        out_specs=pl.BlockSpec((2, kb, l2, w), lambda c, k: (0, k, 0, c)),
        out_shape=jax.ShapeDtypeStruct((2, nblk * kb, l2, cw), F32),
        scratch_shapes=[pltpu.VMEM((2 * kb, l2, w), F32)],
        compiler_params=_cparams("parallel", "arbitrary"),
        name="hy_spec_long",
    )(kern.reshape(l1, l2, cw), kin, tw, m2, abs_sum)


def hyena_pallas(p, conv_w, w1, b1, freq1, w2, b2, freq2, w3, bias, b, n):
    w = HY_WIDTH
    x0, x1, v = seq_conv(p, conv_w, b, n)
    kern, abs_sum = hy_filter(n, w1, b1, freq1, w2, b2, freq2, w3)
    z = v
    if n == 8192:
        kf = hy_spec_long(kern, abs_sum)
        for order, xo in enumerate((x0, x1)):
            z = hy_conv_long(z, xo, kf, bias[order], order, b, n)
        return z
    l2 = 2 * n
    tw = _twiddles(1, l2, 1)
    m2, m2c = _stage2_matrices(l2)
    ak = jnp.stack([kern, jnp.zeros_like(kern)], axis=0).reshape(2, 1, l2, HY_ORDER * w)
    m_out = _const_bf16(np.eye(n, 2 * l2) / l2)
    kf = hy_spec(ak, tw, m2, abs_sum, 1)
    for order, xo in enumerate((x0, x1)):
        zp = jnp.pad(z.reshape(b, 1, 1, n, w), ((0, 0), (0, 0), (0, 0), (0, n), (0, 0)))
        a = jnp.concatenate([zp, jnp.zeros_like(zp)], axis=1)
        bp = hy_mid(a, tw, m2, m2c, kf, order, 1)
        z = hy_out(m_out, bp.reshape(b, 2 * l2, w), xo.reshape(b, n, w), z.reshape(b, n, w),
                   bias[order].reshape(1, w), w).reshape(b * n, w)
    return z


def _prep_w_in(w):
    main = jnp.concatenate([w[:, OFF_Q:OFF_Z], w[:, OFF_HY:OFF_GATE], w[:, OFF_FN:OFF_HY], w[:, OFF_Z:OFF_BETA]],
                           axis=1).astype(BF16)
    small = jnp.pad(w[:, OFF_BETA:OFF_FN], ((0, 0), (0, SMALL_WIDTH - 4 * DN_HEADS))).astype(BF16)
    return main, small, w[:, OFF_GATE:IN_WIDTH].astype(BF16)


def kernel(x, c, ctx, c_ctx, w_mod, b_mod, norm1, norm2, w_in, dn_conv, dn_a_log, dn_dt_bias,
           dn_out_norm, hy_conv, hy_w1, hy_b1, hy_freq1, hy_w2, hy_b2, hy_freq2, hy_w3, hy_bias,
           w_branch_a, w_branch_b, w_branch_c, w_out, w_ff1, w_ff2, final_norm):
    b, n_lat, d = x.shape
    rows = n_lat // GRID_W
    n_ctx = ctx.shape[1]
    tm_x, tm_c = 1024, n_ctx

    c_rows = jnp.concatenate([c, c_ctx[None], jnp.zeros((8 - b - 1, d), F32)], axis=0)
    mods = mod_vectors(c_rows, w_mod, b_mod)
    s_zero = jnp.zeros((2, b, DN_HEADS, DN_HEAD_DIM, DN_HEAD_DIM), F32)
    h, hc = x.reshape(b * n_lat, d), ctx.reshape(b * n_ctx, d)

    for l in range(DEPTH):
        last = l == DEPTH - 1
        mv = mods[l].reshape(8, N_MOD, 1, d)
        mx = [mv[:b, i] for i in range(N_MOD)]
        mc = [mv[b:b + 1, i] for i in range(N_MOD)]
        w_main, w_small, w_gate = _prep_w_in(w_in[l])
        wa, wb, wc, wo = (w.astype(BF16) for w in (w_branch_a[l], w_branch_b[l], w_branch_c[l], w_out[l]))
        w1, w2 = w_ff1[l].astype(BF16), w_ff2[l].astype(BF16)
        n1, n2 = norm1[l][None, None, :], norm2[l][None, None, :]

        p_c, small_c = in_proj(hc, n1 * (1.0 + mc[1]), mc[0], w_main, w_small, b * n_ctx, tm_c)
        p_x, small_x = in_proj(h, n1 * (1.0 + mx[1]), mx[0], w_main, w_small, n_lat, tm_x)

        def mix(p, n):
            y_c = hyena_pallas(p, hy_conv[l], hy_w1[l], hy_b1[l], hy_freq1[l], hy_w2[l], hy_b2[l], hy_freq2[l],
                               hy_w3[l], hy_bias[l], b, n)
            return fnet_branch(p, b, n), y_c

        ocf, ocb, s_ctx = delta_branch(p_c, small_c, dn_conv[l], dn_a_log[l], dn_dt_bias[l], s_zero,
                                       b, n_ctx, 1, n_ctx)
        oxf, oxb, _ = delta_branch(p_x, small_x, dn_conv[l], dn_a_log[l], dn_dt_bias[l], s_ctx,
                                   b, n_lat, rows, GRID_W)

        y_b, y_c = mix(p_x, n_lat)
        h = merge(oxf, oxb, p_x, y_b, y_c, h, n1 * (1.0 + mx[1]), mx[0], mx[2], dn_out_norm[l],
                  w_gate, wa, wb, wc, wo, n_lat, 512)
        h = mlp(h, n2 * (1.0 + mx[4]), mx[3], mx[5], w1, w2, n_lat, tm_x, final_norm if last else None)

        if not last:
            y_b, y_c = mix(p_c, n_ctx)
            hc = merge(ocf, ocb, p_c, y_b, y_c, hc, n1 * (1.0 + mc[1]), mc[0], mc[2], dn_out_norm[l],
                       w_gate, wa, wb, wc, wo, b * n_ctx, tm_c)
            hc = mlp(hc, n2 * (1.0 + mc[4]), mc[3], mc[5], w1, w2, b * n_ctx, tm_c)

    return h.reshape(b, n_lat, d)
```
